```python
import math
import jax, jax.numpy as jnp
from jax import lax
import numpy as np

D_MODEL = 1024
BATCH = 2
SEQ = 8192
DEPTH = 1

CHUNK = 64
Q_BLOCK = 128
MEM_LEN = 256
EPS = 1e-6
SSM_WIDTH = D_MODEL
SSM_GROUP = 16
SSM_GROUPS = SSM_WIDTH // SSM_GROUP
SSM_STATE = 64
DT_MIN = 1e-3
DT_MAX = 1e-1
DA_HEADS = 8
DA_HEAD_DIM = D_MODEL // (2 * DA_HEADS)
DA_V_DIM = 2 * DA_HEAD_DIM
DA_QK_WIDTH = DA_HEADS * 2 * DA_HEAD_DIM
DA_WIDTH = DA_HEADS * DA_V_DIM
ROPE_THETA = 10000.0
XA_HEADS = 4
XA_HEAD_DIM = D_MODEL // XA_HEADS
D_FF = 4 * D_MODEL
COL_SSM = SSM_WIDTH
COL_Q = COL_SSM + DA_QK_WIDTH
COL_K = COL_Q + DA_QK_WIDTH
COL_V = COL_K + DA_WIDTH
IN_COLS = COL_V + 2 * D_MODEL

kernel_name = "hybrid_s5_diffattn_gated_stream_encoder"

F32 = jnp.float32


def rms_norm(x, g):
    xf = x.astype(F32)
    y = xf * lax.rsqrt(jnp.mean(xf * xf, axis=-1, keepdims=True) + EPS)
    return (y * g.astype(F32)).astype(x.dtype)


def rope_tables(seqlen, dim):
    inv = ROPE_THETA ** (-jnp.arange(0, dim, 2, dtype=F32) / dim)
    ang = jnp.arange(seqlen, dtype=F32)[:, None] * inv[None, :]
    return jnp.cos(ang), jnp.sin(ang)


def apply_rope(t, cos, sin):
    half = t.shape[-1] // 2
    c = cos[:, None, None, :]
    s = sin[:, None, None, :]
    t1, t2 = t[..., :half], t[..., half:]
    return jnp.concatenate([t1 * c - t2 * s, t2 * c + t1 * s], axis=-1)


def s5_branch(u, lam_re, lam_im, log_dt, b_re, b_im, c_re, c_im, d_skip):
    bsz, seqlen, _ = u.shape
    ug = u.astype(F32).reshape(bsz, seqlen, SSM_GROUPS, SSM_GROUP).astype(jnp.complex64)
    lam = lax.complex(lam_re.astype(F32), lam_im.astype(F32))
    dt = jnp.exp(log_dt.astype(F32))[:, None]
    lam_bar = jnp.exp(lam * dt)
    b = lax.complex(b_re.astype(F32), b_im.astype(F32))
    c = lax.complex(c_re.astype(F32), c_im.astype(F32))
    b_bar = ((lam_bar - 1.0) / lam)[..., None] * b
    bu = jnp.einsum('gph,blgh->lbgp', b_bar, ug)
    a = jnp.broadcast_to(lam_bar, (seqlen,) + lam_bar.shape)

    def combine(e_i, e_j):
        a_i, s_i = e_i
        a_j, s_j = e_j
        return a_j * a_i, a_j[:, None] * s_i + s_j

    _, states = lax.associative_scan(combine, (a, bu), axis=0)
    y = jnp.einsum('ghp,lbgp->blgh', c, states).real
    y = y.reshape(bsz, seqlen, SSM_WIDTH) + d_skip.astype(F32) * u.astype(F32)
    return y.astype(u.dtype)


def diff_attention(q, k, v, lq1, lk1, lq2, lk2, head_norm, lam_init):
    bsz, seqlen, _ = q.shape
    nb = seqlen // Q_BLOCK
    cos, sin = rope_tables(seqlen, DA_HEAD_DIM)
    q = apply_rope(q.astype(F32).reshape(bsz, seqlen, DA_HEADS, 2, DA_HEAD_DIM), cos, sin)
    k = apply_rope(k.astype(F32).reshape(bsz, seqlen, DA_HEADS, 2, DA_HEAD_DIM), cos, sin)
    q = jnp.transpose(q, (0, 2, 3, 1, 4))
    k = jnp.transpose(k, (0, 2, 3, 1, 4))
    vv = jnp.transpose(v.astype(F32).reshape(bsz, seqlen, DA_HEADS, DA_V_DIM), (0, 2, 1, 3))
    qblocks = jnp.moveaxis(q.reshape(bsz, DA_HEADS, 2, nb, Q_BLOCK, DA_HEAD_DIM), 3, 0)
    lam = (jnp.exp(jnp.sum(lq1.astype(F32) * lk1.astype(F32)))
           - jnp.exp(jnp.sum(lq2.astype(F32) * lk2.astype(F32))) + lam_init)
    scale = DA_HEAD_DIM ** -0.5
    k_chunk = jnp.arange(seqlen) // CHUNK

    def one_block(args):
        qb, bi = args
        s = jnp.einsum('bhcqd,bhckd->bhcqk', qb, k) * scale
        q_chunk = (bi * Q_BLOCK + jnp.arange(Q_BLOCK)) // CHUNK
        mask = k_chunk[None, :] <= q_chunk[:, None]
        s = jnp.where(mask, s, -jnp.inf)
        p = jax.nn.softmax(s, axis=-1)
        w = p[:, :, 0] - lam * p[:, :, 1]
        return jnp.einsum('bhqk,bhke->bhqe', w, vv)

    o = lax.map(one_block, (qblocks, jnp.arange(nb, dtype=jnp.int32)))
    o = jnp.transpose(o, (1, 0, 3, 2, 4)).reshape(bsz, seqlen, DA_HEADS, DA_V_DIM)
    o = o * lax.rsqrt(jnp.mean(o * o, axis=-1, keepdims=True) + EPS) * head_norm.astype(F32)
    o = o * (1.0 - lam_init)
    return o.reshape(bsz, seqlen, DA_WIDTH).astype(v.dtype)


def memory_cross_attention(h, mem_n, w_xq, w_xkv, w_xo):
    bsz, seqlen, _ = h.shape
    q = (h @ w_xq).reshape(bsz, seqlen, XA_HEADS, XA_HEAD_DIM)
    kv = mem_n @ w_xkv
    k, v = jnp.split(kv, 2, axis=-1)
    k = k.reshape(bsz, -1, XA_HEADS, XA_HEAD_DIM)
    v = v.reshape(bsz, -1, XA_HEADS, XA_HEAD_DIM)
    s = jnp.einsum('blhd,bmhd->bhlm', q, k).astype(F32) * (XA_HEAD_DIM ** -0.5)
    p = jax.nn.softmax(s, axis=-1)
    o = jnp.einsum('bhlm,bmhd->blhd', p, v.astype(F32)).astype(h.dtype)
    return o.reshape(bsz, seqlen, D_MODEL) @ w_xo


def setup_inputs(seed: int = 0) -> dict:
    key = jax.random.key(seed)
    ks = iter(jax.random.split(key, 64))
    L = DEPTH

    def nrm(shape, scale):
        return scale * jax.random.normal(next(ks), shape, F32)

    def gain(width=D_MODEL):
        return 1.0 + nrm((L, width), 0.05)

    G, P, Hg = SSM_GROUPS, SSM_STATE, SSM_GROUP
    lam_im = jnp.broadcast_to(jnp.pi * jnp.arange(P, dtype=F32), (L, G, P))
    lam_re = -0.5 + jnp.clip(nrm((L, G, P), 0.01), -0.05, 0.05)
    log_dt = jax.random.uniform(next(ks), (L, G), F32, math.log(DT_MIN), math.log(DT_MAX))
    return {
        "x": nrm((BATCH, SEQ, D_MODEL), 1.0),
        "mem": nrm((BATCH, MEM_LEN, D_MODEL), 1.0),
        "norm_mix_pre": gain(),
        "w_in": nrm((L, D_MODEL, IN_COLS), D_MODEL ** -0.5),
        "b_gate": nrm((L, 2 * D_MODEL), 0.02),
        "ssm_lambda_re": lam_re,
        "ssm_lambda_im": lam_im,
        "ssm_log_dt": log_dt,
        "ssm_b_re": nrm((L, G, P, Hg), (2.0 * Hg) ** -0.5),
        "ssm_b_im": nrm((L, G, P, Hg), (2.0 * Hg) ** -0.5),
        "ssm_c_re": nrm((L, G, Hg, P), (2.0 * P) ** -0.5),
        "ssm_c_im": nrm((L, G, Hg, P), (2.0 * P) ** -0.5),
        "ssm_d": nrm((L, SSM_WIDTH), 1.0),
        "w_glu": nrm((L, SSM_WIDTH, SSM_WIDTH), SSM_WIDTH ** -0.5),
        "b_glu": nrm((L, SSM_WIDTH), 0.02),
        "w_ssm_proj": nrm((L, SSM_WIDTH, D_MODEL), SSM_WIDTH ** -0.5),
        "da_lambda_q1": nrm((L, DA_HEAD_DIM), 0.1),
        "da_lambda_k1": nrm((L, DA_HEAD_DIM), 0.1),
        "da_lambda_q2": nrm((L, DA_HEAD_DIM), 0.1),
        "da_lambda_k2": nrm((L, DA_HEAD_DIM), 0.1),
        "da_head_norm": gain(DA_V_DIM),
        "w_da_proj": nrm((L, DA_WIDTH, D_MODEL), DA_WIDTH ** -0.5),
        "w_mix_out": nrm((L, D_MODEL, D_MODEL), D_MODEL ** -0.5),
        "norm_mix_post": gain(),
        "norm_x_pre": gain(),
        "norm_mem": gain(),
        "w_xq": nrm((L, D_MODEL, D_MODEL), D_MODEL ** -0.5),
        "w_xkv": nrm((L, D_MODEL, 2 * D_MODEL), D_MODEL ** -0.5),
        "w_xo": nrm((L, D_MODEL, D_MODEL), D_MODEL ** -0.5),
        "norm_x_post": gain(),
        "norm_ff_pre": gain(),
        "w_ff1": nrm((L, D_MODEL, D_FF), D_MODEL ** -0.5),
        "w_ff2": nrm((L, D_FF, D_MODEL), D_FF ** -0.5),
        "norm_ff_post": gain(),
    }


def reference(x, mem, norm_mix_pre, w_in, b_gate, ssm_lambda_re, ssm_lambda_im, ssm_log_dt,
              ssm_b_re, ssm_b_im, ssm_c_re, ssm_c_im, ssm_d, w_glu, b_glu, w_ssm_proj,
              da_lambda_q1, da_lambda_k1, da_lambda_q2, da_lambda_k2, da_head_norm, w_da_proj,
              w_mix_out, norm_mix_post, norm_x_pre, norm_mem, w_xq, w_xkv, w_xo, norm_x_post,
              norm_ff_pre, w_ff1, w_ff2, norm_ff_post):
    for l in range(DEPTH):
        lam_init = 0.8 - 0.6 * math.exp(-0.3 * l)
        h = rms_norm(x, norm_mix_pre[l])
        z = h @ w_in[l]
        u_s, q, k, v, g = jnp.split(z, [COL_SSM, COL_Q, COL_K, COL_V], axis=-1)
        ys = s5_branch(u_s, ssm_lambda_re[l], ssm_lambda_im[l], ssm_log_dt[l],
                       ssm_b_re[l], ssm_b_im[l], ssm_c_re[l], ssm_c_im[l], ssm_d[l])
        ys = jax.nn.gelu(ys)
        ys = ys * jax.nn.sigmoid(ys @ w_glu[l] + b_glu[l])
        branch_a = ys @ w_ssm_proj[l]
        ya = diff_attention(q, k, v, da_lambda_q1[l], da_lambda_k1[l], da_lambda_q2[l],
                            da_lambda_k2[l], da_head_norm[l], lam_init)
        branch_b = ya @ w_da_proj[l]
        gates = jax.nn.sigmoid((g + b_gate[l]).astype(F32))
        g_a, g_b = jnp.split(gates, 2, axis=-1)
        merged = (g_a * branch_a.astype(F32) + g_b * branch_b.astype(F32)).astype(x.dtype)
        x = x + rms_norm(merged @ w_mix_out[l], norm_mix_post[l])
        h = rms_norm(x, norm_x_pre[l])
        mem_n = rms_norm(mem, norm_mem[l])
        x = x + rms_norm(memory_cross_attention(h, mem_n, w_xq[l], w_xkv[l], w_xo[l]), norm_x_post[l])
        h = rms_norm(x, norm_ff_pre[l])
        f = jnp.square(jax.nn.relu(h @ w_ff1[l])) @ w_ff2[l]
        x = x + rms_norm(f, norm_ff_post[l])
    return x
```

```python
import functools
import math

import jax
import jax.numpy as jnp
from jax import lax
from jax.experimental import pallas as pl
from jax.experimental.pallas import tpu as pltpu

F32 = jnp.float32
BF16 = jnp.bfloat16

D_MODEL = 1024
EPS = 1e-6
CHUNK = 64
SSM_GROUP = 16
SSM_GROUPS = 64
SSM_STATE = 64
SSM_SLABS = 8
SLAB_GROUPS = SSM_GROUPS // SSM_SLABS
SLAB_STATES = SLAB_GROUPS * SSM_STATE
NLB = 2 * SLAB_STATES // 128
DA_HEADS = 8
DA_HEAD_DIM = 64
DA_V_DIM = 128
ROPE_THETA = 10000.0
XA_HEADS = 4
XA_HEAD_DIM = 256
D_FF = 4096
IN_COLS = 6144
LOG2E = 1.4426950408889634
NEG_BIG = -1e30
VMEM_LIMIT_BYTES = 56 * 1024 * 1024


def _params(*semantics):
    return pltpu.CompilerParams(dimension_semantics=semantics, vmem_limit_bytes=VMEM_LIMIT_BYTES)


def _rms(x, g):
    return x * lax.rsqrt(jnp.mean(x * x, axis=-1, keepdims=True) + EPS) * g


def _bdot(a, b):
    return jnp.dot(a, b, preferred_element_type=F32)


def _inproj_kernel(x_ref, g_ref, w_ref, bg_ref, cq_ref, sq_ref, ck_ref, sk_ref, o_ref, hn_ref):
    j = pl.program_id(1)
    tm = x_ref.shape[0]

    @pl.when(j == 0)
    def _():
        hn_ref[...] = _rms(x_ref[...], g_ref[...]).astype(BF16)

    acc = _bdot(hn_ref[...], w_ref[...])

    @pl.when((j == 0) | (j == 3))
    def _():
        o_ref[...] = acc.astype(BF16)

    def rope(c_ref, s_ref):
        c = c_ref[...]
        s = s_ref[...]
        lane = lax.broadcasted_iota(jnp.int32, (tm, 128), 1)
        first_half = (lane % 64) < 32
        for h in range(DA_HEADS):
            blk = acc[:, h * 128:(h + 1) * 128]
            partner = jnp.where(first_half, pltpu.roll(blk, 96, 1), pltpu.roll(blk, 32, 1))
            o_ref[:, h * 128:(h + 1) * 128] = (blk * c + partner * s).astype(BF16)

    @pl.when(j == 1)
    def _():
        rope(cq_ref, sq_ref)

    @pl.when(j == 2)
    def _():
        rope(ck_ref, sk_ref)

    @pl.when(j >= 4)
    def _():
        o_ref[...] = jax.nn.sigmoid(acc + bg_ref[...]).astype(BF16)


def _in_projection(x2, gain, w_in, b_gate, tabs, seqlen):
    t = x2.shape[0]
    tm = min(1024, seqlen)
    nl = seqlen // tm
    tab_spec = pl.BlockSpec((tm, 128), lambda i, j: (i % nl, 0))
    return pl.pallas_call(
        _inproj_kernel,
        grid=(t // tm, IN_COLS // D_MODEL),
        in_specs=[
            pl.BlockSpec((tm, D_MODEL), lambda i, j: (i, 0)),
            pl.BlockSpec((1, D_MODEL), lambda i, j: (0, 0)),
            pl.BlockSpec((D_MODEL, D_MODEL), lambda i, j: (0, j)),
            pl.BlockSpec((1, D_MODEL), lambda i, j: (0, jnp.maximum(j - 4, 0))),
            tab_spec, tab_spec, tab_spec, tab_spec,
        ],
        out_specs=pl.BlockSpec((tm, D_MODEL), lambda i, j: (i, j)),
        out_shape=jax.ShapeDtypeStruct((t, IN_COLS), BF16),
        scratch_shapes=[pltpu.VMEM((tm, D_MODEL), BF16)],
        compiler_params=_params("parallel", "arbitrary"),
        name="in_projection",
    )(x2, gain, w_in, b_gate, *tabs)


def _s5_kernel(u_ref, bm_ref, cm_ref, lre_ref, lim_ref, d_ref, wg_ref, bgl_ref, o_ref,
               bu_ref, st_ref, y_ref):
    nb, tc, _ = u_ref.shape
    c = pl.program_id(0)

    @pl.when(c == 0)
    def _():
        st_ref[...] = jnp.zeros(st_ref.shape, F32)

    for b in range(nb):
        for j in range(SSM_SLABS):
            bu = _bdot(u_ref[b, :, j * 128:(j + 1) * 128], bm_ref[j])
            for k in range(NLB):
                bu_ref[b, k, j * tc:(j + 1) * tc, :] = bu[:, k * 128:(k + 1) * 128]

    ar = [lre_ref[:, k * 128:(k + 1) * 128] for k in range(NLB // 2)]
    ai = [lim_ref[:, k * 128:(k + 1) * 128] for k in range(NLB // 2)]

    def step(t, carry):
        out = []
        for b in range(nb):
            xs = carry[b]
            new = [None] * NLB
            for k in range(NLB // 2):
                xr, xi = xs[k], xs[k + NLB // 2]
                rows = pl.ds(t, SSM_SLABS, stride=tc)
                new[k] = ar[k] * xr - ai[k] * xi + bu_ref[b, k, rows, :]
                new[k + NLB // 2] = ar[k] * xi + ai[k] * xr + bu_ref[b, k + NLB // 2, rows, :]
                bu_ref[b, k, rows, :] = new[k]
                bu_ref[b, k + NLB // 2, rows, :] = new[k + NLB // 2]
            out.append(tuple(new))
        return tuple(out)

    init = tuple(tuple(st_ref[b, k] for k in range(NLB)) for b in range(nb))
    fin = lax.fori_loop(0, tc, step, init, unroll=8)
    for b in range(nb):
        for k in range(NLB):
            st_ref[b, k] = fin[b][k]

    for b in range(nb):
        for j in range(SSM_SLABS):
            xs = jnp.concatenate([bu_ref[b, k, j * tc:(j + 1) * tc, :] for k in range(NLB)], axis=1)
            y_ref[b * tc:(b + 1) * tc, j * 128:(j + 1) * 128] = _bdot(xs.astype(BF16), cm_ref[j])
    u = u_ref[...].reshape(nb * tc, D_MODEL).astype(F32)
    y = y_ref[...] + d_ref[...] * u
    ys = 0.5 * y * (1.0 + jnp.tanh(math.sqrt(2.0 / math.pi) * (y + 0.044715 * (y * y * y))))
    gate = jax.nn.sigmoid(_bdot(ys.astype(BF16), wg_ref[...]) + bgl_ref[...])
    o_ref[...] = (ys * gate).astype(BF16).reshape(nb, tc, D_MODEL)


def _s5_branch(z3, bmat, cmat, lam_re, lam_im, d_skip, w_glu, b_glu):
    nb, seqlen, _ = z3.shape
    tc = min(256, seqlen)
    const2 = lambda c: (0, 0)
    const3 = lambda c: (0, 0, 0)
    return pl.pallas_call(
        _s5_kernel,
        grid=(seqlen // tc,),
        in_specs=[
            pl.BlockSpec((nb, tc, D_MODEL), lambda c: (0, c, 0)),
            pl.BlockSpec(bmat.shape, const3),
            pl.BlockSpec(cmat.shape, const3),
            pl.BlockSpec(lam_re.shape, const2),
            pl.BlockSpec(lam_im.shape, const2),
            pl.BlockSpec((1, D_MODEL), const2),
            pl.BlockSpec((D_MODEL, D_MODEL), const2),
            pl.BlockSpec((1, D_MODEL), const2),
        ],
        out_specs=pl.BlockSpec((nb, tc, D_MODEL), lambda c: (0, c, 0)),
        out_shape=jax.ShapeDtypeStruct((nb, seqlen, D_MODEL), BF16),
        scratch_shapes=[
            pltpu.VMEM((nb, NLB, SSM_SLABS * tc, 128), F32),
            pltpu.VMEM((nb, NLB, SSM_SLABS, 128), F32),
            pltpu.VMEM((nb * tc, D_MODEL), F32),
        ],
        compiler_params=_params("arbitrary"),
        name="s5_branch",
    )(z3, bmat, cmat, lam_re, lam_im, d_skip, w_glu, b_glu)


def _s5_matrices(lam_re, lam_im, log_dt, b_re, b_im, c_re, c_im):
    lam = lax.complex(lam_re, lam_im)
    dt = jnp.exp(log_dt)[:, None]
    lam_bar = jnp.exp(lam * dt)
    b_bar = ((lam_bar - 1.0) / lam)[..., None] * lax.complex(b_re, b_im)
    eye = jnp.eye(SLAB_GROUPS, dtype=F32)
    bb = b_bar.reshape(SSM_SLABS, SLAB_GROUPS, SSM_STATE, SSM_GROUP)

    def in_mat(part):
        return jnp.einsum('jgnh,gk->jghkn', part, eye).reshape(SSM_SLABS, 128, SLAB_STATES)

    bmat = jnp.concatenate([in_mat(bb.real), in_mat(bb.imag)], axis=2).astype(BF16)
    cr = c_re.reshape(SSM_SLABS, SLAB_GROUPS, SSM_GROUP, SSM_STATE)
    ci = c_im.reshape(SSM_SLABS, SLAB_GROUPS, SSM_GROUP, SSM_STATE)

    def out_mat(part):
        return jnp.einsum('jghn,gk->jkngh', part, eye).reshape(SSM_SLABS, SLAB_STATES, 128)

    cmat = jnp.concatenate([out_mat(cr), out_mat(-ci)], axis=1).astype(BF16)
    lre = lam_bar.real.reshape(SSM_SLABS, SLAB_STATES)
    lim = lam_bar.imag.reshape(SSM_SLABS, SLAB_STATES)
    return bmat, cmat, lre, lim


def _dattn_kernel(q_ref, k_ref, v_ref, lmb_ref, hn_ref, o_ref, acc1_ref, acc2_ref, *, lam_init):
    tq = q_ref.shape[0]
    tk = tq
    qi = pl.program_id(2)
    q = q_ref[...]
    lane = lax.broadcasted_iota(jnp.int32, (tq, 128), 1)
    zero = jnp.zeros_like(q)
    q1 = jnp.where(lane < DA_HEAD_DIM, q, zero)
    q2 = jnp.where(lane >= DA_HEAD_DIM, q, zero)
    nt = (((1,), (1,)), ((), ()))
    acc1_ref[...] = jnp.zeros(acc1_ref.shape, F32)
    acc2_ref[...] = jnp.zeros(acc2_ref.shape, F32)

    def update(s, vb, m, l, acc_ref):
        m_new = jnp.maximum(m, jnp.max(s, axis=1, keepdims=True))
        alpha = jnp.exp2(m - m_new)
        p = jnp.exp2(s - m_new)
        l_new = alpha * l + jnp.sum(p, axis=1, keepdims=True)
        acc_ref[...] = alpha * acc_ref[...] + _bdot(p.astype(BF16), vb)
        return m_new, l_new

    def block(ki, carry, mask):
        m1, l1, m2, l2 = carry
        start = pl.multiple_of(ki * tk, tk)
        kb = k_ref[pl.ds(start, tk), :]
        vb = v_ref[pl.ds(start, tk), :]
        s1 = lax.dot_general(q1, kb, nt, preferred_element_type=F32)
        s2 = lax.dot_general(q2, kb, nt, preferred_element_type=F32)
        if mask is not None:
            s1 = jnp.where(mask, s1, NEG_BIG)
            s2 = jnp.where(mask, s2, NEG_BIG)
        m1, l1 = update(s1, vb, m1, l1, acc1_ref)
        m2, l2 = update(s2, vb, m2, l2, acc2_ref)
        return m1, l1, m2, l2

    neg = jnp.full((tq, 1), NEG_BIG, F32)
    zl = jnp.zeros((tq, 1), F32)
    carry = lax.fori_loop(0, qi, lambda ki, c: block(ki, c, None), (neg, zl, neg, zl))
    row = lax.broadcasted_iota(jnp.int32, (tq, tk), 0) // CHUNK
    col = lax.broadcasted_iota(jnp.int32, (tq, tk), 1) // CHUNK
    m1, l1, m2, l2 = block(qi, carry, col <= row)

    lv = lmb_ref[...]
    lam = (jnp.exp(jnp.sum(lv[0:1] * lv[1:2], axis=1, keepdims=True))
           - jnp.exp(jnp.sum(lv[2:3] * lv[3:4], axis=1, keepdims=True)) + lam_init)
    o = acc1_ref[...] / l1 - lam * (acc2_ref[...] / l2)
    o = o * lax.rsqrt(jnp.mean(o * o, axis=-1, keepdims=True) + EPS) * hn_ref[...]
    o_ref[...] = (o * (1.0 - lam_init)).astype(BF16)


def _diff_attention(z3, lmb, head_norm, lam_init):
    nb, seqlen, _ = z3.shape
    tq = min(256, seqlen)
    qcol, kcol, vcol = 8, 16, 24
    return pl.pallas_call(
        functools.partial(_dattn_kernel, lam_init=lam_init),
        grid=(nb, DA_HEADS, seqlen // tq),
        in_specs=[
            pl.BlockSpec((None, tq, 128), lambda b, h, i: (b, i, qcol + h)),
            pl.BlockSpec((None, seqlen, 128), lambda b, h, i: (b, 0, kcol + h)),
            pl.BlockSpec((None, seqlen, 128), lambda b, h, i: (b, 0, vcol + h)),
            pl.BlockSpec((4, DA_HEAD_DIM), lambda b, h, i: (0, 0)),
            pl.BlockSpec((1, DA_V_DIM), lambda b, h, i: (0, 0)),
        ],
        out_specs=pl.BlockSpec((None, tq, 128), lambda b, h, i: (b, i, h)),
        out_shape=jax.ShapeDtypeStruct((nb, seqlen, D_MODEL), BF16),
        scratch_shapes=[pltpu.VMEM((tq, DA_V_DIM), F32), pltpu.VMEM((tq, DA_V_DIM), F32)],
        compiler_params=_params("parallel", "parallel", "arbitrary"),
        name="diff_attention",
    )(z3, z3, z3, lmb, head_norm)


def _merge_kernel(ys_ref, ya_ref, g_ref, x_ref, wa_ref, wb_ref, wo_ref, n_ref, o_ref):
    a = _bdot(ys_ref[...], wa_ref[...])
    b = _bdot(ya_ref[...], wb_ref[...])
    ga = g_ref[:, :D_MODEL].astype(F32)
    gb = g_ref[:, D_MODEL:].astype(F32)
    merged = (ga * a + gb * b).astype(BF16)
    o_ref[...] = x_ref[...] + _rms(_bdot(merged, wo_ref[...]), n_ref[...])


def _merge(ys2, ya2, z2, x2, w_a, w_b, w_o, gain):
    t = x2.shape[0]
    tm = min(512, t)
    row = lambda i: (i, 0)
    const = lambda i: (0, 0)
    wspec = pl.BlockSpec((D_MODEL, D_MODEL), const)
    return pl.pallas_call(
        _merge_kernel,
        grid=(t // tm,),
        in_specs=[
            pl.BlockSpec((tm, D_MODEL), row),
            pl.BlockSpec((tm, D_MODEL), row),
            pl.BlockSpec((tm, 2 * D_MODEL), lambda i: (i, 2)),
            pl.BlockSpec((tm, D_MODEL), row),
            wspec, wspec, wspec,
            pl.BlockSpec((1, D_MODEL), const),
        ],
        out_specs=pl.BlockSpec((tm, D_MODEL), row),
        out_shape=jax.ShapeDtypeStruct((t, D_MODEL), F32),
        compiler_params=_params("parallel"),
        name="merge",
    )(ys2, ya2, z2, x2, w_a, w_b, w_o, gain)


def _memkv_kernel(m_ref, g_ref, w_ref, o_ref):
    o_ref[...] = _bdot(_rms(m_ref[...], g_ref[...]).astype(BF16), w_ref[...]).astype(BF16)


def _memory_kv(mem2, gain, w_xkv):
    rows = mem2.shape[0]
    const = lambda i: (0, 0)
    return pl.pallas_call(
        _memkv_kernel,
        grid=(1,),
        in_specs=[
            pl.BlockSpec((rows, D_MODEL), const),
            pl.BlockSpec((1, D_MODEL), const),
            pl.BlockSpec((D_MODEL, 2 * D_MODEL), const),
        ],
        out_specs=pl.BlockSpec((rows, 2 * D_MODEL), const),
        out_shape=jax.ShapeDtypeStruct((rows, 2 * D_MODEL), BF16),
        compiler_params=_params("arbitrary"),
        name="memory_kv",
    )(mem2, gain, w_xkv)


def _xattn_kernel(x_ref, kv_ref, npre_ref, wq_ref, wo_ref, npost_ref, o_ref, oh_ref):
    x = x_ref[...]
    h = _rms(x, npre_ref[...]).astype(BF16)
    q = (_bdot(h, wq_ref[...]) * (XA_HEAD_DIM ** -0.5 * LOG2E)).astype(BF16)
    nt = (((1,), (1,)), ((), ()))
    for hd in range(XA_HEADS):
        lo = hd * XA_HEAD_DIM
        k = kv_ref[:, lo:lo + XA_HEAD_DIM]
        v = kv_ref[:, D_MODEL + lo:D_MODEL + lo + XA_HEAD_DIM]
        s = lax.dot_general(q[:, lo:lo + XA_HEAD_DIM], k, nt, preferred_element_type=F32)
        p = jnp.exp2(s - jnp.max(s, axis=1, keepdims=True))
        o = _bdot(p.astype(BF16), v) / jnp.sum(p, axis=1, keepdims=True)
        oh_ref[:, lo:lo + XA_HEAD_DIM] = o.astype(BF16)
    o_ref[...] = x + _rms(_bdot(oh_ref[...], wo_ref[...]), npost_ref[...])


def _cross_attention(x3, kv3, n_pre, w_xq, w_xo, n_post):
    nb, seqlen, _ = x3.shape
    mlen = kv3.shape[1]
    tm = min(512, seqlen)
    const = lambda b, i: (0, 0)
    wspec = pl.BlockSpec((D_MODEL, D_MODEL), const)
    vspec = pl.BlockSpec((1, D_MODEL), const)
    return pl.pallas_call(
        _xattn_kernel,
        grid=(nb, seqlen // tm),
        in_specs=[
            pl.BlockSpec((None, tm, D_MODEL), lambda b, i: (b, i, 0)),
            pl.BlockSpec((None, mlen, 2 * D_MODEL), lambda b, i: (b, 0, 0)),
            vspec, wspec, wspec, vspec,
        ],
        out_specs=pl.BlockSpec((None, tm, D_MODEL), lambda b, i: (b, i, 0)),
        out_shape=jax.ShapeDtypeStruct((nb, seqlen, D_MODEL), F32),
        scratch_shapes=[pltpu.VMEM((tm, D_MODEL), BF16)],
        compiler_params=_params("parallel", "parallel"),
        name="cross_attention",
    )(x3, kv3, n_pre, w_xq, w_xo, n_post)


def _mlp_kernel(x_ref, npre_ref, w1_ref, w2_ref, npost_ref, o_ref):
    x = x_ref[...]
    h = _rms(x, npre_ref[...]).astype(BF16)
    f = jnp.zeros(x.shape, F32)
    for c in range(D_FF // D_MODEL):
        lo = c * D_MODEL
        a = jnp.maximum(_bdot(h, w1_ref[:, lo:lo + D_MODEL]), 0.0)
        f = f + _bdot((a * a).astype(BF16), w2_ref[lo:lo + D_MODEL, :])
    o_ref[...] = x + _rms(f, npost_ref[...])


def _mlp(x2, n_pre, w1, w2, n_post):
    t = x2.shape[0]
    tm = min(512, t)
    row = lambda i: (i, 0)
    const = lambda i: (0, 0)
    vspec = pl.BlockSpec((1, D_MODEL), const)
    return pl.pallas_call(
        _mlp_kernel,
        grid=(t // tm,),
        in_specs=[
            pl.BlockSpec((tm, D_MODEL), row),
            vspec,
            pl.BlockSpec((D_MODEL, D_FF), const),
            pl.BlockSpec((D_FF, D_MODEL), const),
            vspec,
        ],
        out_specs=pl.BlockSpec((tm, D_MODEL), row),
        out_shape=jax.ShapeDtypeStruct((t, D_MODEL), F32),
        compiler_params=_params("parallel"),
        name="mlp",
    )(x2, n_pre, w1, w2, n_post)


def _rope_tables(seqlen):
    inv = ROPE_THETA ** (-jnp.arange(0, DA_HEAD_DIM, 2, dtype=F32) / DA_HEAD_DIM)
    ang = jnp.arange(seqlen, dtype=F32)[:, None] * inv[None, :]
    cos = jnp.tile(jnp.cos(ang), (1, 4))
    sin = jnp.tile(jnp.concatenate([-jnp.sin(ang), jnp.sin(ang)], axis=1), (1, 2))
    qs = DA_HEAD_DIM ** -0.5 * LOG2E
    return cos * qs, sin * qs, cos, sin


def kernel(x, mem, norm_mix_pre, w_in, b_gate, ssm_lambda_re, ssm_lambda_im, ssm_log_dt, ssm_b_re, ssm_b_im, ssm_c_re, ssm_c_im, ssm_d, w_glu, b_glu, w_ssm_proj, da_lambda_q1, da_lambda_k1, da_lambda_q2, da_lambda_k2, da_head_norm, w_da_proj, w_mix_out, norm_mix_post, norm_x_pre, norm_mem, w_xq, w_xkv, w_xo, norm_x_post, norm_ff_pre, w_ff1, w_ff2, norm_ff_post):
    nb, seqlen, _ = x.shape
    t = nb * seqlen
    depth = w_in.shape[0]
    tabs = _rope_tables(seqlen)
    x2 = x.reshape(t, D_MODEL)
    mem2 = mem.reshape(nb * mem.shape[1], D_MODEL)
    for l in range(depth):
        lam_init = 0.8 - 0.6 * math.exp(-0.3 * l)
        z2 = _in_projection(x2, norm_mix_pre[l][None], w_in[l].astype(BF16), b_gate[l][None], tabs, seqlen)
        z3 = z2.reshape(nb, seqlen, IN_COLS)
        bmat, cmat, lre, lim = _s5_matrices(ssm_lambda_re[l], ssm_lambda_im[l], ssm_log_dt[l],
                                            ssm_b_re[l], ssm_b_im[l], ssm_c_re[l], ssm_c_im[l])
        ys = _s5_branch(z3, bmat, cmat, lre, lim, ssm_d[l][None], w_glu[l].astype(BF16), b_glu[l][None])
        lmb = jnp.stack([da_lambda_q1[l], da_lambda_k1[l], da_lambda_q2[l], da_lambda_k2[l]])
        ya = _diff_attention(z3, lmb, da_head_norm[l][None], lam_init)
        x2 = _merge(ys.reshape(t, D_MODEL), ya.reshape(t, D_MODEL), z2, x2,
                    w_ssm_proj[l].astype(BF16), w_da_proj[l].astype(BF16), w_mix_out[l].astype(BF16),
                    norm_mix_post[l][None])
        kv = _memory_kv(mem2, norm_mem[l][None], w_xkv[l].astype(BF16))
        x3 = _cross_attention(x2.reshape(nb, seqlen, D_MODEL), kv.reshape(nb, -1, 2 * D_MODEL),
                              norm_x_pre[l][None], w_xq[l].astype(BF16), w_xo[l].astype(BF16),
                              norm_x_post[l][None])
        x2 = _mlp(x3.reshape(t, D_MODEL), norm_ff_pre[l][None], w_ff1[l].astype(BF16),
                  w_ff2[l].astype(BF16), norm_ff_post[l][None])
    return x2.reshape(nb, seqlen, D_MODEL)
```

```python
import functools
import math

import jax
import jax.numpy as jnp
from jax import lax
from jax.experimental import pallas as pl
from jax.experimental.pallas import tpu as pltpu

F32 = jnp.float32
BF16 = jnp.bfloat16

D_MODEL = 1024
EPS = 1e-6
CHUNK = 64
SSM_GROUP = 16
SSM_GROUPS = 64
SSM_STATE = 64
SSM_SLABS = 8
SLAB_GROUPS = SSM_GROUPS // SSM_SLABS
SLAB_STATES = SLAB_GROUPS * SSM_STATE
NLB = 2 * SLAB_STATES // 128
SLAB_ROW_PAD = 8
DA_HEADS = 8
DA_HEAD_DIM = 64
DA_V_DIM = 128
ROPE_THETA = 10000.0
XA_HEADS = 4
XA_HEAD_DIM = 256
D_FF = 4096
IN_COLS = 6144
LOG2E = 1.4426950408889634
NEG_BIG = -1e30
VMEM_LIMIT_BYTES = 56 * 1024 * 1024


def _params(*semantics):
    return pltpu.CompilerParams(dimension_semantics=semantics, vmem_limit_bytes=VMEM_LIMIT_BYTES)


def _rms(x, g):
    return x * lax.rsqrt(jnp.mean(x * x, axis=-1, keepdims=True) + EPS) * g


def _bdot(a, b):
    return jnp.dot(a, b, preferred_element_type=F32)


def _inproj_kernel(x_ref, g_ref, w_ref, bg_ref, cq_ref, sq_ref, ck_ref, sk_ref, o_ref, hn_ref):
    j = pl.program_id(1)
    tm = x_ref.shape[0]

    @pl.when(j == 0)
    def _():
        hn_ref[...] = _rms(x_ref[...], g_ref[...]).astype(BF16)

    acc = _bdot(hn_ref[...], w_ref[...])

    @pl.when((j == 0) | (j == 3))
    def _():
        o_ref[...] = acc.astype(BF16)

    def rope(c_ref, s_ref):
        c = c_ref[...]
        s = s_ref[...]
        lane = lax.broadcasted_iota(jnp.int32, (tm, 128), 1)
        first_half = (lane % 64) < 32
        for h in range(DA_HEADS):
            blk = acc[:, h * 128:(h + 1) * 128]
            partner = jnp.where(first_half, pltpu.roll(blk, 96, 1), pltpu.roll(blk, 32, 1))
            o_ref[:, h * 128:(h + 1) * 128] = (blk * c + partner * s).astype(BF16)

    @pl.when(j == 1)
    def _():
        rope(cq_ref, sq_ref)

    @pl.when(j == 2)
    def _():
        rope(ck_ref, sk_ref)

    @pl.when(j >= 4)
    def _():
        o_ref[...] = jax.nn.sigmoid(acc + bg_ref[...]).astype(BF16)


def _in_projection(x2, gain, w_in, b_gate, tabs, seqlen):
    t = x2.shape[0]
    tm = min(1024, seqlen)
    nl = seqlen // tm
    tab_spec = pl.BlockSpec((tm, 128), lambda i, j: (i % nl, 0))
    return pl.pallas_call(
        _inproj_kernel,
        grid=(t // tm, IN_COLS // D_MODEL),
        in_specs=[
            pl.BlockSpec((tm, D_MODEL), lambda i, j: (i, 0)),
            pl.BlockSpec((1, D_MODEL), lambda i, j: (0, 0)),
            pl.BlockSpec((D_MODEL, D_MODEL), lambda i, j: (0, j)),
            pl.BlockSpec((1, D_MODEL), lambda i, j: (0, jnp.maximum(j - 4, 0))),
            tab_spec, tab_spec, tab_spec, tab_spec,
        ],
        out_specs=pl.BlockSpec((tm, D_MODEL), lambda i, j: (i, j)),
        out_shape=jax.ShapeDtypeStruct((t, IN_COLS), BF16),
        scratch_shapes=[pltpu.VMEM((tm, D_MODEL), BF16)],
        compiler_params=_params("parallel", "arbitrary"),
        name="in_projection",
    )(x2, gain, w_in, b_gate, *tabs)


def _s5_kernel(u_ref, bm_ref, cm_ref, lre_ref, lim_ref, d_ref, wg_ref, bgl_ref, o_ref,
               bu_ref, st_ref, y_ref):
    nb, tc, _ = u_ref.shape
    pitch = bu_ref.shape[2] // SSM_SLABS
    c = pl.program_id(0)

    @pl.when(c == 0)
    def _():
        st_ref[...] = jnp.zeros(st_ref.shape, F32)

    for b in range(nb):
        for j in range(SSM_SLABS):
            bu = _bdot(u_ref[b, :, j * 128:(j + 1) * 128], bm_ref[j])
            for k in range(NLB):
                bu_ref[b, k, j * pitch:j * pitch + tc, :] = bu[:, k * 128:(k + 1) * 128]

    ar = [lre_ref[:, k * 128:(k + 1) * 128] for k in range(NLB // 2)]
    ai = [lim_ref[:, k * 128:(k + 1) * 128] for k in range(NLB // 2)]

    def step(t, carry):
        out = []
        for b in range(nb):
            xs = carry[b]
            new = [None] * NLB
            for k in range(NLB // 2):
                xr, xi = xs[k], xs[k + NLB // 2]
                rows = pl.ds(t, SSM_SLABS, stride=pitch)
                new[k] = ar[k] * xr - ai[k] * xi + bu_ref[b, k, rows, :]
                new[k + NLB // 2] = ar[k] * xi + ai[k] * xr + bu_ref[b, k + NLB // 2, rows, :]
                bu_ref[b, k, rows, :] = new[k]
                bu_ref[b, k + NLB // 2, rows, :] = new[k + NLB // 2]
            out.append(tuple(new))
        return tuple(out)

    init = tuple(tuple(st_ref[b, k] for k in range(NLB)) for b in range(nb))
    fin = lax.fori_loop(0, tc, step, init, unroll=8)
    for b in range(nb):
        for k in range(NLB):
            st_ref[b, k] = fin[b][k]

    for b in range(nb):
        for j in range(SSM_SLABS):
            xs = jnp.concatenate([bu_ref[b, k, j * pitch:j * pitch + tc, :] for k in range(NLB)], axis=1)
            y_ref[b * tc:(b + 1) * tc, j * 128:(j + 1) * 128] = _bdot(xs.astype(BF16), cm_ref[j])
    u = u_ref[...].reshape(nb * tc, D_MODEL).astype(F32)
    y = y_ref[...] + d_ref[...] * u
    ys = 0.5 * y * (1.0 + jnp.tanh(math.sqrt(2.0 / math.pi) * (y + 0.044715 * (y * y * y))))
    gate = jax.nn.sigmoid(_bdot(ys.astype(BF16), wg_ref[...]) + bgl_ref[...])
    o_ref[...] = (ys * gate).astype(BF16).reshape(nb, tc, D_MODEL)


def _s5_branch(z3, bmat, cmat, lam_re, lam_im, d_skip, w_glu, b_glu):
    nb, seqlen, _ = z3.shape
    tc = min(256, seqlen)
    const2 = lambda c: (0, 0)
    const3 = lambda c: (0, 0, 0)
    return pl.pallas_call(
        _s5_kernel,
        grid=(seqlen // tc,),
        in_specs=[
            pl.BlockSpec((nb, tc, D_MODEL), lambda c: (0, c, 0)),
            pl.BlockSpec(bmat.shape, const3),
            pl.BlockSpec(cmat.shape, const3),
            pl.BlockSpec(lam_re.shape, const2),
            pl.BlockSpec(lam_im.shape, const2),
            pl.BlockSpec((1, D_MODEL), const2),
            pl.BlockSpec((D_MODEL, D_MODEL), const2),
            pl.BlockSpec((1, D_MODEL), const2),
        ],
        out_specs=pl.BlockSpec((nb, tc, D_MODEL), lambda c: (0, c, 0)),
        out_shape=jax.ShapeDtypeStruct((nb, seqlen, D_MODEL), BF16),
        scratch_shapes=[
            pltpu.VMEM((nb, NLB, SSM_SLABS * (tc + SLAB_ROW_PAD), 128), F32),
            pltpu.VMEM((nb, NLB, SSM_SLABS, 128), F32),
            pltpu.VMEM((nb * tc, D_MODEL), F32),
        ],
        compiler_params=_params("arbitrary"),
        name="s5_branch",
    )(z3, bmat, cmat, lam_re, lam_im, d_skip, w_glu, b_glu)


def _s5_matrices(lam_re, lam_im, log_dt, b_re, b_im, c_re, c_im):
    lam = lax.complex(lam_re, lam_im)
    dt = jnp.exp(log_dt)[:, None]
    lam_bar = jnp.exp(lam * dt)
    b_bar = ((lam_bar - 1.0) / lam)[..., None] * lax.complex(b_re, b_im)
    eye = jnp.eye(SLAB_GROUPS, dtype=F32)
    bb = b_bar.reshape(SSM_SLABS, SLAB_GROUPS, SSM_STATE, SSM_GROUP)

    def in_mat(part):
        return jnp.einsum('jgnh,gk->jghkn', part, eye).reshape(SSM_SLABS, 128, SLAB_STATES)

    bmat = jnp.concatenate([in_mat(bb.real), in_mat(bb.imag)], axis=2).astype(BF16)
    cr = c_re.reshape(SSM_SLABS, SLAB_GROUPS, SSM_GROUP, SSM_STATE)
    ci = c_im.reshape(SSM_SLABS, SLAB_GROUPS, SSM_GROUP, SSM_STATE)

    def out_mat(part):
        return jnp.einsum('jghn,gk->jkngh', part, eye).reshape(SSM_SLABS, SLAB_STATES, 128)

    cmat = jnp.concatenate([out_mat(cr), out_mat(-ci)], axis=1).astype(BF16)
    lre = lam_bar.real.reshape(SSM_SLABS, SLAB_STATES)
    lim = lam_bar.imag.reshape(SSM_SLABS, SLAB_STATES)
    return bmat, cmat, lre, lim


def _dattn_kernel(q_ref, k_ref, v_ref, lmb_ref, hn_ref, o_ref, m_ref, l_ref, acc_ref, *, lam_init):
    tq = q_ref.shape[0]
    tk = tq
    nstrips = tk // 128
    qi = pl.program_id(2)
    q = q_ref[...]
    lane = lax.broadcasted_iota(jnp.int32, (tq, 128), 1)
    zero = jnp.zeros_like(q)
    q_maps = (jnp.where(lane < DA_HEAD_DIM, q, zero), jnp.where(lane >= DA_HEAD_DIM, q, zero))
    nt = (((1,), (1,)), ((), ()))
    m_ref[...] = jnp.full(m_ref.shape, NEG_BIG, F32)
    l_ref[...] = jnp.zeros(l_ref.shape, F32)
    acc_ref[...] = jnp.zeros(acc_ref.shape, F32)

    def block(ki, mask):
        start = pl.multiple_of(ki * tk, tk)
        kb = k_ref[pl.ds(start, tk), :]
        vb = v_ref[pl.ds(start, tk), :]
        for mi in range(2):
            s = lax.dot_general(q_maps[mi], kb, nt, preferred_element_type=F32)
            if mask is not None:
                s = jnp.where(mask, s, NEG_BIG)
            m_prev = m_ref[mi]
            m_next = jnp.maximum(m_prev, jnp.max(s, axis=1, keepdims=True))
            alpha = jnp.exp2(m_prev - m_next)
            ps = [jnp.exp2(s[:, c * 128:(c + 1) * 128] - m_next) for c in range(nstrips)]
            l_ref[mi] = alpha * l_ref[mi] + functools.reduce(lambda a, b: a + b, ps)
            p = jnp.concatenate(ps, axis=1).astype(BF16)
            acc_ref[mi] = alpha * acc_ref[mi] + _bdot(p, vb)
            m_ref[mi] = m_next

    def body(ki, carry):
        block(ki, None)
        return carry

    lax.fori_loop(0, qi, body, 0)
    row = lax.broadcasted_iota(jnp.int32, (tq, tk), 0) // CHUNK
    col = lax.broadcasted_iota(jnp.int32, (tq, tk), 1) // CHUNK
    block(qi, col <= row)

    lv = lmb_ref[...]
    lam = (jnp.exp(jnp.sum(lv[0:1] * lv[1:2], axis=1, keepdims=True))
           - jnp.exp(jnp.sum(lv[2:3] * lv[3:4], axis=1, keepdims=True)) + lam_init)
    l1 = jnp.sum(l_ref[0], axis=1, keepdims=True)
    l2 = jnp.sum(l_ref[1], axis=1, keepdims=True)
    o = acc_ref[0] / l1 - lam * (acc_ref[1] / l2)
    o = o * lax.rsqrt(jnp.mean(o * o, axis=-1, keepdims=True) + EPS) * hn_ref[...]
    o_ref[...] = (o * (1.0 - lam_init)).astype(BF16)


def _diff_attention(z3, lmb, head_norm, lam_init):
    nb, seqlen, _ = z3.shape
    tq = min(512, seqlen)
    qcol, kcol, vcol = 8, 16, 24
    return pl.pallas_call(
        functools.partial(_dattn_kernel, lam_init=lam_init),
        grid=(nb, DA_HEADS, seqlen // tq),
        in_specs=[
            pl.BlockSpec((None, tq, 128), lambda b, h, i: (b, i, qcol + h)),
            pl.BlockSpec((None, seqlen, 128), lambda b, h, i: (b, 0, kcol + h)),
            pl.BlockSpec((None, seqlen, 128), lambda b, h, i: (b, 0, vcol + h)),
            pl.BlockSpec((4, DA_HEAD_DIM), lambda b, h, i: (0, 0)),
            pl.BlockSpec((1, DA_V_DIM), lambda b, h, i: (0, 0)),
        ],
        out_specs=pl.BlockSpec((None, tq, 128), lambda b, h, i: (b, i, h)),
        out_shape=jax.ShapeDtypeStruct((nb, seqlen, D_MODEL), BF16),
        scratch_shapes=[pltpu.VMEM((2, tq, 128), F32), pltpu.VMEM((2, tq, 128), F32),
                        pltpu.VMEM((2, tq, DA_V_DIM), F32)],
        compiler_params=_params("parallel", "parallel", "arbitrary"),
        name="diff_attention",
    )(z3, z3, z3, lmb, head_norm)


def _merge_kernel(ys_ref, ya_ref, g_ref, x_ref, wa_ref, wb_ref, wo_ref, n_ref, o_ref):
    a = _bdot(ys_ref[...], wa_ref[...])
    b = _bdot(ya_ref[...], wb_ref[...])
    ga = g_ref[:, :D_MODEL].astype(F32)
    gb = g_ref[:, D_MODEL:].astype(F32)
    merged = (ga * a + gb * b).astype(BF16)
    o_ref[...] = x_ref[...] + _rms(_bdot(merged, wo_ref[...]), n_ref[...])


def _merge(ys2, ya2, z2, x2, w_a, w_b, w_o, gain):
    t = x2.shape[0]
    tm = min(512, t)
    row = lambda i: (i, 0)
    const = lambda i: (0, 0)
    wspec = pl.BlockSpec((D_MODEL, D_MODEL), const)
    return pl.pallas_call(
        _merge_kernel,
        grid=(t // tm,),
        in_specs=[
            pl.BlockSpec((tm, D_MODEL), row),
            pl.BlockSpec((tm, D_MODEL), row),
            pl.BlockSpec((tm, 2 * D_MODEL), lambda i: (i, 2)),
            pl.BlockSpec((tm, D_MODEL), row),
            wspec, wspec, wspec,
            pl.BlockSpec((1, D_MODEL), const),
        ],
        out_specs=pl.BlockSpec((tm, D_MODEL), row),
        out_shape=jax.ShapeDtypeStruct((t, D_MODEL), F32),
        compiler_params=_params("parallel"),
        name="merge",
    )(ys2, ya2, z2, x2, w_a, w_b, w_o, gain)


def _memkv_kernel(m_ref, g_ref, w_ref, o_ref):
    o_ref[...] = _bdot(_rms(m_ref[...], g_ref[...]).astype(BF16), w_ref[...]).astype(BF16)


def _memory_kv(mem2, gain, w_xkv):
    rows = mem2.shape[0]
    const = lambda i: (0, 0)
    return pl.pallas_call(
        _memkv_kernel,
        grid=(1,),
        in_specs=[
            pl.BlockSpec((rows, D_MODEL), const),
            pl.BlockSpec((1, D_MODEL), const),
            pl.BlockSpec((D_MODEL, 2 * D_MODEL), const),
        ],
        out_specs=pl.BlockSpec((rows, 2 * D_MODEL), const),
        out_shape=jax.ShapeDtypeStruct((rows, 2 * D_MODEL), BF16),
        compiler_params=_params("arbitrary"),
        name="memory_kv",
    )(mem2, gain, w_xkv)


def _xattn_kernel(x_ref, kv_ref, npre_ref, wq_ref, wo_ref, npost_ref, o_ref, oh_ref):
    x = x_ref[...]
    h = _rms(x, npre_ref[...]).astype(BF16)
    q = (_bdot(h, wq_ref[...]) * (XA_HEAD_DIM ** -0.5 * LOG2E)).astype(BF16)
    nt = (((1,), (1,)), ((), ()))
    for hd in range(XA_HEADS):
        lo = hd * XA_HEAD_DIM
        k = kv_ref[:, lo:lo + XA_HEAD_DIM]
        v = kv_ref[:, D_MODEL + lo:D_MODEL + lo + XA_HEAD_DIM]
        s = lax.dot_general(q[:, lo:lo + XA_HEAD_DIM], k, nt, preferred_element_type=F32)
        p = jnp.exp2(s - jnp.max(s, axis=1, keepdims=True))
        o = _bdot(p.astype(BF16), v) / jnp.sum(p, axis=1, keepdims=True)
        oh_ref[:, lo:lo + XA_HEAD_DIM] = o.astype(BF16)
    o_ref[...] = x + _rms(_bdot(oh_ref[...], wo_ref[...]), npost_ref[...])


def _cross_attention(x3, kv3, n_pre, w_xq, w_xo, n_post):
    nb, seqlen, _ = x3.shape
    mlen = kv3.shape[1]
    tm = min(512, seqlen)
    const = lambda b, i: (0, 0)
    wspec = pl.BlockSpec((D_MODEL, D_MODEL), const)
    vspec = pl.BlockSpec((1, D_MODEL), const)
    return pl.pallas_call(
        _xattn_kernel,
        grid=(nb, seqlen // tm),
        in_specs=[
            pl.BlockSpec((None, tm, D_MODEL), lambda b, i: (b, i, 0)),
            pl.BlockSpec((None, mlen, 2 * D_MODEL), lambda b, i: (b, 0, 0)),
            vspec, wspec, wspec, vspec,
        ],
        out_specs=pl.BlockSpec((None, tm, D_MODEL), lambda b, i: (b, i, 0)),
        out_shape=jax.ShapeDtypeStruct((nb, seqlen, D_MODEL), F32),
        scratch_shapes=[pltpu.VMEM((tm, D_MODEL), BF16)],
        compiler_params=_params("parallel", "parallel"),
        name="cross_attention",
    )(x3, kv3, n_pre, w_xq, w_xo, n_post)


def _mlp_kernel(x_ref, npre_ref, w1_ref, w2_ref, npost_ref, o_ref):
    x = x_ref[...]
    h = _rms(x, npre_ref[...]).astype(BF16)
    f = jnp.zeros(x.shape, F32)
    for c in range(D_FF // D_MODEL):
        lo = c * D_MODEL
        a = jnp.maximum(_bdot(h, w1_ref[:, lo:lo + D_MODEL]), 0.0)
        f = f + _bdot((a * a).astype(BF16), w2_ref[lo:lo + D_MODEL, :])
    o_ref[...] = x + _rms(f, npost_ref[...])


def _mlp(x2, n_pre, w1, w2, n_post):
    t = x2.shape[0]
    tm = min(512, t)
    row = lambda i: (i, 0)
    const = lambda i: (0, 0)
    vspec = pl.BlockSpec((1, D_MODEL), const)
    return pl.pallas_call(
        _mlp_kernel,
        grid=(t // tm,),
        in_specs=[
            pl.BlockSpec((tm, D_MODEL), row),
            vspec,
            pl.BlockSpec((D_MODEL, D_FF), const),
            pl.BlockSpec((D_FF, D_MODEL), const),
            vspec,
        ],
        out_specs=pl.BlockSpec((tm, D_MODEL), row),
        out_shape=jax.ShapeDtypeStruct((t, D_MODEL), F32),
        compiler_params=_params("parallel"),
        name="mlp",
    )(x2, n_pre, w1, w2, n_post)


def _rope_tables(seqlen):
    inv = ROPE_THETA ** (-jnp.arange(0, DA_HEAD_DIM, 2, dtype=F32) / DA_HEAD_DIM)
    ang = jnp.arange(seqlen, dtype=F32)[:, None] * inv[None, :]
    cos = jnp.tile(jnp.cos(ang), (1, 4))
    sin = jnp.tile(jnp.concatenate([-jnp.sin(ang), jnp.sin(ang)], axis=1), (1, 2))
    qs = DA_HEAD_DIM ** -0.5 * LOG2E
    return cos * qs, sin * qs, cos, sin


def kernel(x, mem, norm_mix_pre, w_in, b_gate, ssm_lambda_re, ssm_lambda_im, ssm_log_dt, ssm_b_re, ssm_b_im, ssm_c_re, ssm_c_im, ssm_d, w_glu, b_glu, w_ssm_proj, da_lambda_q1, da_lambda_k1, da_lambda_q2, da_lambda_k2, da_head_norm, w_da_proj, w_mix_out, norm_mix_post, norm_x_pre, norm_mem, w_xq, w_xkv, w_xo, norm_x_post, norm_ff_pre, w_ff1, w_ff2, norm_ff_post):
    nb, seqlen, _ = x.shape
    t = nb * seqlen
    depth = w_in.shape[0]
    tabs = _rope_tables(seqlen)
    x2 = x.reshape(t, D_MODEL)
    mem2 = mem.reshape(nb * mem.shape[1], D_MODEL)
    for l in range(depth):
        lam_init = 0.8 - 0.6 * math.exp(-0.3 * l)
        z2 = _in_projection(x2, norm_mix_pre[l][None], w_in[l].astype(BF16), b_gate[l][None], tabs, seqlen)
        z3 = z2.reshape(nb, seqlen, IN_COLS)
        bmat, cmat, lre, lim = _s5_matrices(ssm_lambda_re[l], ssm_lambda_im[l], ssm_log_dt[l],
                                            ssm_b_re[l], ssm_b_im[l], ssm_c_re[l], ssm_c_im[l])
        ys = _s5_branch(z3, bmat, cmat, lre, lim, ssm_d[l][None], w_glu[l].astype(BF16), b_glu[l][None])
        lmb = jnp.stack([da_lambda_q1[l], da_lambda_k1[l], da_lambda_q2[l], da_lambda_k2[l]])
        ya = _diff_attention(z3, lmb, da_head_norm[l][None], lam_init)
        x2 = _merge(ys.reshape(t, D_MODEL), ya.reshape(t, D_MODEL), z2, x2,
                    w_ssm_proj[l].astype(BF16), w_da_proj[l].astype(BF16), w_mix_out[l].astype(BF16),
                    norm_mix_post[l][None])
        kv = _memory_kv(mem2, norm_mem[l][None], w_xkv[l].astype(BF16))
        x3 = _cross_attention(x2.reshape(nb, seqlen, D_MODEL), kv.reshape(nb, -1, 2 * D_MODEL),
                              norm_x_pre[l][None], w_xq[l].astype(BF16), w_xo[l].astype(BF16),
                              norm_x_post[l][None])
        x2 = _mlp(x3.reshape(t, D_MODEL), norm_ff_pre[l][None], w_ff1[l].astype(BF16),
                  w_ff2[l].astype(BF16), norm_ff_post[l][None])
    return x2.reshape(nb, seqlen, D_MODEL)
```

```python
import functools
import math

import jax
import jax.numpy as jnp
from jax import lax
from jax.experimental import pallas as pl
from jax.experimental.pallas import tpu as pltpu

F32 = jnp.float32
BF16 = jnp.bfloat16

D_MODEL = 1024
EPS = 1e-6
CHUNK = 64
SSM_GROUP = 16
SSM_GROUPS = 64
SSM_STATE = 64
SSM_SLABS = 8
SLAB_GROUPS = SSM_GROUPS // SSM_SLABS
SLAB_STATES = SLAB_GROUPS * SSM_STATE
NLB = 2 * SLAB_STATES // 128
SLAB_ROW_PAD = 8
DA_HEADS = 8
DA_HEAD_DIM = 64
DA_V_DIM = 128
ROPE_THETA = 10000.0
ATTN_HEADS_PER_STEP = 2
XA_HEADS = 4
XA_HEAD_DIM = 256
D_FF = 4096
IN_COLS = 6144
LOG2E = 1.4426950408889634
NEG_BIG = -1e30
VMEM_LIMIT_BYTES = 56 * 1024 * 1024


def _params(*semantics):
    return pltpu.CompilerParams(dimension_semantics=semantics, vmem_limit_bytes=VMEM_LIMIT_BYTES)


def _rms(x, g):
    return x * lax.rsqrt(jnp.mean(x * x, axis=-1, keepdims=True) + EPS) * g


def _bdot(a, b):
    return jnp.dot(a, b, preferred_element_type=F32)


def _inproj_kernel(x_ref, g_ref, w_ref, bg_ref, cq_ref, sq_ref, ck_ref, sk_ref, o_ref, hn_ref):
    j = pl.program_id(1)
    tm = x_ref.shape[0]

    @pl.when(j == 0)
    def _():
        hn_ref[...] = _rms(x_ref[...], g_ref[...]).astype(BF16)

    acc = _bdot(hn_ref[...], w_ref[...])

    @pl.when((j == 0) | (j == 3))
    def _():
        o_ref[...] = acc.astype(BF16)

    def rope(c_ref, s_ref):
        c = c_ref[...]
        s = s_ref[...]
        lane = lax.broadcasted_iota(jnp.int32, (tm, 128), 1)
        first_half = (lane % 64) < 32
        for h in range(DA_HEADS):
            blk = acc[:, h * 128:(h + 1) * 128]
            partner = jnp.where(first_half, pltpu.roll(blk, 96, 1), pltpu.roll(blk, 32, 1))
            o_ref[:, h * 128:(h + 1) * 128] = (blk * c + partner * s).astype(BF16)

    @pl.when(j == 1)
    def _():
        rope(cq_ref, sq_ref)

    @pl.when(j == 2)
    def _():
        rope(ck_ref, sk_ref)

    @pl.when(j >= 4)
    def _():
        o_ref[...] = jax.nn.sigmoid(acc + bg_ref[...]).astype(BF16)


def _in_projection(x2, gain, w_in, b_gate, tabs, seqlen):
    t = x2.shape[0]
    tm = min(1024, seqlen)
    nl = seqlen // tm
    tab_spec = pl.BlockSpec((tm, 128), lambda i, j: (i % nl, 0))
    return pl.pallas_call(
        _inproj_kernel,
        grid=(t // tm, IN_COLS // D_MODEL),
        in_specs=[
            pl.BlockSpec((tm, D_MODEL), lambda i, j: (i, 0)),
            pl.BlockSpec((1, D_MODEL), lambda i, j: (0, 0)),
            pl.BlockSpec((D_MODEL, D_MODEL), lambda i, j: (0, j)),
            pl.BlockSpec((1, D_MODEL), lambda i, j: (0, jnp.maximum(j - 4, 0))),
            tab_spec, tab_spec, tab_spec, tab_spec,
        ],
        out_specs=pl.BlockSpec((tm, D_MODEL), lambda i, j: (i, j)),
        out_shape=jax.ShapeDtypeStruct((t, IN_COLS), BF16),
        scratch_shapes=[pltpu.VMEM((tm, D_MODEL), BF16)],
        compiler_params=_params("parallel", "arbitrary"),
        name="in_projection",
    )(x2, gain, w_in, b_gate, *tabs)


def _s5_kernel(u_ref, bm_ref, cm_ref, lre_ref, lim_ref, d_ref, wg_ref, bgl_ref, o_ref,
               bu_ref, st_ref, y_ref):
    nb, tc, _ = u_ref.shape
    pitch = bu_ref.shape[2] // SSM_SLABS
    c = pl.program_id(0)

    @pl.when(c == 0)
    def _():
        st_ref[...] = jnp.zeros(st_ref.shape, F32)

    for b in range(nb):
        for j in range(SSM_SLABS):
            bu = _bdot(u_ref[b, :, j * 128:(j + 1) * 128], bm_ref[j])
            for k in range(NLB):
                bu_ref[b, k, j * pitch:j * pitch + tc, :] = bu[:, k * 128:(k + 1) * 128]

    ar = [lre_ref[:, k * 128:(k + 1) * 128] for k in range(NLB // 2)]
    ai = [lim_ref[:, k * 128:(k + 1) * 128] for k in range(NLB // 2)]

    def step(t, carry):
        out = []
        for b in range(nb):
            xs = carry[b]
            new = [None] * NLB
            for k in range(NLB // 2):
                xr, xi = xs[k], xs[k + NLB // 2]
                rows = pl.ds(t, SSM_SLABS, stride=pitch)
                new[k] = ar[k] * xr - ai[k] * xi + bu_ref[b, k, rows, :]
                new[k + NLB // 2] = ar[k] * xi + ai[k] * xr + bu_ref[b, k + NLB // 2, rows, :]
                bu_ref[b, k, rows, :] = new[k]
                bu_ref[b, k + NLB // 2, rows, :] = new[k + NLB // 2]
            out.append(tuple(new))
        return tuple(out)

    init = tuple(tuple(st_ref[b, k] for k in range(NLB)) for b in range(nb))
    fin = lax.fori_loop(0, tc, step, init, unroll=8)
    for b in range(nb):
        for k in range(NLB):
            st_ref[b, k] = fin[b][k]

    for b in range(nb):
        for j in range(SSM_SLABS):
            xs = jnp.concatenate([bu_ref[b, k, j * pitch:j * pitch + tc, :] for k in range(NLB)], axis=1)
            y_ref[b * tc:(b + 1) * tc, j * 128:(j + 1) * 128] = _bdot(xs.astype(BF16), cm_ref[j])
    u = u_ref[...].reshape(nb * tc, D_MODEL).astype(F32)
    y = y_ref[...] + d_ref[...] * u
    ys = 0.5 * y * (1.0 + jnp.tanh(math.sqrt(2.0 / math.pi) * (y + 0.044715 * (y * y * y))))
    gate = jax.nn.sigmoid(_bdot(ys.astype(BF16), wg_ref[...]) + bgl_ref[...])
    o_ref[...] = (ys * gate).astype(BF16).reshape(nb, tc, D_MODEL)


def _s5_branch(z3, bmat, cmat, lam_re, lam_im, d_skip, w_glu, b_glu):
    nb, seqlen, _ = z3.shape
    tc = min(256, seqlen)
    const2 = lambda c: (0, 0)
    const3 = lambda c: (0, 0, 0)
    return pl.pallas_call(
        _s5_kernel,
        grid=(seqlen // tc,),
        in_specs=[
            pl.BlockSpec((nb, tc, D_MODEL), lambda c: (0, c, 0)),
            pl.BlockSpec(bmat.shape, const3),
            pl.BlockSpec(cmat.shape, const3),
            pl.BlockSpec(lam_re.shape, const2),
            pl.BlockSpec(lam_im.shape, const2),
            pl.BlockSpec((1, D_MODEL), const2),
            pl.BlockSpec((D_MODEL, D_MODEL), const2),
            pl.BlockSpec((1, D_MODEL), const2),
        ],
        out_specs=pl.BlockSpec((nb, tc, D_MODEL), lambda c: (0, c, 0)),
        out_shape=jax.ShapeDtypeStruct((nb, seqlen, D_MODEL), BF16),
        scratch_shapes=[
            pltpu.VMEM((nb, NLB, SSM_SLABS * (tc + SLAB_ROW_PAD), 128), F32),
            pltpu.VMEM((nb, NLB, SSM_SLABS, 128), F32),
            pltpu.VMEM((nb * tc, D_MODEL), F32),
        ],
        compiler_params=_params("arbitrary"),
        name="s5_branch",
    )(z3, bmat, cmat, lam_re, lam_im, d_skip, w_glu, b_glu)


def _s5_matrices(lam_re, lam_im, log_dt, b_re, b_im, c_re, c_im):
    lam = lax.complex(lam_re, lam_im)
    dt = jnp.exp(log_dt)[:, None]
    lam_bar = jnp.exp(lam * dt)
    b_bar = ((lam_bar - 1.0) / lam)[..., None] * lax.complex(b_re, b_im)
    eye = jnp.eye(SLAB_GROUPS, dtype=F32)
    bb = b_bar.reshape(SSM_SLABS, SLAB_GROUPS, SSM_STATE, SSM_GROUP)

    def in_mat(part):
        return jnp.einsum('jgnh,gk->jghkn', part, eye).reshape(SSM_SLABS, 128, SLAB_STATES)

    bmat = jnp.concatenate([in_mat(bb.real), in_mat(bb.imag)], axis=2).astype(BF16)
    cr = c_re.reshape(SSM_SLABS, SLAB_GROUPS, SSM_GROUP, SSM_STATE)
    ci = c_im.reshape(SSM_SLABS, SLAB_GROUPS, SSM_GROUP, SSM_STATE)

    def out_mat(part):
        return jnp.einsum('jghn,gk->jkngh', part, eye).reshape(SSM_SLABS, SLAB_STATES, 128)

    cmat = jnp.concatenate([out_mat(cr), out_mat(-ci)], axis=1).astype(BF16)
    lre = lam_bar.real.reshape(SSM_SLABS, SLAB_STATES)
    lim = lam_bar.imag.reshape(SSM_SLABS, SLAB_STATES)
    return bmat, cmat, lre, lim


def _dattn_kernel(q_ref, k_ref, v_ref, lmb_ref, hn_ref, o_ref,
                  qm_ref, sx_ref, sy_ref, m_ref, l_ref, acc_ref, *, lam_init):
    tq = q_ref.shape[0]
    tk = tq
    nstrips = tk // 128
    nheads = q_ref.shape[1] // 128
    qi = pl.program_id(2)
    lane = lax.broadcasted_iota(jnp.int32, (tq, 128), 1)
    nt = (((1,), (1,)), ((), ()))
    for hh in range(nheads):
        q = q_ref[:, hh * 128:(hh + 1) * 128]
        zero = jnp.zeros_like(q)
        qm_ref[2 * hh] = jnp.where(lane < DA_HEAD_DIM, q, zero)
        qm_ref[2 * hh + 1] = jnp.where(lane >= DA_HEAD_DIM, q, zero)
    m_ref[...] = jnp.full(m_ref.shape, NEG_BIG, F32)
    l_ref[...] = jnp.zeros(l_ref.shape, F32)
    acc_ref[...] = jnp.zeros(acc_ref.shape, F32)

    def scores(ci, ki, s_ref):
        start = pl.multiple_of(ki * tk, tk)
        hh = ci // 2
        kb = k_ref[pl.ds(start, tk), hh * 128:(hh + 1) * 128]
        s_ref[ci] = lax.dot_general(qm_ref[ci], kb, nt, preferred_element_type=F32)

    def update(ci, ki, s_ref, mask):
        start = pl.multiple_of(ki * tk, tk)
        hh = ci // 2
        vb = v_ref[pl.ds(start, tk), hh * 128:(hh + 1) * 128]
        s = s_ref[ci]
        if mask is not None:
            s = jnp.where(mask, s, NEG_BIG)
        m_prev = m_ref[ci]
        m_next = jnp.maximum(m_prev, jnp.max(s, axis=1, keepdims=True))
        alpha = jnp.exp2(m_prev - m_next)
        ps = [jnp.exp2(s[:, c * 128:(c + 1) * 128] - m_next) for c in range(nstrips)]
        l_ref[ci] = alpha * l_ref[ci] + functools.reduce(lambda a, b: a + b, ps)
        p = jnp.concatenate(ps, axis=1).astype(BF16)
        acc_ref[ci] = alpha * acc_ref[ci] + _bdot(p, vb)
        m_ref[ci] = m_next

    def stage(k_next, s_next_ref, k_cur, s_cur_ref, mask):
        for ci in range(2 * nheads):
            if k_next is not None:
                scores(ci, k_next, s_next_ref)
            update(ci, k_cur, s_cur_ref, mask)

    for ci in range(2 * nheads):
        scores(ci, 0, sx_ref)

    def pair(p, carry):
        stage(2 * p + 1, sy_ref, 2 * p, sx_ref, None)
        stage(2 * p + 2, sx_ref, 2 * p + 1, sy_ref, None)
        return carry

    lax.fori_loop(0, qi // 2, pair, 0)
    row = lax.broadcasted_iota(jnp.int32, (tq, tk), 0) // CHUNK
    col = lax.broadcasted_iota(jnp.int32, (tq, tk), 1) // CHUNK
    mask = col <= row

    @pl.when(qi % 2 == 0)
    def _():
        stage(None, None, qi, sx_ref, mask)

    @pl.when(qi % 2 == 1)
    def _():
        stage(qi, sy_ref, qi - 1, sx_ref, None)
        stage(None, None, qi, sy_ref, mask)

    lv = lmb_ref[...]
    lam = (jnp.exp(jnp.sum(lv[0:1] * lv[1:2], axis=1, keepdims=True))
           - jnp.exp(jnp.sum(lv[2:3] * lv[3:4], axis=1, keepdims=True)) + lam_init)
    for hh in range(nheads):
        l1 = jnp.sum(l_ref[2 * hh], axis=1, keepdims=True)
        l2 = jnp.sum(l_ref[2 * hh + 1], axis=1, keepdims=True)
        o = acc_ref[2 * hh] / l1 - lam * (acc_ref[2 * hh + 1] / l2)
        o = o * lax.rsqrt(jnp.mean(o * o, axis=-1, keepdims=True) + EPS) * hn_ref[...]
        o_ref[:, hh * 128:(hh + 1) * 128] = (o * (1.0 - lam_init)).astype(BF16)


def _diff_attention(z3, lmb, head_norm, lam_init):
    nb, seqlen, _ = z3.shape
    tq = min(512, seqlen)
    hps = ATTN_HEADS_PER_STEP
    width = hps * 128
    qcol, kcol, vcol = (c // width for c in (D_MODEL, 2 * D_MODEL, 3 * D_MODEL))
    return pl.pallas_call(
        functools.partial(_dattn_kernel, lam_init=lam_init),
        grid=(nb, DA_HEADS // hps, seqlen // tq),
        in_specs=[
            pl.BlockSpec((None, tq, width), lambda b, h, i: (b, i, qcol + h)),
            pl.BlockSpec((None, seqlen, width), lambda b, h, i: (b, 0, kcol + h)),
            pl.BlockSpec((None, seqlen, width), lambda b, h, i: (b, 0, vcol + h)),
            pl.BlockSpec((4, DA_HEAD_DIM), lambda b, h, i: (0, 0)),
            pl.BlockSpec((1, DA_V_DIM), lambda b, h, i: (0, 0)),
        ],
        out_specs=pl.BlockSpec((None, tq, width), lambda b, h, i: (b, i, h)),
        out_shape=jax.ShapeDtypeStruct((nb, seqlen, D_MODEL), BF16),
        scratch_shapes=[pltpu.VMEM((2 * hps, tq, 128), BF16),
                        pltpu.VMEM((2 * hps, tq, tq), F32), pltpu.VMEM((2 * hps, tq, tq), F32),
                        pltpu.VMEM((2 * hps, tq, 128), F32), pltpu.VMEM((2 * hps, tq, 128), F32),
                        pltpu.VMEM((2 * hps, tq, DA_V_DIM), F32)],
        compiler_params=_params("parallel", "parallel", "arbitrary"),
        name="diff_attention",
    )(z3, z3, z3, lmb, head_norm)


def _merge_kernel(ys_ref, ya_ref, g_ref, x_ref, wa_ref, wb_ref, wo_ref, n_ref, o_ref):
    a = _bdot(ys_ref[...], wa_ref[...])
    b = _bdot(ya_ref[...], wb_ref[...])
    ga = g_ref[:, :D_MODEL].astype(F32)
    gb = g_ref[:, D_MODEL:].astype(F32)
    merged = (ga * a + gb * b).astype(BF16)
    o_ref[...] = x_ref[...] + _rms(_bdot(merged, wo_ref[...]), n_ref[...])


def _merge(ys2, ya2, z2, x2, w_a, w_b, w_o, gain):
    t = x2.shape[0]
    tm = min(512, t)
    row = lambda i: (i, 0)
    const = lambda i: (0, 0)
    wspec = pl.BlockSpec((D_MODEL, D_MODEL), const)
    return pl.pallas_call(
        _merge_kernel,
        grid=(t // tm,),
        in_specs=[
            pl.BlockSpec((tm, D_MODEL), row),
            pl.BlockSpec((tm, D_MODEL), row),
            pl.BlockSpec((tm, 2 * D_MODEL), lambda i: (i, 2)),
            pl.BlockSpec((tm, D_MODEL), row),
            wspec, wspec, wspec,
            pl.BlockSpec((1, D_MODEL), const),
        ],
        out_specs=pl.BlockSpec((tm, D_MODEL), row),
        out_shape=jax.ShapeDtypeStruct((t, D_MODEL), F32),
        compiler_params=_params("parallel"),
        name="merge",
    )(ys2, ya2, z2, x2, w_a, w_b, w_o, gain)


def _memkv_kernel(m_ref, g_ref, w_ref, o_ref):
    o_ref[...] = _bdot(_rms(m_ref[...], g_ref[...]).astype(BF16), w_ref[...]).astype(BF16)


def _memory_kv(mem2, gain, w_xkv):
    rows = mem2.shape[0]
    const = lambda i: (0, 0)
    return pl.pallas_call(
        _memkv_kernel,
        grid=(1,),
        in_specs=[
            pl.BlockSpec((rows, D_MODEL), const),
            pl.BlockSpec((1, D_MODEL), const),
            pl.BlockSpec((D_MODEL, 2 * D_MODEL), const),
        ],
        out_specs=pl.BlockSpec((rows, 2 * D_MODEL), const),
        out_shape=jax.ShapeDtypeStruct((rows, 2 * D_MODEL), BF16),
        compiler_params=_params("arbitrary"),
        name="memory_kv",
    )(mem2, gain, w_xkv)


def _xattn_kernel(x_ref, kv_ref, npre_ref, wq_ref, wo_ref, npost_ref, o_ref, oh_ref):
    x = x_ref[...]
    h = _rms(x, npre_ref[...]).astype(BF16)
    q = (_bdot(h, wq_ref[...]) * (XA_HEAD_DIM ** -0.5 * LOG2E)).astype(BF16)
    nt = (((1,), (1,)), ((), ()))
    for hd in range(XA_HEADS):
        lo = hd * XA_HEAD_DIM
        k = kv_ref[:, lo:lo + XA_HEAD_DIM]
        v = kv_ref[:, D_MODEL + lo:D_MODEL + lo + XA_HEAD_DIM]
        s = lax.dot_general(q[:, lo:lo + XA_HEAD_DIM], k, nt, preferred_element_type=F32)
        p = jnp.exp2(s - jnp.max(s, axis=1, keepdims=True))
        o = _bdot(p.astype(BF16), v) / jnp.sum(p, axis=1, keepdims=True)
        oh_ref[:, lo:lo + XA_HEAD_DIM] = o.astype(BF16)
    o_ref[...] = x + _rms(_bdot(oh_ref[...], wo_ref[...]), npost_ref[...])


def _cross_attention(x3, kv3, n_pre, w_xq, w_xo, n_post):
    nb, seqlen, _ = x3.shape
    mlen = kv3.shape[1]
    tm = min(512, seqlen)
    const = lambda b, i: (0, 0)
    wspec = pl.BlockSpec((D_MODEL, D_MODEL), const)
    vspec = pl.BlockSpec((1, D_MODEL), const)
    return pl.pallas_call(
        _xattn_kernel,
        grid=(nb, seqlen // tm),
        in_specs=[
            pl.BlockSpec((None, tm, D_MODEL), lambda b, i: (b, i, 0)),
            pl.BlockSpec((None, mlen, 2 * D_MODEL), lambda b, i: (b, 0, 0)),
            vspec, wspec, wspec, vspec,
        ],
        out_specs=pl.BlockSpec((None, tm, D_MODEL), lambda b, i: (b, i, 0)),
        out_shape=jax.ShapeDtypeStruct((nb, seqlen, D_MODEL), F32),
        scratch_shapes=[pltpu.VMEM((tm, D_MODEL), BF16)],
        compiler_params=_params("parallel", "parallel"),
        name="cross_attention",
    )(x3, kv3, n_pre, w_xq, w_xo, n_post)


def _mlp_kernel(x_ref, npre_ref, w1_ref, w2_ref, npost_ref, o_ref):
    x = x_ref[...]
    h = _rms(x, npre_ref[...]).astype(BF16)
    f = jnp.zeros(x.shape, F32)
    for c in range(D_FF // D_MODEL):
        lo = c * D_MODEL
        a = jnp.maximum(_bdot(h, w1_ref[:, lo:lo + D_MODEL]), 0.0)
        f = f + _bdot((a * a).astype(BF16), w2_ref[lo:lo + D_MODEL, :])
    o_ref[...] = x + _rms(f, npost_ref[...])


def _mlp(x2, n_pre, w1, w2, n_post):
    t = x2.shape[0]
    tm = min(512, t)
    row = lambda i: (i, 0)
    const = lambda i: (0, 0)
    vspec = pl.BlockSpec((1, D_MODEL), const)
    return pl.pallas_call(
        _mlp_kernel,
        grid=(t // tm,),
        in_specs=[
            pl.BlockSpec((tm, D_MODEL), row),
            vspec,
            pl.BlockSpec((D_MODEL, D_FF), const),
            pl.BlockSpec((D_FF, D_MODEL), const),
            vspec,
        ],
        out_specs=pl.BlockSpec((tm, D_MODEL), row),
        out_shape=jax.ShapeDtypeStruct((t, D_MODEL), F32),
        compiler_params=_params("parallel"),
        name="mlp",
    )(x2, n_pre, w1, w2, n_post)


def _rope_tables(seqlen):
    inv = ROPE_THETA ** (-jnp.arange(0, DA_HEAD_DIM, 2, dtype=F32) / DA_HEAD_DIM)
    ang = jnp.arange(seqlen, dtype=F32)[:, None] * inv[None, :]
    cos = jnp.tile(jnp.cos(ang), (1, 4))
    sin = jnp.tile(jnp.concatenate([-jnp.sin(ang), jnp.sin(ang)], axis=1), (1, 2))
    qs = DA_HEAD_DIM ** -0.5 * LOG2E
    return cos * qs, sin * qs, cos, sin


def kernel(x, mem, norm_mix_pre, w_in, b_gate, ssm_lambda_re, ssm_lambda_im, ssm_log_dt, ssm_b_re, ssm_b_im, ssm_c_re, ssm_c_im, ssm_d, w_glu, b_glu, w_ssm_proj, da_lambda_q1, da_lambda_k1, da_lambda_q2, da_lambda_k2, da_head_norm, w_da_proj, w_mix_out, norm_mix_post, norm_x_pre, norm_mem, w_xq, w_xkv, w_xo, norm_x_post, norm_ff_pre, w_ff1, w_ff2, norm_ff_post):
    nb, seqlen, _ = x.shape
    t = nb * seqlen
    depth = w_in.shape[0]
    tabs = _rope_tables(seqlen)
    x2 = x.reshape(t, D_MODEL)
    mem2 = mem.reshape(nb * mem.shape[1], D_MODEL)
    for l in range(depth):
        lam_init = 0.8 - 0.6 * math.exp(-0.3 * l)
        z2 = _in_projection(x2, norm_mix_pre[l][None], w_in[l].astype(BF16), b_gate[l][None], tabs, seqlen)
        z3 = z2.reshape(nb, seqlen, IN_COLS)
        bmat, cmat, lre, lim = _s5_matrices(ssm_lambda_re[l], ssm_lambda_im[l], ssm_log_dt[l],
                                            ssm_b_re[l], ssm_b_im[l], ssm_c_re[l], ssm_c_im[l])
        ys = _s5_branch(z3, bmat, cmat, lre, lim, ssm_d[l][None], w_glu[l].astype(BF16), b_glu[l][None])
        lmb = jnp.stack([da_lambda_q1[l], da_lambda_k1[l], da_lambda_q2[l], da_lambda_k2[l]])
        ya = _diff_attention(z3, lmb, da_head_norm[l][None], lam_init)
        x2 = _merge(ys.reshape(t, D_MODEL), ya.reshape(t, D_MODEL), z2, x2,
                    w_ssm_proj[l].astype(BF16), w_da_proj[l].astype(BF16), w_mix_out[l].astype(BF16),
                    norm_mix_post[l][None])
        kv = _memory_kv(mem2, norm_mem[l][None], w_xkv[l].astype(BF16))
        x3 = _cross_attention(x2.reshape(nb, seqlen, D_MODEL), kv.reshape(nb, -1, 2 * D_MODEL),
                              norm_x_pre[l][None], w_xq[l].astype(BF16), w_xo[l].astype(BF16),
                              norm_x_post[l][None])
        x2 = _mlp(x3.reshape(t, D_MODEL), norm_ff_pre[l][None], w_ff1[l].astype(BF16),
                  w_ff2[l].astype(BF16), norm_ff_post[l][None])
    return x2.reshape(nb, seqlen, D_MODEL)
```

```python
import functools
import math

import jax
import jax.numpy as jnp
from jax import lax
from jax.experimental import pallas as pl
from jax.experimental.pallas import tpu as pltpu

F32 = jnp.float32
BF16 = jnp.bfloat16

D_MODEL = 1024
EPS = 1e-6
CHUNK = 64
SSM_GROUP = 16
SSM_GROUPS = 64
SSM_STATE = 64
SSM_SLABS = 8
SLAB_GROUPS = SSM_GROUPS // SSM_SLABS
SLAB_STATES = SLAB_GROUPS * SSM_STATE
NLB = 2 * SLAB_STATES // 128
SLAB_ROW_PAD = 8
DA_HEADS = 8
DA_HEAD_DIM = 64
DA_V_DIM = 128
ROPE_THETA = 10000.0
ATTN_HEADS_PER_STEP = 2
ATTN_ROW_CHUNK = 32
XA_HEADS = 4
XA_HEAD_DIM = 256
D_FF = 4096
IN_COLS = 6144
LOG2E = 1.4426950408889634
NEG_BIG = -1e30
VMEM_LIMIT_BYTES = 56 * 1024 * 1024


def _params(*semantics):
    return pltpu.CompilerParams(dimension_semantics=semantics, vmem_limit_bytes=VMEM_LIMIT_BYTES)


def _rms(x, g):
    return x * lax.rsqrt(jnp.mean(x * x, axis=-1, keepdims=True) + EPS) * g


def _bdot(a, b):
    return jnp.dot(a, b, preferred_element_type=F32)


def _inproj_kernel(x_ref, g_ref, w_ref, bg_ref, cq_ref, sq_ref, ck_ref, sk_ref, o_ref, hn_ref):
    j = pl.program_id(1)
    tm = x_ref.shape[0]

    @pl.when(j == 0)
    def _():
        hn_ref[...] = _rms(x_ref[...], g_ref[...]).astype(BF16)

    acc = _bdot(hn_ref[...], w_ref[...])

    @pl.when((j == 0) | (j == 3))
    def _():
        o_ref[...] = acc.astype(BF16)

    def rope(c_ref, s_ref):
        c = c_ref[...]
        s = s_ref[...]
        lane = lax.broadcasted_iota(jnp.int32, (tm, 128), 1)
        first_half = (lane % 64) < 32
        for h in range(DA_HEADS):
            blk = acc[:, h * 128:(h + 1) * 128]
            partner = jnp.where(first_half, pltpu.roll(blk, 96, 1), pltpu.roll(blk, 32, 1))
            o_ref[:, h * 128:(h + 1) * 128] = (blk * c + partner * s).astype(BF16)

    @pl.when(j == 1)
    def _():
        rope(cq_ref, sq_ref)

    @pl.when(j == 2)
    def _():
        rope(ck_ref, sk_ref)

    @pl.when(j >= 4)
    def _():
        o_ref[...] = jax.nn.sigmoid(acc + bg_ref[...]).astype(BF16)


def _in_projection(x2, gain, w_in, b_gate, tabs, seqlen):
    t = x2.shape[0]
    tm = min(1024, seqlen)
    nl = seqlen // tm
    tab_spec = pl.BlockSpec((tm, 128), lambda i, j: (i % nl, 0))
    return pl.pallas_call(
        _inproj_kernel,
        grid=(t // tm, IN_COLS // D_MODEL),
        in_specs=[
            pl.BlockSpec((tm, D_MODEL), lambda i, j: (i, 0)),
            pl.BlockSpec((1, D_MODEL), lambda i, j: (0, 0)),
            pl.BlockSpec((D_MODEL, D_MODEL), lambda i, j: (0, j)),
            pl.BlockSpec((1, D_MODEL), lambda i, j: (0, jnp.maximum(j - 4, 0))),
            tab_spec, tab_spec, tab_spec, tab_spec,
        ],
        out_specs=pl.BlockSpec((tm, D_MODEL), lambda i, j: (i, j)),
        out_shape=jax.ShapeDtypeStruct((t, IN_COLS), BF16),
        scratch_shapes=[pltpu.VMEM((tm, D_MODEL), BF16)],
        compiler_params=_params("parallel", "arbitrary"),
        name="in_projection",
    )(x2, gain, w_in, b_gate, *tabs)


def _s5_kernel(u_ref, bm_ref, cm_ref, lre_ref, lim_ref, d_ref, wg_ref, bgl_ref, o_ref,
               bu_ref, st_ref, y_ref):
    nb, tc, _ = u_ref.shape
    pitch = bu_ref.shape[2] // SSM_SLABS
    c = pl.program_id(0)

    @pl.when(c == 0)
    def _():
        st_ref[...] = jnp.zeros(st_ref.shape, F32)

    for b in range(nb):
        for j in range(SSM_SLABS):
            bu = _bdot(u_ref[b, :, j * 128:(j + 1) * 128], bm_ref[j])
            for k in range(NLB):
                bu_ref[b, k, j * pitch:j * pitch + tc, :] = bu[:, k * 128:(k + 1) * 128]

    ar = [lre_ref[:, k * 128:(k + 1) * 128] for k in range(NLB // 2)]
    ai = [lim_ref[:, k * 128:(k + 1) * 128] for k in range(NLB // 2)]

    def step(t, carry):
        out = []
        for b in range(nb):
            xs = carry[b]
            new = [None] * NLB
            for k in range(NLB // 2):
                xr, xi = xs[k], xs[k + NLB // 2]
                rows = pl.ds(t, SSM_SLABS, stride=pitch)
                new[k] = ar[k] * xr - ai[k] * xi + bu_ref[b, k, rows, :]
                new[k + NLB // 2] = ar[k] * xi + ai[k] * xr + bu_ref[b, k + NLB // 2, rows, :]
                bu_ref[b, k, rows, :] = new[k]
                bu_ref[b, k + NLB // 2, rows, :] = new[k + NLB // 2]
            out.append(tuple(new))
        return tuple(out)

    init = tuple(tuple(st_ref[b, k] for k in range(NLB)) for b in range(nb))
    fin = lax.fori_loop(0, tc, step, init, unroll=8)
    for b in range(nb):
        for k in range(NLB):
            st_ref[b, k] = fin[b][k]

    for b in range(nb):
        for j in range(SSM_SLABS):
            xs = jnp.concatenate([bu_ref[b, k, j * pitch:j * pitch + tc, :] for k in range(NLB)], axis=1)
            y_ref[b * tc:(b + 1) * tc, j * 128:(j + 1) * 128] = _bdot(xs.astype(BF16), cm_ref[j])
    u = u_ref[...].reshape(nb * tc, D_MODEL).astype(F32)
    y = y_ref[...] + d_ref[...] * u
    ys = 0.5 * y * (1.0 + jnp.tanh(math.sqrt(2.0 / math.pi) * (y + 0.044715 * (y * y * y))))
    gate = jax.nn.sigmoid(_bdot(ys.astype(BF16), wg_ref[...]) + bgl_ref[...])
    o_ref[...] = (ys * gate).astype(BF16).reshape(nb, tc, D_MODEL)


def _s5_branch(z3, bmat, cmat, lam_re, lam_im, d_skip, w_glu, b_glu):
    nb, seqlen, _ = z3.shape
    tc = min(256, seqlen)
    const2 = lambda c: (0, 0)
    const3 = lambda c: (0, 0, 0)
    return pl.pallas_call(
        _s5_kernel,
        grid=(seqlen // tc,),
        in_specs=[
            pl.BlockSpec((nb, tc, D_MODEL), lambda c: (0, c, 0)),
            pl.BlockSpec(bmat.shape, const3),
            pl.BlockSpec(cmat.shape, const3),
            pl.BlockSpec(lam_re.shape, const2),
            pl.BlockSpec(lam_im.shape, const2),
            pl.BlockSpec((1, D_MODEL), const2),
            pl.BlockSpec((D_MODEL, D_MODEL), const2),
            pl.BlockSpec((1, D_MODEL), const2),
        ],
        out_specs=pl.BlockSpec((nb, tc, D_MODEL), lambda c: (0, c, 0)),
        out_shape=jax.ShapeDtypeStruct((nb, seqlen, D_MODEL), BF16),
        scratch_shapes=[
            pltpu.VMEM((nb, NLB, SSM_SLABS * (tc + SLAB_ROW_PAD), 128), F32),
            pltpu.VMEM((nb, NLB, SSM_SLABS, 128), F32),
            pltpu.VMEM((nb * tc, D_MODEL), F32),
        ],
        compiler_params=_params("arbitrary"),
        name="s5_branch",
    )(z3, bmat, cmat, lam_re, lam_im, d_skip, w_glu, b_glu)


def _s5_matrices(lam_re, lam_im, log_dt, b_re, b_im, c_re, c_im):
    lam = lax.complex(lam_re, lam_im)
    dt = jnp.exp(log_dt)[:, None]
    lam_bar = jnp.exp(lam * dt)
    b_bar = ((lam_bar - 1.0) / lam)[..., None] * lax.complex(b_re, b_im)
    eye = jnp.eye(SLAB_GROUPS, dtype=F32)
    bb = b_bar.reshape(SSM_SLABS, SLAB_GROUPS, SSM_STATE, SSM_GROUP)

    def in_mat(part):
        return jnp.einsum('jgnh,gk->jghkn', part, eye).reshape(SSM_SLABS, 128, SLAB_STATES)

    bmat = jnp.concatenate([in_mat(bb.real), in_mat(bb.imag)], axis=2).astype(BF16)
    cr = c_re.reshape(SSM_SLABS, SLAB_GROUPS, SSM_GROUP, SSM_STATE)
    ci = c_im.reshape(SSM_SLABS, SLAB_GROUPS, SSM_GROUP, SSM_STATE)

    def out_mat(part):
        return jnp.einsum('jghn,gk->jkngh', part, eye).reshape(SSM_SLABS, SLAB_STATES, 128)

    cmat = jnp.concatenate([out_mat(cr), out_mat(-ci)], axis=1).astype(BF16)
    lre = lam_bar.real.reshape(SSM_SLABS, SLAB_STATES)
    lim = lam_bar.imag.reshape(SSM_SLABS, SLAB_STATES)
    return bmat, cmat, lre, lim


def _dattn_kernel(q_ref, k_ref, v_ref, lmb_ref, hn_ref, o_ref,
                  qm_ref, vt_ref, sx_ref, sy_ref, p_ref, m_ref, l_ref, acc_ref, *, lam_init):
    tq = q_ref.shape[0]
    tk = tq
    nheads = q_ref.shape[1] // 128
    nkb = v_ref.shape[0] // tk
    qi = pl.program_id(2)
    lane = lax.broadcasted_iota(jnp.int32, (tq, 128), 1)
    nt = (((1,), (1,)), ((), ()))

    @pl.when(qi == 0)
    def _():
        for hh in range(nheads):
            for c in range(nkb):
                vt_ref[hh, c] = v_ref[c * tk:(c + 1) * tk, hh * 128:(hh + 1) * 128].T

    for hh in range(nheads):
        q = q_ref[:, hh * 128:(hh + 1) * 128]
        zero = jnp.zeros_like(q)
        qm_ref[2 * hh] = jnp.where(lane < DA_HEAD_DIM, q, zero)
        qm_ref[2 * hh + 1] = jnp.where(lane >= DA_HEAD_DIM, q, zero)
    m_ref[...] = jnp.full(m_ref.shape, NEG_BIG, F32)
    l_ref[...] = jnp.zeros(l_ref.shape, F32)
    acc_ref[...] = jnp.zeros(acc_ref.shape, F32)

    def scores(ci, ki, s_ref):
        start = pl.multiple_of(ki * tk, tk)
        hh = ci // 2
        kb = k_ref[pl.ds(start, tk), hh * 128:(hh + 1) * 128]
        s_ref[ci] = lax.dot_general(kb, qm_ref[ci], nt, preferred_element_type=F32)

    def update(ci, ki, s_ref, mask):
        rb = ATTN_ROW_CHUNK

        def sc(r):
            s = s_ref[ci, r * rb:(r + 1) * rb, :]
            return s if mask is None else jnp.where(mask[r * rb:(r + 1) * rb, :], s, NEG_BIG)

        def fold8(x, op):
            return functools.reduce(op, [x[g * 8:(g + 1) * 8, :] for g in range(rb // 8)])

        m_prev = m_ref[ci]
        m8 = functools.reduce(jnp.maximum, [fold8(sc(r), jnp.maximum) for r in range(tk // rb)])
        m_next = jnp.maximum(m_prev, jnp.max(m8, axis=0, keepdims=True))
        alpha = jnp.exp2(m_prev - m_next)
        l8 = alpha * l_ref[ci]
        for r in range(tk // rb):
            p = jnp.exp2(sc(r) - m_next)
            l8 = l8 + fold8(p, jnp.add)
            p_ref[ci, r * rb:(r + 1) * rb, :] = p.astype(BF16)
        l_ref[ci] = l8
        acc_ref[ci] = alpha * acc_ref[ci] + _bdot(vt_ref[ci // 2, ki], p_ref[ci])
        m_ref[ci] = m_next

    def stage(k_next, s_next_ref, k_cur, s_cur_ref, mask):
        for ci in range(2 * nheads):
            if k_next is not None:
                scores(ci, k_next, s_next_ref)
            update(ci, k_cur, s_cur_ref, mask)

    for ci in range(2 * nheads):
        scores(ci, 0, sx_ref)

    def pair(p, carry):
        stage(2 * p + 1, sy_ref, 2 * p, sx_ref, None)
        stage(2 * p + 2, sx_ref, 2 * p + 1, sy_ref, None)
        return carry

    lax.fori_loop(0, qi // 2, pair, 0)
    krow = lax.broadcasted_iota(jnp.int32, (tk, tq), 0) // CHUNK
    qcol = lax.broadcasted_iota(jnp.int32, (tk, tq), 1) // CHUNK
    mask = krow <= qcol

    @pl.when(qi % 2 == 0)
    def _():
        stage(None, None, qi, sx_ref, mask)

    @pl.when(qi % 2 == 1)
    def _():
        stage(qi, sy_ref, qi - 1, sx_ref, None)
        stage(None, None, qi, sy_ref, mask)

    lv = lmb_ref[...]
    lam = (jnp.exp(jnp.sum(lv[0:1] * lv[1:2], axis=1, keepdims=True))
           - jnp.exp(jnp.sum(lv[2:3] * lv[3:4], axis=1, keepdims=True)) + lam_init)
    for hh in range(nheads):
        l1 = jnp.sum(l_ref[2 * hh], axis=0, keepdims=True)
        l2 = jnp.sum(l_ref[2 * hh + 1], axis=0, keepdims=True)
        o = acc_ref[2 * hh] / l1 - lam * (acc_ref[2 * hh + 1] / l2)
        o = o * lax.rsqrt(jnp.mean(o * o, axis=0, keepdims=True) + EPS) * hn_ref[...]
        o_ref[:, hh * 128:(hh + 1) * 128] = (o * (1.0 - lam_init)).T.astype(BF16)


def _diff_attention(z3, lmb, head_norm, lam_init):
    nb, seqlen, _ = z3.shape
    tq = min(512, seqlen)
    hps = ATTN_HEADS_PER_STEP
    width = hps * 128
    qcol, kcol, vcol = (c // width for c in (D_MODEL, 2 * D_MODEL, 3 * D_MODEL))
    return pl.pallas_call(
        functools.partial(_dattn_kernel, lam_init=lam_init),
        grid=(nb, DA_HEADS // hps, seqlen // tq),
        in_specs=[
            pl.BlockSpec((None, tq, width), lambda b, h, i: (b, i, qcol + h)),
            pl.BlockSpec((None, seqlen, width), lambda b, h, i: (b, 0, kcol + h)),
            pl.BlockSpec((None, seqlen, width), lambda b, h, i: (b, 0, vcol + h)),
            pl.BlockSpec((4, DA_HEAD_DIM), lambda b, h, i: (0, 0)),
            pl.BlockSpec((DA_V_DIM, 1), lambda b, h, i: (0, 0)),
        ],
        out_specs=pl.BlockSpec((None, tq, width), lambda b, h, i: (b, i, h)),
        out_shape=jax.ShapeDtypeStruct((nb, seqlen, D_MODEL), BF16),
        scratch_shapes=[pltpu.VMEM((2 * hps, tq, 128), BF16),
                        pltpu.VMEM((hps, seqlen // tq, DA_V_DIM, tq), BF16),
                        pltpu.VMEM((2 * hps, tq, tq), F32), pltpu.VMEM((2 * hps, tq, tq), F32),
                        pltpu.VMEM((2 * hps, tq, tq), BF16),
                        pltpu.VMEM((2 * hps, 1, tq), F32), pltpu.VMEM((2 * hps, 8, tq), F32),
                        pltpu.VMEM((2 * hps, DA_V_DIM, tq), F32)],
        compiler_params=_params("parallel", "parallel", "arbitrary"),
        name="diff_attention",
    )(z3, z3, z3, lmb, head_norm)


def _merge_kernel(ys_ref, ya_ref, g_ref, x_ref, wa_ref, wb_ref, wo_ref, n_ref, o_ref):
    a = _bdot(ys_ref[...], wa_ref[...])
    b = _bdot(ya_ref[...], wb_ref[...])
    ga = g_ref[:, :D_MODEL].astype(F32)
    gb = g_ref[:, D_MODEL:].astype(F32)
    merged = (ga * a + gb * b).astype(BF16)
    o_ref[...] = x_ref[...] + _rms(_bdot(merged, wo_ref[...]), n_ref[...])


def _merge(ys2, ya2, z2, x2, w_a, w_b, w_o, gain):
    t = x2.shape[0]
    tm = min(512, t)
    row = lambda i: (i, 0)
    const = lambda i: (0, 0)
    wspec = pl.BlockSpec((D_MODEL, D_MODEL), const)
    return pl.pallas_call(
        _merge_kernel,
        grid=(t // tm,),
        in_specs=[
            pl.BlockSpec((tm, D_MODEL), row),
            pl.BlockSpec((tm, D_MODEL), row),
            pl.BlockSpec((tm, 2 * D_MODEL), lambda i: (i, 2)),
            pl.BlockSpec((tm, D_MODEL), row),
            wspec, wspec, wspec,
            pl.BlockSpec((1, D_MODEL), const),
        ],
        out_specs=pl.BlockSpec((tm, D_MODEL), row),
        out_shape=jax.ShapeDtypeStruct((t, D_MODEL), F32),
        compiler_params=_params("parallel"),
        name="merge",
    )(ys2, ya2, z2, x2, w_a, w_b, w_o, gain)


def _memkv_kernel(m_ref, g_ref, w_ref, o_ref):
    o_ref[...] = _bdot(_rms(m_ref[...], g_ref[...]).astype(BF16), w_ref[...]).astype(BF16)


def _memory_kv(mem2, gain, w_xkv):
    rows = mem2.shape[0]
    const = lambda i: (0, 0)
    return pl.pallas_call(
        _memkv_kernel,
        grid=(1,),
        in_specs=[
            pl.BlockSpec((rows, D_MODEL), const),
            pl.BlockSpec((1, D_MODEL), const),
            pl.BlockSpec((D_MODEL, 2 * D_MODEL), const),
        ],
        out_specs=pl.BlockSpec((rows, 2 * D_MODEL), const),
        out_shape=jax.ShapeDtypeStruct((rows, 2 * D_MODEL), BF16),
        compiler_params=_params("arbitrary"),
        name="memory_kv",
    )(mem2, gain, w_xkv)


def _xattn_kernel(x_ref, kv_ref, npre_ref, wq_ref, wo_ref, npost_ref, o_ref, oh_ref):
    x = x_ref[...]
    h = _rms(x, npre_ref[...]).astype(BF16)
    q = (_bdot(h, wq_ref[...]) * (XA_HEAD_DIM ** -0.5 * LOG2E)).astype(BF16)
    nt = (((1,), (1,)), ((), ()))
    for hd in range(XA_HEADS):
        lo = hd * XA_HEAD_DIM
        k = kv_ref[:, lo:lo + XA_HEAD_DIM]
        v = kv_ref[:, D_MODEL + lo:D_MODEL + lo + XA_HEAD_DIM]
        s = lax.dot_general(q[:, lo:lo + XA_HEAD_DIM], k, nt, preferred_element_type=F32)
        p = jnp.exp2(s - jnp.max(s, axis=1, keepdims=True))
        o = _bdot(p.astype(BF16), v) / jnp.sum(p, axis=1, keepdims=True)
        oh_ref[:, lo:lo + XA_HEAD_DIM] = o.astype(BF16)
    o_ref[...] = x + _rms(_bdot(oh_ref[...], wo_ref[...]), npost_ref[...])


def _cross_attention(x3, kv3, n_pre, w_xq, w_xo, n_post):
    nb, seqlen, _ = x3.shape
    mlen = kv3.shape[1]
    tm = min(512, seqlen)
    const = lambda b, i: (0, 0)
    wspec = pl.BlockSpec((D_MODEL, D_MODEL), const)
    vspec = pl.BlockSpec((1, D_MODEL), const)
    return pl.pallas_call(
        _xattn_kernel,
        grid=(nb, seqlen // tm),
        in_specs=[
            pl.BlockSpec((None, tm, D_MODEL), lambda b, i: (b, i, 0)),
            pl.BlockSpec((None, mlen, 2 * D_MODEL), lambda b, i: (b, 0, 0)),
            vspec, wspec, wspec, vspec,
        ],
        out_specs=pl.BlockSpec((None, tm, D_MODEL), lambda b, i: (b, i, 0)),
        out_shape=jax.ShapeDtypeStruct((nb, seqlen, D_MODEL), F32),
        scratch_shapes=[pltpu.VMEM((tm, D_MODEL), BF16)],
        compiler_params=_params("parallel", "parallel"),
        name="cross_attention",
    )(x3, kv3, n_pre, w_xq, w_xo, n_post)


def _mlp_kernel(x_ref, npre_ref, w1_ref, w2_ref, npost_ref, o_ref):
    x = x_ref[...]
    h = _rms(x, npre_ref[...]).astype(BF16)
    f = jnp.zeros(x.shape, F32)
    for c in range(D_FF // D_MODEL):
        lo = c * D_MODEL
        a = jnp.maximum(_bdot(h, w1_ref[:, lo:lo + D_MODEL]), 0.0)
        f = f + _bdot((a * a).astype(BF16), w2_ref[lo:lo + D_MODEL, :])
    o_ref[...] = x + _rms(f, npost_ref[...])


def _mlp(x2, n_pre, w1, w2, n_post):
    t = x2.shape[0]
    tm = min(512, t)
    row = lambda i: (i, 0)
    const = lambda i: (0, 0)
    vspec = pl.BlockSpec((1, D_MODEL), const)
    return pl.pallas_call(
        _mlp_kernel,
        grid=(t // tm,),
        in_specs=[
            pl.BlockSpec((tm, D_MODEL), row),
            vspec,
            pl.BlockSpec((D_MODEL, D_FF), const),
            pl.BlockSpec((D_FF, D_MODEL), const),
            vspec,
        ],
        out_specs=pl.BlockSpec((tm, D_MODEL), row),
        out_shape=jax.ShapeDtypeStruct((t, D_MODEL), F32),
        compiler_params=_params("parallel"),
        name="mlp",
    )(x2, n_pre, w1, w2, n_post)


def _rope_tables(seqlen):
    inv = ROPE_THETA ** (-jnp.arange(0, DA_HEAD_DIM, 2, dtype=F32) / DA_HEAD_DIM)
    ang = jnp.arange(seqlen, dtype=F32)[:, None] * inv[None, :]
    cos = jnp.tile(jnp.cos(ang), (1, 4))
    sin = jnp.tile(jnp.concatenate([-jnp.sin(ang), jnp.sin(ang)], axis=1), (1, 2))
    qs = DA_HEAD_DIM ** -0.5 * LOG2E
    return cos * qs, sin * qs, cos, sin


def kernel(x, mem, norm_mix_pre, w_in, b_gate, ssm_lambda_re, ssm_lambda_im, ssm_log_dt, ssm_b_re, ssm_b_im, ssm_c_re, ssm_c_im, ssm_d, w_glu, b_glu, w_ssm_proj, da_lambda_q1, da_lambda_k1, da_lambda_q2, da_lambda_k2, da_head_norm, w_da_proj, w_mix_out, norm_mix_post, norm_x_pre, norm_mem, w_xq, w_xkv, w_xo, norm_x_post, norm_ff_pre, w_ff1, w_ff2, norm_ff_post):
    nb, seqlen, _ = x.shape
    t = nb * seqlen
    depth = w_in.shape[0]
    tabs = _rope_tables(seqlen)
    x2 = x.reshape(t, D_MODEL)
    mem2 = mem.reshape(nb * mem.shape[1], D_MODEL)
    for l in range(depth):
        lam_init = 0.8 - 0.6 * math.exp(-0.3 * l)
        z2 = _in_projection(x2, norm_mix_pre[l][None], w_in[l].astype(BF16), b_gate[l][None], tabs, seqlen)
        z3 = z2.reshape(nb, seqlen, IN_COLS)
        bmat, cmat, lre, lim = _s5_matrices(ssm_lambda_re[l], ssm_lambda_im[l], ssm_log_dt[l],
                                            ssm_b_re[l], ssm_b_im[l], ssm_c_re[l], ssm_c_im[l])
        ys = _s5_branch(z3, bmat, cmat, lre, lim, ssm_d[l][None], w_glu[l].astype(BF16), b_glu[l][None])
        lmb = jnp.stack([da_lambda_q1[l], da_lambda_k1[l], da_lambda_q2[l], da_lambda_k2[l]])
        ya = _diff_attention(z3, lmb, da_head_norm[l][:, None], lam_init)
        x2 = _merge(ys.reshape(t, D_MODEL), ya.reshape(t, D_MODEL), z2, x2,
                    w_ssm_proj[l].astype(BF16), w_da_proj[l].astype(BF16), w_mix_out[l].astype(BF16),
                    norm_mix_post[l][None])
        kv = _memory_kv(mem2, norm_mem[l][None], w_xkv[l].astype(BF16))
        x3 = _cross_attention(x2.reshape(nb, seqlen, D_MODEL), kv.reshape(nb, -1, 2 * D_MODEL),
                              norm_x_pre[l][None], w_xq[l].astype(BF16), w_xo[l].astype(BF16),
                              norm_x_post[l][None])
        x2 = _mlp(x3.reshape(t, D_MODEL), norm_ff_pre[l][None], w_ff1[l].astype(BF16),
                  w_ff2[l].astype(BF16), norm_ff_post[l][None])
    return x2.reshape(nb, seqlen, D_MODEL)
```

```python
import functools
import math

import jax
import jax.numpy as jnp
from jax import lax
from jax.experimental import pallas as pl
from jax.experimental.pallas import tpu as pltpu

F32 = jnp.float32
BF16 = jnp.bfloat16

D_MODEL = 1024
EPS = 1e-6
CHUNK = 64
SSM_GROUP = 16
SSM_GROUPS = 64
SSM_STATE = 64
SSM_SLABS = 8
SLAB_GROUPS = SSM_GROUPS // SSM_SLABS
SLAB_STATES = SLAB_GROUPS * SSM_STATE
NLB = 2 * SLAB_STATES // 128
SLAB_ROW_PAD = 8
DA_HEADS = 8
DA_HEAD_DIM = 64
DA_V_DIM = 128
ROPE_THETA = 10000.0
ATTN_HEADS_PER_STEP = 2
XA_HEADS = 4
XA_HEAD_DIM = 256
D_FF = 4096
IN_COLS = 6144
LOG2E = 1.4426950408889634
NEG_BIG = -1e30
VMEM_LIMIT_BYTES = 56 * 1024 * 1024


def _params(*semantics):
    return pltpu.CompilerParams(dimension_semantics=semantics, vmem_limit_bytes=VMEM_LIMIT_BYTES)


def _rms(x, g):
    return x * lax.rsqrt(jnp.mean(x * x, axis=-1, keepdims=True) + EPS) * g


def _bdot(a, b):
    return jnp.dot(a, b, preferred_element_type=F32)


def _sigmoid(x):
    return 0.5 * jnp.tanh(0.5 * x) + 0.5


def _inproj_kernel(x_ref, g_ref, w_ref, bg_ref, cq_ref, sq_ref, ck_ref, sk_ref, o_ref):
    tm = x_ref.shape[0]
    hn = _rms(x_ref[...], g_ref[...]).astype(BF16)
    lane = lax.broadcasted_iota(jnp.int32, (tm, 128), 1)
    first_half = (lane % 64) < 32

    def rope(acc, c_ref, s_ref, col0):
        c = c_ref[...]
        s = s_ref[...]
        for h in range(DA_HEADS):
            blk = acc[:, h * 128:(h + 1) * 128]
            partner = jnp.where(first_half, pltpu.roll(blk, 96, 1), pltpu.roll(blk, 32, 1))
            o_ref[:, col0 + h * 128:col0 + (h + 1) * 128] = (blk * c + partner * s).astype(BF16)

    for j in range(IN_COLS // D_MODEL):
        col0 = j * D_MODEL
        acc = _bdot(hn, w_ref[:, col0:col0 + D_MODEL])
        if j == 1:
            rope(acc, cq_ref, sq_ref, col0)
        elif j == 2:
            rope(acc, ck_ref, sk_ref, col0)
        elif j >= 4:
            gate = _sigmoid(acc + bg_ref[:, col0 - 4 * D_MODEL:col0 - 3 * D_MODEL])
            o_ref[:, col0:col0 + D_MODEL] = gate.astype(BF16)
        else:
            o_ref[:, col0:col0 + D_MODEL] = acc.astype(BF16)


def _in_projection(x2, gain, w_in, b_gate, tabs, seqlen):
    t = x2.shape[0]
    tm = min(512, seqlen)
    nl = seqlen // tm
    const = lambda i: (0, 0)
    tab_spec = pl.BlockSpec((tm, 128), lambda i: (i % nl, 0))
    return pl.pallas_call(
        _inproj_kernel,
        grid=(t // tm,),
        in_specs=[
            pl.BlockSpec((tm, D_MODEL), lambda i: (i, 0)),
            pl.BlockSpec((1, D_MODEL), const),
            pl.BlockSpec((D_MODEL, IN_COLS), const, pipeline_mode=pl.Buffered(1)),
            pl.BlockSpec((1, 2 * D_MODEL), const),
            tab_spec, tab_spec, tab_spec, tab_spec,
        ],
        out_specs=pl.BlockSpec((tm, IN_COLS), lambda i: (i, 0)),
        out_shape=jax.ShapeDtypeStruct((t, IN_COLS), BF16),
        compiler_params=_params("parallel"),
        name="in_projection",
    )(x2, gain, w_in, b_gate, *tabs)


def _s5_kernel(u_ref, bm_ref, cm_ref, lre_ref, lim_ref, d_ref, wg_ref, bgl_ref, o_ref,
               bu_ref, st_ref, y_ref):
    nb, tc, _ = u_ref.shape
    pitch = bu_ref.shape[2] // SSM_SLABS
    c = pl.program_id(0)

    @pl.when(c == 0)
    def _():
        st_ref[...] = jnp.zeros(st_ref.shape, F32)

    for b in range(nb):
        for j in range(SSM_SLABS):
            bu = _bdot(u_ref[b, :, j * 128:(j + 1) * 128], bm_ref[j])
            for k in range(NLB):
                bu_ref[b, k, j * pitch:j * pitch + tc, :] = bu[:, k * 128:(k + 1) * 128]

    ar = [lre_ref[:, k * 128:(k + 1) * 128] for k in range(NLB // 2)]
    ai = [lim_ref[:, k * 128:(k + 1) * 128] for k in range(NLB // 2)]

    def step(t, carry):
        out = []
        for b in range(nb):
            xs = carry[b]
            new = [None] * NLB
            for k in range(NLB // 2):
                xr, xi = xs[k], xs[k + NLB // 2]
                rows = pl.ds(t, SSM_SLABS, stride=pitch)
                new[k] = ar[k] * xr - ai[k] * xi + bu_ref[b, k, rows, :]
                new[k + NLB // 2] = ar[k] * xi + ai[k] * xr + bu_ref[b, k + NLB // 2, rows, :]
                bu_ref[b, k, rows, :] = new[k]
                bu_ref[b, k + NLB // 2, rows, :] = new[k + NLB // 2]
            out.append(tuple(new))
        return tuple(out)

    init = tuple(tuple(st_ref[b, k] for k in range(NLB)) for b in range(nb))
    fin = lax.fori_loop(0, tc, step, init, unroll=8)
    for b in range(nb):
        for k in range(NLB):
            st_ref[b, k] = fin[b][k]

    for b in range(nb):
        for j in range(SSM_SLABS):
            xs = jnp.concatenate([bu_ref[b, k, j * pitch:j * pitch + tc, :] for k in range(NLB)], axis=1)
            y_ref[b * tc:(b + 1) * tc, j * 128:(j + 1) * 128] = _bdot(xs.astype(BF16), cm_ref[j])
    u = u_ref[...].reshape(nb * tc, D_MODEL).astype(F32)
    y = y_ref[...] + d_ref[...] * u
    ys = 0.5 * y * (1.0 + jnp.tanh(math.sqrt(2.0 / math.pi) * (y + 0.044715 * (y * y * y))))
    gate = _sigmoid(_bdot(ys.astype(BF16), wg_ref[...]) + bgl_ref[...])
    o_ref[...] = (ys * gate).astype(BF16).reshape(nb, tc, D_MODEL)


def _s5_branch(z3, bmat, cmat, lam_re, lam_im, d_skip, w_glu, b_glu):
    nb, seqlen, _ = z3.shape
    tc = min(256, seqlen)
    const2 = lambda c: (0, 0)
    const3 = lambda c: (0, 0, 0)
    return pl.pallas_call(
        _s5_kernel,
        grid=(seqlen // tc,),
        in_specs=[
            pl.BlockSpec((nb, tc, D_MODEL), lambda c: (0, c, 0)),
            pl.BlockSpec(bmat.shape, const3),
            pl.BlockSpec(cmat.shape, const3),
            pl.BlockSpec(lam_re.shape, const2),
            pl.BlockSpec(lam_im.shape, const2),
            pl.BlockSpec((1, D_MODEL), const2),
            pl.BlockSpec((D_MODEL, D_MODEL), const2),
            pl.BlockSpec((1, D_MODEL), const2),
        ],
        out_specs=pl.BlockSpec((nb, tc, D_MODEL), lambda c: (0, c, 0)),
        out_shape=jax.ShapeDtypeStruct((nb, seqlen, D_MODEL), BF16),
        scratch_shapes=[
            pltpu.VMEM((nb, NLB, SSM_SLABS * (tc + SLAB_ROW_PAD), 128), F32),
            pltpu.VMEM((nb, NLB, SSM_SLABS, 128), F32),
            pltpu.VMEM((nb * tc, D_MODEL), F32),
        ],
        compiler_params=_params("arbitrary"),
        name="s5_branch",
    )(z3, bmat, cmat, lam_re, lam_im, d_skip, w_glu, b_glu)


def _s5_matrices(lam_re, lam_im, log_dt, b_re, b_im, c_re, c_im):
    lam = lax.complex(lam_re, lam_im)
    dt = jnp.exp(log_dt)[:, None]
    lam_bar = jnp.exp(lam * dt)
    b_bar = ((lam_bar - 1.0) / lam)[..., None] * lax.complex(b_re, b_im)
    eye = jnp.eye(SLAB_GROUPS, dtype=F32)
    bb = b_bar.reshape(SSM_SLABS, SLAB_GROUPS, SSM_STATE, SSM_GROUP)

    def in_mat(part):
        return jnp.einsum('jgnh,gk->jghkn', part, eye).reshape(SSM_SLABS, 128, SLAB_STATES)

    bmat = jnp.concatenate([in_mat(bb.real), in_mat(bb.imag)], axis=2).astype(BF16)
    cr = c_re.reshape(SSM_SLABS, SLAB_GROUPS, SSM_GROUP, SSM_STATE)
    ci = c_im.reshape(SSM_SLABS, SLAB_GROUPS, SSM_GROUP, SSM_STATE)

    def out_mat(part):
        return jnp.einsum('jghn,gk->jkngh', part, eye).reshape(SSM_SLABS, SLAB_STATES, 128)

    cmat = jnp.concatenate([out_mat(cr), out_mat(-ci)], axis=1).astype(BF16)
    lre = lam_bar.real.reshape(SSM_SLABS, SLAB_STATES)
    lim = lam_bar.imag.reshape(SSM_SLABS, SLAB_STATES)
    return bmat, cmat, lre, lim


def _dattn_kernel(q_ref, k_ref, v_ref, lmb_ref, hn_ref, o_ref,
                  qm_ref, vx_ref, sx_ref, sy_ref, m_ref, acc_ref, *, lam_init):
    tq = q_ref.shape[0]
    tk = tq
    nstrips = tk // 128
    nheads = q_ref.shape[1] // 128
    qi = pl.program_id(2)
    lane = lax.broadcasted_iota(jnp.int32, (tq, 128), 1)
    nt = (((1,), (1,)), ((), ()))

    @pl.when(qi == 0)
    def _():
        for hh in range(nheads):
            vx_ref[hh, :, :DA_V_DIM] = v_ref[:, hh * 128:(hh + 1) * 128]
            vx_ref[hh, :, DA_V_DIM:] = jnp.ones((v_ref.shape[0], DA_V_DIM), BF16)

    for hh in range(nheads):
        q = q_ref[:, hh * 128:(hh + 1) * 128]
        zero = jnp.zeros_like(q)
        qm_ref[2 * hh] = jnp.where(lane < DA_HEAD_DIM, q, zero)
        qm_ref[2 * hh + 1] = jnp.where(lane >= DA_HEAD_DIM, q, zero)
    m_ref[...] = jnp.full(m_ref.shape, NEG_BIG, F32)
    acc_ref[...] = jnp.zeros(acc_ref.shape, F32)

    def scores(ci, ki, s_ref):
        start = pl.multiple_of(ki * tk, tk)
        hh = ci // 2
        kb = k_ref[pl.ds(start, tk), hh * 128:(hh + 1) * 128]
        s_ref[ci] = lax.dot_general(qm_ref[ci], kb, nt, preferred_element_type=F32)

    def update(ci, ki, s_ref, mask):
        start = pl.multiple_of(ki * tk, tk)
        hh = ci // 2
        vb = vx_ref[hh, pl.ds(start, tk), :]
        s = s_ref[ci]
        if mask is not None:
            s = jnp.where(mask, s, NEG_BIG)
        m_prev = m_ref[ci]
        m_next = jnp.maximum(m_prev, jnp.max(s, axis=1, keepdims=True))
        alpha = jnp.exp2(m_prev - m_next)
        ps = [jnp.exp2(s[:, c * 128:(c + 1) * 128] - m_next) for c in range(nstrips)]
        p = jnp.concatenate(ps, axis=1).astype(BF16)
        acc_ref[ci] = jnp.concatenate([alpha, alpha], axis=1) * acc_ref[ci] + _bdot(p, vb)
        m_ref[ci] = m_next

    def stage(k_next, s_next_ref, k_cur, s_cur_ref, mask):
        for ci in range(2 * nheads):
            if k_next is not None:
                scores(ci, k_next, s_next_ref)
            update(ci, k_cur, s_cur_ref, mask)

    for ci in range(2 * nheads):
        scores(ci, 0, sx_ref)

    def pair(p, carry):
        stage(2 * p + 1, sy_ref, 2 * p, sx_ref, None)
        stage(2 * p + 2, sx_ref, 2 * p + 1, sy_ref, None)
        return carry

    lax.fori_loop(0, qi // 2, pair, 0)
    row = lax.broadcasted_iota(jnp.int32, (tq, tk), 0) // CHUNK
    col = lax.broadcasted_iota(jnp.int32, (tq, tk), 1) // CHUNK
    mask = col <= row

    @pl.when(qi % 2 == 0)
    def _():
        stage(None, None, qi, sx_ref, mask)

    @pl.when(qi % 2 == 1)
    def _():
        stage(qi, sy_ref, qi - 1, sx_ref, None)
        stage(None, None, qi, sy_ref, mask)

    lv = lmb_ref[...]
    lam = (jnp.exp(jnp.sum(lv[0:1] * lv[1:2], axis=1, keepdims=True))
           - jnp.exp(jnp.sum(lv[2:3] * lv[3:4], axis=1, keepdims=True)) + lam_init)
    for hh in range(nheads):
        o = (acc_ref[2 * hh, :, :DA_V_DIM] / acc_ref[2 * hh, :, DA_V_DIM:]
             - lam * (acc_ref[2 * hh + 1, :, :DA_V_DIM] / acc_ref[2 * hh + 1, :, DA_V_DIM:]))
        o = o * lax.rsqrt(jnp.mean(o * o, axis=-1, keepdims=True) + EPS) * hn_ref[...]
        o_ref[:, hh * 128:(hh + 1) * 128] = (o * (1.0 - lam_init)).astype(BF16)


def _diff_attention(z3, lmb, head_norm, lam_init):
    nb, seqlen, _ = z3.shape
    tq = min(512, seqlen)
    hps = ATTN_HEADS_PER_STEP
    width = hps * 128
    qcol, kcol, vcol = (c // width for c in (D_MODEL, 2 * D_MODEL, 3 * D_MODEL))
    return pl.pallas_call(
        functools.partial(_dattn_kernel, lam_init=lam_init),
        grid=(nb, DA_HEADS // hps, seqlen // tq),
        in_specs=[
            pl.BlockSpec((None, tq, width), lambda b, h, i: (b, i, qcol + h)),
            pl.BlockSpec((None, seqlen, width), lambda b, h, i: (b, 0, kcol + h)),
            pl.BlockSpec((None, seqlen, width), lambda b, h, i: (b, 0, vcol + h)),
            pl.BlockSpec((4, DA_HEAD_DIM), lambda b, h, i: (0, 0)),
            pl.BlockSpec((1, DA_V_DIM), lambda b, h, i: (0, 0)),
        ],
        out_specs=pl.BlockSpec((None, tq, width), lambda b, h, i: (b, i, h)),
        out_shape=jax.ShapeDtypeStruct((nb, seqlen, D_MODEL), BF16),
        scratch_shapes=[pltpu.VMEM((2 * hps, tq, 128), BF16),
                        pltpu.VMEM((hps, seqlen, 2 * DA_V_DIM), BF16),
                        pltpu.VMEM((2 * hps, tq, tq), F32), pltpu.VMEM((2 * hps, tq, tq), F32),
                        pltpu.VMEM((2 * hps, tq, 128), F32),
                        pltpu.VMEM((2 * hps, tq, 2 * DA_V_DIM), F32)],
        compiler_params=_params("parallel", "parallel", "arbitrary"),
        name="diff_attention",
    )(z3, z3, z3, lmb, head_norm)


def _merge_kernel(ys_ref, ya_ref, g_ref, x_ref, wa_ref, wb_ref, wo_ref, n_ref, o_ref):
    a = _bdot(ys_ref[...], wa_ref[...])
    b = _bdot(ya_ref[...], wb_ref[...])
    ga = g_ref[:, :D_MODEL].astype(F32)
    gb = g_ref[:, D_MODEL:].astype(F32)
    merged = (ga * a + gb * b).astype(BF16)
    o_ref[...] = x_ref[...] + _rms(_bdot(merged, wo_ref[...]), n_ref[...])


def _merge(ys2, ya2, z2, x2, w_a, w_b, w_o, gain):
    t = x2.shape[0]
    tm = min(512, t)
    row = lambda i: (i, 0)
    const = lambda i: (0, 0)
    wspec = pl.BlockSpec((D_MODEL, D_MODEL), const)
    return pl.pallas_call(
        _merge_kernel,
        grid=(t // tm,),
        in_specs=[
            pl.BlockSpec((tm, D_MODEL), row),
            pl.BlockSpec((tm, D_MODEL), row),
            pl.BlockSpec((tm, 2 * D_MODEL), lambda i: (i, 2)),
            pl.BlockSpec((tm, D_MODEL), row),
            wspec, wspec, wspec,
            pl.BlockSpec((1, D_MODEL), const),
        ],
        out_specs=pl.BlockSpec((tm, D_MODEL), row),
        out_shape=jax.ShapeDtypeStruct((t, D_MODEL), F32),
        compiler_params=_params("parallel"),
        name="merge",
    )(ys2, ya2, z2, x2, w_a, w_b, w_o, gain)


def _memkv_kernel(m_ref, g_ref, w_ref, o_ref):
    o_ref[...] = _bdot(_rms(m_ref[...], g_ref[...]).astype(BF16), w_ref[...]).astype(BF16)


def _memory_kv(mem2, gain, w_xkv):
    rows = mem2.shape[0]
    const = lambda i: (0, 0)
    return pl.pallas_call(
        _memkv_kernel,
        grid=(1,),
        in_specs=[
            pl.BlockSpec((rows, D_MODEL), const),
            pl.BlockSpec((1, D_MODEL), const),
            pl.BlockSpec((D_MODEL, 2 * D_MODEL), const),
        ],
        out_specs=pl.BlockSpec((rows, 2 * D_MODEL), const),
        out_shape=jax.ShapeDtypeStruct((rows, 2 * D_MODEL), BF16),
        compiler_params=_params("arbitrary"),
        name="memory_kv",
    )(mem2, gain, w_xkv)


def _xattn_kernel(x_ref, kv_ref, npre_ref, wq_ref, wo_ref, npost_ref, o_ref, oh_ref):
    x = x_ref[...]
    h = _rms(x, npre_ref[...]).astype(BF16)
    q = (_bdot(h, wq_ref[...]) * (XA_HEAD_DIM ** -0.5 * LOG2E)).astype(BF16)
    nt = (((1,), (1,)), ((), ()))
    for hd in range(XA_HEADS):
        lo = hd * XA_HEAD_DIM
        k = kv_ref[:, lo:lo + XA_HEAD_DIM]
        v = kv_ref[:, D_MODEL + lo:D_MODEL + lo + XA_HEAD_DIM]
        s = lax.dot_general(q[:, lo:lo + XA_HEAD_DIM], k, nt, preferred_element_type=F32)
        p = jnp.exp2(s - jnp.max(s, axis=1, keepdims=True))
        o = _bdot(p.astype(BF16), v) / jnp.sum(p, axis=1, keepdims=True)
        oh_ref[:, lo:lo + XA_HEAD_DIM] = o.astype(BF16)
    o_ref[...] = x + _rms(_bdot(oh_ref[...], wo_ref[...]), npost_ref[...])


def _cross_attention(x3, kv3, n_pre, w_xq, w_xo, n_post):
    nb, seqlen, _ = x3.shape
    mlen = kv3.shape[1]
    tm = min(512, seqlen)
    const = lambda b, i: (0, 0)
    wspec = pl.BlockSpec((D_MODEL, D_MODEL), const)
    vspec = pl.BlockSpec((1, D_MODEL), const)
    return pl.pallas_call(
        _xattn_kernel,
        grid=(nb, seqlen // tm),
        in_specs=[
            pl.BlockSpec((None, tm, D_MODEL), lambda b, i: (b, i, 0)),
            pl.BlockSpec((None, mlen, 2 * D_MODEL), lambda b, i: (b, 0, 0)),
            vspec, wspec, wspec, vspec,
        ],
        out_specs=pl.BlockSpec((None, tm, D_MODEL), lambda b, i: (b, i, 0)),
        out_shape=jax.ShapeDtypeStruct((nb, seqlen, D_MODEL), F32),
        scratch_shapes=[pltpu.VMEM((tm, D_MODEL), BF16)],
        compiler_params=_params("parallel", "parallel"),
        name="cross_attention",
    )(x3, kv3, n_pre, w_xq, w_xo, n_post)


def _mlp_kernel(x_ref, npre_ref, w1_ref, w2_ref, npost_ref, o_ref):
    x = x_ref[...]
    h = _rms(x, npre_ref[...]).astype(BF16)
    f = jnp.zeros(x.shape, F32)
    for c in range(D_FF // D_MODEL):
        lo = c * D_MODEL
        a = jnp.maximum(_bdot(h, w1_ref[:, lo:lo + D_MODEL]), 0.0)
        f = f + _bdot((a * a).astype(BF16), w2_ref[lo:lo + D_MODEL, :])
    o_ref[...] = x + _rms(f, npost_ref[...])


def _mlp(x2, n_pre, w1, w2, n_post):
    t = x2.shape[0]
    tm = min(512, t)
    row = lambda i: (i, 0)
    const = lambda i: (0, 0)
    vspec = pl.BlockSpec((1, D_MODEL), const)
    return pl.pallas_call(
        _mlp_kernel,
        grid=(t // tm,),
        in_specs=[
            pl.BlockSpec((tm, D_MODEL), row),
            vspec,
            pl.BlockSpec((D_MODEL, D_FF), const),
            pl.BlockSpec((D_FF, D_MODEL), const),
            vspec,
        ],
        out_specs=pl.BlockSpec((tm, D_MODEL), row),
        out_shape=jax.ShapeDtypeStruct((t, D_MODEL), F32),
        compiler_params=_params("parallel"),
        name="mlp",
    )(x2, n_pre, w1, w2, n_post)


def _rope_tables(seqlen):
    inv = ROPE_THETA ** (-jnp.arange(0, DA_HEAD_DIM, 2, dtype=F32) / DA_HEAD_DIM)
    ang = jnp.arange(seqlen, dtype=F32)[:, None] * inv[None, :]
    cos = jnp.tile(jnp.cos(ang), (1, 4))
    sin = jnp.tile(jnp.concatenate([-jnp.sin(ang), jnp.sin(ang)], axis=1), (1, 2))
    qs = DA_HEAD_DIM ** -0.5 * LOG2E
    return cos * qs, sin * qs, cos, sin


def kernel(x, mem, norm_mix_pre, w_in, b_gate, ssm_lambda_re, ssm_lambda_im, ssm_log_dt, ssm_b_re, ssm_b_im, ssm_c_re, ssm_c_im, ssm_d, w_glu, b_glu, w_ssm_proj, da_lambda_q1, da_lambda_k1, da_lambda_q2, da_lambda_k2, da_head_norm, w_da_proj, w_mix_out, norm_mix_post, norm_x_pre, norm_mem, w_xq, w_xkv, w_xo, norm_x_post, norm_ff_pre, w_ff1, w_ff2, norm_ff_post):
    nb, seqlen, _ = x.shape
    t = nb * seqlen
    depth = w_in.shape[0]
    tabs = _rope_tables(seqlen)
    x2 = x.reshape(t, D_MODEL)
    mem2 = mem.reshape(nb * mem.shape[1], D_MODEL)
    for l in range(depth):
        lam_init = 0.8 - 0.6 * math.exp(-0.3 * l)
        z2 = _in_projection(x2, norm_mix_pre[l][None], w_in[l].astype(BF16), b_gate[l][None], tabs, seqlen)
        z3 = z2.reshape(nb, seqlen, IN_COLS)
        bmat, cmat, lre, lim = _s5_matrices(ssm_lambda_re[l], ssm_lambda_im[l], ssm_log_dt[l],
                                            ssm_b_re[l], ssm_b_im[l], ssm_c_re[l], ssm_c_im[l])
        ys = _s5_branch(z3, bmat, cmat, lre, lim, ssm_d[l][None], w_glu[l].astype(BF16), b_glu[l][None])
        lmb = jnp.stack([da_lambda_q1[l], da_lambda_k1[l], da_lambda_q2[l], da_lambda_k2[l]])
        ya = _diff_attention(z3, lmb, da_head_norm[l][None], lam_init)
        x2 = _merge(ys.reshape(t, D_MODEL), ya.reshape(t, D_MODEL), z2, x2,
                    w_ssm_proj[l].astype(BF16), w_da_proj[l].astype(BF16), w_mix_out[l].astype(BF16),
                    norm_mix_post[l][None])
        kv = _memory_kv(mem2, norm_mem[l][None], w_xkv[l].astype(BF16))
        x3 = _cross_attention(x2.reshape(nb, seqlen, D_MODEL), kv.reshape(nb, -1, 2 * D_MODEL),
                              norm_x_pre[l][None], w_xq[l].astype(BF16), w_xo[l].astype(BF16),
                              norm_x_post[l][None])
        x2 = _mlp(x3.reshape(t, D_MODEL), norm_ff_pre[l][None], w_ff1[l].astype(BF16),
                  w_ff2[l].astype(BF16), norm_ff_post[l][None])
    return x2.reshape(nb, seqlen, D_MODEL)
```

```python
import functools
import math

import jax
import jax.numpy as jnp
from jax import lax
from jax.experimental import pallas as pl
from jax.experimental.pallas import tpu as pltpu

F32 = jnp.float32
BF16 = jnp.bfloat16

D_MODEL = 1024
EPS = 1e-6
CHUNK = 64
SSM_GROUP = 16
SSM_GROUPS = 64
SSM_STATE = 64
SSM_SLABS = 8
SLAB_GROUPS = SSM_GROUPS // SSM_SLABS
SLAB_STATES = SLAB_GROUPS * SSM_STATE
NLB = 2 * SLAB_STATES // 128
SLAB_ROW_PAD = 8
DA_HEADS = 8
DA_HEAD_DIM = 64
DA_V_DIM = 128
ROPE_THETA = 10000.0
ATTN_HEADS_PER_STEP = 2
XA_HEADS = 4
XA_HEAD_DIM = 256
D_FF = 4096
IN_COLS = 6144
LOG2E = 1.4426950408889634
NEG_BIG = -1e30
VMEM_LIMIT_BYTES = 56 * 1024 * 1024


def _params(*semantics):
    return pltpu.CompilerParams(dimension_semantics=semantics, vmem_limit_bytes=VMEM_LIMIT_BYTES)


def _rms(x, g):
    return x * lax.rsqrt(jnp.mean(x * x, axis=-1, keepdims=True) + EPS) * g


def _bdot(a, b):
    return jnp.dot(a, b, preferred_element_type=F32)


def _sigmoid(x):
    return 0.5 * jnp.tanh(0.5 * x) + 0.5


def _inproj_kernel(x_ref, g_ref, w_ref, bg_ref, cq_ref, sq_ref, ck_ref, sk_ref, o_ref):
    tm = x_ref.shape[0]
    hn = _rms(x_ref[...], g_ref[...]).astype(BF16)
    lane = lax.broadcasted_iota(jnp.int32, (tm, 128), 1)
    first_half = (lane % 64) < 32

    def rope(acc, c_ref, s_ref, col0):
        c = c_ref[...]
        s = s_ref[...]
        for h in range(DA_HEADS):
            blk = acc[:, h * 128:(h + 1) * 128]
            partner = jnp.where(first_half, pltpu.roll(blk, 96, 1), pltpu.roll(blk, 32, 1))
            o_ref[:, col0 + h * 128:col0 + (h + 1) * 128] = (blk * c + partner * s).astype(BF16)

    for j in range(IN_COLS // D_MODEL):
        col0 = j * D_MODEL
        acc = _bdot(hn, w_ref[:, col0:col0 + D_MODEL])
        if j == 1:
            rope(acc, cq_ref, sq_ref, col0)
        elif j == 2:
            rope(acc, ck_ref, sk_ref, col0)
        elif j >= 4:
            gate = _sigmoid(acc + bg_ref[:, col0 - 4 * D_MODEL:col0 - 3 * D_MODEL])
            o_ref[:, col0:col0 + D_MODEL] = gate.astype(BF16)
        else:
            o_ref[:, col0:col0 + D_MODEL] = acc.astype(BF16)


def _in_projection(x2, gain, w_in, b_gate, tabs, seqlen):
    t = x2.shape[0]
    tm = min(512, seqlen)
    nl = seqlen // tm
    const = lambda i: (0, 0)
    tab_spec = pl.BlockSpec((tm, 128), lambda i: (i % nl, 0))
    return pl.pallas_call(
        _inproj_kernel,
        grid=(t // tm,),
        in_specs=[
            pl.BlockSpec((tm, D_MODEL), lambda i: (i, 0)),
            pl.BlockSpec((1, D_MODEL), const),
            pl.BlockSpec((D_MODEL, IN_COLS), const, pipeline_mode=pl.Buffered(1)),
            pl.BlockSpec((1, 2 * D_MODEL), const),
            tab_spec, tab_spec, tab_spec, tab_spec,
        ],
        out_specs=pl.BlockSpec((tm, IN_COLS), lambda i: (i, 0)),
        out_shape=jax.ShapeDtypeStruct((t, IN_COLS), BF16),
        compiler_params=_params("parallel"),
        name="in_projection",
    )(x2, gain, w_in, b_gate, *tabs)


def _s5_kernel(ue_ref, uo_ref, bm_ref, cm_ref, lre_ref, lim_ref, d_ref, wg_ref, bgl_ref, o_ref,
               bu_ref, st_ref, wc_ref, yo_ref, w_ref, k0_ref):
    nb, tp, _ = ue_ref.shape
    pitch = bu_ref.shape[2] // SSM_SLABS
    c = pl.program_id(0)

    @pl.when(c == 0)
    def _():
        st_ref[...] = jnp.zeros(st_ref.shape, F32)
        wc_ref[...] = jnp.zeros(wc_ref.shape, F32)

    for b in range(nb):
        for j in range(SSM_SLABS):
            lhs = jnp.concatenate([ue_ref[b, :, j * 128:(j + 1) * 128],
                                   uo_ref[b, :, j * 128:(j + 1) * 128]], axis=1)
            bu = _bdot(lhs, bm_ref[j])
            for k in range(NLB):
                bu_ref[b, k, j * pitch:j * pitch + tp, :] = bu[:, k * 128:(k + 1) * 128]
            k0_ref[b * tp:(b + 1) * tp, j * 128:(j + 1) * 128] = bu[:, NLB * 128:]

    ar = [lre_ref[:, k * 128:(k + 1) * 128] for k in range(NLB // 2)]
    ai = [lim_ref[:, k * 128:(k + 1) * 128] for k in range(NLB // 2)]

    def step(t, carry):
        out = []
        for b in range(nb):
            xs = carry[b]
            new = [None] * NLB
            for k in range(NLB // 2):
                xr, xi = xs[k], xs[k + NLB // 2]
                rows = pl.ds(t, SSM_SLABS, stride=pitch)
                new[k] = ar[k] * xr - ai[k] * xi + bu_ref[b, k, rows, :]
                new[k + NLB // 2] = ar[k] * xi + ai[k] * xr + bu_ref[b, k + NLB // 2, rows, :]
                bu_ref[b, k, rows, :] = new[k]
                bu_ref[b, k + NLB // 2, rows, :] = new[k + NLB // 2]
            out.append(tuple(new))
        return tuple(out)

    init = tuple(tuple(st_ref[b, k] for k in range(NLB)) for b in range(nb))
    fin = lax.fori_loop(0, tp, step, init, unroll=8)
    for b in range(nb):
        for k in range(NLB):
            st_ref[b, k] = fin[b][k]

    for b in range(nb):
        for j in range(SSM_SLABS):
            xs = jnp.concatenate([bu_ref[b, k, j * pitch:j * pitch + tp, :] for k in range(NLB)], axis=1)
            r = _bdot(xs.astype(BF16), cm_ref[j])
            yo_ref[b * tp:(b + 1) * tp, j * 128:(j + 1) * 128] = r[:, :128]
            w_ref[b * tp:(b + 1) * tp, j * 128:(j + 1) * 128] = r[:, 128:]

    def glu(y):
        ys = 0.5 * y * (1.0 + jnp.tanh(math.sqrt(2.0 / math.pi) * (y + 0.044715 * (y * y * y))))
        gate = _sigmoid(_bdot(ys.astype(BF16), wg_ref[...]) + bgl_ref[...])
        return (ys * gate).astype(BF16).reshape(nb, tp, D_MODEL)

    first_row = lax.broadcasted_iota(jnp.int32, (tp, D_MODEL), 0) == 0
    shifted = []
    for b in range(nb):
        w = w_ref[b * tp:(b + 1) * tp, :]
        shifted.append(jnp.where(first_row, wc_ref[b], pltpu.roll(w, 1, 0)))
        wc_ref[b] = w[tp - 1:tp, :]
    ue = ue_ref[...].reshape(nb * tp, D_MODEL).astype(F32)
    uo = uo_ref[...].reshape(nb * tp, D_MODEL).astype(F32)
    o_ref[:, :, :D_MODEL] = glu(jnp.concatenate(shifted, axis=0) + k0_ref[...] + d_ref[...] * ue)
    o_ref[:, :, D_MODEL:] = glu(yo_ref[...] + d_ref[...] * uo)


def _s5_branch(z3, bmat, cmat, lam_re, lam_im, d_skip, w_glu, b_glu):
    nb, seqlen, _ = z3.shape
    npairs = seqlen // 2
    zp = z3.reshape(nb, npairs, 2 * IN_COLS)
    tp = min(256, npairs)
    const2 = lambda c: (0, 0)
    const3 = lambda c: (0, 0, 0)
    single = pl.Buffered(1)
    out = pl.pallas_call(
        _s5_kernel,
        grid=(npairs // tp,),
        in_specs=[
            pl.BlockSpec((nb, tp, D_MODEL), lambda c: (0, c, 0)),
            pl.BlockSpec((nb, tp, D_MODEL), lambda c: (0, c, IN_COLS // D_MODEL)),
            pl.BlockSpec(bmat.shape, const3, pipeline_mode=single),
            pl.BlockSpec(cmat.shape, const3, pipeline_mode=single),
            pl.BlockSpec(lam_re.shape, const2),
            pl.BlockSpec(lam_im.shape, const2),
            pl.BlockSpec((1, D_MODEL), const2),
            pl.BlockSpec((D_MODEL, D_MODEL), const2, pipeline_mode=single),
            pl.BlockSpec((1, D_MODEL), const2),
        ],
        out_specs=pl.BlockSpec((nb, tp, 2 * D_MODEL), lambda c: (0, c, 0)),
        out_shape=jax.ShapeDtypeStruct((nb, npairs, 2 * D_MODEL), BF16),
        scratch_shapes=[
            pltpu.VMEM((nb, NLB, SSM_SLABS * (tp + SLAB_ROW_PAD), 128), F32),
            pltpu.VMEM((nb, NLB, SSM_SLABS, 128), F32),
            pltpu.VMEM((nb, 1, D_MODEL), F32),
            pltpu.VMEM((nb * tp, D_MODEL), F32),
            pltpu.VMEM((nb * tp, D_MODEL), F32),
            pltpu.VMEM((nb * tp, D_MODEL), F32),
        ],
        compiler_params=_params("arbitrary"),
        name="s5_branch",
    )(zp, zp, bmat, cmat, lam_re, lam_im, d_skip, w_glu, b_glu)
    return out.reshape(nb, seqlen, D_MODEL)


def _s5_matrices(lam_re, lam_im, log_dt, b_re, b_im, c_re, c_im):
    lam = lax.complex(lam_re, lam_im)
    dt = jnp.exp(log_dt)[:, None]
    lam_bar = jnp.exp(lam * dt)
    b_odd = ((lam_bar - 1.0) / lam)[..., None] * lax.complex(b_re, b_im)
    b_even = lam_bar[..., None] * b_odd
    c_cur = lax.complex(c_re, c_im)
    c_next = c_cur * lam_bar[:, None, :]
    k0 = jnp.einsum('ghn,gnk->gkh', c_cur, b_odd).real
    lam2 = lam_bar * lam_bar
    eye = jnp.eye(SLAB_GROUPS, dtype=F32)

    def slabs(x):
        return x.reshape((SSM_SLABS, SLAB_GROUPS) + x.shape[1:])

    def in_mat(part):
        return jnp.einsum('jgnh,gk->jghkn', slabs(part), eye).reshape(SSM_SLABS, 128, SLAB_STATES)

    def out_mat(part):
        return jnp.einsum('jghn,gk->jkngh', slabs(part), eye).reshape(SSM_SLABS, SLAB_STATES, 128)

    k0_mat = jnp.einsum('jgkh,gm->jgkmh', slabs(k0), eye).reshape(SSM_SLABS, 128, 128)
    top = jnp.concatenate([in_mat(b_even.real), in_mat(b_even.imag), k0_mat], axis=2)
    bot = jnp.concatenate([in_mat(b_odd.real), in_mat(b_odd.imag), jnp.zeros_like(k0_mat)], axis=2)
    bmat = jnp.concatenate([top, bot], axis=1).astype(BF16)
    cur = jnp.concatenate([out_mat(c_cur.real), out_mat(-c_cur.imag)], axis=1)
    nxt = jnp.concatenate([out_mat(c_next.real), out_mat(-c_next.imag)], axis=1)
    cmat = jnp.concatenate([cur, nxt], axis=2).astype(BF16)
    lre = lam2.real.reshape(SSM_SLABS, SLAB_STATES)
    lim = lam2.imag.reshape(SSM_SLABS, SLAB_STATES)
    return bmat, cmat, lre, lim


def _dattn_kernel(q_ref, k_ref, v_ref, lmb_ref, hn_ref, o_ref,
                  qm_ref, vx_ref, sx_ref, sy_ref, m_ref, acc_ref, *, lam_init):
    tq = q_ref.shape[0]
    tk = tq
    nstrips = tk // 128
    nheads = q_ref.shape[1] // 128
    qi = pl.program_id(2)
    lane = lax.broadcasted_iota(jnp.int32, (tq, 128), 1)
    nt = (((1,), (1,)), ((), ()))

    @pl.when(qi == 0)
    def _():
        for hh in range(nheads):
            vx_ref[hh, :, :DA_V_DIM] = v_ref[:, hh * 128:(hh + 1) * 128]
            vx_ref[hh, :, DA_V_DIM:] = jnp.ones((v_ref.shape[0], DA_V_DIM), BF16)

    for hh in range(nheads):
        q = q_ref[:, hh * 128:(hh + 1) * 128]
        zero = jnp.zeros_like(q)
        qm_ref[2 * hh] = jnp.where(lane < DA_HEAD_DIM, q, zero)
        qm_ref[2 * hh + 1] = jnp.where(lane >= DA_HEAD_DIM, q, zero)
    m_ref[...] = jnp.full(m_ref.shape, NEG_BIG, F32)
    acc_ref[...] = jnp.zeros(acc_ref.shape, F32)

    def scores(ci, ki, s_ref):
        start = pl.multiple_of(ki * tk, tk)
        hh = ci // 2
        kb = k_ref[pl.ds(start, tk), hh * 128:(hh + 1) * 128]
        s_ref[ci] = lax.dot_general(qm_ref[ci], kb, nt, preferred_element_type=F32)

    def update(ci, ki, s_ref, mask):
        start = pl.multiple_of(ki * tk, tk)
        hh = ci // 2
        vb = vx_ref[hh, pl.ds(start, tk), :]
        s = s_ref[ci]
        if mask is not None:
            s = jnp.where(mask, s, NEG_BIG)
        m_prev = m_ref[ci]
        m_next = jnp.maximum(m_prev, jnp.max(s, axis=1, keepdims=True))
        alpha = jnp.exp2(m_prev - m_next)
        ps = [jnp.exp2(s[:, c * 128:(c + 1) * 128] - m_next) for c in range(nstrips)]
        p = jnp.concatenate(ps, axis=1).astype(BF16)
        acc_ref[ci] = jnp.concatenate([alpha, alpha], axis=1) * acc_ref[ci] + _bdot(p, vb)
        m_ref[ci] = m_next

    def stage(k_next, s_next_ref, k_cur, s_cur_ref, mask):
        for ci in range(2 * nheads):
            if k_next is not None:
                scores(ci, k_next, s_next_ref)
            update(ci, k_cur, s_cur_ref, mask)

    for ci in range(2 * nheads):
        scores(ci, 0, sx_ref)

    def pair(p, carry):
        stage(2 * p + 1, sy_ref, 2 * p, sx_ref, None)
        stage(2 * p + 2, sx_ref, 2 * p + 1, sy_ref, None)
        return carry

    lax.fori_loop(0, qi // 2, pair, 0)
    row = lax.broadcasted_iota(jnp.int32, (tq, tk), 0) // CHUNK
    col = lax.broadcasted_iota(jnp.int32, (tq, tk), 1) // CHUNK
    mask = col <= row

    @pl.when(qi % 2 == 0)
    def _():
        stage(None, None, qi, sx_ref, mask)

    @pl.when(qi % 2 == 1)
    def _():
        stage(qi, sy_ref, qi - 1, sx_ref, None)
        stage(None, None, qi, sy_ref, mask)

    lv = lmb_ref[...]
    lam = (jnp.exp(jnp.sum(lv[0:1] * lv[1:2], axis=1, keepdims=True))
           - jnp.exp(jnp.sum(lv[2:3] * lv[3:4], axis=1, keepdims=True)) + lam_init)
    for hh in range(nheads):
        o = (acc_ref[2 * hh, :, :DA_V_DIM] / acc_ref[2 * hh, :, DA_V_DIM:]
             - lam * (acc_ref[2 * hh + 1, :, :DA_V_DIM] / acc_ref[2 * hh + 1, :, DA_V_DIM:]))
        o = o * lax.rsqrt(jnp.mean(o * o, axis=-1, keepdims=True) + EPS) * hn_ref[...]
        o_ref[:, hh * 128:(hh + 1) * 128] = (o * (1.0 - lam_init)).astype(BF16)


def _diff_attention(z3, lmb, head_norm, lam_init):
    nb, seqlen, _ = z3.shape
    tq = min(512, seqlen)
    hps = ATTN_HEADS_PER_STEP
    width = hps * 128
    qcol, kcol, vcol = (c // width for c in (D_MODEL, 2 * D_MODEL, 3 * D_MODEL))
    return pl.pallas_call(
        functools.partial(_dattn_kernel, lam_init=lam_init),
        grid=(nb, DA_HEADS // hps, seqlen // tq),
        in_specs=[
            pl.BlockSpec((None, tq, width), lambda b, h, i: (b, i, qcol + h)),
            pl.BlockSpec((None, seqlen, width), lambda b, h, i: (b, 0, kcol + h)),
            pl.BlockSpec((None, seqlen, width), lambda b, h, i: (b, 0, vcol + h)),
            pl.BlockSpec((4, DA_HEAD_DIM), lambda b, h, i: (0, 0)),
            pl.BlockSpec((1, DA_V_DIM), lambda b, h, i: (0, 0)),
        ],
        out_specs=pl.BlockSpec((None, tq, width), lambda b, h, i: (b, i, h)),
        out_shape=jax.ShapeDtypeStruct((nb, seqlen, D_MODEL), BF16),
        scratch_shapes=[pltpu.VMEM((2 * hps, tq, 128), BF16),
                        pltpu.VMEM((hps, seqlen, 2 * DA_V_DIM), BF16),
                        pltpu.VMEM((2 * hps, tq, tq), F32), pltpu.VMEM((2 * hps, tq, tq), F32),
                        pltpu.VMEM((2 * hps, tq, 128), F32),
                        pltpu.VMEM((2 * hps, tq, 2 * DA_V_DIM), F32)],
        compiler_params=_params("parallel", "parallel", "arbitrary"),
        name="diff_attention",
    )(z3, z3, z3, lmb, head_norm)


def _merge_kernel(ys_ref, ya_ref, g_ref, x_ref, wa_ref, wb_ref, wo_ref, n_ref, o_ref):
    a = _bdot(ys_ref[...], wa_ref[...])
    b = _bdot(ya_ref[...], wb_ref[...])
    ga = g_ref[:, :D_MODEL].astype(F32)
    gb = g_ref[:, D_MODEL:].astype(F32)
    merged = (ga * a + gb * b).astype(BF16)
    o_ref[...] = x_ref[...] + _rms(_bdot(merged, wo_ref[...]), n_ref[...])


def _merge(ys2, ya2, z2, x2, w_a, w_b, w_o, gain):
    t = x2.shape[0]
    tm = min(512, t)
    row = lambda i: (i, 0)
    const = lambda i: (0, 0)
    wspec = pl.BlockSpec((D_MODEL, D_MODEL), const)
    return pl.pallas_call(
        _merge_kernel,
        grid=(t // tm,),
        in_specs=[
            pl.BlockSpec((tm, D_MODEL), row),
            pl.BlockSpec((tm, D_MODEL), row),
            pl.BlockSpec((tm, 2 * D_MODEL), lambda i: (i, 2)),
            pl.BlockSpec((tm, D_MODEL), row),
            wspec, wspec, wspec,
            pl.BlockSpec((1, D_MODEL), const),
        ],
        out_specs=pl.BlockSpec((tm, D_MODEL), row),
        out_shape=jax.ShapeDtypeStruct((t, D_MODEL), F32),
        compiler_params=_params("parallel"),
        name="merge",
    )(ys2, ya2, z2, x2, w_a, w_b, w_o, gain)


def _memkv_kernel(m_ref, g_ref, w_ref, o_ref):
    o_ref[...] = _bdot(_rms(m_ref[...], g_ref[...]).astype(BF16), w_ref[...]).astype(BF16)


def _memory_kv(mem2, gain, w_xkv):
    rows = mem2.shape[0]
    const = lambda i: (0, 0)
    return pl.pallas_call(
        _memkv_kernel,
        grid=(1,),
        in_specs=[
            pl.BlockSpec((rows, D_MODEL), const),
            pl.BlockSpec((1, D_MODEL), const),
            pl.BlockSpec((D_MODEL, 2 * D_MODEL), const),
        ],
        out_specs=pl.BlockSpec((rows, 2 * D_MODEL), const),
        out_shape=jax.ShapeDtypeStruct((rows, 2 * D_MODEL), BF16),
        compiler_params=_params("arbitrary"),
        name="memory_kv",
    )(mem2, gain, w_xkv)


def _xattn_kernel(x_ref, kv_ref, npre_ref, wq_ref, wo_ref, npost_ref, o_ref, oh_ref):
    x = x_ref[...]
    h = _rms(x, npre_ref[...]).astype(BF16)
    q = (_bdot(h, wq_ref[...]) * (XA_HEAD_DIM ** -0.5 * LOG2E)).astype(BF16)
    nt = (((1,), (1,)), ((), ()))
    for hd in range(XA_HEADS):
        lo = hd * XA_HEAD_DIM
        k = kv_ref[:, lo:lo + XA_HEAD_DIM]
        v = kv_ref[:, D_MODEL + lo:D_MODEL + lo + XA_HEAD_DIM]
        s = lax.dot_general(q[:, lo:lo + XA_HEAD_DIM], k, nt, preferred_element_type=F32)
        p = jnp.exp2(s - jnp.max(s, axis=1, keepdims=True))
        o = _bdot(p.astype(BF16), v) / jnp.sum(p, axis=1, keepdims=True)
        oh_ref[:, lo:lo + XA_HEAD_DIM] = o.astype(BF16)
    o_ref[...] = x + _rms(_bdot(oh_ref[...], wo_ref[...]), npost_ref[...])


def _cross_attention(x3, kv3, n_pre, w_xq, w_xo, n_post):
    nb, seqlen, _ = x3.shape
    mlen = kv3.shape[1]
    tm = min(512, seqlen)
    const = lambda b, i: (0, 0)
    wspec = pl.BlockSpec((D_MODEL, D_MODEL), const)
    vspec = pl.BlockSpec((1, D_MODEL), const)
    return pl.pallas_call(
        _xattn_kernel,
        grid=(nb, seqlen // tm),
        in_specs=[
            pl.BlockSpec((None, tm, D_MODEL), lambda b, i: (b, i, 0)),
            pl.BlockSpec((None, mlen, 2 * D_MODEL), lambda b, i: (b, 0, 0)),
            vspec, wspec, wspec, vspec,
        ],
        out_specs=pl.BlockSpec((None, tm, D_MODEL), lambda b, i: (b, i, 0)),
        out_shape=jax.ShapeDtypeStruct((nb, seqlen, D_MODEL), F32),
        scratch_shapes=[pltpu.VMEM((tm, D_MODEL), BF16)],
        compiler_params=_params("parallel", "parallel"),
        name="cross_attention",
    )(x3, kv3, n_pre, w_xq, w_xo, n_post)


def _mlp_kernel(x_ref, npre_ref, w1_ref, w2_ref, npost_ref, o_ref):
    x = x_ref[...]
    h = _rms(x, npre_ref[...]).astype(BF16)
    f = jnp.zeros(x.shape, F32)
    for c in range(D_FF // D_MODEL):
        lo = c * D_MODEL
        a = jnp.maximum(_bdot(h, w1_ref[:, lo:lo + D_MODEL]), 0.0)
        f = f + _bdot((a * a).astype(BF16), w2_ref[lo:lo + D_MODEL, :])
    o_ref[...] = x + _rms(f, npost_ref[...])


def _mlp(x2, n_pre, w1, w2, n_post):
    t = x2.shape[0]
    tm = min(512, t)
    row = lambda i: (i, 0)
    const = lambda i: (0, 0)
    vspec = pl.BlockSpec((1, D_MODEL), const)
    return pl.pallas_call(
        _mlp_kernel,
        grid=(t // tm,),
        in_specs=[
            pl.BlockSpec((tm, D_MODEL), row),
            vspec,
            pl.BlockSpec((D_MODEL, D_FF), const),
            pl.BlockSpec((D_FF, D_MODEL), const),
            vspec,
        ],
        out_specs=pl.BlockSpec((tm, D_MODEL), row),
        out_shape=jax.ShapeDtypeStruct((t, D_MODEL), F32),
        compiler_params=_params("parallel"),
        name="mlp",
    )(x2, n_pre, w1, w2, n_post)


def _rope_tables(seqlen):
    inv = ROPE_THETA ** (-jnp.arange(0, DA_HEAD_DIM, 2, dtype=F32) / DA_HEAD_DIM)
    ang = jnp.arange(seqlen, dtype=F32)[:, None] * inv[None, :]
    cos = jnp.tile(jnp.cos(ang), (1, 4))
    sin = jnp.tile(jnp.concatenate([-jnp.sin(ang), jnp.sin(ang)], axis=1), (1, 2))
    qs = DA_HEAD_DIM ** -0.5 * LOG2E
    return cos * qs, sin * qs, cos, sin


def kernel(x, mem, norm_mix_pre, w_in, b_gate, ssm_lambda_re, ssm_lambda_im, ssm_log_dt, ssm_b_re, ssm_b_im, ssm_c_re, ssm_c_im, ssm_d, w_glu, b_glu, w_ssm_proj, da_lambda_q1, da_lambda_k1, da_lambda_q2, da_lambda_k2, da_head_norm, w_da_proj, w_mix_out, norm_mix_post, norm_x_pre, norm_mem, w_xq, w_xkv, w_xo, norm_x_post, norm_ff_pre, w_ff1, w_ff2, norm_ff_post):
    nb, seqlen, _ = x.shape
    t = nb * seqlen
    depth = w_in.shape[0]
    tabs = _rope_tables(seqlen)
    x2 = x.reshape(t, D_MODEL)
    mem2 = mem.reshape(nb * mem.shape[1], D_MODEL)
    for l in range(depth):
        lam_init = 0.8 - 0.6 * math.exp(-0.3 * l)
        z2 = _in_projection(x2, norm_mix_pre[l][None], w_in[l].astype(BF16), b_gate[l][None], tabs, seqlen)
        z3 = z2.reshape(nb, seqlen, IN_COLS)
        bmat, cmat, lre, lim = _s5_matrices(ssm_lambda_re[l], ssm_lambda_im[l], ssm_log_dt[l],
                                            ssm_b_re[l], ssm_b_im[l], ssm_c_re[l], ssm_c_im[l])
        ys = _s5_branch(z3, bmat, cmat, lre, lim, ssm_d[l][None], w_glu[l].astype(BF16), b_glu[l][None])
        lmb = jnp.stack([da_lambda_q1[l], da_lambda_k1[l], da_lambda_q2[l], da_lambda_k2[l]])
        ya = _diff_attention(z3, lmb, da_head_norm[l][None], lam_init)
        x2 = _merge(ys.reshape(t, D_MODEL), ya.reshape(t, D_MODEL), z2, x2,
                    w_ssm_proj[l].astype(BF16), w_da_proj[l].astype(BF16), w_mix_out[l].astype(BF16),
                    norm_mix_post[l][None])
        kv = _memory_kv(mem2, norm_mem[l][None], w_xkv[l].astype(BF16))
        x3 = _cross_attention(x2.reshape(nb, seqlen, D_MODEL), kv.reshape(nb, -1, 2 * D_MODEL),
                              norm_x_pre[l][None], w_xq[l].astype(BF16), w_xo[l].astype(BF16),
                              norm_x_post[l][None])
        x2 = _mlp(x3.reshape(t, D_MODEL), norm_ff_pre[l][None], w_ff1[l].astype(BF16),
                  w_ff2[l].astype(BF16), norm_ff_post[l][None])
    return x2.reshape(nb, seqlen, D_MODEL)
```

```python
import functools
import math

import jax
import jax.numpy as jnp
from jax import lax
from jax.experimental import pallas as pl
from jax.experimental.pallas import tpu as pltpu

F32 = jnp.float32
BF16 = jnp.bfloat16

D_MODEL = 1024
EPS = 1e-6
CHUNK = 64
SSM_GROUP = 16
SSM_GROUPS = 64
SSM_STATE = 64
SSM_SLABS = 8
SLAB_GROUPS = SSM_GROUPS // SSM_SLABS
SLAB_STATES = SLAB_GROUPS * SSM_STATE
NLB = 2 * SLAB_STATES // 128
SLAB_ROW_PAD = 8
DA_HEADS = 8
DA_HEAD_DIM = 64
DA_V_DIM = 128
ROPE_THETA = 10000.0
ATTN_HEADS_PER_STEP = 2
XA_HEADS = 4
XA_HEAD_DIM = 256
D_FF = 4096
IN_COLS = 6144
LOG2E = 1.4426950408889634
NEG_BIG = -1e30
VMEM_LIMIT_BYTES = 56 * 1024 * 1024


def _params(*semantics):
    return pltpu.CompilerParams(dimension_semantics=semantics, vmem_limit_bytes=VMEM_LIMIT_BYTES)


def _rms(x, g):
    return x * lax.rsqrt(jnp.mean(x * x, axis=-1, keepdims=True) + EPS) * g


def _bdot(a, b):
    return jnp.dot(a, b, preferred_element_type=F32)


def _sigmoid(x):
    return 0.5 * jnp.tanh(0.5 * x) + 0.5


def _inproj_kernel(x_ref, g_ref, w_ref, bg_ref, cq_ref, sq_ref, ck_ref, sk_ref, o_ref):
    tm = x_ref.shape[0]
    hn = _rms(x_ref[...], g_ref[...]).astype(BF16)
    lane = lax.broadcasted_iota(jnp.int32, (tm, 128), 1)
    first_half = (lane % 64) < 32

    def rope(acc, c_ref, s_ref, col0):
        c = c_ref[...]
        s = s_ref[...]
        for h in range(DA_HEADS):
            blk = acc[:, h * 128:(h + 1) * 128]
            partner = jnp.where(first_half, pltpu.roll(blk, 96, 1), pltpu.roll(blk, 32, 1))
            o_ref[:, col0 + h * 128:col0 + (h + 1) * 128] = (blk * c + partner * s).astype(BF16)

    for j in range(IN_COLS // D_MODEL):
        col0 = j * D_MODEL
        acc = _bdot(hn, w_ref[:, col0:col0 + D_MODEL])
        if j == 1:
            rope(acc, cq_ref, sq_ref, col0)
        elif j == 2:
            rope(acc, ck_ref, sk_ref, col0)
        elif j >= 4:
            gate = _sigmoid(acc + bg_ref[:, col0 - 4 * D_MODEL:col0 - 3 * D_MODEL])
            o_ref[:, col0:col0 + D_MODEL] = gate.astype(BF16)
        else:
            o_ref[:, col0:col0 + D_MODEL] = acc.astype(BF16)


def _in_projection(x2, gain, w_in, b_gate, tabs, seqlen):
    t = x2.shape[0]
    tm = min(512, seqlen)
    nl = seqlen // tm
    const = lambda i: (0, 0)
    tab_spec = pl.BlockSpec((tm, 128), lambda i: (i % nl, 0))
    return pl.pallas_call(
        _inproj_kernel,
        grid=(t // tm,),
        in_specs=[
            pl.BlockSpec((tm, D_MODEL), lambda i: (i, 0)),
            pl.BlockSpec((1, D_MODEL), const),
            pl.BlockSpec((D_MODEL, IN_COLS), const, pipeline_mode=pl.Buffered(1)),
            pl.BlockSpec((1, 2 * D_MODEL), const),
            tab_spec, tab_spec, tab_spec, tab_spec,
        ],
        out_specs=pl.BlockSpec((tm, IN_COLS), lambda i: (i, 0)),
        out_shape=jax.ShapeDtypeStruct((t, IN_COLS), BF16),
        compiler_params=_params("parallel"),
        name="in_projection",
    )(x2, gain, w_in, b_gate, *tabs)


def _s5_kernel(u_ref, bm_ref, cm_ref, lre_ref, lim_ref, d_ref, wg_ref, bgl_ref, o_ref,
               bu_ref, st_ref, wc_ref, il_ref, ue_ref, uo_ref, yo_ref, w_ref, k0_ref):
    nb, tc, _ = u_ref.shape
    tp = tc // 2
    pitch = bu_ref.shape[2] // SSM_SLABS
    c = pl.program_id(0)

    @pl.when(c == 0)
    def _():
        st_ref[...] = jnp.zeros(st_ref.shape, F32)
        wc_ref[...] = jnp.zeros(wc_ref.shape, F32)

    for b in range(nb):
        for j in range(SSM_SLABS):
            il_ref[j] = u_ref[b, :, j * 128:(j + 1) * 128].astype(F32)
        for j in range(SSM_SLABS):
            u_e = il_ref[j, pl.ds(0, tp, stride=2), :]
            u_o = il_ref[j, pl.ds(1, tp, stride=2), :]
            ue_ref[b * tp:(b + 1) * tp, j * 128:(j + 1) * 128] = u_e
            uo_ref[b * tp:(b + 1) * tp, j * 128:(j + 1) * 128] = u_o
            lhs = jnp.concatenate([u_e, u_o], axis=1).astype(BF16)
            bu = _bdot(lhs, bm_ref[j])
            for k in range(NLB):
                bu_ref[b, k, j * pitch:j * pitch + tp, :] = bu[:, k * 128:(k + 1) * 128]
            k0_ref[b * tp:(b + 1) * tp, j * 128:(j + 1) * 128] = bu[:, NLB * 128:]

    ar = [lre_ref[:, k * 128:(k + 1) * 128] for k in range(NLB // 2)]
    ai = [lim_ref[:, k * 128:(k + 1) * 128] for k in range(NLB // 2)]

    def step(t, carry):
        out = []
        for b in range(nb):
            xs = carry[b]
            new = [None] * NLB
            for k in range(NLB // 2):
                xr, xi = xs[k], xs[k + NLB // 2]
                rows = pl.ds(t, SSM_SLABS, stride=pitch)
                new[k] = ar[k] * xr - ai[k] * xi + bu_ref[b, k, rows, :]
                new[k + NLB // 2] = ar[k] * xi + ai[k] * xr + bu_ref[b, k + NLB // 2, rows, :]
                bu_ref[b, k, rows, :] = new[k]
                bu_ref[b, k + NLB // 2, rows, :] = new[k + NLB // 2]
            out.append(tuple(new))
        return tuple(out)

    init = tuple(tuple(st_ref[b, k] for k in range(NLB)) for b in range(nb))
    fin = lax.fori_loop(0, tp, step, init, unroll=8)
    for b in range(nb):
        for k in range(NLB):
            st_ref[b, k] = fin[b][k]

    for b in range(nb):
        for j in range(SSM_SLABS):
            xs = jnp.concatenate([bu_ref[b, k, j * pitch:j * pitch + tp, :] for k in range(NLB)], axis=1)
            r = _bdot(xs.astype(BF16), cm_ref[j])
            yo_ref[b * tp:(b + 1) * tp, j * 128:(j + 1) * 128] = r[:, :128]
            w_ref[b * tp:(b + 1) * tp, j * 128:(j + 1) * 128] = r[:, 128:]

    def glu(y):
        ys = 0.5 * y * (1.0 + jnp.tanh(math.sqrt(2.0 / math.pi) * (y + 0.044715 * (y * y * y))))
        gate = _sigmoid(_bdot(ys.astype(BF16), wg_ref[...]) + bgl_ref[...])
        return ys * gate

    first_row = lax.broadcasted_iota(jnp.int32, (tp, D_MODEL), 0) == 0
    shifted = []
    for b in range(nb):
        w = w_ref[b * tp:(b + 1) * tp, :]
        shifted.append(jnp.where(first_row, wc_ref[b], pltpu.roll(w, 1, 0)))
        wc_ref[b] = w[tp - 1:tp, :]
    yo_ref[...] = glu(yo_ref[...] + d_ref[...] * uo_ref[...])
    w_ref[...] = glu(jnp.concatenate(shifted, axis=0) + k0_ref[...] + d_ref[...] * ue_ref[...])
    for b in range(nb):
        for j in range(SSM_SLABS):
            il_ref[j, pl.ds(0, tp, stride=2), :] = w_ref[b * tp:(b + 1) * tp, j * 128:(j + 1) * 128]
            il_ref[j, pl.ds(1, tp, stride=2), :] = yo_ref[b * tp:(b + 1) * tp, j * 128:(j + 1) * 128]
        for j in range(SSM_SLABS):
            o_ref[b, :, j * 128:(j + 1) * 128] = il_ref[j].astype(BF16)


def _s5_branch(z3, bmat, cmat, lam_re, lam_im, d_skip, w_glu, b_glu):
    nb, seqlen, _ = z3.shape
    tc = min(512, seqlen)
    tp = tc // 2
    const2 = lambda c: (0, 0)
    const3 = lambda c: (0, 0, 0)
    single = pl.Buffered(1)
    return pl.pallas_call(
        _s5_kernel,
        grid=(seqlen // tc,),
        in_specs=[
            pl.BlockSpec((nb, tc, D_MODEL), lambda c: (0, c, 0)),
            pl.BlockSpec(bmat.shape, const3, pipeline_mode=single),
            pl.BlockSpec(cmat.shape, const3, pipeline_mode=single),
            pl.BlockSpec(lam_re.shape, const2),
            pl.BlockSpec(lam_im.shape, const2),
            pl.BlockSpec((1, D_MODEL), const2),
            pl.BlockSpec((D_MODEL, D_MODEL), const2, pipeline_mode=single),
            pl.BlockSpec((1, D_MODEL), const2),
        ],
        out_specs=pl.BlockSpec((nb, tc, D_MODEL), lambda c: (0, c, 0)),
        out_shape=jax.ShapeDtypeStruct((nb, seqlen, D_MODEL), BF16),
        scratch_shapes=[
            pltpu.VMEM((nb, NLB, SSM_SLABS * (tp + SLAB_ROW_PAD), 128), F32),
            pltpu.VMEM((nb, NLB, SSM_SLABS, 128), F32),
            pltpu.VMEM((nb, 1, D_MODEL), F32),
            pltpu.VMEM((SSM_SLABS, tc, 128), F32),
        ] + [pltpu.VMEM((nb * tp, D_MODEL), F32)] * 5,
        compiler_params=_params("arbitrary"),
        name="s5_branch",
    )(z3, bmat, cmat, lam_re, lam_im, d_skip, w_glu, b_glu)


def _s5_matrices(lam_re, lam_im, log_dt, b_re, b_im, c_re, c_im):
    lam = lax.complex(lam_re, lam_im)
    dt = jnp.exp(log_dt)[:, None]
    lam_bar = jnp.exp(lam * dt)
    b_odd = ((lam_bar - 1.0) / lam)[..., None] * lax.complex(b_re, b_im)
    b_even = lam_bar[..., None] * b_odd
    c_cur = lax.complex(c_re, c_im)
    c_next = c_cur * lam_bar[:, None, :]
    k0 = jnp.einsum('ghn,gnk->gkh', c_cur, b_odd).real
    lam2 = lam_bar * lam_bar
    eye = jnp.eye(SLAB_GROUPS, dtype=F32)

    def slabs(x):
        return x.reshape((SSM_SLABS, SLAB_GROUPS) + x.shape[1:])

    def in_mat(part):
        return jnp.einsum('jgnh,gk->jghkn', slabs(part), eye).reshape(SSM_SLABS, 128, SLAB_STATES)

    def out_mat(part):
        return jnp.einsum('jghn,gk->jkngh', slabs(part), eye).reshape(SSM_SLABS, SLAB_STATES, 128)

    k0_mat = jnp.einsum('jgkh,gm->jgkmh', slabs(k0), eye).reshape(SSM_SLABS, 128, 128)
    top = jnp.concatenate([in_mat(b_even.real), in_mat(b_even.imag), k0_mat], axis=2)
    bot = jnp.concatenate([in_mat(b_odd.real), in_mat(b_odd.imag), jnp.zeros_like(k0_mat)], axis=2)
    bmat = jnp.concatenate([top, bot], axis=1).astype(BF16)
    cur = jnp.concatenate([out_mat(c_cur.real), out_mat(-c_cur.imag)], axis=1)
    nxt = jnp.concatenate([out_mat(c_next.real), out_mat(-c_next.imag)], axis=1)
    cmat = jnp.concatenate([cur, nxt], axis=2).astype(BF16)
    lre = lam2.real.reshape(SSM_SLABS, SLAB_STATES)
    lim = lam2.imag.reshape(SSM_SLABS, SLAB_STATES)
    return bmat, cmat, lre, lim


def _dattn_kernel(q_ref, k_ref, v_ref, lmb_ref, hn_ref, o_ref,
                  qm_ref, vx_ref, sx_ref, sy_ref, m_ref, acc_ref, *, lam_init):
    tq = q_ref.shape[0]
    tk = tq
    nstrips = tk // 128
    nheads = q_ref.shape[1] // 128
    qi = pl.program_id(2)
    lane = lax.broadcasted_iota(jnp.int32, (tq, 128), 1)
    nt = (((1,), (1,)), ((), ()))

    @pl.when(qi == 0)
    def _():
        for hh in range(nheads):
            vx_ref[hh, :, :DA_V_DIM] = v_ref[:, hh * 128:(hh + 1) * 128]
            vx_ref[hh, :, DA_V_DIM:] = jnp.ones((v_ref.shape[0], DA_V_DIM), BF16)

    for hh in range(nheads):
        q = q_ref[:, hh * 128:(hh + 1) * 128]
        zero = jnp.zeros_like(q)
        qm_ref[2 * hh] = jnp.where(lane < DA_HEAD_DIM, q, zero)
        qm_ref[2 * hh + 1] = jnp.where(lane >= DA_HEAD_DIM, q, zero)
    m_ref[...] = jnp.full(m_ref.shape, NEG_BIG, F32)
    acc_ref[...] = jnp.zeros(acc_ref.shape, F32)

    def scores(ci, ki, s_ref):
        start = pl.multiple_of(ki * tk, tk)
        hh = ci // 2
        kb = k_ref[pl.ds(start, tk), hh * 128:(hh + 1) * 128]
        s_ref[ci] = lax.dot_general(qm_ref[ci], kb, nt, preferred_element_type=F32)

    def update(ci, ki, s_ref, mask):
        start = pl.multiple_of(ki * tk, tk)
        hh = ci // 2
        vb = vx_ref[hh, pl.ds(start, tk), :]
        s = s_ref[ci]
        if mask is not None:
            s = jnp.where(mask, s, NEG_BIG)
        m_prev = m_ref[ci]
        m_next = jnp.maximum(m_prev, jnp.max(s, axis=1, keepdims=True))
        alpha = jnp.exp2(m_prev - m_next)
        ps = [jnp.exp2(s[:, c * 128:(c + 1) * 128] - m_next) for c in range(nstrips)]
        p = jnp.concatenate(ps, axis=1).astype(BF16)
        acc_ref[ci] = jnp.concatenate([alpha, alpha], axis=1) * acc_ref[ci] + _bdot(p, vb)
        m_ref[ci] = m_next

    def stage(k_next, s_next_ref, k_cur, s_cur_ref, mask):
        for ci in range(2 * nheads):
            if k_next is not None:
                scores(ci, k_next, s_next_ref)
            update(ci, k_cur, s_cur_ref, mask)

    for ci in range(2 * nheads):
        scores(ci, 0, sx_ref)

    def pair(p, carry):
        stage(2 * p + 1, sy_ref, 2 * p, sx_ref, None)
        stage(2 * p + 2, sx_ref, 2 * p + 1, sy_ref, None)
        return carry

    lax.fori_loop(0, qi // 2, pair, 0)
    row = lax.broadcasted_iota(jnp.int32, (tq, tk), 0) // CHUNK
    col = lax.broadcasted_iota(jnp.int32, (tq, tk), 1) // CHUNK
    mask = col <= row

    @pl.when(qi % 2 == 0)
    def _():
        stage(None, None, qi, sx_ref, mask)

    @pl.when(qi % 2 == 1)
    def _():
        stage(qi, sy_ref, qi - 1, sx_ref, None)
        stage(None, None, qi, sy_ref, mask)

    lv = lmb_ref[...]
    lam = (jnp.exp(jnp.sum(lv[0:1] * lv[1:2], axis=1, keepdims=True))
           - jnp.exp(jnp.sum(lv[2:3] * lv[3:4], axis=1, keepdims=True)) + lam_init)
    for hh in range(nheads):
        o = (acc_ref[2 * hh, :, :DA_V_DIM] / acc_ref[2 * hh, :, DA_V_DIM:]
             - lam * (acc_ref[2 * hh + 1, :, :DA_V_DIM] / acc_ref[2 * hh + 1, :, DA_V_DIM:]))
        o = o * lax.rsqrt(jnp.mean(o * o, axis=-1, keepdims=True) + EPS) * hn_ref[...]
        o_ref[:, hh * 128:(hh + 1) * 128] = (o * (1.0 - lam_init)).astype(BF16)


def _diff_attention(z3, lmb, head_norm, lam_init):
    nb, seqlen, _ = z3.shape
    tq = min(512, seqlen)
    hps = ATTN_HEADS_PER_STEP
    width = hps * 128
    qcol, kcol, vcol = (c // width for c in (D_MODEL, 2 * D_MODEL, 3 * D_MODEL))
    return pl.pallas_call(
        functools.partial(_dattn_kernel, lam_init=lam_init),
        grid=(nb, DA_HEADS // hps, seqlen // tq),
        in_specs=[
            pl.BlockSpec((None, tq, width), lambda b, h, i: (b, i, qcol + h)),
            pl.BlockSpec((None, seqlen, width), lambda b, h, i: (b, 0, kcol + h)),
            pl.BlockSpec((None, seqlen, width), lambda b, h, i: (b, 0, vcol + h)),
            pl.BlockSpec((4, DA_HEAD_DIM), lambda b, h, i: (0, 0)),
            pl.BlockSpec((1, DA_V_DIM), lambda b, h, i: (0, 0)),
        ],
        out_specs=pl.BlockSpec((None, tq, width), lambda b, h, i: (b, i, h)),
        out_shape=jax.ShapeDtypeStruct((nb, seqlen, D_MODEL), BF16),
        scratch_shapes=[pltpu.VMEM((2 * hps, tq, 128), BF16),
                        pltpu.VMEM((hps, seqlen, 2 * DA_V_DIM), BF16),
                        pltpu.VMEM((2 * hps, tq, tq), F32), pltpu.VMEM((2 * hps, tq, tq), F32),
                        pltpu.VMEM((2 * hps, tq, 128), F32),
                        pltpu.VMEM((2 * hps, tq, 2 * DA_V_DIM), F32)],
        compiler_params=_params("parallel", "parallel", "arbitrary"),
        name="diff_attention",
    )(z3, z3, z3, lmb, head_norm)


def _merge_kernel(ys_ref, ya_ref, g_ref, x_ref, wa_ref, wb_ref, wo_ref, n_ref, o_ref):
    a = _bdot(ys_ref[...], wa_ref[...])
    b = _bdot(ya_ref[...], wb_ref[...])
    ga = g_ref[:, :D_MODEL].astype(F32)
    gb = g_ref[:, D_MODEL:].astype(F32)
    merged = (ga * a + gb * b).astype(BF16)
    o_ref[...] = x_ref[...] + _rms(_bdot(merged, wo_ref[...]), n_ref[...])


def _merge(ys2, ya2, z2, x2, w_a, w_b, w_o, gain):
    t = x2.shape[0]
    tm = min(512, t)
    row = lambda i: (i, 0)
    const = lambda i: (0, 0)
    wspec = pl.BlockSpec((D_MODEL, D_MODEL), const)
    return pl.pallas_call(
        _merge_kernel,
        grid=(t // tm,),
        in_specs=[
            pl.BlockSpec((tm, D_MODEL), row),
            pl.BlockSpec((tm, D_MODEL), row),
            pl.BlockSpec((tm, 2 * D_MODEL), lambda i: (i, 2)),
            pl.BlockSpec((tm, D_MODEL), row),
            wspec, wspec, wspec,
            pl.BlockSpec((1, D_MODEL), const),
        ],
        out_specs=pl.BlockSpec((tm, D_MODEL), row),
        out_shape=jax.ShapeDtypeStruct((t, D_MODEL), F32),
        compiler_params=_params("parallel"),
        name="merge",
    )(ys2, ya2, z2, x2, w_a, w_b, w_o, gain)


def _memkv_kernel(m_ref, g_ref, w_ref, o_ref):
    o_ref[...] = _bdot(_rms(m_ref[...], g_ref[...]).astype(BF16), w_ref[...]).astype(BF16)


def _memory_kv(mem2, gain, w_xkv):
    rows = mem2.shape[0]
    const = lambda i: (0, 0)
    return pl.pallas_call(
        _memkv_kernel,
        grid=(1,),
        in_specs=[
            pl.BlockSpec((rows, D_MODEL), const),
            pl.BlockSpec((1, D_MODEL), const),
            pl.BlockSpec((D_MODEL, 2 * D_MODEL), const),
        ],
        out_specs=pl.BlockSpec((rows, 2 * D_MODEL), const),
        out_shape=jax.ShapeDtypeStruct((rows, 2 * D_MODEL), BF16),
        compiler_params=_params("arbitrary"),
        name="memory_kv",
    )(mem2, gain, w_xkv)


def _xattn_kernel(x_ref, kv_ref, npre_ref, wq_ref, wo_ref, npost_ref, o_ref, oh_ref):
    x = x_ref[...]
    h = _rms(x, npre_ref[...]).astype(BF16)
    q = (_bdot(h, wq_ref[...]) * (XA_HEAD_DIM ** -0.5 * LOG2E)).astype(BF16)
    nt = (((1,), (1,)), ((), ()))
    for hd in range(XA_HEADS):
        lo = hd * XA_HEAD_DIM
        k = kv_ref[:, lo:lo + XA_HEAD_DIM]
        v = kv_ref[:, D_MODEL + lo:D_MODEL + lo + XA_HEAD_DIM]
        s = lax.dot_general(q[:, lo:lo + XA_HEAD_DIM], k, nt, preferred_element_type=F32)
        p = jnp.exp2(s - jnp.max(s, axis=1, keepdims=True))
        o = _bdot(p.astype(BF16), v) / jnp.sum(p, axis=1, keepdims=True)
        oh_ref[:, lo:lo + XA_HEAD_DIM] = o.astype(BF16)
    o_ref[...] = x + _rms(_bdot(oh_ref[...], wo_ref[...]), npost_ref[...])


def _cross_attention(x3, kv3, n_pre, w_xq, w_xo, n_post):
    nb, seqlen, _ = x3.shape
    mlen = kv3.shape[1]
    tm = min(512, seqlen)
    const = lambda b, i: (0, 0)
    wspec = pl.BlockSpec((D_MODEL, D_MODEL), const)
    vspec = pl.BlockSpec((1, D_MODEL), const)
    return pl.pallas_call(
        _xattn_kernel,
        grid=(nb, seqlen // tm),
        in_specs=[
            pl.BlockSpec((None, tm, D_MODEL), lambda b, i: (b, i, 0)),
            pl.BlockSpec((None, mlen, 2 * D_MODEL), lambda b, i: (b, 0, 0)),
            vspec, wspec, wspec, vspec,
        ],
        out_specs=pl.BlockSpec((None, tm, D_MODEL), lambda b, i: (b, i, 0)),
        out_shape=jax.ShapeDtypeStruct((nb, seqlen, D_MODEL), F32),
        scratch_shapes=[pltpu.VMEM((tm, D_MODEL), BF16)],
        compiler_params=_params("parallel", "parallel"),
        name="cross_attention",
    )(x3, kv3, n_pre, w_xq, w_xo, n_post)


def _mlp_kernel(x_ref, npre_ref, w1_ref, w2_ref, npost_ref, o_ref):
    x = x_ref[...]
    h = _rms(x, npre_ref[...]).astype(BF16)
    f = jnp.zeros(x.shape, F32)
    for c in range(D_FF // D_MODEL):
        lo = c * D_MODEL
        a = jnp.maximum(_bdot(h, w1_ref[:, lo:lo + D_MODEL]), 0.0)
        f = f + _bdot((a * a).astype(BF16), w2_ref[lo:lo + D_MODEL, :])
    o_ref[...] = x + _rms(f, npost_ref[...])


def _mlp(x2, n_pre, w1, w2, n_post):
    t = x2.shape[0]
    tm = min(512, t)
    row = lambda i: (i, 0)
    const = lambda i: (0, 0)
    vspec = pl.BlockSpec((1, D_MODEL), const)
    return pl.pallas_call(
        _mlp_kernel,
        grid=(t // tm,),
        in_specs=[
            pl.BlockSpec((tm, D_MODEL), row),
            vspec,
            pl.BlockSpec((D_MODEL, D_FF), const),
            pl.BlockSpec((D_FF, D_MODEL), const),
            vspec,
        ],
        out_specs=pl.BlockSpec((tm, D_MODEL), row),
        out_shape=jax.ShapeDtypeStruct((t, D_MODEL), F32),
        compiler_params=_params("parallel"),
        name="mlp",
    )(x2, n_pre, w1, w2, n_post)


def _rope_tables(seqlen):
    inv = ROPE_THETA ** (-jnp.arange(0, DA_HEAD_DIM, 2, dtype=F32) / DA_HEAD_DIM)
    ang = jnp.arange(seqlen, dtype=F32)[:, None] * inv[None, :]
    cos = jnp.tile(jnp.cos(ang), (1, 4))
    sin = jnp.tile(jnp.concatenate([-jnp.sin(ang), jnp.sin(ang)], axis=1), (1, 2))
    qs = DA_HEAD_DIM ** -0.5 * LOG2E
    return cos * qs, sin * qs, cos, sin


def kernel(x, mem, norm_mix_pre, w_in, b_gate, ssm_lambda_re, ssm_lambda_im, ssm_log_dt, ssm_b_re, ssm_b_im, ssm_c_re, ssm_c_im, ssm_d, w_glu, b_glu, w_ssm_proj, da_lambda_q1, da_lambda_k1, da_lambda_q2, da_lambda_k2, da_head_norm, w_da_proj, w_mix_out, norm_mix_post, norm_x_pre, norm_mem, w_xq, w_xkv, w_xo, norm_x_post, norm_ff_pre, w_ff1, w_ff2, norm_ff_post):
    nb, seqlen, _ = x.shape
    t = nb * seqlen
    depth = w_in.shape[0]
    tabs = _rope_tables(seqlen)
    x2 = x.reshape(t, D_MODEL)
    mem2 = mem.reshape(nb * mem.shape[1], D_MODEL)
    for l in range(depth):
        lam_init = 0.8 - 0.6 * math.exp(-0.3 * l)
        z2 = _in_projection(x2, norm_mix_pre[l][None], w_in[l].astype(BF16), b_gate[l][None], tabs, seqlen)
        z3 = z2.reshape(nb, seqlen, IN_COLS)
        bmat, cmat, lre, lim = _s5_matrices(ssm_lambda_re[l], ssm_lambda_im[l], ssm_log_dt[l],
                                            ssm_b_re[l], ssm_b_im[l], ssm_c_re[l], ssm_c_im[l])
        ys = _s5_branch(z3, bmat, cmat, lre, lim, ssm_d[l][None], w_glu[l].astype(BF16), b_glu[l][None])
        lmb = jnp.stack([da_lambda_q1[l], da_lambda_k1[l], da_lambda_q2[l], da_lambda_k2[l]])
        ya = _diff_attention(z3, lmb, da_head_norm[l][None], lam_init)
        x2 = _merge(ys.reshape(t, D_MODEL), ya.reshape(t, D_MODEL), z2, x2,
                    w_ssm_proj[l].astype(BF16), w_da_proj[l].astype(BF16), w_mix_out[l].astype(BF16),
                    norm_mix_post[l][None])
        kv = _memory_kv(mem2, norm_mem[l][None], w_xkv[l].astype(BF16))
        x3 = _cross_attention(x2.reshape(nb, seqlen, D_MODEL), kv.reshape(nb, -1, 2 * D_MODEL),
                              norm_x_pre[l][None], w_xq[l].astype(BF16), w_xo[l].astype(BF16),
                              norm_x_post[l][None])
        x2 = _mlp(x3.reshape(t, D_MODEL), norm_ff_pre[l][None], w_ff1[l].astype(BF16),
                  w_ff2[l].astype(BF16), norm_ff_post[l][None])
    return x2.reshape(nb, seqlen, D_MODEL)
```

```python
import functools
import math

import jax
import jax.numpy as jnp
from jax import lax
from jax.experimental import pallas as pl
from jax.experimental.pallas import tpu as pltpu

F32 = jnp.float32
BF16 = jnp.bfloat16

D_MODEL = 1024
EPS = 1e-6
CHUNK = 64
SSM_GROUP = 16
SSM_GROUPS = 64
SSM_STATE = 64
SSM_SLABS = 8
SLAB_GROUPS = SSM_GROUPS // SSM_SLABS
SLAB_STATES = SLAB_GROUPS * SSM_STATE
NLB = 2 * SLAB_STATES // 128
SLAB_ROW_PAD = 8
DA_HEADS = 8
DA_HEAD_DIM = 64
DA_V_DIM = 128
ROPE_THETA = 10000.0
ATTN_HEADS_PER_STEP = 2
XA_HEADS = 4
XA_HEAD_DIM = 256
D_FF = 4096
IN_COLS = 6144
LOG2E = 1.4426950408889634
NEG_BIG = -1e30
VMEM_LIMIT_BYTES = 56 * 1024 * 1024


def _params(*semantics):
    return pltpu.CompilerParams(dimension_semantics=semantics, vmem_limit_bytes=VMEM_LIMIT_BYTES)


def _rms(x, g):
    return x * lax.rsqrt(jnp.mean(x * x, axis=-1, keepdims=True) + EPS) * g


def _bdot(a, b):
    return jnp.dot(a, b, preferred_element_type=F32)


def _sigmoid(x):
    return 0.5 * jnp.tanh(0.5 * x) + 0.5


def _inproj_kernel(x_ref, g_ref, w_ref, bg_ref, cq_ref, sq_ref, ck_ref, sk_ref, o_ref):
    tm = x_ref.shape[0]
    hn = _rms(x_ref[...], g_ref[...]).astype(BF16)
    lane = lax.broadcasted_iota(jnp.int32, (tm, 128), 1)
    first_half = (lane % 64) < 32

    def rope(acc, c_ref, s_ref, col0):
        c = c_ref[...]
        s = s_ref[...]
        for h in range(DA_HEADS):
            blk = acc[:, h * 128:(h + 1) * 128]
            partner = jnp.where(first_half, pltpu.roll(blk, 96, 1), pltpu.roll(blk, 32, 1))
            o_ref[:, col0 + h * 128:col0 + (h + 1) * 128] = (blk * c + partner * s).astype(BF16)

    for j in range(IN_COLS // D_MODEL):
        col0 = j * D_MODEL
        acc = _bdot(hn, w_ref[:, col0:col0 + D_MODEL])
        if j == 1:
            rope(acc, cq_ref, sq_ref, col0)
        elif j == 2:
            rope(acc, ck_ref, sk_ref, col0)
        elif j >= 4:
            gate = _sigmoid(acc + bg_ref[:, col0 - 4 * D_MODEL:col0 - 3 * D_MODEL])
            o_ref[:, col0:col0 + D_MODEL] = gate.astype(BF16)
        else:
            o_ref[:, col0:col0 + D_MODEL] = acc.astype(BF16)


def _in_projection(x2, gain, w_in, b_gate, tabs, seqlen):
    t = x2.shape[0]
    tm = min(512, seqlen)
    nl = seqlen // tm
    const = lambda i: (0, 0)
    tab_spec = pl.BlockSpec((tm, 128), lambda i: (i % nl, 0))
    return pl.pallas_call(
        _inproj_kernel,
        grid=(t // tm,),
        in_specs=[
            pl.BlockSpec((tm, D_MODEL), lambda i: (i, 0)),
            pl.BlockSpec((1, D_MODEL), const),
            pl.BlockSpec((D_MODEL, IN_COLS), const, pipeline_mode=pl.Buffered(1)),
            pl.BlockSpec((1, 2 * D_MODEL), const),
            tab_spec, tab_spec, tab_spec, tab_spec,
        ],
        out_specs=pl.BlockSpec((tm, IN_COLS), lambda i: (i, 0)),
        out_shape=jax.ShapeDtypeStruct((t, IN_COLS), BF16),
        compiler_params=_params("parallel"),
        name="in_projection",
    )(x2, gain, w_in, b_gate, *tabs)


def _s5_kernel(u_ref, bm_ref, cm_ref, lre_ref, lim_ref, d_ref, wg_ref, bgl_ref, o_ref,
               bu_ref, st_ref, wc_ref, il_ref, ue_ref, uo_ref, yo_ref, w_ref, k0_ref):
    nb, tc, _ = u_ref.shape
    tp = tc // 2
    pitch = bu_ref.shape[2] // SSM_SLABS
    c = pl.program_id(0)

    @pl.when(c == 0)
    def _():
        st_ref[...] = jnp.zeros(st_ref.shape, F32)
        wc_ref[...] = jnp.zeros(wc_ref.shape, F32)

    for b in range(nb):
        for j in range(SSM_SLABS):
            il_ref[j] = u_ref[b, :, j * 128:(j + 1) * 128].astype(F32)
        for j in range(SSM_SLABS):
            u_e = il_ref[j, pl.ds(0, tp, stride=2), :]
            u_o = il_ref[j, pl.ds(1, tp, stride=2), :]
            ue_ref[b * tp:(b + 1) * tp, j * 128:(j + 1) * 128] = u_e
            uo_ref[b * tp:(b + 1) * tp, j * 128:(j + 1) * 128] = u_o
            lhs = jnp.concatenate([u_e, u_o], axis=1).astype(BF16)
            bu = _bdot(lhs, bm_ref[j])
            for k in range(NLB):
                bu_ref[b, k, j * pitch:j * pitch + tp, :] = bu[:, k * 128:(k + 1) * 128]
            k0_ref[b * tp:(b + 1) * tp, j * 128:(j + 1) * 128] = bu[:, NLB * 128:]

    ar = [lre_ref[:, k * 128:(k + 1) * 128] for k in range(NLB // 2)]
    ai = [lim_ref[:, k * 128:(k + 1) * 128] for k in range(NLB // 2)]

    def step(t, carry):
        out = []
        for b in range(nb):
            xs = carry[b]
            new = [None] * NLB
            for k in range(NLB // 2):
                xr, xi = xs[k], xs[k + NLB // 2]
                rows = pl.ds(t, SSM_SLABS, stride=pitch)
                new[k] = ar[k] * xr - ai[k] * xi + bu_ref[b, k, rows, :]
                new[k + NLB // 2] = ar[k] * xi + ai[k] * xr + bu_ref[b, k + NLB // 2, rows, :]
                bu_ref[b, k, rows, :] = new[k]
                bu_ref[b, k + NLB // 2, rows, :] = new[k + NLB // 2]
            out.append(tuple(new))
        return tuple(out)

    init = tuple(tuple(st_ref[b, k] for k in range(NLB)) for b in range(nb))
    fin = lax.fori_loop(0, tp, step, init, unroll=8)
    for b in range(nb):
        for k in range(NLB):
            st_ref[b, k] = fin[b][k]

    for b in range(nb):
        for j in range(SSM_SLABS):
            xs = jnp.concatenate([bu_ref[b, k, j * pitch:j * pitch + tp, :] for k in range(NLB)], axis=1)
            r = _bdot(xs.astype(BF16), cm_ref[j])
            yo_ref[b * tp:(b + 1) * tp, j * 128:(j + 1) * 128] = r[:, :128]
            w_ref[b * tp:(b + 1) * tp, j * 128:(j + 1) * 128] = r[:, 128:]

    def glu(y):
        ys = 0.5 * y * (1.0 + jnp.tanh(math.sqrt(2.0 / math.pi) * (y + 0.044715 * (y * y * y))))
        gate = _sigmoid(_bdot(ys.astype(BF16), wg_ref[...]) + bgl_ref[...])
        return ys * gate

    first_row = lax.broadcasted_iota(jnp.int32, (tp, D_MODEL), 0) == 0
    shifted = []
    for b in range(nb):
        w = w_ref[b * tp:(b + 1) * tp, :]
        shifted.append(jnp.where(first_row, wc_ref[b], pltpu.roll(w, 1, 0)))
        wc_ref[b] = w[tp - 1:tp, :]
    yo_ref[...] = glu(yo_ref[...] + d_ref[...] * uo_ref[...])
    w_ref[...] = glu(jnp.concatenate(shifted, axis=0) + k0_ref[...] + d_ref[...] * ue_ref[...])
    for b in range(nb):
        for j in range(SSM_SLABS):
            il_ref[j, pl.ds(0, tp, stride=2), :] = w_ref[b * tp:(b + 1) * tp, j * 128:(j + 1) * 128]
            il_ref[j, pl.ds(1, tp, stride=2), :] = yo_ref[b * tp:(b + 1) * tp, j * 128:(j + 1) * 128]
        for j in range(SSM_SLABS):
            o_ref[b, :, j * 128:(j + 1) * 128] = il_ref[j].astype(BF16)


def _s5_branch(z3, bmat, cmat, lam_re, lam_im, d_skip, w_glu, b_glu):
    nb, seqlen, _ = z3.shape
    tc = min(512, seqlen)
    tp = tc // 2
    const2 = lambda c: (0, 0)
    const3 = lambda c: (0, 0, 0)
    single = pl.Buffered(1)
    return pl.pallas_call(
        _s5_kernel,
        grid=(seqlen // tc,),
        in_specs=[
            pl.BlockSpec((nb, tc, D_MODEL), lambda c: (0, c, 0)),
            pl.BlockSpec(bmat.shape, const3, pipeline_mode=single),
            pl.BlockSpec(cmat.shape, const3, pipeline_mode=single),
            pl.BlockSpec(lam_re.shape, const2),
            pl.BlockSpec(lam_im.shape, const2),
            pl.BlockSpec((1, D_MODEL), const2),
            pl.BlockSpec((D_MODEL, D_MODEL), const2, pipeline_mode=single),
            pl.BlockSpec((1, D_MODEL), const2),
        ],
        out_specs=pl.BlockSpec((nb, tc, D_MODEL), lambda c: (0, c, 0)),
        out_shape=jax.ShapeDtypeStruct((nb, seqlen, D_MODEL), BF16),
        scratch_shapes=[
            pltpu.VMEM((nb, NLB, SSM_SLABS * (tp + SLAB_ROW_PAD), 128), F32),
            pltpu.VMEM((nb, NLB, SSM_SLABS, 128), F32),
            pltpu.VMEM((nb, 1, D_MODEL), F32),
            pltpu.VMEM((SSM_SLABS, tc, 128), F32),
        ] + [pltpu.VMEM((nb * tp, D_MODEL), F32)] * 5,
        compiler_params=_params("arbitrary"),
        name="s5_branch",
    )(z3, bmat, cmat, lam_re, lam_im, d_skip, w_glu, b_glu)


def _s5_matrices(lam_re, lam_im, log_dt, b_re, b_im, c_re, c_im):
    lam = lax.complex(lam_re, lam_im)
    dt = jnp.exp(log_dt)[:, None]
    lam_bar = jnp.exp(lam * dt)
    b_odd = ((lam_bar - 1.0) / lam)[..., None] * lax.complex(b_re, b_im)
    b_even = lam_bar[..., None] * b_odd
    c_cur = lax.complex(c_re, c_im)
    c_next = c_cur * lam_bar[:, None, :]
    k0 = jnp.einsum('ghn,gnk->gkh', c_cur, b_odd).real
    lam2 = lam_bar * lam_bar
    eye = jnp.eye(SLAB_GROUPS, dtype=F32)

    def slabs(x):
        return x.reshape((SSM_SLABS, SLAB_GROUPS) + x.shape[1:])

    def in_mat(part):
        p = slabs(part).transpose(0, 1, 3, 2)[:, :, :, None, :]
        return (p * eye[None, :, None, :, None]).reshape(SSM_SLABS, 128, SLAB_STATES)

    def out_mat(part):
        p = slabs(part).transpose(0, 3, 1, 2)[:, None, :, :, :]
        return (p * eye[None, :, None, :, None]).reshape(SSM_SLABS, SLAB_STATES, 128)

    k0_mat = (slabs(k0)[:, :, :, None, :] * eye[None, :, None, :, None]).reshape(SSM_SLABS, 128, 128)
    top = jnp.concatenate([in_mat(b_even.real), in_mat(b_even.imag), k0_mat], axis=2)
    bot = jnp.concatenate([in_mat(b_odd.real), in_mat(b_odd.imag), jnp.zeros_like(k0_mat)], axis=2)
    bmat = jnp.concatenate([top, bot], axis=1).astype(BF16)
    cur = jnp.concatenate([out_mat(c_cur.real), out_mat(-c_cur.imag)], axis=1)
    nxt = jnp.concatenate([out_mat(c_next.real), out_mat(-c_next.imag)], axis=1)
    cmat = jnp.concatenate([cur, nxt], axis=2).astype(BF16)
    lre = lam2.real.reshape(SSM_SLABS, SLAB_STATES)
    lim = lam2.imag.reshape(SSM_SLABS, SLAB_STATES)
    return bmat, cmat, lre, lim


def _dattn_kernel(q_ref, k_ref, v_ref, lmb_ref, hn_ref, o_ref,
                  qm_ref, vx_ref, sx_ref, sy_ref, m_ref, acc_ref, *, lam_init):
    tq = q_ref.shape[0]
    tk = tq
    nstrips = tk // 128
    nheads = q_ref.shape[1] // 128
    qi = pl.program_id(2)
    lane = lax.broadcasted_iota(jnp.int32, (tq, 128), 1)
    nt = (((1,), (1,)), ((), ()))

    @pl.when(qi == 0)
    def _():
        for hh in range(nheads):
            vx_ref[hh, :, :DA_V_DIM] = v_ref[:, hh * 128:(hh + 1) * 128]
            vx_ref[hh, :, DA_V_DIM:] = jnp.ones((v_ref.shape[0], DA_V_DIM), BF16)

    for hh in range(nheads):
        q = q_ref[:, hh * 128:(hh + 1) * 128]
        zero = jnp.zeros_like(q)
        qm_ref[2 * hh] = jnp.where(lane < DA_HEAD_DIM, q, zero)
        qm_ref[2 * hh + 1] = jnp.where(lane >= DA_HEAD_DIM, q, zero)
    m_ref[...] = jnp.full(m_ref.shape, NEG_BIG, F32)
    acc_ref[...] = jnp.zeros(acc_ref.shape, F32)

    def scores(ci, ki, s_ref):
        start = pl.multiple_of(ki * tk, tk)
        hh = ci // 2
        kb = k_ref[pl.ds(start, tk), hh * 128:(hh + 1) * 128]
        s_ref[ci] = lax.dot_general(qm_ref[ci], kb, nt, preferred_element_type=F32)

    def update(ci, ki, s_ref):
        start = pl.multiple_of(ki * tk, tk)
        hh = ci // 2
        vb = vx_ref[hh, pl.ds(start, tk), :]
        s = s_ref[ci]
        m_prev = m_ref[ci]
        m_next = jnp.maximum(m_prev, jnp.max(s, axis=1, keepdims=True))
        alpha = jnp.exp2(m_prev - m_next)
        ps = [jnp.exp2(s[:, c * 128:(c + 1) * 128] - m_next) for c in range(nstrips)]
        p = jnp.concatenate(ps, axis=1).astype(BF16)
        acc_ref[ci] = jnp.concatenate([alpha, alpha], axis=1) * acc_ref[ci] + _bdot(p, vb)
        m_ref[ci] = m_next

    def update_diag(ci, ki, s_ref):
        half = tq // 2
        start = pl.multiple_of(ki * tk, tk)
        hh = ci // 2
        row = lax.broadcasted_iota(jnp.int32, (half, half), 0) // CHUNK
        col = lax.broadcasted_iota(jnp.int32, (half, half), 1) // CHUNK
        visible = col <= row
        for part in range(2):
            rows = slice(part * half, (part + 1) * half)
            ncols = half * (part + 1)
            strips = []
            for c in range(ncols // 128):
                s = s_ref[ci, rows, c * 128:(c + 1) * 128]
                off = c * 128 - part * half
                if off >= 0:
                    s = jnp.where(visible[:, off:off + 128], s, NEG_BIG)
                strips.append(s)
            m_prev = m_ref[ci, rows, :]
            m_cur = functools.reduce(jnp.maximum, strips)
            m_next = jnp.maximum(m_prev, jnp.max(m_cur, axis=1, keepdims=True))
            alpha = jnp.exp2(m_prev - m_next)
            p = jnp.concatenate([jnp.exp2(s - m_next) for s in strips], axis=1).astype(BF16)
            pv = _bdot(p, vx_ref[hh, pl.ds(start, ncols), :])
            acc_ref[ci, rows, :] = jnp.concatenate([alpha, alpha], axis=1) * acc_ref[ci, rows, :] + pv
            m_ref[ci, rows, :] = m_next

    def stage(k_next, s_next_ref, k_cur, s_cur_ref, diag=False):
        for ci in range(2 * nheads):
            if k_next is not None:
                scores(ci, k_next, s_next_ref)
            if diag:
                update_diag(ci, k_cur, s_cur_ref)
            else:
                update(ci, k_cur, s_cur_ref)

    for ci in range(2 * nheads):
        scores(ci, 0, sx_ref)

    def pair(p, carry):
        stage(2 * p + 1, sy_ref, 2 * p, sx_ref)
        stage(2 * p + 2, sx_ref, 2 * p + 1, sy_ref)
        return carry

    lax.fori_loop(0, qi // 2, pair, 0)

    @pl.when(qi % 2 == 0)
    def _():
        stage(None, None, qi, sx_ref, diag=True)

    @pl.when(qi % 2 == 1)
    def _():
        stage(qi, sy_ref, qi - 1, sx_ref)
        stage(None, None, qi, sy_ref, diag=True)

    lv = lmb_ref[...]
    lam = (jnp.exp(jnp.sum(lv[0:1] * lv[1:2], axis=1, keepdims=True))
           - jnp.exp(jnp.sum(lv[2:3] * lv[3:4], axis=1, keepdims=True)) + lam_init)
    for hh in range(nheads):
        o = (acc_ref[2 * hh, :, :DA_V_DIM] / acc_ref[2 * hh, :, DA_V_DIM:]
             - lam * (acc_ref[2 * hh + 1, :, :DA_V_DIM] / acc_ref[2 * hh + 1, :, DA_V_DIM:]))
        o = o * lax.rsqrt(jnp.mean(o * o, axis=-1, keepdims=True) + EPS) * hn_ref[...]
        o_ref[:, hh * 128:(hh + 1) * 128] = (o * (1.0 - lam_init)).astype(BF16)


def _diff_attention(z3, lmb, head_norm, lam_init):
    nb, seqlen, _ = z3.shape
    tq = min(512, seqlen)
    hps = ATTN_HEADS_PER_STEP
    width = hps * 128
    qcol, kcol, vcol = (c // width for c in (D_MODEL, 2 * D_MODEL, 3 * D_MODEL))
    return pl.pallas_call(
        functools.partial(_dattn_kernel, lam_init=lam_init),
        grid=(nb, DA_HEADS // hps, seqlen // tq),
        in_specs=[
            pl.BlockSpec((None, tq, width), lambda b, h, i: (b, i, qcol + h)),
            pl.BlockSpec((None, seqlen, width), lambda b, h, i: (b, 0, kcol + h)),
            pl.BlockSpec((None, seqlen, width), lambda b, h, i: (b, 0, vcol + h)),
            pl.BlockSpec((4, DA_HEAD_DIM), lambda b, h, i: (0, 0)),
            pl.BlockSpec((1, DA_V_DIM), lambda b, h, i: (0, 0)),
        ],
        out_specs=pl.BlockSpec((None, tq, width), lambda b, h, i: (b, i, h)),
        out_shape=jax.ShapeDtypeStruct((nb, seqlen, D_MODEL), BF16),
        scratch_shapes=[pltpu.VMEM((2 * hps, tq, 128), BF16),
                        pltpu.VMEM((hps, seqlen, 2 * DA_V_DIM), BF16),
                        pltpu.VMEM((2 * hps, tq, tq), F32), pltpu.VMEM((2 * hps, tq, tq), F32),
                        pltpu.VMEM((2 * hps, tq, 128), F32),
                        pltpu.VMEM((2 * hps, tq, 2 * DA_V_DIM), F32)],
        compiler_params=_params("parallel", "parallel", "arbitrary"),
        name="diff_attention",
    )(z3, z3, z3, lmb, head_norm)


def _merge_kernel(ys_ref, ya_ref, g_ref, x_ref, wa_ref, wb_ref, wo_ref, n_ref, o_ref):
    a = _bdot(ys_ref[...], wa_ref[...])
    b = _bdot(ya_ref[...], wb_ref[...])
    ga = g_ref[:, :D_MODEL].astype(F32)
    gb = g_ref[:, D_MODEL:].astype(F32)
    merged = (ga * a + gb * b).astype(BF16)
    o_ref[...] = x_ref[...] + _rms(_bdot(merged, wo_ref[...]), n_ref[...])


def _merge(ys2, ya2, z2, x2, w_a, w_b, w_o, gain):
    t = x2.shape[0]
    tm = min(1024, t)
    row = lambda i: (i, 0)
    const = lambda i: (0, 0)
    wspec = pl.BlockSpec((D_MODEL, D_MODEL), const)
    return pl.pallas_call(
        _merge_kernel,
        grid=(t // tm,),
        in_specs=[
            pl.BlockSpec((tm, D_MODEL), row),
            pl.BlockSpec((tm, D_MODEL), row),
            pl.BlockSpec((tm, 2 * D_MODEL), lambda i: (i, 2)),
            pl.BlockSpec((tm, D_MODEL), row),
            wspec, wspec, wspec,
            pl.BlockSpec((1, D_MODEL), const),
        ],
        out_specs=pl.BlockSpec((tm, D_MODEL), row),
        out_shape=jax.ShapeDtypeStruct((t, D_MODEL), F32),
        compiler_params=_params("parallel"),
        name="merge",
    )(ys2, ya2, z2, x2, w_a, w_b, w_o, gain)


def _memkv_kernel(m_ref, g_ref, w_ref, o_ref):
    o_ref[...] = _bdot(_rms(m_ref[...], g_ref[...]).astype(BF16), w_ref[...]).astype(BF16)


def _memory_kv(mem2, gain, w_xkv):
    rows = mem2.shape[0]
    const = lambda i: (0, 0)
    return pl.pallas_call(
        _memkv_kernel,
        grid=(1,),
        in_specs=[
            pl.BlockSpec((rows, D_MODEL), const),
            pl.BlockSpec((1, D_MODEL), const),
            pl.BlockSpec((D_MODEL, 2 * D_MODEL), const),
        ],
        out_specs=pl.BlockSpec((rows, 2 * D_MODEL), const),
        out_shape=jax.ShapeDtypeStruct((rows, 2 * D_MODEL), BF16),
        compiler_params=_params("arbitrary"),
        name="memory_kv",
    )(mem2, gain, w_xkv)


def _xattn_kernel(x_ref, kv_ref, npre_ref, wq_ref, wo_ref, npost_ref, o_ref, oh_ref):
    x = x_ref[...]
    h = _rms(x, npre_ref[...]).astype(BF16)
    q = (_bdot(h, wq_ref[...]) * (XA_HEAD_DIM ** -0.5 * LOG2E)).astype(BF16)
    nt = (((1,), (1,)), ((), ()))
    for hd in range(XA_HEADS):
        lo = hd * XA_HEAD_DIM
        k = kv_ref[:, lo:lo + XA_HEAD_DIM]
        v = kv_ref[:, D_MODEL + lo:D_MODEL + lo + XA_HEAD_DIM]
        s = lax.dot_general(q[:, lo:lo + XA_HEAD_DIM], k, nt, preferred_element_type=F32)
        p = jnp.exp2(s - jnp.max(s, axis=1, keepdims=True))
        o = _bdot(p.astype(BF16), v) / jnp.sum(p, axis=1, keepdims=True)
        oh_ref[:, lo:lo + XA_HEAD_DIM] = o.astype(BF16)
    o_ref[...] = x + _rms(_bdot(oh_ref[...], wo_ref[...]), npost_ref[...])


def _cross_attention(x3, kv3, n_pre, w_xq, w_xo, n_post):
    nb, seqlen, _ = x3.shape
    mlen = kv3.shape[1]
    tm = min(1024, seqlen)
    const = lambda b, i: (0, 0)
    wspec = pl.BlockSpec((D_MODEL, D_MODEL), const)
    vspec = pl.BlockSpec((1, D_MODEL), const)
    return pl.pallas_call(
        _xattn_kernel,
        grid=(nb, seqlen // tm),
        in_specs=[
            pl.BlockSpec((None, tm, D_MODEL), lambda b, i: (b, i, 0)),
            pl.BlockSpec((None, mlen, 2 * D_MODEL), lambda b, i: (b, 0, 0)),
            vspec, wspec, wspec, vspec,
        ],
        out_specs=pl.BlockSpec((None, tm, D_MODEL), lambda b, i: (b, i, 0)),
        out_shape=jax.ShapeDtypeStruct((nb, seqlen, D_MODEL), F32),
        scratch_shapes=[pltpu.VMEM((tm, D_MODEL), BF16)],
        compiler_params=_params("parallel", "parallel"),
        name="cross_attention",
    )(x3, kv3, n_pre, w_xq, w_xo, n_post)


def _mlp_kernel(x_ref, npre_ref, w1_ref, w2_ref, npost_ref, o_ref):
    x = x_ref[...]
    h = _rms(x, npre_ref[...]).astype(BF16)
    f = jnp.zeros(x.shape, F32)
    for c in range(D_FF // D_MODEL):
        lo = c * D_MODEL
        a = jnp.maximum(_bdot(h, w1_ref[:, lo:lo + D_MODEL]), 0.0)
        f = f + _bdot((a * a).astype(BF16), w2_ref[lo:lo + D_MODEL, :])
    o_ref[...] = x + _rms(f, npost_ref[...])


def _mlp(x2, n_pre, w1, w2, n_post):
    t = x2.shape[0]
    tm = min(512, t)
    row = lambda i: (i, 0)
    const = lambda i: (0, 0)
    vspec = pl.BlockSpec((1, D_MODEL), const)
    return pl.pallas_call(
        _mlp_kernel,
        grid=(t // tm,),
        in_specs=[
            pl.BlockSpec((tm, D_MODEL), row),
            vspec,
            pl.BlockSpec((D_MODEL, D_FF), const),
            pl.BlockSpec((D_FF, D_MODEL), const),
            vspec,
        ],
        out_specs=pl.BlockSpec((tm, D_MODEL), row),
        out_shape=jax.ShapeDtypeStruct((t, D_MODEL), F32),
        compiler_params=_params("parallel"),
        name="mlp",
    )(x2, n_pre, w1, w2, n_post)


def _rope_tables(seqlen):
    inv = ROPE_THETA ** (-jnp.arange(0, DA_HEAD_DIM, 2, dtype=F32) / DA_HEAD_DIM)
    ang = jnp.arange(seqlen, dtype=F32)[:, None] * inv[None, :]
    cos = jnp.tile(jnp.cos(ang), (1, 4))
    sin = jnp.tile(jnp.concatenate([-jnp.sin(ang), jnp.sin(ang)], axis=1), (1, 2))
    qs = DA_HEAD_DIM ** -0.5 * LOG2E
    return cos * qs, sin * qs, cos, sin


def kernel(x, mem, norm_mix_pre, w_in, b_gate, ssm_lambda_re, ssm_lambda_im, ssm_log_dt, ssm_b_re, ssm_b_im, ssm_c_re, ssm_c_im, ssm_d, w_glu, b_glu, w_ssm_proj, da_lambda_q1, da_lambda_k1, da_lambda_q2, da_lambda_k2, da_head_norm, w_da_proj, w_mix_out, norm_mix_post, norm_x_pre, norm_mem, w_xq, w_xkv, w_xo, norm_x_post, norm_ff_pre, w_ff1, w_ff2, norm_ff_post):
    nb, seqlen, _ = x.shape
    t = nb * seqlen
    depth = w_in.shape[0]
    tabs = _rope_tables(seqlen)
    x2 = x.reshape(t, D_MODEL)
    mem2 = mem.reshape(nb * mem.shape[1], D_MODEL)
    for l in range(depth):
        lam_init = 0.8 - 0.6 * math.exp(-0.3 * l)
        z2 = _in_projection(x2, norm_mix_pre[l][None], w_in[l].astype(BF16), b_gate[l][None], tabs, seqlen)
        z3 = z2.reshape(nb, seqlen, IN_COLS)
        bmat, cmat, lre, lim = _s5_matrices(ssm_lambda_re[l], ssm_lambda_im[l], ssm_log_dt[l],
                                            ssm_b_re[l], ssm_b_im[l], ssm_c_re[l], ssm_c_im[l])
        ys = _s5_branch(z3, bmat, cmat, lre, lim, ssm_d[l][None], w_glu[l].astype(BF16), b_glu[l][None])
        lmb = jnp.stack([da_lambda_q1[l], da_lambda_k1[l], da_lambda_q2[l], da_lambda_k2[l]])
        ya = _diff_attention(z3, lmb, da_head_norm[l][None], lam_init)
        x2 = _merge(ys.reshape(t, D_MODEL), ya.reshape(t, D_MODEL), z2, x2,
                    w_ssm_proj[l].astype(BF16), w_da_proj[l].astype(BF16), w_mix_out[l].astype(BF16),
                    norm_mix_post[l][None])
        kv = _memory_kv(mem2, norm_mem[l][None], w_xkv[l].astype(BF16))
        x3 = _cross_attention(x2.reshape(nb, seqlen, D_MODEL), kv.reshape(nb, -1, 2 * D_MODEL),
                              norm_x_pre[l][None], w_xq[l].astype(BF16), w_xo[l].astype(BF16),
                              norm_x_post[l][None])
        x2 = _mlp(x3.reshape(t, D_MODEL), norm_ff_pre[l][None], w_ff1[l].astype(BF16),
                  w_ff2[l].astype(BF16), norm_ff_post[l][None])
    return x2.reshape(nb, seqlen, D_MODEL)
```

```python
import functools
import math

import jax
import jax.numpy as jnp
from jax import lax
from jax.experimental import pallas as pl
from jax.experimental.pallas import tpu as pltpu

F32 = jnp.float32
BF16 = jnp.bfloat16

D_MODEL = 1024
EPS = 1e-6
CHUNK = 64
SSM_GROUP = 16
SSM_GROUPS = 64
SSM_STATE = 64
SSM_SLABS = 8
SLAB_GROUPS = SSM_GROUPS // SSM_SLABS
SLAB_STATES = SLAB_GROUPS * SSM_STATE
NLB = 2 * SLAB_STATES // 128
SLAB_ROW_PAD = 8
DA_HEADS = 8
DA_HEAD_DIM = 64
DA_V_DIM = 128
ROPE_THETA = 10000.0
ATTN_HEADS_PER_STEP = 2
XA_HEADS = 4
XA_HEAD_DIM = 256
D_FF = 4096
IN_COLS = 6144
LOG2E = 1.4426950408889634
NEG_BIG = -1e30
VMEM_LIMIT_BYTES = 56 * 1024 * 1024


def _params(*semantics):
    return pltpu.CompilerParams(dimension_semantics=semantics, vmem_limit_bytes=VMEM_LIMIT_BYTES)


def _rms(x, g):
    return x * lax.rsqrt(jnp.mean(x * x, axis=-1, keepdims=True) + EPS) * g


def _bdot(a, b):
    return jnp.dot(a, b, preferred_element_type=F32)


def _sigmoid(x):
    return 0.5 * jnp.tanh(0.5 * x) + 0.5


def _inproj_kernel(x_ref, g_ref, w_ref, bg_ref, cos_ref, sin_ref, o_ref):
    tm = x_ref.shape[0]
    hn = _rms(x_ref[...], g_ref[...]).astype(BF16)
    lane = lax.broadcasted_iota(jnp.int32, (tm, 128), 1)
    first_half = (lane % 64) < 32

    def rope(acc, col0, scale):
        c = cos_ref[...] * scale
        s = sin_ref[...] * scale
        for h in range(DA_HEADS):
            blk = acc[:, h * 128:(h + 1) * 128]
            partner = jnp.where(first_half, pltpu.roll(blk, 96, 1), pltpu.roll(blk, 32, 1))
            o_ref[:, col0 + h * 128:col0 + (h + 1) * 128] = (blk * c + partner * s).astype(BF16)

    for j in range(IN_COLS // D_MODEL):
        col0 = j * D_MODEL
        acc = _bdot(hn, w_ref[:, col0:col0 + D_MODEL])
        if j == 1:
            rope(acc, col0, DA_HEAD_DIM ** -0.5 * LOG2E)
        elif j == 2:
            rope(acc, col0, 1.0)
        elif j >= 4:
            gate = _sigmoid(acc + bg_ref[:, col0 - 4 * D_MODEL:col0 - 3 * D_MODEL])
            o_ref[:, col0:col0 + D_MODEL] = gate.astype(BF16)
        else:
            o_ref[:, col0:col0 + D_MODEL] = acc.astype(BF16)


def _in_projection(x2, gain, w_in, b_gate, tabs, seqlen):
    t = x2.shape[0]
    tm = min(512, seqlen)
    nl = seqlen // tm
    const = lambda i: (0, 0)
    tab_spec = pl.BlockSpec((tm, 128), lambda i: (i % nl, 0))
    return pl.pallas_call(
        _inproj_kernel,
        grid=(t // tm,),
        in_specs=[
            pl.BlockSpec((tm, D_MODEL), lambda i: (i, 0)),
            pl.BlockSpec((1, D_MODEL), const),
            pl.BlockSpec((D_MODEL, IN_COLS), const, pipeline_mode=pl.Buffered(1)),
            pl.BlockSpec((1, 2 * D_MODEL), const),
            tab_spec, tab_spec,
        ],
        out_specs=pl.BlockSpec((tm, IN_COLS), lambda i: (i, 0)),
        out_shape=jax.ShapeDtypeStruct((t, IN_COLS), BF16),
        compiler_params=_params("parallel"),
        name="in_projection",
    )(x2, gain, w_in, b_gate, *tabs)


def _s5_kernel(u_ref, bm_ref, cm_ref, lre_ref, lim_ref, d_ref, wg_ref, bgl_ref, o_ref,
               bu_ref, st_ref, wc_ref, il_ref, ue_ref, uo_ref, yo_ref, w_ref, k0_ref):
    nb, tc, _ = u_ref.shape
    tp = tc // 2
    pitch = bu_ref.shape[2] // SSM_SLABS
    c = pl.program_id(0)

    @pl.when(c == 0)
    def _():
        st_ref[...] = jnp.zeros(st_ref.shape, F32)
        wc_ref[...] = jnp.zeros(wc_ref.shape, F32)

    for b in range(nb):
        for j in range(SSM_SLABS):
            il_ref[j] = u_ref[b, :, j * 128:(j + 1) * 128].astype(F32)
        for j in range(SSM_SLABS):
            u_e = il_ref[j, pl.ds(0, tp, stride=2), :]
            u_o = il_ref[j, pl.ds(1, tp, stride=2), :]
            ue_ref[b * tp:(b + 1) * tp, j * 128:(j + 1) * 128] = u_e
            uo_ref[b * tp:(b + 1) * tp, j * 128:(j + 1) * 128] = u_o
            lhs = jnp.concatenate([u_e, u_o], axis=1).astype(BF16)
            bu = _bdot(lhs, bm_ref[j])
            for k in range(NLB):
                bu_ref[b, k, j * pitch:j * pitch + tp, :] = bu[:, k * 128:(k + 1) * 128]
            k0_ref[b * tp:(b + 1) * tp, j * 128:(j + 1) * 128] = bu[:, NLB * 128:]

    ar = [lre_ref[:, k * 128:(k + 1) * 128] for k in range(NLB // 2)]
    ai = [lim_ref[:, k * 128:(k + 1) * 128] for k in range(NLB // 2)]

    def step(t, carry):
        out = []
        for b in range(nb):
            xs = carry[b]
            new = [None] * NLB
            for k in range(NLB // 2):
                xr, xi = xs[k], xs[k + NLB // 2]
                rows = pl.ds(t, SSM_SLABS, stride=pitch)
                new[k] = ar[k] * xr - ai[k] * xi + bu_ref[b, k, rows, :]
                new[k + NLB // 2] = ar[k] * xi + ai[k] * xr + bu_ref[b, k + NLB // 2, rows, :]
                bu_ref[b, k, rows, :] = new[k]
                bu_ref[b, k + NLB // 2, rows, :] = new[k + NLB // 2]
            out.append(tuple(new))
        return tuple(out)

    init = tuple(tuple(st_ref[b, k] for k in range(NLB)) for b in range(nb))
    fin = lax.fori_loop(0, tp, step, init, unroll=8)
    for b in range(nb):
        for k in range(NLB):
            st_ref[b, k] = fin[b][k]

    for b in range(nb):
        for j in range(SSM_SLABS):
            xs = jnp.concatenate([bu_ref[b, k, j * pitch:j * pitch + tp, :] for k in range(NLB)], axis=1)
            r = _bdot(xs.astype(BF16), cm_ref[j])
            yo_ref[b * tp:(b + 1) * tp, j * 128:(j + 1) * 128] = r[:, :128]
            w_ref[b * tp:(b + 1) * tp, j * 128:(j + 1) * 128] = r[:, 128:]

    def glu(y):
        ys = 0.5 * y * (1.0 + jnp.tanh(math.sqrt(2.0 / math.pi) * (y + 0.044715 * (y * y * y))))
        gate = _sigmoid(_bdot(ys.astype(BF16), wg_ref[...]) + bgl_ref[...])
        return ys * gate

    first_row = lax.broadcasted_iota(jnp.int32, (tp, D_MODEL), 0) == 0
    shifted = []
    for b in range(nb):
        w = w_ref[b * tp:(b + 1) * tp, :]
        shifted.append(jnp.where(first_row, wc_ref[b], pltpu.roll(w, 1, 0)))
        wc_ref[b] = w[tp - 1:tp, :]
    yo_ref[...] = glu(yo_ref[...] + d_ref[...] * uo_ref[...])
    w_ref[...] = glu(jnp.concatenate(shifted, axis=0) + k0_ref[...] + d_ref[...] * ue_ref[...])
    for b in range(nb):
        for j in range(SSM_SLABS):
            il_ref[j, pl.ds(0, tp, stride=2), :] = w_ref[b * tp:(b + 1) * tp, j * 128:(j + 1) * 128]
            il_ref[j, pl.ds(1, tp, stride=2), :] = yo_ref[b * tp:(b + 1) * tp, j * 128:(j + 1) * 128]
        for j in range(SSM_SLABS):
            o_ref[b, :, j * 128:(j + 1) * 128] = il_ref[j].astype(BF16)


def _s5_branch(z3, bmat, cmat, lam_re, lam_im, d_skip, w_glu, b_glu):
    nb, seqlen, _ = z3.shape
    tc = min(512, seqlen)
    tp = tc // 2
    const2 = lambda c: (0, 0)
    const3 = lambda c: (0, 0, 0)
    single = pl.Buffered(1)
    return pl.pallas_call(
        _s5_kernel,
        grid=(seqlen // tc,),
        in_specs=[
            pl.BlockSpec((nb, tc, D_MODEL), lambda c: (0, c, 0)),
            pl.BlockSpec(bmat.shape, const3, pipeline_mode=single),
            pl.BlockSpec(cmat.shape, const3, pipeline_mode=single),
            pl.BlockSpec(lam_re.shape, const2),
            pl.BlockSpec(lam_im.shape, const2),
            pl.BlockSpec((1, D_MODEL), const2),
            pl.BlockSpec((D_MODEL, D_MODEL), const2, pipeline_mode=single),
            pl.BlockSpec((1, D_MODEL), const2),
        ],
        out_specs=pl.BlockSpec((nb, tc, D_MODEL), lambda c: (0, c, 0)),
        out_shape=jax.ShapeDtypeStruct((nb, seqlen, D_MODEL), BF16),
        scratch_shapes=[
            pltpu.VMEM((nb, NLB, SSM_SLABS * (tp + SLAB_ROW_PAD), 128), F32),
            pltpu.VMEM((nb, NLB, SSM_SLABS, 128), F32),
            pltpu.VMEM((nb, 1, D_MODEL), F32),
            pltpu.VMEM((SSM_SLABS, tc, 128), F32),
        ] + [pltpu.VMEM((nb * tp, D_MODEL), F32)] * 5,
        compiler_params=_params("arbitrary"),
        name="s5_branch",
    )(z3, bmat, cmat, lam_re, lam_im, d_skip, w_glu, b_glu)


def _s5_matrices(lam_re, lam_im, log_dt, b_re, b_im, c_re, c_im):
    dt = jnp.exp(log_dt)[:, None]
    mag = jnp.exp(lam_re * dt)
    lb_re, lb_im = mag * jnp.cos(lam_im * dt), mag * jnp.sin(lam_im * dt)
    den = lam_re * lam_re + lam_im * lam_im
    cf_re = ((lb_re - 1.0) * lam_re + lb_im * lam_im) / den
    cf_im = (lb_im * lam_re - (lb_re - 1.0) * lam_im) / den
    eye = jnp.eye(SLAB_GROUPS, dtype=F32)[None, :, None, :, None]
    by_slab = (SSM_SLABS, SLAB_GROUPS)

    def col_factor(x):
        return x.reshape(by_slab + (SSM_STATE,))[:, None, None, :, :]

    def row_factor(x):
        return x.reshape(by_slab + (SSM_STATE,))[:, :, :, None, None]

    def in_place(x):
        return x.reshape(by_slab + (SSM_STATE, SSM_GROUP)).transpose(0, 1, 3, 2)[:, :, :, None, :] * eye

    def out_place(x):
        return x.reshape(by_slab + (SSM_GROUP, SSM_STATE)).transpose(0, 3, 1, 2)[:, None] * eye

    def in_flat(x):
        return x.reshape(SSM_SLABS, 128, SLAB_STATES)

    def out_flat(x):
        return x.reshape(SSM_SLABS, SLAB_STATES, 128)

    br, bi = in_place(b_re), in_place(b_im)
    bo_re = col_factor(cf_re) * br - col_factor(cf_im) * bi
    bo_im = col_factor(cf_re) * bi + col_factor(cf_im) * br
    be_re = col_factor(lb_re) * bo_re - col_factor(lb_im) * bo_im
    be_im = col_factor(lb_re) * bo_im + col_factor(lb_im) * bo_re
    cr, ci = out_place(c_re), out_place(c_im)
    cn_re = row_factor(lb_re) * cr - row_factor(lb_im) * ci
    cn_im = row_factor(lb_re) * ci + row_factor(lb_im) * cr
    b_odd = jnp.concatenate([in_flat(bo_re), in_flat(bo_im)], axis=2)
    cur = jnp.concatenate([out_flat(cr), out_flat(-ci)], axis=1)
    k0 = jnp.einsum('jrs,jsc->jrc', b_odd, cur, precision=lax.Precision.HIGHEST)
    top = jnp.concatenate([in_flat(be_re), in_flat(be_im), k0], axis=2)
    bot = jnp.concatenate([b_odd, jnp.zeros_like(k0)], axis=2)
    bmat = jnp.concatenate([top, bot], axis=1).astype(BF16)
    nxt = jnp.concatenate([out_flat(cn_re), out_flat(-cn_im)], axis=1)
    cmat = jnp.concatenate([cur, nxt], axis=2).astype(BF16)
    lre = (lb_re * lb_re - lb_im * lb_im).reshape(SSM_SLABS, SLAB_STATES)
    lim = (2.0 * lb_re * lb_im).reshape(SSM_SLABS, SLAB_STATES)
    return bmat, cmat, lre, lim


def _dattn_kernel(q_ref, k_ref, v_ref, lmb_ref, hn_ref, o_ref,
                  qm_ref, vx_ref, sx_ref, sy_ref, m_ref, acc_ref, *, lam_init):
    tq = q_ref.shape[0]
    tk = tq
    nstrips = tk // 128
    nheads = q_ref.shape[1] // 128
    qi = pl.program_id(2)
    lane = lax.broadcasted_iota(jnp.int32, (tq, 128), 1)
    nt = (((1,), (1,)), ((), ()))

    @pl.when(qi == 0)
    def _():
        for hh in range(nheads):
            vx_ref[hh, :, :DA_V_DIM] = v_ref[:, hh * 128:(hh + 1) * 128]
            vx_ref[hh, :, DA_V_DIM:] = jnp.ones((v_ref.shape[0], DA_V_DIM), BF16)

    for hh in range(nheads):
        q = q_ref[:, hh * 128:(hh + 1) * 128]
        zero = jnp.zeros_like(q)
        qm_ref[2 * hh] = jnp.where(lane < DA_HEAD_DIM, q, zero)
        qm_ref[2 * hh + 1] = jnp.where(lane >= DA_HEAD_DIM, q, zero)
    m_ref[...] = jnp.full(m_ref.shape, NEG_BIG, F32)
    acc_ref[...] = jnp.zeros(acc_ref.shape, F32)

    def scores(ci, ki, s_ref):
        start = pl.multiple_of(ki * tk, tk)
        hh = ci // 2
        kb = k_ref[pl.ds(start, tk), hh * 128:(hh + 1) * 128]
        s_ref[ci] = lax.dot_general(qm_ref[ci], kb, nt, preferred_element_type=F32)

    def update(ci, ki, s_ref):
        start = pl.multiple_of(ki * tk, tk)
        hh = ci // 2
        vb = vx_ref[hh, pl.ds(start, tk), :]
        s = s_ref[ci]
        m_prev = m_ref[ci]
        m_next = jnp.maximum(m_prev, jnp.max(s, axis=1, keepdims=True))
        alpha = jnp.exp2(m_prev - m_next)
        ps = [jnp.exp2(s[:, c * 128:(c + 1) * 128] - m_next) for c in range(nstrips)]
        p = jnp.concatenate(ps, axis=1).astype(BF16)
        acc_ref[ci] = jnp.concatenate([alpha, alpha], axis=1) * acc_ref[ci] + _bdot(p, vb)
        m_ref[ci] = m_next

    def update_diag(ci, ki, s_ref):
        half = tq // 2
        start = pl.multiple_of(ki * tk, tk)
        hh = ci // 2
        row = lax.broadcasted_iota(jnp.int32, (half, half), 0) // CHUNK
        col = lax.broadcasted_iota(jnp.int32, (half, half), 1) // CHUNK
        visible = col <= row
        for part in range(2):
            rows = slice(part * half, (part + 1) * half)
            ncols = half * (part + 1)
            strips = []
            for c in range(ncols // 128):
                s = s_ref[ci, rows, c * 128:(c + 1) * 128]
                off = c * 128 - part * half
                if off >= 0:
                    s = jnp.where(visible[:, off:off + 128], s, NEG_BIG)
                strips.append(s)
            m_prev = m_ref[ci, rows, :]
            m_cur = functools.reduce(jnp.maximum, strips)
            m_next = jnp.maximum(m_prev, jnp.max(m_cur, axis=1, keepdims=True))
            alpha = jnp.exp2(m_prev - m_next)
            p = jnp.concatenate([jnp.exp2(s - m_next) for s in strips], axis=1).astype(BF16)
            pv = _bdot(p, vx_ref[hh, pl.ds(start, ncols), :])
            acc_ref[ci, rows, :] = jnp.concatenate([alpha, alpha], axis=1) * acc_ref[ci, rows, :] + pv
            m_ref[ci, rows, :] = m_next

    def stage(k_next, s_next_ref, k_cur, s_cur_ref, diag=False):
        for ci in range(2 * nheads):
            if k_next is not None:
                scores(ci, k_next, s_next_ref)
            if diag:
                update_diag(ci, k_cur, s_cur_ref)
            else:
                update(ci, k_cur, s_cur_ref)

    for ci in range(2 * nheads):
        scores(ci, 0, sx_ref)

    def pair(p, carry):
        stage(2 * p + 1, sy_ref, 2 * p, sx_ref)
        stage(2 * p + 2, sx_ref, 2 * p + 1, sy_ref)
        return carry

    lax.fori_loop(0, qi // 2, pair, 0)

    @pl.when(qi % 2 == 0)
    def _():
        stage(None, None, qi, sx_ref, diag=True)

    @pl.when(qi % 2 == 1)
    def _():
        stage(qi, sy_ref, qi - 1, sx_ref)
        stage(None, None, qi, sy_ref, diag=True)

    lv = lmb_ref[...]
    lam = (jnp.exp(jnp.sum(lv[0:1] * lv[1:2], axis=1, keepdims=True))
           - jnp.exp(jnp.sum(lv[2:3] * lv[3:4], axis=1, keepdims=True)) + lam_init)
    for hh in range(nheads):
        o = (acc_ref[2 * hh, :, :DA_V_DIM] / acc_ref[2 * hh, :, DA_V_DIM:]
             - lam * (acc_ref[2 * hh + 1, :, :DA_V_DIM] / acc_ref[2 * hh + 1, :, DA_V_DIM:]))
        o = o * lax.rsqrt(jnp.mean(o * o, axis=-1, keepdims=True) + EPS) * hn_ref[...]
        o_ref[:, hh * 128:(hh + 1) * 128] = (o * (1.0 - lam_init)).astype(BF16)


def _diff_attention(z3, lmb, head_norm, lam_init):
    nb, seqlen, _ = z3.shape
    tq = min(512, seqlen)
    hps = ATTN_HEADS_PER_STEP
    width = hps * 128
    qcol, kcol, vcol = (c // width for c in (D_MODEL, 2 * D_MODEL, 3 * D_MODEL))
    return pl.pallas_call(
        functools.partial(_dattn_kernel, lam_init=lam_init),
        grid=(nb, DA_HEADS // hps, seqlen // tq),
        in_specs=[
            pl.BlockSpec((None, tq, width), lambda b, h, i: (b, i, qcol + h)),
            pl.BlockSpec((None, seqlen, width), lambda b, h, i: (b, 0, kcol + h)),
            pl.BlockSpec((None, seqlen, width), lambda b, h, i: (b, 0, vcol + h)),
            pl.BlockSpec((4, DA_HEAD_DIM), lambda b, h, i: (0, 0)),
            pl.BlockSpec((1, DA_V_DIM), lambda b, h, i: (0, 0)),
        ],
        out_specs=pl.BlockSpec((None, tq, width), lambda b, h, i: (b, i, h)),
        out_shape=jax.ShapeDtypeStruct((nb, seqlen, D_MODEL), BF16),
        scratch_shapes=[pltpu.VMEM((2 * hps, tq, 128), BF16),
                        pltpu.VMEM((hps, seqlen, 2 * DA_V_DIM), BF16),
                        pltpu.VMEM((2 * hps, tq, tq), F32), pltpu.VMEM((2 * hps, tq, tq), F32),
                        pltpu.VMEM((2 * hps, tq, 128), F32),
                        pltpu.VMEM((2 * hps, tq, 2 * DA_V_DIM), F32)],
        compiler_params=_params("parallel", "parallel", "arbitrary"),
        name="diff_attention",
    )(z3, z3, z3, lmb, head_norm)


def _merge_kernel(ys_ref, ya_ref, g_ref, x_ref, wa_ref, wb_ref, wo_ref, n_ref, o_ref):
    a = _bdot(ys_ref[...], wa_ref[...])
    b = _bdot(ya_ref[...], wb_ref[...])
    ga = g_ref[:, :D_MODEL].astype(F32)
    gb = g_ref[:, D_MODEL:].astype(F32)
    merged = (ga * a + gb * b).astype(BF16)
    o_ref[...] = x_ref[...] + _rms(_bdot(merged, wo_ref[...]), n_ref[...])


def _merge(ys2, ya2, z2, x2, w_a, w_b, w_o, gain):
    t = x2.shape[0]
    tm = min(1024, t)
    row = lambda i: (i, 0)
    const = lambda i: (0, 0)
    wspec = pl.BlockSpec((D_MODEL, D_MODEL), const)
    return pl.pallas_call(
        _merge_kernel,
        grid=(t // tm,),
        in_specs=[
            pl.BlockSpec((tm, D_MODEL), row),
            pl.BlockSpec((tm, D_MODEL), row),
            pl.BlockSpec((tm, 2 * D_MODEL), lambda i: (i, 2)),
            pl.BlockSpec((tm, D_MODEL), row),
            wspec, wspec, wspec,
            pl.BlockSpec((1, D_MODEL), const),
        ],
        out_specs=pl.BlockSpec((tm, D_MODEL), row),
        out_shape=jax.ShapeDtypeStruct((t, D_MODEL), F32),
        compiler_params=_params("parallel"),
        name="merge",
    )(ys2, ya2, z2, x2, w_a, w_b, w_o, gain)


def _memkv_kernel(m_ref, g_ref, w_ref, o_ref):
    o_ref[...] = _bdot(_rms(m_ref[...], g_ref[...]).astype(BF16), w_ref[...]).astype(BF16)


def _memory_kv(mem2, gain, w_xkv):
    rows = mem2.shape[0]
    const = lambda i: (0, 0)
    return pl.pallas_call(
        _memkv_kernel,
        grid=(1,),
        in_specs=[
            pl.BlockSpec((rows, D_MODEL), const),
            pl.BlockSpec((1, D_MODEL), const),
            pl.BlockSpec((D_MODEL, 2 * D_MODEL), const),
        ],
        out_specs=pl.BlockSpec((rows, 2 * D_MODEL), const),
        out_shape=jax.ShapeDtypeStruct((rows, 2 * D_MODEL), BF16),
        compiler_params=_params("arbitrary"),
        name="memory_kv",
    )(mem2, gain, w_xkv)


def _xattn_kernel(x_ref, kv_ref, npre_ref, wq_ref, wo_ref, npost_ref, o_ref, oh_ref):
    x = x_ref[...]
    h = _rms(x, npre_ref[...]).astype(BF16)
    q = (_bdot(h, wq_ref[...]) * (XA_HEAD_DIM ** -0.5 * LOG2E)).astype(BF16)
    nt = (((1,), (1,)), ((), ()))
    for hd in range(XA_HEADS):
        lo = hd * XA_HEAD_DIM
        k = kv_ref[:, lo:lo + XA_HEAD_DIM]
        v = kv_ref[:, D_MODEL + lo:D_MODEL + lo + XA_HEAD_DIM]
        s = lax.dot_general(q[:, lo:lo + XA_HEAD_DIM], k, nt, preferred_element_type=F32)
        p = jnp.exp2(s - jnp.max(s, axis=1, keepdims=True))
        o = _bdot(p.astype(BF16), v) / jnp.sum(p, axis=1, keepdims=True)
        oh_ref[:, lo:lo + XA_HEAD_DIM] = o.astype(BF16)
    o_ref[...] = x + _rms(_bdot(oh_ref[...], wo_ref[...]), npost_ref[...])


def _cross_attention(x3, kv3, n_pre, w_xq, w_xo, n_post):
    nb, seqlen, _ = x3.shape
    mlen = kv3.shape[1]
    tm = min(1024, seqlen)
    const = lambda b, i: (0, 0)
    wspec = pl.BlockSpec((D_MODEL, D_MODEL), const)
    vspec = pl.BlockSpec((1, D_MODEL), const)
    return pl.pallas_call(
        _xattn_kernel,
        grid=(nb, seqlen // tm),
        in_specs=[
            pl.BlockSpec((None, tm, D_MODEL), lambda b, i: (b, i, 0)),
            pl.BlockSpec((None, mlen, 2 * D_MODEL), lambda b, i: (b, 0, 0)),
            vspec, wspec, wspec, vspec,
        ],
        out_specs=pl.BlockSpec((None, tm, D_MODEL), lambda b, i: (b, i, 0)),
        out_shape=jax.ShapeDtypeStruct((nb, seqlen, D_MODEL), F32),
        scratch_shapes=[pltpu.VMEM((tm, D_MODEL), BF16)],
        compiler_params=_params("parallel", "parallel"),
        name="cross_attention",
    )(x3, kv3, n_pre, w_xq, w_xo, n_post)


def _mlp_kernel(x_ref, npre_ref, w1_ref, w2_ref, npost_ref, o_ref):
    x = x_ref[...]
    h = _rms(x, npre_ref[...]).astype(BF16)
    f = jnp.zeros(x.shape, F32)
    for c in range(D_FF // D_MODEL):
        lo = c * D_MODEL
        a = jnp.maximum(_bdot(h, w1_ref[:, lo:lo + D_MODEL]), 0.0)
        f = f + _bdot((a * a).astype(BF16), w2_ref[lo:lo + D_MODEL, :])
    o_ref[...] = x + _rms(f, npost_ref[...])


def _mlp(x2, n_pre, w1, w2, n_post):
    t = x2.shape[0]
    tm = min(512, t)
    row = lambda i: (i, 0)
    const = lambda i: (0, 0)
    vspec = pl.BlockSpec((1, D_MODEL), const)
    return pl.pallas_call(
        _mlp_kernel,
        grid=(t // tm,),
        in_specs=[
            pl.BlockSpec((tm, D_MODEL), row),
            vspec,
            pl.BlockSpec((D_MODEL, D_FF), const),
            pl.BlockSpec((D_FF, D_MODEL), const),
            vspec,
        ],
        out_specs=pl.BlockSpec((tm, D_MODEL), row),
        out_shape=jax.ShapeDtypeStruct((t, D_MODEL), F32),
        compiler_params=_params("parallel"),
        name="mlp",
    )(x2, n_pre, w1, w2, n_post)


def _rope_tables(seqlen):
    inv = ROPE_THETA ** (-jnp.arange(0, DA_HEAD_DIM, 2, dtype=F32) / DA_HEAD_DIM)
    ang = jnp.arange(seqlen, dtype=F32)[:, None] * inv[None, :]
    cos = jnp.tile(jnp.cos(ang), (1, 4))
    sin = jnp.tile(jnp.concatenate([-jnp.sin(ang), jnp.sin(ang)], axis=1), (1, 2))
    return cos, sin


def kernel(x, mem, norm_mix_pre, w_in, b_gate, ssm_lambda_re, ssm_lambda_im, ssm_log_dt, ssm_b_re, ssm_b_im, ssm_c_re, ssm_c_im, ssm_d, w_glu, b_glu, w_ssm_proj, da_lambda_q1, da_lambda_k1, da_lambda_q2, da_lambda_k2, da_head_norm, w_da_proj, w_mix_out, norm_mix_post, norm_x_pre, norm_mem, w_xq, w_xkv, w_xo, norm_x_post, norm_ff_pre, w_ff1, w_ff2, norm_ff_post):
    nb, seqlen, _ = x.shape
    t = nb * seqlen
    depth = w_in.shape[0]
    tabs = _rope_tables(seqlen)
    x2 = x.reshape(t, D_MODEL)
    mem2 = mem.reshape(nb * mem.shape[1], D_MODEL)
    for l in range(depth):
        lam_init = 0.8 - 0.6 * math.exp(-0.3 * l)
        z2 = _in_projection(x2, norm_mix_pre[l][None], w_in[l].astype(BF16), b_gate[l][None], tabs, seqlen)
        z3 = z2.reshape(nb, seqlen, IN_COLS)
        bmat, cmat, lre, lim = _s5_matrices(ssm_lambda_re[l], ssm_lambda_im[l], ssm_log_dt[l],
                                            ssm_b_re[l], ssm_b_im[l], ssm_c_re[l], ssm_c_im[l])
        ys = _s5_branch(z3, bmat, cmat, lre, lim, ssm_d[l][None], w_glu[l].astype(BF16), b_glu[l][None])
        lmb = jnp.stack([da_lambda_q1[l], da_lambda_k1[l], da_lambda_q2[l], da_lambda_k2[l]])
        ya = _diff_attention(z3, lmb, da_head_norm[l][None], lam_init)
        x2 = _merge(ys.reshape(t, D_MODEL), ya.reshape(t, D_MODEL), z2, x2,
                    w_ssm_proj[l].astype(BF16), w_da_proj[l].astype(BF16), w_mix_out[l].astype(BF16),
                    norm_mix_post[l][None])
        kv = _memory_kv(mem2, norm_mem[l][None], w_xkv[l].astype(BF16))
        x3 = _cross_attention(x2.reshape(nb, seqlen, D_MODEL), kv.reshape(nb, -1, 2 * D_MODEL),
                              norm_x_pre[l][None], w_xq[l].astype(BF16), w_xo[l].astype(BF16),
                              norm_x_post[l][None])
        x2 = _mlp(x3.reshape(t, D_MODEL), norm_ff_pre[l][None], w_ff1[l].astype(BF16),
                  w_ff2[l].astype(BF16), norm_ff_post[l][None])
    return x2.reshape(nb, seqlen, D_MODEL)
```

```python
import functools
import math

import jax
import jax.numpy as jnp
from jax import lax
from jax.experimental import pallas as pl
from jax.experimental.pallas import tpu as pltpu

F32 = jnp.float32
BF16 = jnp.bfloat16

D_MODEL = 1024
EPS = 1e-6
CHUNK = 64
SSM_GROUP = 16
SSM_GROUPS = 64
SSM_STATE = 64
SSM_SLABS = 8
SLAB_GROUPS = SSM_GROUPS // SSM_SLABS
SLAB_STATES = SLAB_GROUPS * SSM_STATE
NLB = 2 * SLAB_STATES // 128
SLAB_ROW_PAD = 8
DA_HEADS = 8
DA_HEAD_DIM = 64
DA_V_DIM = 128
ROPE_THETA = 10000.0
ATTN_HEADS_PER_STEP = 2
XA_HEADS = 4
XA_HEAD_DIM = 256
D_FF = 4096
IN_COLS = 6144
LOG2E = 1.4426950408889634
NEG_BIG = -1e30
VMEM_LIMIT_BYTES = 56 * 1024 * 1024


def _params(*semantics):
    return pltpu.CompilerParams(dimension_semantics=semantics, vmem_limit_bytes=VMEM_LIMIT_BYTES)


def _rms(x, g):
    return x * lax.rsqrt(jnp.mean(x * x, axis=-1, keepdims=True) + EPS) * g


def _bdot(a, b):
    return jnp.dot(a, b, preferred_element_type=F32)


def _sigmoid(x):
    return 0.5 * jnp.tanh(0.5 * x) + 0.5


def _inproj_kernel(x_ref, g_ref, w_ref, bg_ref, cos_ref, sin_ref, o_ref):
    tm = x_ref.shape[0]
    hn = _rms(x_ref[...], g_ref[...]).astype(BF16)
    lane = lax.broadcasted_iota(jnp.int32, (tm, 128), 1)
    first_half = (lane % 64) < 32

    def rope(acc, col0, scale):
        c = cos_ref[...] * scale
        s = sin_ref[...] * scale
        for h in range(DA_HEADS):
            blk = acc[:, h * 128:(h + 1) * 128]
            partner = jnp.where(first_half, pltpu.roll(blk, 96, 1), pltpu.roll(blk, 32, 1))
            o_ref[:, col0 + h * 128:col0 + (h + 1) * 128] = (blk * c + partner * s).astype(BF16)

    for j in range(IN_COLS // D_MODEL):
        col0 = j * D_MODEL
        acc = _bdot(hn, w_ref[:, col0:col0 + D_MODEL])
        if j == 1:
            rope(acc, col0, DA_HEAD_DIM ** -0.5 * LOG2E)
        elif j == 2:
            rope(acc, col0, 1.0)
        elif j >= 4:
            gate = _sigmoid(acc + bg_ref[:, col0 - 4 * D_MODEL:col0 - 3 * D_MODEL])
            o_ref[:, col0:col0 + D_MODEL] = gate.astype(BF16)
        else:
            o_ref[:, col0:col0 + D_MODEL] = acc.astype(BF16)


def _in_projection(x2, gain, w_in, b_gate, tabs, seqlen):
    t = x2.shape[0]
    tm = min(512, seqlen)
    nl = seqlen // tm
    const = lambda i: (0, 0)
    tab_spec = pl.BlockSpec((tm, 128), lambda i: (i % nl, 0))
    return pl.pallas_call(
        _inproj_kernel,
        grid=(t // tm,),
        in_specs=[
            pl.BlockSpec((tm, D_MODEL), lambda i: (i, 0)),
            pl.BlockSpec((1, D_MODEL), const),
            pl.BlockSpec((D_MODEL, IN_COLS), const, pipeline_mode=pl.Buffered(1)),
            pl.BlockSpec((1, 2 * D_MODEL), const),
            tab_spec, tab_spec,
        ],
        out_specs=pl.BlockSpec((tm, IN_COLS), lambda i: (i, 0)),
        out_shape=jax.ShapeDtypeStruct((t, IN_COLS), BF16),
        compiler_params=_params("parallel"),
        name="in_projection",
    )(x2, gain, w_in, b_gate, *tabs)


def _s5_kernel(u_ref, bm_ref, cm_ref, lre_ref, lim_ref, d_ref, wg_ref, bgl_ref, o_ref,
               bu_ref, st_ref, wc_ref, il_ref, ue_ref, uo_ref, yo_ref, w_ref, k0_ref):
    nb, tc, _ = u_ref.shape
    tp = tc // 2
    pitch = bu_ref.shape[2] // SSM_SLABS
    c = pl.program_id(0)

    @pl.when(c == 0)
    def _():
        st_ref[...] = jnp.zeros(st_ref.shape, F32)
        wc_ref[...] = jnp.zeros(wc_ref.shape, F32)

    for b in range(nb):
        for j in range(SSM_SLABS):
            il_ref[j] = u_ref[b, :, j * 128:(j + 1) * 128].astype(F32)
        for j in range(SSM_SLABS):
            u_e = il_ref[j, pl.ds(0, tp, stride=2), :]
            u_o = il_ref[j, pl.ds(1, tp, stride=2), :]
            ue_ref[b * tp:(b + 1) * tp, j * 128:(j + 1) * 128] = u_e
            uo_ref[b * tp:(b + 1) * tp, j * 128:(j + 1) * 128] = u_o
            lhs = jnp.concatenate([u_e, u_o], axis=1).astype(BF16)
            bu = _bdot(lhs, bm_ref[j])
            for k in range(NLB):
                bu_ref[b, k, j * pitch:j * pitch + tp, :] = bu[:, k * 128:(k + 1) * 128]
            k0_ref[b * tp:(b + 1) * tp, j * 128:(j + 1) * 128] = bu[:, NLB * 128:]

    ar = [lre_ref[:, k * 128:(k + 1) * 128] for k in range(NLB // 2)]
    ai = [lim_ref[:, k * 128:(k + 1) * 128] for k in range(NLB // 2)]

    def step(t, carry):
        out = []
        for b in range(nb):
            xs = carry[b]
            new = [None] * NLB
            for k in range(NLB // 2):
                xr, xi = xs[k], xs[k + NLB // 2]
                rows = pl.ds(t, SSM_SLABS, stride=pitch)
                new[k] = ar[k] * xr - ai[k] * xi + bu_ref[b, k, rows, :]
                new[k + NLB // 2] = ar[k] * xi + ai[k] * xr + bu_ref[b, k + NLB // 2, rows, :]
                bu_ref[b, k, rows, :] = new[k]
                bu_ref[b, k + NLB // 2, rows, :] = new[k + NLB // 2]
            out.append(tuple(new))
        return tuple(out)

    init = tuple(tuple(st_ref[b, k] for k in range(NLB)) for b in range(nb))
    fin = lax.fori_loop(0, tp, step, init, unroll=8)
    for b in range(nb):
        for k in range(NLB):
            st_ref[b, k] = fin[b][k]

    for b in range(nb):
        for j in range(SSM_SLABS):
            xs = jnp.concatenate([bu_ref[b, k, j * pitch:j * pitch + tp, :] for k in range(NLB)], axis=1)
            r = _bdot(xs.astype(BF16), cm_ref[j])
            yo_ref[b * tp:(b + 1) * tp, j * 128:(j + 1) * 128] = r[:, :128]
            w_ref[b * tp:(b + 1) * tp, j * 128:(j + 1) * 128] = r[:, 128:]

    def glu(y):
        ys = 0.5 * y * (1.0 + jnp.tanh(math.sqrt(2.0 / math.pi) * (y + 0.044715 * (y * y * y))))
        gate = _sigmoid(_bdot(ys.astype(BF16), wg_ref[...]) + bgl_ref[...])
        return ys * gate

    first_row = lax.broadcasted_iota(jnp.int32, (tp, D_MODEL), 0) == 0
    shifted = []
    for b in range(nb):
        w = w_ref[b * tp:(b + 1) * tp, :]
        shifted.append(jnp.where(first_row, wc_ref[b], pltpu.roll(w, 1, 0)))
        wc_ref[b] = w[tp - 1:tp, :]
    yo_ref[...] = glu(yo_ref[...] + d_ref[...] * uo_ref[...])
    w_ref[...] = glu(jnp.concatenate(shifted, axis=0) + k0_ref[...] + d_ref[...] * ue_ref[...])
    for b in range(nb):
        for j in range(SSM_SLABS):
            il_ref[j, pl.ds(0, tp, stride=2), :] = w_ref[b * tp:(b + 1) * tp, j * 128:(j + 1) * 128]
            il_ref[j, pl.ds(1, tp, stride=2), :] = yo_ref[b * tp:(b + 1) * tp, j * 128:(j + 1) * 128]
        for j in range(SSM_SLABS):
            o_ref[b, :, j * 128:(j + 1) * 128] = il_ref[j].astype(BF16)


def _s5_branch(z3, bmat, cmat, lam_re, lam_im, d_skip, w_glu, b_glu):
    nb, seqlen, _ = z3.shape
    tc = min(512, seqlen)
    tp = tc // 2
    const2 = lambda c: (0, 0)
    const3 = lambda c: (0, 0, 0)
    single = pl.Buffered(1)
    return pl.pallas_call(
        _s5_kernel,
        grid=(seqlen // tc,),
        in_specs=[
            pl.BlockSpec((nb, tc, D_MODEL), lambda c: (0, c, 0)),
            pl.BlockSpec(bmat.shape, const3, pipeline_mode=single),
            pl.BlockSpec(cmat.shape, const3, pipeline_mode=single),
            pl.BlockSpec(lam_re.shape, const2),
            pl.BlockSpec(lam_im.shape, const2),
            pl.BlockSpec((1, D_MODEL), const2),
            pl.BlockSpec((D_MODEL, D_MODEL), const2, pipeline_mode=single),
            pl.BlockSpec((1, D_MODEL), const2),
        ],
        out_specs=pl.BlockSpec((nb, tc, D_MODEL), lambda c: (0, c, 0)),
        out_shape=jax.ShapeDtypeStruct((nb, seqlen, D_MODEL), BF16),
        scratch_shapes=[
            pltpu.VMEM((nb, NLB, SSM_SLABS * (tp + SLAB_ROW_PAD), 128), F32),
            pltpu.VMEM((nb, NLB, SSM_SLABS, 128), F32),
            pltpu.VMEM((nb, 1, D_MODEL), F32),
            pltpu.VMEM((SSM_SLABS, tc, 128), F32),
        ] + [pltpu.VMEM((nb * tp, D_MODEL), F32)] * 5,
        compiler_params=_params("arbitrary"),
        name="s5_branch",
    )(z3, bmat, cmat, lam_re, lam_im, d_skip, w_glu, b_glu)


def _s5_matrices(lam_re, lam_im, log_dt, b_re, b_im, c_re, c_im):
    dt = jnp.exp(log_dt)[:, None]
    mag = jnp.exp(lam_re * dt)
    lb_re, lb_im = mag * jnp.cos(lam_im * dt), mag * jnp.sin(lam_im * dt)
    den = lam_re * lam_re + lam_im * lam_im
    cf_re = ((lb_re - 1.0) * lam_re + lb_im * lam_im) / den
    cf_im = (lb_im * lam_re - (lb_re - 1.0) * lam_im) / den
    eye = jnp.eye(SLAB_GROUPS, dtype=F32)[None, :, None, :, None]
    by_slab = (SSM_SLABS, SLAB_GROUPS)

    def col_factor(x):
        return x.reshape(by_slab + (SSM_STATE,))[:, None, None, :, :]

    def row_factor(x):
        return x.reshape(by_slab + (SSM_STATE,))[:, :, :, None, None]

    def in_place(x):
        return x.reshape(by_slab + (SSM_STATE, SSM_GROUP)).transpose(0, 1, 3, 2)[:, :, :, None, :] * eye

    def out_place(x):
        return x.reshape(by_slab + (SSM_GROUP, SSM_STATE)).transpose(0, 3, 1, 2)[:, None] * eye

    def in_flat(x):
        return x.reshape(SSM_SLABS, 128, SLAB_STATES)

    def out_flat(x):
        return x.reshape(SSM_SLABS, SLAB_STATES, 128)

    br, bi = in_place(b_re), in_place(b_im)
    bo_re = col_factor(cf_re) * br - col_factor(cf_im) * bi
    bo_im = col_factor(cf_re) * bi + col_factor(cf_im) * br
    be_re = col_factor(lb_re) * bo_re - col_factor(lb_im) * bo_im
    be_im = col_factor(lb_re) * bo_im + col_factor(lb_im) * bo_re
    cr, ci = out_place(c_re), out_place(c_im)
    cn_re = row_factor(lb_re) * cr - row_factor(lb_im) * ci
    cn_im = row_factor(lb_re) * ci + row_factor(lb_im) * cr
    b_odd = jnp.concatenate([in_flat(bo_re), in_flat(bo_im)], axis=2)
    cur = jnp.concatenate([out_flat(cr), out_flat(-ci)], axis=1)
    k0 = jnp.einsum('jrs,jsc->jrc', b_odd, cur, precision=lax.Precision.HIGHEST)
    top = jnp.concatenate([in_flat(be_re), in_flat(be_im), k0], axis=2)
    bot = jnp.concatenate([b_odd, jnp.zeros_like(k0)], axis=2)
    bmat = jnp.concatenate([top, bot], axis=1).astype(BF16)
    nxt = jnp.concatenate([out_flat(cn_re), out_flat(-cn_im)], axis=1)
    cmat = jnp.concatenate([cur, nxt], axis=2).astype(BF16)
    lre = (lb_re * lb_re - lb_im * lb_im).reshape(SSM_SLABS, SLAB_STATES)
    lim = (2.0 * lb_re * lb_im).reshape(SSM_SLABS, SLAB_STATES)
    return bmat, cmat, lre, lim


def _dattn_kernel(q_ref, k_ref, v_ref, lmb_ref, hn_ref, o_ref,
                  qm_ref, vx_ref, sx_ref, sy_ref, m_ref, acc_ref, *, lam_init):
    tq = q_ref.shape[0]
    tk = tq
    nstrips = tk // 128
    nheads = q_ref.shape[1] // 128
    qi = pl.program_id(2)
    lane = lax.broadcasted_iota(jnp.int32, (tq, 128), 1)
    nt = (((1,), (1,)), ((), ()))

    @pl.when(qi == 0)
    def _():
        for hh in range(nheads):
            vx_ref[hh, :, :DA_V_DIM] = v_ref[:, hh * 128:(hh + 1) * 128]
            vx_ref[hh, :, DA_V_DIM:] = jnp.ones((v_ref.shape[0], DA_V_DIM), BF16)

    for hh in range(nheads):
        q = q_ref[:, hh * 128:(hh + 1) * 128]
        zero = jnp.zeros_like(q)
        qm_ref[2 * hh] = jnp.where(lane < DA_HEAD_DIM, q, zero)
        qm_ref[2 * hh + 1] = jnp.where(lane >= DA_HEAD_DIM, q, zero)
    m_ref[...] = jnp.full(m_ref.shape, NEG_BIG, F32)
    acc_ref[...] = jnp.zeros(acc_ref.shape, F32)

    def scores(ci, ki, s_ref):
        start = pl.multiple_of(ki * tk, tk)
        hh = ci // 2
        kb = k_ref[pl.ds(start, tk), hh * 128:(hh + 1) * 128]
        s = lax.dot_general(qm_ref[ci], kb, nt, preferred_element_type=F32)
        s_ref[ci, :, :tk] = s
        s_ref[ci, :, tk:] = functools.reduce(jnp.maximum, [s[:, c * 128:(c + 1) * 128] for c in range(nstrips)])

    def update(ci, ki, s_ref):
        start = pl.multiple_of(ki * tk, tk)
        hh = ci // 2
        vb = vx_ref[hh, pl.ds(start, tk), :]
        m_prev = m_ref[ci]
        m_next = jnp.maximum(m_prev, jnp.max(s_ref[ci, :, tk:], axis=1, keepdims=True))
        alpha = jnp.exp2(m_prev - m_next)
        ps = [jnp.exp2(s_ref[ci, :, c * 128:(c + 1) * 128] - m_next) for c in range(nstrips)]
        p = jnp.concatenate(ps, axis=1).astype(BF16)
        acc_ref[ci] = jnp.concatenate([alpha, alpha], axis=1) * acc_ref[ci] + _bdot(p, vb)
        m_ref[ci] = m_next

    def update_diag(ci, ki, s_ref):
        half = tq // 2
        start = pl.multiple_of(ki * tk, tk)
        hh = ci // 2
        row = lax.broadcasted_iota(jnp.int32, (half, half), 0) // CHUNK
        col = lax.broadcasted_iota(jnp.int32, (half, half), 1) // CHUNK
        visible = col <= row
        for part in range(2):
            rows = slice(part * half, (part + 1) * half)
            ncols = half * (part + 1)
            strips = []
            for c in range(ncols // 128):
                s = s_ref[ci, rows, c * 128:(c + 1) * 128]
                off = c * 128 - part * half
                if off >= 0:
                    s = jnp.where(visible[:, off:off + 128], s, NEG_BIG)
                strips.append(s)
            m_prev = m_ref[ci, rows, :]
            m_cur = functools.reduce(jnp.maximum, strips)
            m_next = jnp.maximum(m_prev, jnp.max(m_cur, axis=1, keepdims=True))
            alpha = jnp.exp2(m_prev - m_next)
            p = jnp.concatenate([jnp.exp2(s - m_next) for s in strips], axis=1).astype(BF16)
            pv = _bdot(p, vx_ref[hh, pl.ds(start, ncols), :])
            acc_ref[ci, rows, :] = jnp.concatenate([alpha, alpha], axis=1) * acc_ref[ci, rows, :] + pv
            m_ref[ci, rows, :] = m_next

    def stage(k_next, s_next_ref, k_cur, s_cur_ref, diag=False):
        for ci in range(2 * nheads):
            if k_next is not None:
                scores(ci, k_next, s_next_ref)
            if diag:
                update_diag(ci, k_cur, s_cur_ref)
            else:
                update(ci, k_cur, s_cur_ref)

    for ci in range(2 * nheads):
        scores(ci, 0, sx_ref)

    def pair(p, carry):
        stage(2 * p + 1, sy_ref, 2 * p, sx_ref)
        stage(2 * p + 2, sx_ref, 2 * p + 1, sy_ref)
        return carry

    lax.fori_loop(0, qi // 2, pair, 0)

    @pl.when(qi % 2 == 0)
    def _():
        stage(None, None, qi, sx_ref, diag=True)

    @pl.when(qi % 2 == 1)
    def _():
        stage(qi, sy_ref, qi - 1, sx_ref)
        stage(None, None, qi, sy_ref, diag=True)

    lv = lmb_ref[...]
    lam = (jnp.exp(jnp.sum(lv[0:1] * lv[1:2], axis=1, keepdims=True))
           - jnp.exp(jnp.sum(lv[2:3] * lv[3:4], axis=1, keepdims=True)) + lam_init)
    for hh in range(nheads):
        o = (acc_ref[2 * hh, :, :DA_V_DIM] / acc_ref[2 * hh, :, DA_V_DIM:]
             - lam * (acc_ref[2 * hh + 1, :, :DA_V_DIM] / acc_ref[2 * hh + 1, :, DA_V_DIM:]))
        o = o * lax.rsqrt(jnp.mean(o * o, axis=-1, keepdims=True) + EPS) * hn_ref[...]
        o_ref[:, hh * 128:(hh + 1) * 128] = (o * (1.0 - lam_init)).astype(BF16)


def _diff_attention(z3, lmb, head_norm, lam_init):
    nb, seqlen, _ = z3.shape
    tq = min(512, seqlen)
    hps = ATTN_HEADS_PER_STEP
    width = hps * 128
    qcol, kcol, vcol = (c // width for c in (D_MODEL, 2 * D_MODEL, 3 * D_MODEL))
    return pl.pallas_call(
        functools.partial(_dattn_kernel, lam_init=lam_init),
        grid=(nb, DA_HEADS // hps, seqlen // tq),
        in_specs=[
            pl.BlockSpec((None, tq, width), lambda b, h, i: (b, i, qcol + h)),
            pl.BlockSpec((None, seqlen, width), lambda b, h, i: (b, 0, kcol + h)),
            pl.BlockSpec((None, seqlen, width), lambda b, h, i: (b, 0, vcol + h)),
            pl.BlockSpec((4, DA_HEAD_DIM), lambda b, h, i: (0, 0)),
            pl.BlockSpec((1, DA_V_DIM), lambda b, h, i: (0, 0)),
        ],
        out_specs=pl.BlockSpec((None, tq, width), lambda b, h, i: (b, i, h)),
        out_shape=jax.ShapeDtypeStruct((nb, seqlen, D_MODEL), BF16),
        scratch_shapes=[pltpu.VMEM((2 * hps, tq, 128), BF16),
                        pltpu.VMEM((hps, seqlen, 2 * DA_V_DIM), BF16),
                        pltpu.VMEM((2 * hps, tq, tq + 128), F32), pltpu.VMEM((2 * hps, tq, tq + 128), F32),
                        pltpu.VMEM((2 * hps, tq, 128), F32),
                        pltpu.VMEM((2 * hps, tq, 2 * DA_V_DIM), F32)],
        compiler_params=_params("parallel", "parallel", "arbitrary"),
        name="diff_attention",
    )(z3, z3, z3, lmb, head_norm)


def _merge_kernel(ys_ref, ya_ref, g_ref, x_ref, wa_ref, wb_ref, wo_ref, n_ref, o_ref):
    a = _bdot(ys_ref[...], wa_ref[...])
    b = _bdot(ya_ref[...], wb_ref[...])
    ga = g_ref[:, :D_MODEL].astype(F32)
    gb = g_ref[:, D_MODEL:].astype(F32)
    merged = (ga * a + gb * b).astype(BF16)
    o_ref[...] = x_ref[...] + _rms(_bdot(merged, wo_ref[...]), n_ref[...])


def _merge(ys2, ya2, z2, x2, w_a, w_b, w_o, gain):
    t = x2.shape[0]
    tm = min(1024, t)
    row = lambda i: (i, 0)
    const = lambda i: (0, 0)
    wspec = pl.BlockSpec((D_MODEL, D_MODEL), const)
    return pl.pallas_call(
        _merge_kernel,
        grid=(t // tm,),
        in_specs=[
            pl.BlockSpec((tm, D_MODEL), row),
            pl.BlockSpec((tm, D_MODEL), row),
            pl.BlockSpec((tm, 2 * D_MODEL), lambda i: (i, 2)),
            pl.BlockSpec((tm, D_MODEL), row),
            wspec, wspec, wspec,
            pl.BlockSpec((1, D_MODEL), const),
        ],
        out_specs=pl.BlockSpec((tm, D_MODEL), row),
        out_shape=jax.ShapeDtypeStruct((t, D_MODEL), F32),
        compiler_params=_params("parallel"),
        name="merge",
    )(ys2, ya2, z2, x2, w_a, w_b, w_o, gain)


def _memkv_kernel(m_ref, g_ref, w_ref, o_ref):
    o_ref[...] = _bdot(_rms(m_ref[...], g_ref[...]).astype(BF16), w_ref[...]).astype(BF16)


def _memory_kv(mem2, gain, w_xkv):
    rows = mem2.shape[0]
    const = lambda i: (0, 0)
    return pl.pallas_call(
        _memkv_kernel,
        grid=(1,),
        in_specs=[
            pl.BlockSpec((rows, D_MODEL), const),
            pl.BlockSpec((1, D_MODEL), const),
            pl.BlockSpec((D_MODEL, 2 * D_MODEL), const),
        ],
        out_specs=pl.BlockSpec((rows, 2 * D_MODEL), const),
        out_shape=jax.ShapeDtypeStruct((rows, 2 * D_MODEL), BF16),
        compiler_params=_params("arbitrary"),
        name="memory_kv",
    )(mem2, gain, w_xkv)


def _xattn_kernel(x_ref, kv_ref, npre_ref, wq_ref, wo_ref, npost_ref, o_ref, oh_ref):
    x = x_ref[...]
    h = _rms(x, npre_ref[...]).astype(BF16)
    q = (_bdot(h, wq_ref[...]) * (XA_HEAD_DIM ** -0.5 * LOG2E)).astype(BF16)
    nt = (((1,), (1,)), ((), ()))
    for hd in range(XA_HEADS):
        lo = hd * XA_HEAD_DIM
        k = kv_ref[:, lo:lo + XA_HEAD_DIM]
        v = kv_ref[:, D_MODEL + lo:D_MODEL + lo + XA_HEAD_DIM]
        s = lax.dot_general(q[:, lo:lo + XA_HEAD_DIM], k, nt, preferred_element_type=F32)
        p = jnp.exp2(s - jnp.max(s, axis=1, keepdims=True))
        o = _bdot(p.astype(BF16), v) / jnp.sum(p, axis=1, keepdims=True)
        oh_ref[:, lo:lo + XA_HEAD_DIM] = o.astype(BF16)
    o_ref[...] = x + _rms(_bdot(oh_ref[...], wo_ref[...]), npost_ref[...])


def _cross_attention(x3, kv3, n_pre, w_xq, w_xo, n_post):
    nb, seqlen, _ = x3.shape
    mlen = kv3.shape[1]
    tm = min(1024, seqlen)
    const = lambda b, i: (0, 0)
    wspec = pl.BlockSpec((D_MODEL, D_MODEL), const)
    vspec = pl.BlockSpec((1, D_MODEL), const)
    return pl.pallas_call(
        _xattn_kernel,
        grid=(nb, seqlen // tm),
        in_specs=[
            pl.BlockSpec((None, tm, D_MODEL), lambda b, i: (b, i, 0)),
            pl.BlockSpec((None, mlen, 2 * D_MODEL), lambda b, i: (b, 0, 0)),
            vspec, wspec, wspec, vspec,
        ],
        out_specs=pl.BlockSpec((None, tm, D_MODEL), lambda b, i: (b, i, 0)),
        out_shape=jax.ShapeDtypeStruct((nb, seqlen, D_MODEL), F32),
        scratch_shapes=[pltpu.VMEM((tm, D_MODEL), BF16)],
        compiler_params=_params("parallel", "parallel"),
        name="cross_attention",
    )(x3, kv3, n_pre, w_xq, w_xo, n_post)


def _mlp_kernel(x_ref, npre_ref, w1_ref, w2_ref, npost_ref, o_ref):
    x = x_ref[...]
    h = _rms(x, npre_ref[...]).astype(BF16)
    f = jnp.zeros(x.shape, F32)
    for c in range(D_FF // D_MODEL):
        lo = c * D_MODEL
        a = jnp.maximum(_bdot(h, w1_ref[:, lo:lo + D_MODEL]), 0.0)
        f = f + _bdot((a * a).astype(BF16), w2_ref[lo:lo + D_MODEL, :])
    o_ref[...] = x + _rms(f, npost_ref[...])


def _mlp(x2, n_pre, w1, w2, n_post):
    t = x2.shape[0]
    tm = min(512, t)
    row = lambda i: (i, 0)
    const = lambda i: (0, 0)
    vspec = pl.BlockSpec((1, D_MODEL), const)
    return pl.pallas_call(
        _mlp_kernel,
        grid=(t // tm,),
        in_specs=[
            pl.BlockSpec((tm, D_MODEL), row),
            vspec,
            pl.BlockSpec((D_MODEL, D_FF), const),
            pl.BlockSpec((D_FF, D_MODEL), const),
            vspec,
        ],
        out_specs=pl.BlockSpec((tm, D_MODEL), row),
        out_shape=jax.ShapeDtypeStruct((t, D_MODEL), F32),
        compiler_params=_params("parallel"),
        name="mlp",
    )(x2, n_pre, w1, w2, n_post)


def _rope_tables(seqlen):
    inv = ROPE_THETA ** (-jnp.arange(0, DA_HEAD_DIM, 2, dtype=F32) / DA_HEAD_DIM)
    ang = jnp.arange(seqlen, dtype=F32)[:, None] * inv[None, :]
    cos = jnp.tile(jnp.cos(ang), (1, 4))
    sin = jnp.tile(jnp.concatenate([-jnp.sin(ang), jnp.sin(ang)], axis=1), (1, 2))
    return cos, sin


def kernel(x, mem, norm_mix_pre, w_in, b_gate, ssm_lambda_re, ssm_lambda_im, ssm_log_dt, ssm_b_re, ssm_b_im, ssm_c_re, ssm_c_im, ssm_d, w_glu, b_glu, w_ssm_proj, da_lambda_q1, da_lambda_k1, da_lambda_q2, da_lambda_k2, da_head_norm, w_da_proj, w_mix_out, norm_mix_post, norm_x_pre, norm_mem, w_xq, w_xkv, w_xo, norm_x_post, norm_ff_pre, w_ff1, w_ff2, norm_ff_post):
    nb, seqlen, _ = x.shape
    t = nb * seqlen
    depth = w_in.shape[0]
    tabs = _rope_tables(seqlen)
    x2 = x.reshape(t, D_MODEL)
    mem2 = mem.reshape(nb * mem.shape[1], D_MODEL)
    for l in range(depth):
        lam_init = 0.8 - 0.6 * math.exp(-0.3 * l)
        z2 = _in_projection(x2, norm_mix_pre[l][None], w_in[l].astype(BF16), b_gate[l][None], tabs, seqlen)
        z3 = z2.reshape(nb, seqlen, IN_COLS)
        bmat, cmat, lre, lim = _s5_matrices(ssm_lambda_re[l], ssm_lambda_im[l], ssm_log_dt[l],
                                            ssm_b_re[l], ssm_b_im[l], ssm_c_re[l], ssm_c_im[l])
        ys = _s5_branch(z3, bmat, cmat, lre, lim, ssm_d[l][None], w_glu[l].astype(BF16), b_glu[l][None])
        lmb = jnp.stack([da_lambda_q1[l], da_lambda_k1[l], da_lambda_q2[l], da_lambda_k2[l]])
        ya = _diff_attention(z3, lmb, da_head_norm[l][None], lam_init)
        x2 = _merge(ys.reshape(t, D_MODEL), ya.reshape(t, D_MODEL), z2, x2,
                    w_ssm_proj[l].astype(BF16), w_da_proj[l].astype(BF16), w_mix_out[l].astype(BF16),
                    norm_mix_post[l][None])
        kv = _memory_kv(mem2, norm_mem[l][None], w_xkv[l].astype(BF16))
        x3 = _cross_attention(x2.reshape(nb, seqlen, D_MODEL), kv.reshape(nb, -1, 2 * D_MODEL),
                              norm_x_pre[l][None], w_xq[l].astype(BF16), w_xo[l].astype(BF16),
                              norm_x_post[l][None])
        x2 = _mlp(x3.reshape(t, D_MODEL), norm_ff_pre[l][None], w_ff1[l].astype(BF16),
                  w_ff2[l].astype(BF16), norm_ff_post[l][None])
    return x2.reshape(nb, seqlen, D_MODEL)
```

```python
import functools
import math

import jax
import jax.numpy as jnp
from jax import lax
from jax.experimental import pallas as pl
from jax.experimental.pallas import tpu as pltpu

F32 = jnp.float32
BF16 = jnp.bfloat16

D_MODEL = 1024
EPS = 1e-6
CHUNK = 64
SSM_GROUP = 16
SSM_GROUPS = 64
SSM_STATE = 64
SSM_SLABS = 8
SLAB_GROUPS = SSM_GROUPS // SSM_SLABS
SLAB_STATES = SLAB_GROUPS * SSM_STATE
NLB = 2 * SLAB_STATES // 128
SLAB_ROW_PAD = 8
DA_HEADS = 8
DA_HEAD_DIM = 64
DA_V_DIM = 128
ROPE_THETA = 10000.0
ATTN_HEADS_PER_STEP = 2
XA_HEADS = 4
XA_HEAD_DIM = 256
D_FF = 4096
IN_COLS = 6144
LOG2E = 1.4426950408889634
NEG_BIG = -1e30
VMEM_LIMIT_BYTES = 56 * 1024 * 1024


def _params(*semantics):
    return pltpu.CompilerParams(dimension_semantics=semantics, vmem_limit_bytes=VMEM_LIMIT_BYTES)


def _rms(x, g):
    return x * lax.rsqrt(jnp.mean(x * x, axis=-1, keepdims=True) + EPS) * g


def _bdot(a, b):
    return jnp.dot(a, b, preferred_element_type=F32)


def _wdot(a, w):
    return _bdot(a, w.astype(BF16))


def _sigmoid(x):
    return 0.5 * jnp.tanh(0.5 * x) + 0.5


def _inproj_kernel(x_ref, g_ref, w_ref, bg_ref, cos_ref, sin_ref, o_ref):
    tm = x_ref.shape[0]
    hn = _rms(x_ref[...], g_ref[...]).astype(BF16)
    lane = lax.broadcasted_iota(jnp.int32, (tm, 128), 1)
    first_half = (lane % 64) < 32

    def rope(acc, col0, scale):
        c = cos_ref[...] * scale
        s = sin_ref[...] * scale
        for h in range(DA_HEADS):
            blk = acc[:, h * 128:(h + 1) * 128]
            partner = jnp.where(first_half, pltpu.roll(blk, 96, 1), pltpu.roll(blk, 32, 1))
            o_ref[:, col0 + h * 128:col0 + (h + 1) * 128] = (blk * c + partner * s).astype(BF16)

    for j in range(IN_COLS // D_MODEL):
        col0 = j * D_MODEL
        acc = _wdot(hn, w_ref[:, col0:col0 + D_MODEL])
        if j == 1:
            rope(acc, col0, DA_HEAD_DIM ** -0.5 * LOG2E)
        elif j == 2:
            rope(acc, col0, 1.0)
        elif j >= 4:
            gate = _sigmoid(acc + bg_ref[:, col0 - 4 * D_MODEL:col0 - 3 * D_MODEL])
            o_ref[:, col0:col0 + D_MODEL] = gate.astype(BF16)
        else:
            o_ref[:, col0:col0 + D_MODEL] = acc.astype(BF16)


def _in_projection(x2, gain, w_in, b_gate, tabs, seqlen):
    t = x2.shape[0]
    tm = min(512, seqlen)
    nl = seqlen // tm
    const = lambda i: (0, 0)
    tab_spec = pl.BlockSpec((tm, 128), lambda i: (i % nl, 0))
    return pl.pallas_call(
        _inproj_kernel,
        grid=(t // tm,),
        in_specs=[
            pl.BlockSpec((tm, D_MODEL), lambda i: (i, 0)),
            pl.BlockSpec((1, D_MODEL), const),
            pl.BlockSpec((D_MODEL, IN_COLS), const, pipeline_mode=pl.Buffered(1)),
            pl.BlockSpec((1, 2 * D_MODEL), const),
            tab_spec, tab_spec,
        ],
        out_specs=pl.BlockSpec((tm, IN_COLS), lambda i: (i, 0)),
        out_shape=jax.ShapeDtypeStruct((t, IN_COLS), BF16),
        compiler_params=_params("parallel"),
        name="in_projection",
    )(x2, gain, w_in, b_gate, *tabs)


def _s5_kernel(u_ref, bm_ref, cm_ref, lre_ref, lim_ref, d_ref, wg_ref, bgl_ref, o_ref,
               bu_ref, st_ref, wc_ref, il_ref, ue_ref, uo_ref, yo_ref, w_ref, k0_ref):
    nb, tc, _ = u_ref.shape
    tp = tc // 2
    pitch = bu_ref.shape[2] // SSM_SLABS
    c = pl.program_id(0)

    @pl.when(c == 0)
    def _():
        st_ref[...] = jnp.zeros(st_ref.shape, F32)
        wc_ref[...] = jnp.zeros(wc_ref.shape, F32)

    for b in range(nb):
        for j in range(SSM_SLABS):
            il_ref[j] = u_ref[b, :, j * 128:(j + 1) * 128].astype(F32)
        for j in range(SSM_SLABS):
            u_e = il_ref[j, pl.ds(0, tp, stride=2), :]
            u_o = il_ref[j, pl.ds(1, tp, stride=2), :]
            ue_ref[b * tp:(b + 1) * tp, j * 128:(j + 1) * 128] = u_e
            uo_ref[b * tp:(b + 1) * tp, j * 128:(j + 1) * 128] = u_o
            lhs = jnp.concatenate([u_e, u_o], axis=1).astype(BF16)
            bu = _bdot(lhs, bm_ref[j])
            for k in range(NLB):
                bu_ref[b, k, j * pitch:j * pitch + tp, :] = bu[:, k * 128:(k + 1) * 128]
            k0_ref[b * tp:(b + 1) * tp, j * 128:(j + 1) * 128] = bu[:, NLB * 128:]

    ar = [lre_ref[:, k * 128:(k + 1) * 128] for k in range(NLB // 2)]
    ai = [lim_ref[:, k * 128:(k + 1) * 128] for k in range(NLB // 2)]

    def step(t, carry):
        out = []
        for b in range(nb):
            xs = carry[b]
            new = [None] * NLB
            for k in range(NLB // 2):
                xr, xi = xs[k], xs[k + NLB // 2]
                rows = pl.ds(t, SSM_SLABS, stride=pitch)
                new[k] = ar[k] * xr - ai[k] * xi + bu_ref[b, k, rows, :]
                new[k + NLB // 2] = ar[k] * xi + ai[k] * xr + bu_ref[b, k + NLB // 2, rows, :]
                bu_ref[b, k, rows, :] = new[k]
                bu_ref[b, k + NLB // 2, rows, :] = new[k + NLB // 2]
            out.append(tuple(new))
        return tuple(out)

    init = tuple(tuple(st_ref[b, k] for k in range(NLB)) for b in range(nb))
    fin = lax.fori_loop(0, tp, step, init, unroll=8)
    for b in range(nb):
        for k in range(NLB):
            st_ref[b, k] = fin[b][k]

    for b in range(nb):
        for j in range(SSM_SLABS):
            xs = jnp.concatenate([bu_ref[b, k, j * pitch:j * pitch + tp, :] for k in range(NLB)], axis=1)
            r = _bdot(xs.astype(BF16), cm_ref[j])
            yo_ref[b * tp:(b + 1) * tp, j * 128:(j + 1) * 128] = r[:, :128]
            w_ref[b * tp:(b + 1) * tp, j * 128:(j + 1) * 128] = r[:, 128:]

    def glu(y):
        ys = 0.5 * y * (1.0 + jnp.tanh(math.sqrt(2.0 / math.pi) * (y + 0.044715 * (y * y * y))))
        gate = _sigmoid(_wdot(ys.astype(BF16), wg_ref[...]) + bgl_ref[...])
        return ys * gate

    first_row = lax.broadcasted_iota(jnp.int32, (tp, D_MODEL), 0) == 0
    shifted = []
    for b in range(nb):
        w = w_ref[b * tp:(b + 1) * tp, :]
        shifted.append(jnp.where(first_row, wc_ref[b], pltpu.roll(w, 1, 0)))
        wc_ref[b] = w[tp - 1:tp, :]
    yo_ref[...] = glu(yo_ref[...] + d_ref[...] * uo_ref[...])
    w_ref[...] = glu(jnp.concatenate(shifted, axis=0) + k0_ref[...] + d_ref[...] * ue_ref[...])
    for b in range(nb):
        for j in range(SSM_SLABS):
            il_ref[j, pl.ds(0, tp, stride=2), :] = w_ref[b * tp:(b + 1) * tp, j * 128:(j + 1) * 128]
            il_ref[j, pl.ds(1, tp, stride=2), :] = yo_ref[b * tp:(b + 1) * tp, j * 128:(j + 1) * 128]
        for j in range(SSM_SLABS):
            o_ref[b, :, j * 128:(j + 1) * 128] = il_ref[j].astype(BF16)


def _s5_branch(z3, bmat, cmat, lam_re, lam_im, d_skip, w_glu, b_glu):
    nb, seqlen, _ = z3.shape
    tc = min(512, seqlen)
    tp = tc // 2
    const2 = lambda c: (0, 0)
    const3 = lambda c: (0, 0, 0)
    single = pl.Buffered(1)
    return pl.pallas_call(
        _s5_kernel,
        grid=(seqlen // tc,),
        in_specs=[
            pl.BlockSpec((nb, tc, D_MODEL), lambda c: (0, c, 0)),
            pl.BlockSpec(bmat.shape, const3, pipeline_mode=single),
            pl.BlockSpec(cmat.shape, const3, pipeline_mode=single),
            pl.BlockSpec(lam_re.shape, const2),
            pl.BlockSpec(lam_im.shape, const2),
            pl.BlockSpec((1, D_MODEL), const2),
            pl.BlockSpec((D_MODEL, D_MODEL), const2, pipeline_mode=single),
            pl.BlockSpec((1, D_MODEL), const2),
        ],
        out_specs=pl.BlockSpec((nb, tc, D_MODEL), lambda c: (0, c, 0)),
        out_shape=jax.ShapeDtypeStruct((nb, seqlen, D_MODEL), BF16),
        scratch_shapes=[
            pltpu.VMEM((nb, NLB, SSM_SLABS * (tp + SLAB_ROW_PAD), 128), F32),
            pltpu.VMEM((nb, NLB, SSM_SLABS, 128), F32),
            pltpu.VMEM((nb, 1, D_MODEL), F32),
            pltpu.VMEM((SSM_SLABS, tc, 128), F32),
        ] + [pltpu.VMEM((nb * tp, D_MODEL), F32)] * 5,
        compiler_params=_params("arbitrary"),
        name="s5_branch",
    )(z3, bmat, cmat, lam_re, lam_im, d_skip, w_glu, b_glu)


def _s5_matrices(lam_re, lam_im, log_dt, b_re, b_im, c_re, c_im):
    dt = jnp.exp(log_dt)[:, None]
    mag = jnp.exp(lam_re * dt)
    lb_re, lb_im = mag * jnp.cos(lam_im * dt), mag * jnp.sin(lam_im * dt)
    den = lam_re * lam_re + lam_im * lam_im
    cf_re = ((lb_re - 1.0) * lam_re + lb_im * lam_im) / den
    cf_im = (lb_im * lam_re - (lb_re - 1.0) * lam_im) / den
    eye = jnp.eye(SLAB_GROUPS, dtype=F32)[None, :, None, :, None]
    by_slab = (SSM_SLABS, SLAB_GROUPS)

    def col_factor(x):
        return x.reshape(by_slab + (SSM_STATE,))[:, None, None, :, :]

    def row_factor(x):
        return x.reshape(by_slab + (SSM_STATE,))[:, :, :, None, None]

    def in_place(x):
        return x.reshape(by_slab + (SSM_STATE, SSM_GROUP)).transpose(0, 1, 3, 2)[:, :, :, None, :] * eye

    def out_place(x):
        return x.reshape(by_slab + (SSM_GROUP, SSM_STATE)).transpose(0, 3, 1, 2)[:, None] * eye

    def in_flat(x):
        return x.reshape(SSM_SLABS, 128, SLAB_STATES)

    def out_flat(x):
        return x.reshape(SSM_SLABS, SLAB_STATES, 128)

    br, bi = in_place(b_re), in_place(b_im)
    bo_re = col_factor(cf_re) * br - col_factor(cf_im) * bi
    bo_im = col_factor(cf_re) * bi + col_factor(cf_im) * br
    be_re = col_factor(lb_re) * bo_re - col_factor(lb_im) * bo_im
    be_im = col_factor(lb_re) * bo_im + col_factor(lb_im) * bo_re
    cr, ci = out_place(c_re), out_place(c_im)
    cn_re = row_factor(lb_re) * cr - row_factor(lb_im) * ci
    cn_im = row_factor(lb_re) * ci + row_factor(lb_im) * cr
    b_odd = jnp.concatenate([in_flat(bo_re), in_flat(bo_im)], axis=2)
    cur = jnp.concatenate([out_flat(cr), out_flat(-ci)], axis=1)
    k0 = jnp.einsum('jrs,jsc->jrc', b_odd, cur, precision=lax.Precision.HIGHEST)
    top = jnp.concatenate([in_flat(be_re), in_flat(be_im), k0], axis=2)
    bot = jnp.concatenate([b_odd, jnp.zeros_like(k0)], axis=2)
    bmat = jnp.concatenate([top, bot], axis=1).astype(BF16)
    nxt = jnp.concatenate([out_flat(cn_re), out_flat(-cn_im)], axis=1)
    cmat = jnp.concatenate([cur, nxt], axis=2).astype(BF16)
    lre = (lb_re * lb_re - lb_im * lb_im).reshape(SSM_SLABS, SLAB_STATES)
    lim = (2.0 * lb_re * lb_im).reshape(SSM_SLABS, SLAB_STATES)
    return bmat, cmat, lre, lim


def _dattn_kernel(q_ref, k_ref, v_ref, lmb_ref, hn_ref, o_ref,
                  qm_ref, vx_ref, sx_ref, sy_ref, m_ref, acc_ref, *, lam_init):
    tq = q_ref.shape[0]
    tk = tq
    nstrips = tk // 128
    nheads = q_ref.shape[1] // 128
    qi = pl.program_id(2)
    lane = lax.broadcasted_iota(jnp.int32, (tq, 128), 1)
    nt = (((1,), (1,)), ((), ()))

    @pl.when(qi == 0)
    def _():
        for hh in range(nheads):
            vx_ref[hh, :, :DA_V_DIM] = v_ref[:, hh * 128:(hh + 1) * 128]
            vx_ref[hh, :, DA_V_DIM:] = jnp.ones((v_ref.shape[0], DA_V_DIM), BF16)

    for hh in range(nheads):
        q = q_ref[:, hh * 128:(hh + 1) * 128]
        zero = jnp.zeros_like(q)
        qm_ref[2 * hh] = jnp.where(lane < DA_HEAD_DIM, q, zero)
        qm_ref[2 * hh + 1] = jnp.where(lane >= DA_HEAD_DIM, q, zero)
    m_ref[...] = jnp.full(m_ref.shape, NEG_BIG, F32)
    acc_ref[...] = jnp.zeros(acc_ref.shape, F32)

    def scores(ci, ki, s_ref):
        start = pl.multiple_of(ki * tk, tk)
        hh = ci // 2
        kb = k_ref[pl.ds(start, tk), hh * 128:(hh + 1) * 128]
        s_ref[ci] = lax.dot_general(qm_ref[ci], kb, nt, preferred_element_type=F32)

    def update(ci, ki, s_ref):
        start = pl.multiple_of(ki * tk, tk)
        hh = ci // 2
        vb = vx_ref[hh, pl.ds(start, tk), :]
        s = s_ref[ci]
        m_prev = m_ref[ci]
        m_next = jnp.maximum(m_prev, jnp.max(s, axis=1, keepdims=True))
        alpha = jnp.exp2(m_prev - m_next)
        ps = [jnp.exp2(s[:, c * 128:(c + 1) * 128] - m_next) for c in range(nstrips)]
        p = jnp.concatenate(ps, axis=1).astype(BF16)
        acc_ref[ci] = jnp.concatenate([alpha, alpha], axis=1) * acc_ref[ci] + _bdot(p, vb)
        m_ref[ci] = m_next

    def update_diag(ci, ki, s_ref):
        half = tq // 2
        start = pl.multiple_of(ki * tk, tk)
        hh = ci // 2
        row = lax.broadcasted_iota(jnp.int32, (half, half), 0) // CHUNK
        col = lax.broadcasted_iota(jnp.int32, (half, half), 1) // CHUNK
        visible = col <= row
        for part in range(2):
            rows = slice(part * half, (part + 1) * half)
            ncols = half * (part + 1)
            strips = []
            for c in range(ncols // 128):
                s = s_ref[ci, rows, c * 128:(c + 1) * 128]
                off = c * 128 - part * half
                if off >= 0:
                    s = jnp.where(visible[:, off:off + 128], s, NEG_BIG)
                strips.append(s)
            m_prev = m_ref[ci, rows, :]
            m_cur = functools.reduce(jnp.maximum, strips)
            m_next = jnp.maximum(m_prev, jnp.max(m_cur, axis=1, keepdims=True))
            alpha = jnp.exp2(m_prev - m_next)
            p = jnp.concatenate([jnp.exp2(s - m_next) for s in strips], axis=1).astype(BF16)
            pv = _bdot(p, vx_ref[hh, pl.ds(start, ncols), :])
            acc_ref[ci, rows, :] = jnp.concatenate([alpha, alpha], axis=1) * acc_ref[ci, rows, :] + pv
            m_ref[ci, rows, :] = m_next

    def stage(k_next, s_next_ref, k_cur, s_cur_ref, diag=False):
        for ci in range(2 * nheads):
            if k_next is not None:
                scores(ci, k_next, s_next_ref)
            if diag:
                update_diag(ci, k_cur, s_cur_ref)
            else:
                update(ci, k_cur, s_cur_ref)

    for ci in range(2 * nheads):
        scores(ci, 0, sx_ref)

    def pair(p, carry):
        stage(2 * p + 1, sy_ref, 2 * p, sx_ref)
        stage(2 * p + 2, sx_ref, 2 * p + 1, sy_ref)
        return carry

    lax.fori_loop(0, qi // 2, pair, 0)

    @pl.when(qi % 2 == 0)
    def _():
        stage(None, None, qi, sx_ref, diag=True)

    @pl.when(qi % 2 == 1)
    def _():
        stage(qi, sy_ref, qi - 1, sx_ref)
        stage(None, None, qi, sy_ref, diag=True)

    lv = lmb_ref[...]
    lam = (jnp.exp(jnp.sum(lv[0:1] * lv[1:2], axis=1, keepdims=True))
           - jnp.exp(jnp.sum(lv[2:3] * lv[3:4], axis=1, keepdims=True)) + lam_init)
    for hh in range(nheads):
        o = (acc_ref[2 * hh, :, :DA_V_DIM] / acc_ref[2 * hh, :, DA_V_DIM:]
             - lam * (acc_ref[2 * hh + 1, :, :DA_V_DIM] / acc_ref[2 * hh + 1, :, DA_V_DIM:]))
        o = o * lax.rsqrt(jnp.mean(o * o, axis=-1, keepdims=True) + EPS) * hn_ref[...]
        o_ref[:, hh * 128:(hh + 1) * 128] = (o * (1.0 - lam_init)).astype(BF16)


def _diff_attention(z3, lmb, head_norm, lam_init):
    nb, seqlen, _ = z3.shape
    tq = min(512, seqlen)
    hps = ATTN_HEADS_PER_STEP
    width = hps * 128
    qcol, kcol, vcol = (c // width for c in (D_MODEL, 2 * D_MODEL, 3 * D_MODEL))
    return pl.pallas_call(
        functools.partial(_dattn_kernel, lam_init=lam_init),
        grid=(nb, DA_HEADS // hps, seqlen // tq),
        in_specs=[
            pl.BlockSpec((None, tq, width), lambda b, h, i: (b, i, qcol + h)),
            pl.BlockSpec((None, seqlen, width), lambda b, h, i: (b, 0, kcol + h)),
            pl.BlockSpec((None, seqlen, width), lambda b, h, i: (b, 0, vcol + h)),
            pl.BlockSpec((4, DA_HEAD_DIM), lambda b, h, i: (0, 0)),
            pl.BlockSpec((1, DA_V_DIM), lambda b, h, i: (0, 0)),
        ],
        out_specs=pl.BlockSpec((None, tq, width), lambda b, h, i: (b, i, h)),
        out_shape=jax.ShapeDtypeStruct((nb, seqlen, D_MODEL), BF16),
        scratch_shapes=[pltpu.VMEM((2 * hps, tq, 128), BF16),
                        pltpu.VMEM((hps, seqlen, 2 * DA_V_DIM), BF16),
                        pltpu.VMEM((2 * hps, tq, tq), F32), pltpu.VMEM((2 * hps, tq, tq), F32),
                        pltpu.VMEM((2 * hps, tq, 128), F32),
                        pltpu.VMEM((2 * hps, tq, 2 * DA_V_DIM), F32)],
        compiler_params=_params("parallel", "parallel", "arbitrary"),
        name="diff_attention",
    )(z3, z3, z3, lmb, head_norm)


def _merge_kernel(ys_ref, ya_ref, g_ref, x_ref, wa_ref, wb_ref, wo_ref, n_ref, o_ref):
    a = _wdot(ys_ref[...], wa_ref[...])
    b = _wdot(ya_ref[...], wb_ref[...])
    ga = g_ref[:, :D_MODEL].astype(F32)
    gb = g_ref[:, D_MODEL:].astype(F32)
    merged = (ga * a + gb * b).astype(BF16)
    o_ref[...] = x_ref[...] + _rms(_wdot(merged, wo_ref[...]), n_ref[...])


def _merge(ys2, ya2, z2, x2, w_a, w_b, w_o, gain):
    t = x2.shape[0]
    tm = min(1024, t)
    row = lambda i: (i, 0)
    const = lambda i: (0, 0)
    wspec = pl.BlockSpec((D_MODEL, D_MODEL), const, pipeline_mode=pl.Buffered(1))
    return pl.pallas_call(
        _merge_kernel,
        grid=(t // tm,),
        in_specs=[
            pl.BlockSpec((tm, D_MODEL), row),
            pl.BlockSpec((tm, D_MODEL), row),
            pl.BlockSpec((tm, 2 * D_MODEL), lambda i: (i, 2)),
            pl.BlockSpec((tm, D_MODEL), row),
            wspec, wspec, wspec,
            pl.BlockSpec((1, D_MODEL), const),
        ],
        out_specs=pl.BlockSpec((tm, D_MODEL), row),
        out_shape=jax.ShapeDtypeStruct((t, D_MODEL), F32),
        compiler_params=_params("parallel"),
        name="merge",
    )(ys2, ya2, z2, x2, w_a, w_b, w_o, gain)


def _memkv_kernel(m_ref, g_ref, w_ref, o_ref):
    o_ref[...] = _wdot(_rms(m_ref[...], g_ref[...]).astype(BF16), w_ref[...]).astype(BF16)


def _memory_kv(mem2, gain, w_xkv):
    rows = mem2.shape[0]
    const = lambda i: (0, 0)
    return pl.pallas_call(
        _memkv_kernel,
        grid=(1,),
        in_specs=[
            pl.BlockSpec((rows, D_MODEL), const),
            pl.BlockSpec((1, D_MODEL), const),
            pl.BlockSpec((D_MODEL, 2 * D_MODEL), const),
        ],
        out_specs=pl.BlockSpec((rows, 2 * D_MODEL), const),
        out_shape=jax.ShapeDtypeStruct((rows, 2 * D_MODEL), BF16),
        compiler_params=_params("arbitrary"),
        name="memory_kv",
    )(mem2, gain, w_xkv)


def _xattn_kernel(x_ref, kv_ref, npre_ref, wq_ref, wo_ref, npost_ref, o_ref, oh_ref):
    x = x_ref[...]
    h = _rms(x, npre_ref[...]).astype(BF16)
    q = (_wdot(h, wq_ref[...]) * (XA_HEAD_DIM ** -0.5 * LOG2E)).astype(BF16)
    nt = (((1,), (1,)), ((), ()))
    for hd in range(XA_HEADS):
        lo = hd * XA_HEAD_DIM
        k = kv_ref[:, lo:lo + XA_HEAD_DIM]
        v = kv_ref[:, D_MODEL + lo:D_MODEL + lo + XA_HEAD_DIM]
        s = lax.dot_general(q[:, lo:lo + XA_HEAD_DIM], k, nt, preferred_element_type=F32)
        p = jnp.exp2(s - jnp.max(s, axis=1, keepdims=True))
        o = _bdot(p.astype(BF16), v) / jnp.sum(p, axis=1, keepdims=True)
        oh_ref[:, lo:lo + XA_HEAD_DIM] = o.astype(BF16)
    o_ref[...] = x + _rms(_wdot(oh_ref[...], wo_ref[...]), npost_ref[...])


def _cross_attention(x3, kv3, n_pre, w_xq, w_xo, n_post):
    nb, seqlen, _ = x3.shape
    mlen = kv3.shape[1]
    tm = min(1024, seqlen)
    const = lambda b, i: (0, 0)
    wspec = pl.BlockSpec((D_MODEL, D_MODEL), const, pipeline_mode=pl.Buffered(1))
    vspec = pl.BlockSpec((1, D_MODEL), const)
    return pl.pallas_call(
        _xattn_kernel,
        grid=(nb, seqlen // tm),
        in_specs=[
            pl.BlockSpec((None, tm, D_MODEL), lambda b, i: (b, i, 0)),
            pl.BlockSpec((None, mlen, 2 * D_MODEL), lambda b, i: (b, 0, 0)),
            vspec, wspec, wspec, vspec,
        ],
        out_specs=pl.BlockSpec((None, tm, D_MODEL), lambda b, i: (b, i, 0)),
        out_shape=jax.ShapeDtypeStruct((nb, seqlen, D_MODEL), F32),
        scratch_shapes=[pltpu.VMEM((tm, D_MODEL), BF16)],
        compiler_params=_params("parallel", "parallel"),
        name="cross_attention",
    )(x3, kv3, n_pre, w_xq, w_xo, n_post)


def _mlp_kernel(x_ref, npre_ref, w1_ref, w2_ref, npost_ref, o_ref):
    x = x_ref[...]
    h = _rms(x, npre_ref[...]).astype(BF16)
    f = jnp.zeros(x.shape, F32)
    for c in range(D_FF // D_MODEL):
        lo = c * D_MODEL
        a = jnp.maximum(_wdot(h, w1_ref[:, lo:lo + D_MODEL]), 0.0)
        f = f + _wdot((a * a).astype(BF16), w2_ref[lo:lo + D_MODEL, :])
    o_ref[...] = x + _rms(f, npost_ref[...])


def _mlp(x2, n_pre, w1, w2, n_post):
    t = x2.shape[0]
    tm = min(512, t)
    row = lambda i: (i, 0)
    const = lambda i: (0, 0)
    vspec = pl.BlockSpec((1, D_MODEL), const)
    return pl.pallas_call(
        _mlp_kernel,
        grid=(t // tm,),
        in_specs=[
            pl.BlockSpec((tm, D_MODEL), row),
            vspec,
            pl.BlockSpec((D_MODEL, D_FF), const, pipeline_mode=pl.Buffered(1)),
            pl.BlockSpec((D_FF, D_MODEL), const, pipeline_mode=pl.Buffered(1)),
            vspec,
        ],
        out_specs=pl.BlockSpec((tm, D_MODEL), row),
        out_shape=jax.ShapeDtypeStruct((t, D_MODEL), F32),
        compiler_params=_params("parallel"),
        name="mlp",
    )(x2, n_pre, w1, w2, n_post)


def _rope_tables(seqlen):
    inv = ROPE_THETA ** (-jnp.arange(0, DA_HEAD_DIM, 2, dtype=F32) / DA_HEAD_DIM)
    ang = jnp.arange(seqlen, dtype=F32)[:, None] * inv[None, :]
    cos = jnp.tile(jnp.cos(ang), (1, 4))
    sin = jnp.tile(jnp.concatenate([-jnp.sin(ang), jnp.sin(ang)], axis=1), (1, 2))
    return cos, sin


def kernel(x, mem, norm_mix_pre, w_in, b_gate, ssm_lambda_re, ssm_lambda_im, ssm_log_dt, ssm_b_re, ssm_b_im, ssm_c_re, ssm_c_im, ssm_d, w_glu, b_glu, w_ssm_proj, da_lambda_q1, da_lambda_k1, da_lambda_q2, da_lambda_k2, da_head_norm, w_da_proj, w_mix_out, norm_mix_post, norm_x_pre, norm_mem, w_xq, w_xkv, w_xo, norm_x_post, norm_ff_pre, w_ff1, w_ff2, norm_ff_post):
    nb, seqlen, _ = x.shape
    t = nb * seqlen
    depth = w_in.shape[0]
    tabs = _rope_tables(seqlen)
    x2 = x.reshape(t, D_MODEL)
    mem2 = mem.reshape(nb * mem.shape[1], D_MODEL)
    for l in range(depth):
        lam_init = 0.8 - 0.6 * math.exp(-0.3 * l)
        z2 = _in_projection(x2, norm_mix_pre[l][None], w_in[l], b_gate[l][None], tabs, seqlen)
        z3 = z2.reshape(nb, seqlen, IN_COLS)
        bmat, cmat, lre, lim = _s5_matrices(ssm_lambda_re[l], ssm_lambda_im[l], ssm_log_dt[l],
                                            ssm_b_re[l], ssm_b_im[l], ssm_c_re[l], ssm_c_im[l])
        ys = _s5_branch(z3, bmat, cmat, lre, lim, ssm_d[l][None], w_glu[l], b_glu[l][None])
        lmb = jnp.stack([da_lambda_q1[l], da_lambda_k1[l], da_lambda_q2[l], da_lambda_k2[l]])
        ya = _diff_attention(z3, lmb, da_head_norm[l][None], lam_init)
        x2 = _merge(ys.reshape(t, D_MODEL), ya.reshape(t, D_MODEL), z2, x2,
                    w_ssm_proj[l], w_da_proj[l], w_mix_out[l],
                    norm_mix_post[l][None])
        kv = _memory_kv(mem2, norm_mem[l][None], w_xkv[l])
        x3 = _cross_attention(x2.reshape(nb, seqlen, D_MODEL), kv.reshape(nb, -1, 2 * D_MODEL),
                              norm_x_pre[l][None], w_xq[l], w_xo[l],
                              norm_x_post[l][None])
        x2 = _mlp(x3.reshape(t, D_MODEL), norm_ff_pre[l][None], w_ff1[l],
                  w_ff2[l], norm_ff_post[l][None])
    return x2.reshape(nb, seqlen, D_MODEL)
```

```python
import functools
import math

import jax
import jax.numpy as jnp
from jax import lax
from jax.experimental import pallas as pl
from jax.experimental.pallas import tpu as pltpu

F32 = jnp.float32
BF16 = jnp.bfloat16

D_MODEL = 1024
EPS = 1e-6
CHUNK = 64
SSM_GROUP = 16
SSM_GROUPS = 64
SSM_STATE = 64
SSM_SLABS = 8
SLAB_GROUPS = SSM_GROUPS // SSM_SLABS
SLAB_STATES = SLAB_GROUPS * SSM_STATE
NLB = 2 * SLAB_STATES // 128
SLAB_ROW_PAD = 8
DA_HEADS = 8
DA_HEAD_DIM = 64
DA_V_DIM = 128
ROPE_THETA = 10000.0
ATTN_HEADS_PER_STEP = 2
XA_HEADS = 4
XA_HEAD_DIM = 256
D_FF = 4096
IN_COLS = 6144
LOG2E = 1.4426950408889634
NEG_BIG = -1e30
VMEM_LIMIT_BYTES = 56 * 1024 * 1024


def _params(*semantics):
    return pltpu.CompilerParams(dimension_semantics=semantics, vmem_limit_bytes=VMEM_LIMIT_BYTES)


def _rms(x, g):
    return x * lax.rsqrt(jnp.mean(x * x, axis=-1, keepdims=True) + EPS) * g


def _bdot(a, b):
    return jnp.dot(a, b, preferred_element_type=F32)


def _wdot(a, w):
    return _bdot(a, w.astype(BF16))


def _sigmoid(x):
    return 0.5 * jnp.tanh(0.5 * x) + 0.5


def _inproj_kernel(x_ref, g_ref, w_ref, bg_ref, cos_ref, sin_ref, o_ref):
    tm = x_ref.shape[0]
    hn = _rms(x_ref[...], g_ref[...]).astype(BF16)
    lane = lax.broadcasted_iota(jnp.int32, (tm, 128), 1)
    first_half = (lane % 64) < 32

    def rope(acc, col0, scale):
        c = cos_ref[...] * scale
        s = sin_ref[...] * scale
        for h in range(DA_HEADS):
            blk = acc[:, h * 128:(h + 1) * 128]
            partner = jnp.where(first_half, pltpu.roll(blk, 96, 1), pltpu.roll(blk, 32, 1))
            o_ref[:, col0 + h * 128:col0 + (h + 1) * 128] = (blk * c + partner * s).astype(BF16)

    for j in range(IN_COLS // D_MODEL):
        col0 = j * D_MODEL
        acc = _wdot(hn, w_ref[:, col0:col0 + D_MODEL])
        if j == 1:
            rope(acc, col0, DA_HEAD_DIM ** -0.5 * LOG2E)
        elif j == 2:
            rope(acc, col0, 1.0)
        elif j >= 4:
            gate = _sigmoid(acc + bg_ref[:, col0 - 4 * D_MODEL:col0 - 3 * D_MODEL])
            o_ref[:, col0:col0 + D_MODEL] = gate.astype(BF16)
        else:
            o_ref[:, col0:col0 + D_MODEL] = acc.astype(BF16)


def _in_projection(x2, gain, w_in, b_gate, tabs, seqlen):
    t = x2.shape[0]
    tm = min(512, seqlen)
    nl = seqlen // tm
    const = lambda i: (0, 0)
    tab_spec = pl.BlockSpec((tm, 128), lambda i: (i % nl, 0))
    return pl.pallas_call(
        _inproj_kernel,
        grid=(t // tm,),
        in_specs=[
            pl.BlockSpec((tm, D_MODEL), lambda i: (i, 0)),
            pl.BlockSpec((1, D_MODEL), const),
            pl.BlockSpec((D_MODEL, IN_COLS), const, pipeline_mode=pl.Buffered(1)),
            pl.BlockSpec((1, 2 * D_MODEL), const),
            tab_spec, tab_spec,
        ],
        out_specs=pl.BlockSpec((tm, IN_COLS), lambda i: (i, 0)),
        out_shape=jax.ShapeDtypeStruct((t, IN_COLS), BF16),
        compiler_params=_params("parallel"),
        name="in_projection",
    )(x2, gain, w_in, b_gate, *tabs)


def _s5_kernel(u_ref, bm_ref, cm_ref, lre_ref, lim_ref, d_ref, wg_ref, bgl_ref, o_ref,
               bu_ref, st_ref, wc_ref, il_ref, ue_ref, uo_ref, yo_ref, w_ref, k0_ref):
    nb, tc, _ = u_ref.shape
    tp = tc // 2
    pitch = bu_ref.shape[2] // SSM_SLABS
    c = pl.program_id(0)

    @pl.when(c == 0)
    def _():
        st_ref[...] = jnp.zeros(st_ref.shape, F32)
        wc_ref[...] = jnp.zeros(wc_ref.shape, F32)

    for b in range(nb):
        for j in range(SSM_SLABS):
            il_ref[j] = u_ref[b, :, j * 128:(j + 1) * 128].astype(F32)
        for j in range(SSM_SLABS):
            u_e = il_ref[j, pl.ds(0, tp, stride=2), :]
            u_o = il_ref[j, pl.ds(1, tp, stride=2), :]
            ue_ref[b * tp:(b + 1) * tp, j * 128:(j + 1) * 128] = u_e
            uo_ref[b * tp:(b + 1) * tp, j * 128:(j + 1) * 128] = u_o
            lhs = jnp.concatenate([u_e, u_o], axis=1).astype(BF16)
            bu = _bdot(lhs, bm_ref[j])
            for k in range(NLB):
                bu_ref[b, k, j * pitch:j * pitch + tp, :] = bu[:, k * 128:(k + 1) * 128]
            k0_ref[b * tp:(b + 1) * tp, j * 128:(j + 1) * 128] = bu[:, NLB * 128:]

    ar = [lre_ref[:, k * 128:(k + 1) * 128] for k in range(NLB // 2)]
    ai = [lim_ref[:, k * 128:(k + 1) * 128] for k in range(NLB // 2)]

    def step(t, carry):
        out = []
        for b in range(nb):
            xs = carry[b]
            new = [None] * NLB
            for k in range(NLB // 2):
                xr, xi = xs[k], xs[k + NLB // 2]
                rows = pl.ds(t, SSM_SLABS, stride=pitch)
                new[k] = ar[k] * xr - ai[k] * xi + bu_ref[b, k, rows, :]
                new[k + NLB // 2] = ar[k] * xi + ai[k] * xr + bu_ref[b, k + NLB // 2, rows, :]
                bu_ref[b, k, rows, :] = new[k]
                bu_ref[b, k + NLB // 2, rows, :] = new[k + NLB // 2]
            out.append(tuple(new))
        return tuple(out)

    init = tuple(tuple(st_ref[b, k] for k in range(NLB)) for b in range(nb))
    fin = lax.fori_loop(0, tp, step, init, unroll=8)
    for b in range(nb):
        for k in range(NLB):
            st_ref[b, k] = fin[b][k]

    for b in range(nb):
        for j in range(SSM_SLABS):
            xs = jnp.concatenate([bu_ref[b, k, j * pitch:j * pitch + tp, :] for k in range(NLB)], axis=1)
            r = _bdot(xs.astype(BF16), cm_ref[j])
            yo_ref[b * tp:(b + 1) * tp, j * 128:(j + 1) * 128] = r[:, :128]
            w_ref[b * tp:(b + 1) * tp, j * 128:(j + 1) * 128] = r[:, 128:]

    def glu(y):
        ys = 0.5 * y * (1.0 + jnp.tanh(math.sqrt(2.0 / math.pi) * (y + 0.044715 * (y * y * y))))
        gate = _sigmoid(_wdot(ys.astype(BF16), wg_ref[...]) + bgl_ref[...])
        return ys * gate

    first_row = lax.broadcasted_iota(jnp.int32, (tp, D_MODEL), 0) == 0
    shifted = []
    for b in range(nb):
        w = w_ref[b * tp:(b + 1) * tp, :]
        shifted.append(jnp.where(first_row, wc_ref[b], pltpu.roll(w, 1, 0)))
        wc_ref[b] = w[tp - 1:tp, :]
    yo_ref[...] = glu(yo_ref[...] + d_ref[...] * uo_ref[...])
    w_ref[...] = glu(jnp.concatenate(shifted, axis=0) + k0_ref[...] + d_ref[...] * ue_ref[...])
    for b in range(nb):
        for j in range(SSM_SLABS):
            il_ref[j, pl.ds(0, tp, stride=2), :] = w_ref[b * tp:(b + 1) * tp, j * 128:(j + 1) * 128]
            il_ref[j, pl.ds(1, tp, stride=2), :] = yo_ref[b * tp:(b + 1) * tp, j * 128:(j + 1) * 128]
        for j in range(SSM_SLABS):
            o_ref[b, :, j * 128:(j + 1) * 128] = il_ref[j].astype(BF16)


def _s5_branch(z3, bmat, cmat, lam_re, lam_im, d_skip, w_glu, b_glu):
    nb, seqlen, _ = z3.shape
    tc = min(512, seqlen)
    tp = tc // 2
    const2 = lambda c: (0, 0)
    const3 = lambda c: (0, 0, 0)
    single = pl.Buffered(1)
    return pl.pallas_call(
        _s5_kernel,
        grid=(seqlen // tc,),
        in_specs=[
            pl.BlockSpec((nb, tc, D_MODEL), lambda c: (0, c, 0)),
            pl.BlockSpec(bmat.shape, const3, pipeline_mode=single),
            pl.BlockSpec(cmat.shape, const3, pipeline_mode=single),
            pl.BlockSpec(lam_re.shape, const2),
            pl.BlockSpec(lam_im.shape, const2),
            pl.BlockSpec((1, D_MODEL), const2),
            pl.BlockSpec((D_MODEL, D_MODEL), const2, pipeline_mode=single),
            pl.BlockSpec((1, D_MODEL), const2),
        ],
        out_specs=pl.BlockSpec((nb, tc, D_MODEL), lambda c: (0, c, 0)),
        out_shape=jax.ShapeDtypeStruct((nb, seqlen, D_MODEL), BF16),
        scratch_shapes=[
            pltpu.VMEM((nb, NLB, SSM_SLABS * (tp + SLAB_ROW_PAD), 128), F32),
            pltpu.VMEM((nb, NLB, SSM_SLABS, 128), F32),
            pltpu.VMEM((nb, 1, D_MODEL), F32),
            pltpu.VMEM((SSM_SLABS, tc, 128), F32),
        ] + [pltpu.VMEM((nb * tp, D_MODEL), F32)] * 5,
        compiler_params=_params("arbitrary"),
        name="s5_branch",
    )(z3, bmat, cmat, lam_re, lam_im, d_skip, w_glu, b_glu)


def _s5_matrices(lam_re, lam_im, log_dt, b_re, b_im, c_re, c_im):
    dt = jnp.exp(log_dt)[:, None]
    mag = jnp.exp(lam_re * dt)
    lb_re, lb_im = mag * jnp.cos(lam_im * dt), mag * jnp.sin(lam_im * dt)
    den = lam_re * lam_re + lam_im * lam_im
    cf_re = ((lb_re - 1.0) * lam_re + lb_im * lam_im) / den
    cf_im = (lb_im * lam_re - (lb_re - 1.0) * lam_im) / den
    eye = jnp.eye(SLAB_GROUPS, dtype=F32)[None, :, None, :, None]
    by_slab = (SSM_SLABS, SLAB_GROUPS)

    def col_factor(x):
        return x.reshape(by_slab + (SSM_STATE,))[:, None, None, :, :]

    def row_factor(x):
        return x.reshape(by_slab + (SSM_STATE,))[:, :, :, None, None]

    def in_place(x):
        return x.reshape(by_slab + (SSM_STATE, SSM_GROUP)).transpose(0, 1, 3, 2)[:, :, :, None, :] * eye

    def out_place(x):
        return x.reshape(by_slab + (SSM_GROUP, SSM_STATE)).transpose(0, 3, 1, 2)[:, None] * eye

    def in_flat(x):
        return x.reshape(SSM_SLABS, 128, SLAB_STATES)

    def out_flat(x):
        return x.reshape(SSM_SLABS, SLAB_STATES, 128)

    br, bi = in_place(b_re), in_place(b_im)
    bo_re = col_factor(cf_re) * br - col_factor(cf_im) * bi
    bo_im = col_factor(cf_re) * bi + col_factor(cf_im) * br
    be_re = col_factor(lb_re) * bo_re - col_factor(lb_im) * bo_im
    be_im = col_factor(lb_re) * bo_im + col_factor(lb_im) * bo_re
    cr, ci = out_place(c_re), out_place(c_im)
    cn_re = row_factor(lb_re) * cr - row_factor(lb_im) * ci
    cn_im = row_factor(lb_re) * ci + row_factor(lb_im) * cr
    b_odd = jnp.concatenate([in_flat(bo_re), in_flat(bo_im)], axis=2)
    cur = jnp.concatenate([out_flat(cr), out_flat(-ci)], axis=1)
    k0 = jnp.einsum('jrs,jsc->jrc', b_odd, cur, precision=lax.Precision.HIGHEST)
    top = jnp.concatenate([in_flat(be_re), in_flat(be_im), k0], axis=2)
    bot = jnp.concatenate([b_odd, jnp.zeros_like(k0)], axis=2)
    bmat = jnp.concatenate([top, bot], axis=1).astype(BF16)
    nxt = jnp.concatenate([out_flat(cn_re), out_flat(-cn_im)], axis=1)
    cmat = jnp.concatenate([cur, nxt], axis=2).astype(BF16)
    lre = (lb_re * lb_re - lb_im * lb_im).reshape(SSM_SLABS, SLAB_STATES)
    lim = (2.0 * lb_re * lb_im).reshape(SSM_SLABS, SLAB_STATES)
    return bmat, cmat, lre, lim


def _dattn_kernel(q_ref, qn_ref, k_ref, v_ref, lmb_ref, hn_ref, o_ref,
                  qm_ref, vx_ref, sx_ref, sy_ref, m_ref, acc_ref, *, lam_init):
    tq = q_ref.shape[0]
    tk = tq
    nstrips = tk // 128
    nheads = q_ref.shape[1] // 128
    qi = pl.program_id(2)
    lane = lax.broadcasted_iota(jnp.int32, (tq, 128), 1)
    nt = (((1,), (1,)), ((), ()))

    @pl.when(qi == 0)
    def _():
        for hh in range(nheads):
            vx_ref[hh, :, :DA_V_DIM] = v_ref[:, hh * 128:(hh + 1) * 128]
            vx_ref[hh, :, DA_V_DIM:] = jnp.ones((v_ref.shape[0], DA_V_DIM), BF16)

    def mask_maps(src_ref):
        for hh in range(nheads):
            q = src_ref[:, hh * 128:(hh + 1) * 128]
            zero = jnp.zeros_like(q)
            qm_ref[2 * hh] = jnp.where(lane < DA_HEAD_DIM, q, zero)
            qm_ref[2 * hh + 1] = jnp.where(lane >= DA_HEAD_DIM, q, zero)

    mask_maps(q_ref)
    m_ref[...] = jnp.full(m_ref.shape, NEG_BIG, F32)
    acc_ref[...] = jnp.zeros(acc_ref.shape, F32)

    def scores(ci, ki, s_ref):
        start = pl.multiple_of(ki * tk, tk)
        hh = ci // 2
        kb = k_ref[pl.ds(start, tk), hh * 128:(hh + 1) * 128]
        s_ref[ci] = lax.dot_general(qm_ref[ci], kb, nt, preferred_element_type=F32)

    def update(ci, ki, s_ref):
        start = pl.multiple_of(ki * tk, tk)
        hh = ci // 2
        vb = vx_ref[hh, pl.ds(start, tk), :]
        s = s_ref[ci]
        m_prev = m_ref[ci]
        m_next = jnp.maximum(m_prev, jnp.max(s, axis=1, keepdims=True))
        alpha = jnp.exp2(m_prev - m_next)
        ps = [jnp.exp2(s[:, c * 128:(c + 1) * 128] - m_next) for c in range(nstrips)]
        p = jnp.concatenate(ps, axis=1).astype(BF16)
        acc_ref[ci] = jnp.concatenate([alpha, alpha], axis=1) * acc_ref[ci] + _bdot(p, vb)
        m_ref[ci] = m_next

    def update_diag(ci, ki, s_ref):
        half = tq // 2
        start = pl.multiple_of(ki * tk, tk)
        hh = ci // 2
        row = lax.broadcasted_iota(jnp.int32, (half, half), 0) // CHUNK
        col = lax.broadcasted_iota(jnp.int32, (half, half), 1) // CHUNK
        visible = col <= row
        for part in range(2):
            rows = slice(part * half, (part + 1) * half)
            ncols = half * (part + 1)
            strips = []
            for c in range(ncols // 128):
                s = s_ref[ci, rows, c * 128:(c + 1) * 128]
                off = c * 128 - part * half
                if off >= 0:
                    s = jnp.where(visible[:, off:off + 128], s, NEG_BIG)
                strips.append(s)
            m_prev = m_ref[ci, rows, :]
            m_cur = functools.reduce(jnp.maximum, strips)
            m_next = jnp.maximum(m_prev, jnp.max(m_cur, axis=1, keepdims=True))
            alpha = jnp.exp2(m_prev - m_next)
            p = jnp.concatenate([jnp.exp2(s - m_next) for s in strips], axis=1).astype(BF16)
            pv = _bdot(p, vx_ref[hh, pl.ds(start, ncols), :])
            acc_ref[ci, rows, :] = jnp.concatenate([alpha, alpha], axis=1) * acc_ref[ci, rows, :] + pv
            m_ref[ci, rows, :] = m_next

    def stage(k_next, s_next_ref, k_cur, s_cur_ref, diag=False):
        for ci in range(2 * nheads):
            if k_next is not None:
                scores(ci, k_next, s_next_ref)
            if diag:
                update_diag(ci, k_cur, s_cur_ref)
            else:
                update(ci, k_cur, s_cur_ref)

    @pl.when(qi == 0)
    def _():
        for ci in range(2 * nheads):
            scores(ci, 0, sx_ref)

    def pair(p, carry):
        stage(2 * p + 1, sy_ref, 2 * p, sx_ref)
        stage(2 * p + 2, sx_ref, 2 * p + 1, sy_ref)
        return carry

    lax.fori_loop(0, qi // 2, pair, 0)

    @pl.when(qi % 2 == 0)
    def _():
        stage(None, None, qi, sx_ref, diag=True)

    @pl.when(qi % 2 == 1)
    def _():
        stage(qi, sy_ref, qi - 1, sx_ref)
        stage(None, None, qi, sy_ref, diag=True)

    lv = lmb_ref[...]
    lam = (jnp.exp(jnp.sum(lv[0:1] * lv[1:2], axis=1, keepdims=True))
           - jnp.exp(jnp.sum(lv[2:3] * lv[3:4], axis=1, keepdims=True)) + lam_init)
    for hh in range(nheads):
        o = (acc_ref[2 * hh, :, :DA_V_DIM] / acc_ref[2 * hh, :, DA_V_DIM:]
             - lam * (acc_ref[2 * hh + 1, :, :DA_V_DIM] / acc_ref[2 * hh + 1, :, DA_V_DIM:]))
        o = o * lax.rsqrt(jnp.mean(o * o, axis=-1, keepdims=True) + EPS) * hn_ref[...]
        o_ref[:, hh * 128:(hh + 1) * 128] = (o * (1.0 - lam_init)).astype(BF16)

    mask_maps(qn_ref)
    for ci in range(2 * nheads):
        scores(ci, 0, sx_ref)


def _diff_attention(z3, lmb, head_norm, lam_init):
    nb, seqlen, _ = z3.shape
    tq = min(512, seqlen)
    hps = ATTN_HEADS_PER_STEP
    width = hps * 128
    qcol, kcol, vcol = (c // width for c in (D_MODEL, 2 * D_MODEL, 3 * D_MODEL))
    return pl.pallas_call(
        functools.partial(_dattn_kernel, lam_init=lam_init),
        grid=(nb, DA_HEADS // hps, seqlen // tq),
        in_specs=[
            pl.BlockSpec((None, tq, width), lambda b, h, i: (b, i, qcol + h)),
            pl.BlockSpec((None, tq, width), lambda b, h, i: (b, jnp.minimum(i + 1, seqlen // tq - 1), qcol + h)),
            pl.BlockSpec((None, seqlen, width), lambda b, h, i: (b, 0, kcol + h)),
            pl.BlockSpec((None, seqlen, width), lambda b, h, i: (b, 0, vcol + h)),
            pl.BlockSpec((4, DA_HEAD_DIM), lambda b, h, i: (0, 0)),
            pl.BlockSpec((1, DA_V_DIM), lambda b, h, i: (0, 0)),
        ],
        out_specs=pl.BlockSpec((None, tq, width), lambda b, h, i: (b, i, h)),
        out_shape=jax.ShapeDtypeStruct((nb, seqlen, D_MODEL), BF16),
        scratch_shapes=[pltpu.VMEM((2 * hps, tq, 128), BF16),
                        pltpu.VMEM((hps, seqlen, 2 * DA_V_DIM), BF16),
                        pltpu.VMEM((2 * hps, tq, tq), F32), pltpu.VMEM((2 * hps, tq, tq), F32),
                        pltpu.VMEM((2 * hps, tq, 128), F32),
                        pltpu.VMEM((2 * hps, tq, 2 * DA_V_DIM), F32)],
        compiler_params=_params("parallel", "parallel", "arbitrary"),
        name="diff_attention",
    )(z3, z3, z3, z3, lmb, head_norm)


def _merge_kernel(ys_ref, ya_ref, g_ref, x_ref, wa_ref, wb_ref, wo_ref, n_ref, o_ref):
    a = _wdot(ys_ref[...], wa_ref[...])
    b = _wdot(ya_ref[...], wb_ref[...])
    ga = g_ref[:, :D_MODEL].astype(F32)
    gb = g_ref[:, D_MODEL:].astype(F32)
    merged = (ga * a + gb * b).astype(BF16)
    o_ref[...] = x_ref[...] + _rms(_wdot(merged, wo_ref[...]), n_ref[...])


def _merge(ys2, ya2, z2, x2, w_a, w_b, w_o, gain):
    t = x2.shape[0]
    tm = min(1024, t)
    row = lambda i: (i, 0)
    const = lambda i: (0, 0)
    wspec = pl.BlockSpec((D_MODEL, D_MODEL), const, pipeline_mode=pl.Buffered(1))
    return pl.pallas_call(
        _merge_kernel,
        grid=(t // tm,),
        in_specs=[
            pl.BlockSpec((tm, D_MODEL), row),
            pl.BlockSpec((tm, D_MODEL), row),
            pl.BlockSpec((tm, 2 * D_MODEL), lambda i: (i, 2)),
            pl.BlockSpec((tm, D_MODEL), row),
            wspec, wspec, wspec,
            pl.BlockSpec((1, D_MODEL), const),
        ],
        out_specs=pl.BlockSpec((tm, D_MODEL), row),
        out_shape=jax.ShapeDtypeStruct((t, D_MODEL), F32),
        compiler_params=_params("parallel"),
        name="merge",
    )(ys2, ya2, z2, x2, w_a, w_b, w_o, gain)


def _memkv_kernel(m_ref, g_ref, w_ref, o_ref):
    o_ref[...] = _wdot(_rms(m_ref[...], g_ref[...]).astype(BF16), w_ref[...]).astype(BF16)


def _memory_kv(mem2, gain, w_xkv):
    rows = mem2.shape[0]
    const = lambda i: (0, 0)
    return pl.pallas_call(
        _memkv_kernel,
        grid=(1,),
        in_specs=[
            pl.BlockSpec((rows, D_MODEL), const),
            pl.BlockSpec((1, D_MODEL), const),
            pl.BlockSpec((D_MODEL, 2 * D_MODEL), const),
        ],
        out_specs=pl.BlockSpec((rows, 2 * D_MODEL), const),
        out_shape=jax.ShapeDtypeStruct((rows, 2 * D_MODEL), BF16),
        compiler_params=_params("arbitrary"),
        name="memory_kv",
    )(mem2, gain, w_xkv)


def _xattn_kernel(x_ref, kv_ref, npre_ref, wq_ref, wo_ref, npost_ref, o_ref, oh_ref):
    x = x_ref[...]
    h = _rms(x, npre_ref[...]).astype(BF16)
    q = (_wdot(h, wq_ref[...]) * (XA_HEAD_DIM ** -0.5 * LOG2E)).astype(BF16)
    nt = (((1,), (1,)), ((), ()))
    for hd in range(XA_HEADS):
        lo = hd * XA_HEAD_DIM
        k = kv_ref[:, lo:lo + XA_HEAD_DIM]
        v = kv_ref[:, D_MODEL + lo:D_MODEL + lo + XA_HEAD_DIM]
        s = lax.dot_general(q[:, lo:lo + XA_HEAD_DIM], k, nt, preferred_element_type=F32)
        p = jnp.exp2(s - jnp.max(s, axis=1, keepdims=True))
        o = _bdot(p.astype(BF16), v) / jnp.sum(p, axis=1, keepdims=True)
        oh_ref[:, lo:lo + XA_HEAD_DIM] = o.astype(BF16)
    o_ref[...] = x + _rms(_wdot(oh_ref[...], wo_ref[...]), npost_ref[...])


def _cross_attention(x3, kv3, n_pre, w_xq, w_xo, n_post):
    nb, seqlen, _ = x3.shape
    mlen = kv3.shape[1]
    tm = min(1024, seqlen)
    const = lambda b, i: (0, 0)
    wspec = pl.BlockSpec((D_MODEL, D_MODEL), const, pipeline_mode=pl.Buffered(1))
    vspec = pl.BlockSpec((1, D_MODEL), const)
    return pl.pallas_call(
        _xattn_kernel,
        grid=(nb, seqlen // tm),
        in_specs=[
            pl.BlockSpec((None, tm, D_MODEL), lambda b, i: (b, i, 0)),
            pl.BlockSpec((None, mlen, 2 * D_MODEL), lambda b, i: (b, 0, 0)),
            vspec, wspec, wspec, vspec,
        ],
        out_specs=pl.BlockSpec((None, tm, D_MODEL), lambda b, i: (b, i, 0)),
        out_shape=jax.ShapeDtypeStruct((nb, seqlen, D_MODEL), F32),
        scratch_shapes=[pltpu.VMEM((tm, D_MODEL), BF16)],
        compiler_params=_params("parallel", "parallel"),
        name="cross_attention",
    )(x3, kv3, n_pre, w_xq, w_xo, n_post)


def _mlp_kernel(x_ref, npre_ref, w1_ref, w2_ref, npost_ref, o_ref):
    x = x_ref[...]
    h = _rms(x, npre_ref[...]).astype(BF16)
    f = jnp.zeros(x.shape, F32)
    for c in range(D_FF // D_MODEL):
        lo = c * D_MODEL
        a = jnp.maximum(_wdot(h, w1_ref[:, lo:lo + D_MODEL]), 0.0)
        f = f + _wdot((a * a).astype(BF16), w2_ref[lo:lo + D_MODEL, :])
    o_ref[...] = x + _rms(f, npost_ref[...])


def _mlp(x2, n_pre, w1, w2, n_post):
    t = x2.shape[0]
    tm = min(512, t)
    row = lambda i: (i, 0)
    const = lambda i: (0, 0)
    vspec = pl.BlockSpec((1, D_MODEL), const)
    return pl.pallas_call(
        _mlp_kernel,
        grid=(t // tm,),
        in_specs=[
            pl.BlockSpec((tm, D_MODEL), row),
            vspec,
            pl.BlockSpec((D_MODEL, D_FF), const, pipeline_mode=pl.Buffered(1)),
            pl.BlockSpec((D_FF, D_MODEL), const, pipeline_mode=pl.Buffered(1)),
            vspec,
        ],
        out_specs=pl.BlockSpec((tm, D_MODEL), row),
        out_shape=jax.ShapeDtypeStruct((t, D_MODEL), F32),
        compiler_params=_params("parallel"),
        name="mlp",
    )(x2, n_pre, w1, w2, n_post)


def _rope_tables(seqlen):
    inv = ROPE_THETA ** (-jnp.arange(0, DA_HEAD_DIM, 2, dtype=F32) / DA_HEAD_DIM)
    ang = jnp.arange(seqlen, dtype=F32)[:, None] * inv[None, :]
    cos = jnp.tile(jnp.cos(ang), (1, 4))
    sin = jnp.tile(jnp.concatenate([-jnp.sin(ang), jnp.sin(ang)], axis=1), (1, 2))
    return cos, sin


def kernel(x, mem, norm_mix_pre, w_in, b_gate, ssm_lambda_re, ssm_lambda_im, ssm_log_dt, ssm_b_re, ssm_b_im, ssm_c_re, ssm_c_im, ssm_d, w_glu, b_glu, w_ssm_proj, da_lambda_q1, da_lambda_k1, da_lambda_q2, da_lambda_k2, da_head_norm, w_da_proj, w_mix_out, norm_mix_post, norm_x_pre, norm_mem, w_xq, w_xkv, w_xo, norm_x_post, norm_ff_pre, w_ff1, w_ff2, norm_ff_post):
    nb, seqlen, _ = x.shape
    t = nb * seqlen
    depth = w_in.shape[0]
    tabs = _rope_tables(seqlen)
    x2 = x.reshape(t, D_MODEL)
    mem2 = mem.reshape(nb * mem.shape[1], D_MODEL)
    for l in range(depth):
        lam_init = 0.8 - 0.6 * math.exp(-0.3 * l)
        z2 = _in_projection(x2, norm_mix_pre[l][None], w_in[l], b_gate[l][None], tabs, seqlen)
        z3 = z2.reshape(nb, seqlen, IN_COLS)
        bmat, cmat, lre, lim = _s5_matrices(ssm_lambda_re[l], ssm_lambda_im[l], ssm_log_dt[l],
                                            ssm_b_re[l], ssm_b_im[l], ssm_c_re[l], ssm_c_im[l])
        ys = _s5_branch(z3, bmat, cmat, lre, lim, ssm_d[l][None], w_glu[l], b_glu[l][None])
        lmb = jnp.stack([da_lambda_q1[l], da_lambda_k1[l], da_lambda_q2[l], da_lambda_k2[l]])
        ya = _diff_attention(z3, lmb, da_head_norm[l][None], lam_init)
        x2 = _merge(ys.reshape(t, D_MODEL), ya.reshape(t, D_MODEL), z2, x2,
                    w_ssm_proj[l], w_da_proj[l], w_mix_out[l],
                    norm_mix_post[l][None])
        kv = _memory_kv(mem2, norm_mem[l][None], w_xkv[l])
        x3 = _cross_attention(x2.reshape(nb, seqlen, D_MODEL), kv.reshape(nb, -1, 2 * D_MODEL),
                              norm_x_pre[l][None], w_xq[l], w_xo[l],
                              norm_x_post[l][None])
        x2 = _mlp(x3.reshape(t, D_MODEL), norm_ff_pre[l][None], w_ff1[l],
                  w_ff2[l], norm_ff_post[l][None])
    return x2.reshape(nb, seqlen, D_MODEL)
```

```python
import functools
import math

import jax
import jax.numpy as jnp
from jax import lax
from jax.experimental import pallas as pl
from jax.experimental.pallas import tpu as pltpu

F32 = jnp.float32
BF16 = jnp.bfloat16

D_MODEL = 1024
EPS = 1e-6
CHUNK = 64
SSM_GROUP = 16
SSM_GROUPS = 64
SSM_STATE = 64
SSM_SLABS = 8
SLAB_GROUPS = SSM_GROUPS // SSM_SLABS
SLAB_STATES = SLAB_GROUPS * SSM_STATE
NLB = 2 * SLAB_STATES // 128
SLAB_ROW_PAD = 8
DA_HEADS = 8
DA_HEAD_DIM = 64
DA_V_DIM = 128
ROPE_THETA = 10000.0
ATTN_HEADS_PER_STEP = 2
XA_HEADS = 4
XA_HEAD_DIM = 256
D_FF = 4096
IN_COLS = 6144
LOG2E = 1.4426950408889634
NEG_BIG = -1e30
VMEM_LIMIT_BYTES = 56 * 1024 * 1024


def _params(*semantics):
    return pltpu.CompilerParams(dimension_semantics=semantics, vmem_limit_bytes=VMEM_LIMIT_BYTES)


def _rms(x, g):
    return x * lax.rsqrt(jnp.mean(x * x, axis=-1, keepdims=True) + EPS) * g


def _bdot(a, b):
    return jnp.dot(a, b, preferred_element_type=F32)


def _wdot(a, w):
    return _bdot(a, w.astype(BF16))


def _sigmoid(x):
    return 0.5 * jnp.tanh(0.5 * x) + 0.5


def _inproj_kernel(x_ref, g_ref, w_ref, bg_ref, cos_ref, sin_ref, o_ref):
    tm = x_ref.shape[0]
    hn = _rms(x_ref[...], g_ref[...]).astype(BF16)
    lane = lax.broadcasted_iota(jnp.int32, (tm, 128), 1)
    first_half = (lane % 64) < 32

    def rope(acc, col0, scale):
        c = cos_ref[...] * scale
        s = sin_ref[...] * scale
        for h in range(DA_HEADS):
            blk = acc[:, h * 128:(h + 1) * 128]
            partner = jnp.where(first_half, pltpu.roll(blk, 96, 1), pltpu.roll(blk, 32, 1))
            o_ref[:, col0 + h * 128:col0 + (h + 1) * 128] = (blk * c + partner * s).astype(BF16)

    for j in range(IN_COLS // D_MODEL):
        col0 = j * D_MODEL
        acc = _wdot(hn, w_ref[:, col0:col0 + D_MODEL])
        if j == 1:
            rope(acc, col0, DA_HEAD_DIM ** -0.5 * LOG2E)
        elif j == 2:
            rope(acc, col0, 1.0)
        elif j >= 4:
            gate = _sigmoid(acc + bg_ref[:, col0 - 4 * D_MODEL:col0 - 3 * D_MODEL])
            o_ref[:, col0:col0 + D_MODEL] = gate.astype(BF16)
        else:
            o_ref[:, col0:col0 + D_MODEL] = acc.astype(BF16)


def _in_projection(x2, gain, w_in, b_gate, tabs, seqlen):
    t = x2.shape[0]
    tm = min(512, seqlen)
    nl = seqlen // tm
    const = lambda i: (0, 0)
    tab_spec = pl.BlockSpec((tm, 128), lambda i: (i % nl, 0))
    return pl.pallas_call(
        _inproj_kernel,
        grid=(t // tm,),
        in_specs=[
            pl.BlockSpec((tm, D_MODEL), lambda i: (i, 0)),
            pl.BlockSpec((1, D_MODEL), const),
            pl.BlockSpec((D_MODEL, IN_COLS), const, pipeline_mode=pl.Buffered(1)),
            pl.BlockSpec((1, 2 * D_MODEL), const),
            tab_spec, tab_spec,
        ],
        out_specs=pl.BlockSpec((tm, IN_COLS), lambda i: (i, 0)),
        out_shape=jax.ShapeDtypeStruct((t, IN_COLS), BF16),
        compiler_params=_params("parallel"),
        name="in_projection",
    )(x2, gain, w_in, b_gate, *tabs)


def _s5_kernel(u_ref, bm_ref, cm_ref, lre_ref, lim_ref, d_ref, wg_ref, bgl_ref, o_ref,
               bu_ref, st_ref, wc_ref, il_ref, ue_ref, uo_ref, yo_ref, w_ref, k0_ref):
    nb, tc, _ = u_ref.shape
    tp = tc // 2
    pitch = bu_ref.shape[2] // SSM_SLABS
    c = pl.program_id(0)

    @pl.when(c == 0)
    def _():
        st_ref[...] = jnp.zeros(st_ref.shape, F32)
        wc_ref[...] = jnp.zeros(wc_ref.shape, F32)

    for b in range(nb):
        for j in range(SSM_SLABS):
            il_ref[j] = u_ref[b, :, j * 128:(j + 1) * 128].astype(F32)
        for j in range(SSM_SLABS):
            u_e = il_ref[j, pl.ds(0, tp, stride=2), :]
            u_o = il_ref[j, pl.ds(1, tp, stride=2), :]
            ue_ref[b * tp:(b + 1) * tp, j * 128:(j + 1) * 128] = u_e
            uo_ref[b * tp:(b + 1) * tp, j * 128:(j + 1) * 128] = u_o
            lhs = jnp.concatenate([u_e, u_o], axis=1).astype(BF16)
            bu = _bdot(lhs, bm_ref[j])
            for k in range(NLB):
                bu_ref[b, k, j * pitch:j * pitch + tp, :] = bu[:, k * 128:(k + 1) * 128]
            k0_ref[b * tp:(b + 1) * tp, j * 128:(j + 1) * 128] = bu[:, NLB * 128:]

    ar = [lre_ref[:, k * 128:(k + 1) * 128] for k in range(NLB // 2)]
    ai = [lim_ref[:, k * 128:(k + 1) * 128] for k in range(NLB // 2)]

    def step(t, carry):
        out = []
        for b in range(nb):
            xs = carry[b]
            new = [None] * NLB
            for k in range(NLB // 2):
                xr, xi = xs[k], xs[k + NLB // 2]
                rows = pl.ds(t, SSM_SLABS, stride=pitch)
                new[k] = ar[k] * xr - ai[k] * xi + bu_ref[b, k, rows, :]
                new[k + NLB // 2] = ar[k] * xi + ai[k] * xr + bu_ref[b, k + NLB // 2, rows, :]
                bu_ref[b, k, rows, :] = new[k]
                bu_ref[b, k + NLB // 2, rows, :] = new[k + NLB // 2]
            out.append(tuple(new))
        return tuple(out)

    init = tuple(tuple(st_ref[b, k] for k in range(NLB)) for b in range(nb))
    fin = lax.fori_loop(0, tp, step, init, unroll=8)
    for b in range(nb):
        for k in range(NLB):
            st_ref[b, k] = fin[b][k]

    for b in range(nb):
        for j in range(SSM_SLABS):
            xs = jnp.concatenate([bu_ref[b, k, j * pitch:j * pitch + tp, :] for k in range(NLB)], axis=1)
            r = _bdot(xs.astype(BF16), cm_ref[j])
            yo_ref[b * tp:(b + 1) * tp, j * 128:(j + 1) * 128] = r[:, :128]
            w_ref[b * tp:(b + 1) * tp, j * 128:(j + 1) * 128] = r[:, 128:]

    def glu(y):
        ys = 0.5 * y * (1.0 + jnp.tanh(math.sqrt(2.0 / math.pi) * (y + 0.044715 * (y * y * y))))
        gate = _sigmoid(_wdot(ys.astype(BF16), wg_ref[...]) + bgl_ref[...])
        return ys * gate

    first_row = lax.broadcasted_iota(jnp.int32, (tp, D_MODEL), 0) == 0
    shifted = []
    for b in range(nb):
        w = w_ref[b * tp:(b + 1) * tp, :]
        shifted.append(jnp.where(first_row, wc_ref[b], pltpu.roll(w, 1, 0)))
        wc_ref[b] = w[tp - 1:tp, :]
    yo_ref[...] = glu(yo_ref[...] + d_ref[...] * uo_ref[...])
    w_ref[...] = glu(jnp.concatenate(shifted, axis=0) + k0_ref[...] + d_ref[...] * ue_ref[...])
    for b in range(nb):
        for j in range(SSM_SLABS):
            il_ref[j, pl.ds(0, tp, stride=2), :] = w_ref[b * tp:(b + 1) * tp, j * 128:(j + 1) * 128]
            il_ref[j, pl.ds(1, tp, stride=2), :] = yo_ref[b * tp:(b + 1) * tp, j * 128:(j + 1) * 128]
        for j in range(SSM_SLABS):
            o_ref[b, :, j * 128:(j + 1) * 128] = il_ref[j].astype(BF16)


def _s5_branch(z3, bmat, cmat, lam_re, lam_im, d_skip, w_glu, b_glu):
    nb, seqlen, _ = z3.shape
    tc = min(512, seqlen)
    tp = tc // 2
    const2 = lambda c: (0, 0)
    const3 = lambda c: (0, 0, 0)
    single = pl.Buffered(1)
    return pl.pallas_call(
        _s5_kernel,
        grid=(seqlen // tc,),
        in_specs=[
            pl.BlockSpec((nb, tc, D_MODEL), lambda c: (0, c, 0)),
            pl.BlockSpec(bmat.shape, const3, pipeline_mode=single),
            pl.BlockSpec(cmat.shape, const3, pipeline_mode=single),
            pl.BlockSpec(lam_re.shape, const2),
            pl.BlockSpec(lam_im.shape, const2),
            pl.BlockSpec((1, D_MODEL), const2),
            pl.BlockSpec((D_MODEL, D_MODEL), const2, pipeline_mode=single),
            pl.BlockSpec((1, D_MODEL), const2),
        ],
        out_specs=pl.BlockSpec((nb, tc, D_MODEL), lambda c: (0, c, 0)),
        out_shape=jax.ShapeDtypeStruct((nb, seqlen, D_MODEL), BF16),
        scratch_shapes=[
            pltpu.VMEM((nb, NLB, SSM_SLABS * (tp + SLAB_ROW_PAD), 128), F32),
            pltpu.VMEM((nb, NLB, SSM_SLABS, 128), F32),
            pltpu.VMEM((nb, 1, D_MODEL), F32),
            pltpu.VMEM((SSM_SLABS, tc, 128), F32),
        ] + [pltpu.VMEM((nb * tp, D_MODEL), F32)] * 5,
        compiler_params=_params("arbitrary"),
        name="s5_branch",
    )(z3, bmat, cmat, lam_re, lam_im, d_skip, w_glu, b_glu)


def _s5_matrices(lam_re, lam_im, log_dt, b_re, b_im, c_re, c_im):
    dt = jnp.exp(log_dt)[:, None]
    mag = jnp.exp(lam_re * dt)
    lb_re, lb_im = mag * jnp.cos(lam_im * dt), mag * jnp.sin(lam_im * dt)
    den = lam_re * lam_re + lam_im * lam_im
    cf_re = ((lb_re - 1.0) * lam_re + lb_im * lam_im) / den
    cf_im = (lb_im * lam_re - (lb_re - 1.0) * lam_im) / den
    eye = jnp.eye(SLAB_GROUPS, dtype=F32)[None, :, None, :, None]
    by_slab = (SSM_SLABS, SLAB_GROUPS)

    def col_factor(x):
        return x.reshape(by_slab + (SSM_STATE,))[:, None, None, :, :]

    def row_factor(x):
        return x.reshape(by_slab + (SSM_STATE,))[:, :, :, None, None]

    def in_place(x):
        return x.reshape(by_slab + (SSM_STATE, SSM_GROUP)).transpose(0, 1, 3, 2)[:, :, :, None, :] * eye

    def out_place(x):
        return x.reshape(by_slab + (SSM_GROUP, SSM_STATE)).transpose(0, 3, 1, 2)[:, None] * eye

    def in_flat(x):
        return x.reshape(SSM_SLABS, 128, SLAB_STATES)

    def out_flat(x):
        return x.reshape(SSM_SLABS, SLAB_STATES, 128)

    br, bi = in_place(b_re), in_place(b_im)
    bo_re = col_factor(cf_re) * br - col_factor(cf_im) * bi
    bo_im = col_factor(cf_re) * bi + col_factor(cf_im) * br
    be_re = col_factor(lb_re) * bo_re - col_factor(lb_im) * bo_im
    be_im = col_factor(lb_re) * bo_im + col_factor(lb_im) * bo_re
    cr, ci = out_place(c_re), out_place(c_im)
    cn_re = row_factor(lb_re) * cr - row_factor(lb_im) * ci
    cn_im = row_factor(lb_re) * ci + row_factor(lb_im) * cr
    b_odd = jnp.concatenate([in_flat(bo_re), in_flat(bo_im)], axis=2)
    cur = jnp.concatenate([out_flat(cr), out_flat(-ci)], axis=1)
    k0 = jnp.einsum('jrs,jsc->jrc', b_odd, cur, precision=lax.Precision.HIGHEST)
    top = jnp.concatenate([in_flat(be_re), in_flat(be_im), k0], axis=2)
    bot = jnp.concatenate([b_odd, jnp.zeros_like(k0)], axis=2)
    bmat = jnp.concatenate([top, bot], axis=1).astype(BF16)
    nxt = jnp.concatenate([out_flat(cn_re), out_flat(-cn_im)], axis=1)
    cmat = jnp.concatenate([cur, nxt], axis=2).astype(BF16)
    lre = (lb_re * lb_re - lb_im * lb_im).reshape(SSM_SLABS, SLAB_STATES)
    lim = (2.0 * lb_re * lb_im).reshape(SSM_SLABS, SLAB_STATES)
    return bmat, cmat, lre, lim


def _dattn_kernel(q_ref, qn_ref, k_ref, v_ref, lmb_ref, hn_ref, o_ref,
                  qm_ref, vx_ref, sx_ref, sy_ref, m_ref, acc_ref, *, lam_init):
    tq = q_ref.shape[0]
    tk = tq
    nstrips = tk // 128
    nheads = q_ref.shape[1] // 128
    nkb = k_ref.shape[0] // tk
    qi = pl.program_id(2)
    lane = lax.broadcasted_iota(jnp.int32, (tq, 128), 1)
    nt = (((1,), (1,)), ((), ()))

    @pl.when(qi == 0)
    def _():
        for hh in range(nheads):
            vx_ref[hh, :, :DA_V_DIM] = v_ref[:, hh * 128:(hh + 1) * 128]
            vx_ref[hh, :, DA_V_DIM:] = jnp.ones((v_ref.shape[0], DA_V_DIM), BF16)

    def mask_maps(src_ref):
        for hh in range(nheads):
            q = src_ref[:, hh * 128:(hh + 1) * 128]
            zero = jnp.zeros_like(q)
            qm_ref[2 * hh] = jnp.where(lane < DA_HEAD_DIM, q, zero)
            qm_ref[2 * hh + 1] = jnp.where(lane >= DA_HEAD_DIM, q, zero)

    def scores(ci, ki, s_ref):
        start = pl.multiple_of(ki * tk, tk)
        hh = ci // 2
        kb = k_ref[pl.ds(start, tk), hh * 128:(hh + 1) * 128]
        s_ref[ci] = lax.dot_general(qm_ref[ci], kb, nt, preferred_element_type=F32)

    def update(ci, ki, s_ref, first=False):
        start = pl.multiple_of(ki * tk, tk)
        hh = ci // 2
        vb = vx_ref[hh, pl.ds(start, tk), :]
        s = s_ref[ci]
        m_next = jnp.max(s, axis=1, keepdims=True)
        if first:
            m_next = jnp.broadcast_to(m_next, (tq, 128))
        else:
            m_prev = m_ref[ci]
            m_next = jnp.maximum(m_prev, m_next)
            alpha = jnp.exp2(m_prev - m_next)
        ps = [jnp.exp2(s[:, c * 128:(c + 1) * 128] - m_next) for c in range(nstrips)]
        pv = _bdot(jnp.concatenate(ps, axis=1).astype(BF16), vb)
        acc_ref[ci] = pv if first else jnp.concatenate([alpha, alpha], axis=1) * acc_ref[ci] + pv
        m_ref[ci] = m_next

    def update_diag(ci, ki, s_ref):
        half = tq // 2
        start = pl.multiple_of(ki * tk, tk)
        hh = ci // 2
        row = lax.broadcasted_iota(jnp.int32, (half, half), 0) // CHUNK
        col = lax.broadcasted_iota(jnp.int32, (half, half), 1) // CHUNK
        visible = col <= row
        for part in range(2):
            rows = slice(part * half, (part + 1) * half)
            ncols = half * (part + 1)
            strips = []
            for c in range(ncols // 128):
                s = s_ref[ci, rows, c * 128:(c + 1) * 128]
                off = c * 128 - part * half
                if off >= 0:
                    s = jnp.where(visible[:, off:off + 128], s, NEG_BIG)
                strips.append(s)
            m_prev = m_ref[ci, rows, :]
            m_cur = functools.reduce(jnp.maximum, strips)
            m_next = jnp.maximum(m_prev, jnp.max(m_cur, axis=1, keepdims=True))
            alpha = jnp.exp2(m_prev - m_next)
            p = jnp.concatenate([jnp.exp2(s - m_next) for s in strips], axis=1).astype(BF16)
            pv = _bdot(p, vx_ref[hh, pl.ds(start, ncols), :])
            acc_ref[ci, rows, :] = jnp.concatenate([alpha, alpha], axis=1) * acc_ref[ci, rows, :] + pv
            m_ref[ci, rows, :] = m_next

    def stage(k_next, s_next_ref, k_cur, s_cur_ref, diag=False):
        for ci in range(2 * nheads):
            if k_next is not None:
                scores(ci, k_next, s_next_ref)
            if diag:
                update_diag(ci, k_cur, s_cur_ref)
            else:
                update(ci, k_cur, s_cur_ref)

    @pl.when(qi == 0)
    def _():
        mask_maps(q_ref)
        m_ref[...] = jnp.full(m_ref.shape, NEG_BIG, F32)
        acc_ref[...] = jnp.zeros(acc_ref.shape, F32)
        for ci in range(2 * nheads):
            scores(ci, 0, sx_ref)

    def pair(p, carry):
        stage(2 * p + 2, sy_ref, 2 * p + 1, sx_ref)
        stage(2 * p + 3, sx_ref, 2 * p + 2, sy_ref)
        return carry

    lax.fori_loop(0, (qi - 1) // 2, pair, 0)
    diag_in_sx = (qi == 0) | (qi % 2 == 1)

    @pl.when(diag_in_sx)
    def _():
        stage(None, None, qi, sx_ref, diag=True)

    @pl.when(jnp.logical_not(diag_in_sx))
    def _():
        stage(qi, sy_ref, qi - 1, sx_ref)
        stage(None, None, qi, sy_ref, diag=True)

    lv = lmb_ref[...]
    lam = (jnp.exp(jnp.sum(lv[0:1] * lv[1:2], axis=1, keepdims=True))
           - jnp.exp(jnp.sum(lv[2:3] * lv[3:4], axis=1, keepdims=True)) + lam_init)
    for hh in range(nheads):
        o = (acc_ref[2 * hh, :, :DA_V_DIM] / acc_ref[2 * hh, :, DA_V_DIM:]
             - lam * (acc_ref[2 * hh + 1, :, :DA_V_DIM] / acc_ref[2 * hh + 1, :, DA_V_DIM:]))
        o = o * lax.rsqrt(jnp.mean(o * o, axis=-1, keepdims=True) + EPS) * hn_ref[...]
        o_ref[:, hh * 128:(hh + 1) * 128] = (o * (1.0 - lam_init)).astype(BF16)

    mask_maps(qn_ref)
    for ci in range(2 * nheads):
        scores(ci, 0, sy_ref)
    for ci in range(2 * nheads):
        scores(ci, min(1, nkb - 1), sx_ref)
        update(ci, 0, sy_ref, first=True)


def _diff_attention(z3, lmb, head_norm, lam_init):
    nb, seqlen, _ = z3.shape
    tq = min(512, seqlen)
    hps = ATTN_HEADS_PER_STEP
    width = hps * 128
    qcol, kcol, vcol = (c // width for c in (D_MODEL, 2 * D_MODEL, 3 * D_MODEL))
    return pl.pallas_call(
        functools.partial(_dattn_kernel, lam_init=lam_init),
        grid=(nb, DA_HEADS // hps, seqlen // tq),
        in_specs=[
            pl.BlockSpec((None, tq, width), lambda b, h, i: (b, i, qcol + h)),
            pl.BlockSpec((None, tq, width), lambda b, h, i: (b, jnp.minimum(i + 1, seqlen // tq - 1), qcol + h)),
            pl.BlockSpec((None, seqlen, width), lambda b, h, i: (b, 0, kcol + h)),
            pl.BlockSpec((None, seqlen, width), lambda b, h, i: (b, 0, vcol + h)),
            pl.BlockSpec((4, DA_HEAD_DIM), lambda b, h, i: (0, 0)),
            pl.BlockSpec((1, DA_V_DIM), lambda b, h, i: (0, 0)),
        ],
        out_specs=pl.BlockSpec((None, tq, width), lambda b, h, i: (b, i, h)),
        out_shape=jax.ShapeDtypeStruct((nb, seqlen, D_MODEL), BF16),
        scratch_shapes=[pltpu.VMEM((2 * hps, tq, 128), BF16),
                        pltpu.VMEM((hps, seqlen, 2 * DA_V_DIM), BF16),
                        pltpu.VMEM((2 * hps, tq, tq), F32), pltpu.VMEM((2 * hps, tq, tq), F32),
                        pltpu.VMEM((2 * hps, tq, 128), F32),
                        pltpu.VMEM((2 * hps, tq, 2 * DA_V_DIM), F32)],
        compiler_params=_params("parallel", "parallel", "arbitrary"),
        name="diff_attention",
    )(z3, z3, z3, z3, lmb, head_norm)


def _merge_kernel(ys_ref, ya_ref, g_ref, x_ref, wa_ref, wb_ref, wo_ref, n_ref, o_ref):
    a = _wdot(ys_ref[...], wa_ref[...])
    b = _wdot(ya_ref[...], wb_ref[...])
    ga = g_ref[:, :D_MODEL].astype(F32)
    gb = g_ref[:, D_MODEL:].astype(F32)
    merged = (ga * a + gb * b).astype(BF16)
    o_ref[...] = x_ref[...] + _rms(_wdot(merged, wo_ref[...]), n_ref[...])


def _merge(ys2, ya2, z2, x2, w_a, w_b, w_o, gain):
    t = x2.shape[0]
    tm = min(1024, t)
    row = lambda i: (i, 0)
    const = lambda i: (0, 0)
    wspec = pl.BlockSpec((D_MODEL, D_MODEL), const, pipeline_mode=pl.Buffered(1))
    return pl.pallas_call(
        _merge_kernel,
        grid=(t // tm,),
        in_specs=[
            pl.BlockSpec((tm, D_MODEL), row),
            pl.BlockSpec((tm, D_MODEL), row),
            pl.BlockSpec((tm, 2 * D_MODEL), lambda i: (i, 2)),
            pl.BlockSpec((tm, D_MODEL), row),
            wspec, wspec, wspec,
            pl.BlockSpec((1, D_MODEL), const),
        ],
        out_specs=pl.BlockSpec((tm, D_MODEL), row),
        out_shape=jax.ShapeDtypeStruct((t, D_MODEL), F32),
        compiler_params=_params("parallel"),
        name="merge",
    )(ys2, ya2, z2, x2, w_a, w_b, w_o, gain)


def _memkv_kernel(m_ref, g_ref, w_ref, o_ref):
    o_ref[...] = _wdot(_rms(m_ref[...], g_ref[...]).astype(BF16), w_ref[...]).astype(BF16)


def _memory_kv(mem2, gain, w_xkv):
    rows = mem2.shape[0]
    const = lambda i: (0, 0)
    return pl.pallas_call(
        _memkv_kernel,
        grid=(1,),
        in_specs=[
            pl.BlockSpec((rows, D_MODEL), const),
            pl.BlockSpec((1, D_MODEL), const),
            pl.BlockSpec((D_MODEL, 2 * D_MODEL), const),
        ],
        out_specs=pl.BlockSpec((rows, 2 * D_MODEL), const),
        out_shape=jax.ShapeDtypeStruct((rows, 2 * D_MODEL), BF16),
        compiler_params=_params("arbitrary"),
        name="memory_kv",
    )(mem2, gain, w_xkv)


def _xattn_kernel(x_ref, kv_ref, npre_ref, wq_ref, wo_ref, npost_ref, o_ref, oh_ref):
    x = x_ref[...]
    h = _rms(x, npre_ref[...]).astype(BF16)
    q = (_wdot(h, wq_ref[...]) * (XA_HEAD_DIM ** -0.5 * LOG2E)).astype(BF16)
    nt = (((1,), (1,)), ((), ()))
    for hd in range(XA_HEADS):
        lo = hd * XA_HEAD_DIM
        k = kv_ref[:, lo:lo + XA_HEAD_DIM]
        v = kv_ref[:, D_MODEL + lo:D_MODEL + lo + XA_HEAD_DIM]
        s = lax.dot_general(q[:, lo:lo + XA_HEAD_DIM], k, nt, preferred_element_type=F32)
        p = jnp.exp2(s - jnp.max(s, axis=1, keepdims=True))
        o = _bdot(p.astype(BF16), v) / jnp.sum(p, axis=1, keepdims=True)
        oh_ref[:, lo:lo + XA_HEAD_DIM] = o.astype(BF16)
    o_ref[...] = x + _rms(_wdot(oh_ref[...], wo_ref[...]), npost_ref[...])


def _cross_attention(x3, kv3, n_pre, w_xq, w_xo, n_post):
    nb, seqlen, _ = x3.shape
    mlen = kv3.shape[1]
    tm = min(1024, seqlen)
    const = lambda b, i: (0, 0)
    wspec = pl.BlockSpec((D_MODEL, D_MODEL), const, pipeline_mode=pl.Buffered(1))
    vspec = pl.BlockSpec((1, D_MODEL), const)
    return pl.pallas_call(
        _xattn_kernel,
        grid=(nb, seqlen // tm),
        in_specs=[
            pl.BlockSpec((None, tm, D_MODEL), lambda b, i: (b, i, 0)),
            pl.BlockSpec((None, mlen, 2 * D_MODEL), lambda b, i: (b, 0, 0)),
            vspec, wspec, wspec, vspec,
        ],
        out_specs=pl.BlockSpec((None, tm, D_MODEL), lambda b, i: (b, i, 0)),
        out_shape=jax.ShapeDtypeStruct((nb, seqlen, D_MODEL), F32),
        scratch_shapes=[pltpu.VMEM((tm, D_MODEL), BF16)],
        compiler_params=_params("parallel", "parallel"),
        name="cross_attention",
    )(x3, kv3, n_pre, w_xq, w_xo, n_post)


def _mlp_kernel(x_ref, npre_ref, w1_ref, w2_ref, npost_ref, o_ref):
    x = x_ref[...]
    h = _rms(x, npre_ref[...]).astype(BF16)
    f = jnp.zeros(x.shape, F32)
    for c in range(D_FF // D_MODEL):
        lo = c * D_MODEL
        a = jnp.maximum(_wdot(h, w1_ref[:, lo:lo + D_MODEL]), 0.0)
        f = f + _wdot((a * a).astype(BF16), w2_ref[lo:lo + D_MODEL, :])
    o_ref[...] = x + _rms(f, npost_ref[...])


def _mlp(x2, n_pre, w1, w2, n_post):
    t = x2.shape[0]
    tm = min(512, t)
    row = lambda i: (i, 0)
    const = lambda i: (0, 0)
    vspec = pl.BlockSpec((1, D_MODEL), const)
    return pl.pallas_call(
        _mlp_kernel,
        grid=(t // tm,),
        in_specs=[
            pl.BlockSpec((tm, D_MODEL), row),
            vspec,
            pl.BlockSpec((D_MODEL, D_FF), const, pipeline_mode=pl.Buffered(1)),
            pl.BlockSpec((D_FF, D_MODEL), const, pipeline_mode=pl.Buffered(1)),
            vspec,
        ],
        out_specs=pl.BlockSpec((tm, D_MODEL), row),
        out_shape=jax.ShapeDtypeStruct((t, D_MODEL), F32),
        compiler_params=_params("parallel"),
        name="mlp",
    )(x2, n_pre, w1, w2, n_post)


def _rope_tables(seqlen):
    inv = ROPE_THETA ** (-jnp.arange(0, DA_HEAD_DIM, 2, dtype=F32) / DA_HEAD_DIM)
    ang = jnp.arange(seqlen, dtype=F32)[:, None] * inv[None, :]
    cos = jnp.tile(jnp.cos(ang), (1, 4))
    sin = jnp.tile(jnp.concatenate([-jnp.sin(ang), jnp.sin(ang)], axis=1), (1, 2))
    return cos, sin


def kernel(x, mem, norm_mix_pre, w_in, b_gate, ssm_lambda_re, ssm_lambda_im, ssm_log_dt, ssm_b_re, ssm_b_im, ssm_c_re, ssm_c_im, ssm_d, w_glu, b_glu, w_ssm_proj, da_lambda_q1, da_lambda_k1, da_lambda_q2, da_lambda_k2, da_head_norm, w_da_proj, w_mix_out, norm_mix_post, norm_x_pre, norm_mem, w_xq, w_xkv, w_xo, norm_x_post, norm_ff_pre, w_ff1, w_ff2, norm_ff_post):
    nb, seqlen, _ = x.shape
    t = nb * seqlen
    depth = w_in.shape[0]
    tabs = _rope_tables(seqlen)
    x2 = x.reshape(t, D_MODEL)
    mem2 = mem.reshape(nb * mem.shape[1], D_MODEL)
    for l in range(depth):
        lam_init = 0.8 - 0.6 * math.exp(-0.3 * l)
        z2 = _in_projection(x2, norm_mix_pre[l][None], w_in[l], b_gate[l][None], tabs, seqlen)
        z3 = z2.reshape(nb, seqlen, IN_COLS)
        bmat, cmat, lre, lim = _s5_matrices(ssm_lambda_re[l], ssm_lambda_im[l], ssm_log_dt[l],
                                            ssm_b_re[l], ssm_b_im[l], ssm_c_re[l], ssm_c_im[l])
        ys = _s5_branch(z3, bmat, cmat, lre, lim, ssm_d[l][None], w_glu[l], b_glu[l][None])
        lmb = jnp.stack([da_lambda_q1[l], da_lambda_k1[l], da_lambda_q2[l], da_lambda_k2[l]])
        ya = _diff_attention(z3, lmb, da_head_norm[l][None], lam_init)
        x2 = _merge(ys.reshape(t, D_MODEL), ya.reshape(t, D_MODEL), z2, x2,
                    w_ssm_proj[l], w_da_proj[l], w_mix_out[l],
                    norm_mix_post[l][None])
        kv = _memory_kv(mem2, norm_mem[l][None], w_xkv[l])
        x3 = _cross_attention(x2.reshape(nb, seqlen, D_MODEL), kv.reshape(nb, -1, 2 * D_MODEL),
                              norm_x_pre[l][None], w_xq[l], w_xo[l],
                              norm_x_post[l][None])
        x2 = _mlp(x3.reshape(t, D_MODEL), norm_ff_pre[l][None], w_ff1[l],
                  w_ff2[l], norm_ff_post[l][None])
    return x2.reshape(nb, seqlen, D_MODEL)
```

```python
import functools
import math

import jax
import jax.numpy as jnp
from jax import lax
from jax.experimental import pallas as pl
from jax.experimental.pallas import tpu as pltpu

F32 = jnp.float32
BF16 = jnp.bfloat16

LANES = 128
TILE_IN_PROJ = 512
TILE_S5 = 512
TILE_ATTN = 512
TILE_MERGE = 1024
TILE_XATTN = 1024
TILE_MLP = 512

D_MODEL = 1024
EPS = 1e-6
CHUNK = 64
SSM_GROUP = 16
SSM_GROUPS = 64
SSM_STATE = 64
SSM_SLABS = 8
SLAB_GROUPS = SSM_GROUPS // SSM_SLABS
SLAB_STATES = SLAB_GROUPS * SSM_STATE
NLB = 2 * SLAB_STATES // LANES
SLAB_ROW_PAD = 8
DA_HEADS = 8
DA_HEAD_DIM = 64
DA_V_DIM = 128
ROPE_THETA = 10000.0
ATTN_HEADS_PER_STEP = 2
XA_HEADS = 4
XA_HEAD_DIM = 256
D_FF = 4096
IN_COLS = 6144
LOG2E = 1.4426950408889634
NEG_BIG = -1e30
VMEM_LIMIT_BYTES = 56 * 1024 * 1024


def _params(*semantics):
    return pltpu.CompilerParams(dimension_semantics=semantics, vmem_limit_bytes=VMEM_LIMIT_BYTES)


def _rms(x, g):
    return x * lax.rsqrt(jnp.mean(x * x, axis=-1, keepdims=True) + EPS) * g


def _bdot(a, b):
    return jnp.dot(a, b, preferred_element_type=F32)


def _wdot(a, w):
    return _bdot(a, w.astype(BF16))


def _sigmoid(x):
    return 0.5 * jnp.tanh(0.5 * x) + 0.5


def _inproj_kernel(x_ref, g_ref, w_ref, bg_ref, cos_ref, sin_ref, o_ref):
    tm = x_ref.shape[0]
    hn = _rms(x_ref[...], g_ref[...]).astype(BF16)
    lane = lax.broadcasted_iota(jnp.int32, (tm, LANES), 1)
    first_half = (lane % 64) < 32

    def rope(acc, col0, scale):
        c = cos_ref[...] * scale
        s = sin_ref[...] * scale
        for h in range(DA_HEADS):
            blk = acc[:, h * LANES:(h + 1) * LANES]
            partner = jnp.where(first_half, pltpu.roll(blk, 96, 1), pltpu.roll(blk, 32, 1))
            o_ref[:, col0 + h * LANES:col0 + (h + 1) * LANES] = (blk * c + partner * s).astype(BF16)

    for j in range(IN_COLS // D_MODEL):
        col0 = j * D_MODEL
        acc = _wdot(hn, w_ref[:, col0:col0 + D_MODEL])
        if j == 1:
            rope(acc, col0, DA_HEAD_DIM ** -0.5 * LOG2E)
        elif j == 2:
            rope(acc, col0, 1.0)
        elif j >= 4:
            gate = _sigmoid(acc + bg_ref[:, col0 - 4 * D_MODEL:col0 - 3 * D_MODEL])
            o_ref[:, col0:col0 + D_MODEL] = gate.astype(BF16)
        else:
            o_ref[:, col0:col0 + D_MODEL] = acc.astype(BF16)


def _in_projection(x2, gain, w_in, b_gate, tabs, seqlen):
    t = x2.shape[0]
    tm = min(TILE_IN_PROJ, seqlen)
    nl = seqlen // tm
    const = lambda i: (0, 0)
    tab_spec = pl.BlockSpec((tm, LANES), lambda i: (i % nl, 0))
    return pl.pallas_call(
        _inproj_kernel,
        grid=(t // tm,),
        in_specs=[
            pl.BlockSpec((tm, D_MODEL), lambda i: (i, 0)),
            pl.BlockSpec((1, D_MODEL), const),
            pl.BlockSpec((D_MODEL, IN_COLS), const, pipeline_mode=pl.Buffered(1)),
            pl.BlockSpec((1, 2 * D_MODEL), const),
            tab_spec, tab_spec,
        ],
        out_specs=pl.BlockSpec((tm, IN_COLS), lambda i: (i, 0)),
        out_shape=jax.ShapeDtypeStruct((t, IN_COLS), BF16),
        compiler_params=_params("parallel"),
        name="in_projection",
    )(x2, gain, w_in, b_gate, *tabs)


def _s5_kernel(u_ref, bm_ref, cm_ref, lre_ref, lim_ref, d_ref, wg_ref, bgl_ref, o_ref,
               bu_ref, st_ref, wc_ref, il_ref, ue_ref, uo_ref, yo_ref, w_ref, k0_ref):
    nb, tc, _ = u_ref.shape
    tp = tc // 2
    pitch = bu_ref.shape[2] // SSM_SLABS
    c = pl.program_id(0)

    @pl.when(c == 0)
    def _():
        st_ref[...] = jnp.zeros(st_ref.shape, F32)
        wc_ref[...] = jnp.zeros(wc_ref.shape, F32)

    for b in range(nb):
        for j in range(SSM_SLABS):
            il_ref[j] = u_ref[b, :, j * LANES:(j + 1) * LANES].astype(F32)
        for j in range(SSM_SLABS):
            u_e = il_ref[j, pl.ds(0, tp, stride=2), :]
            u_o = il_ref[j, pl.ds(1, tp, stride=2), :]
            ue_ref[b * tp:(b + 1) * tp, j * LANES:(j + 1) * LANES] = u_e
            uo_ref[b * tp:(b + 1) * tp, j * LANES:(j + 1) * LANES] = u_o
            lhs = jnp.concatenate([u_e, u_o], axis=1).astype(BF16)
            bu = _bdot(lhs, bm_ref[j])
            for k in range(NLB):
                bu_ref[b, k, j * pitch:j * pitch + tp, :] = bu[:, k * LANES:(k + 1) * LANES]
            k0_ref[b * tp:(b + 1) * tp, j * LANES:(j + 1) * LANES] = bu[:, NLB * LANES:]

    ar = [lre_ref[:, k * LANES:(k + 1) * LANES] for k in range(NLB // 2)]
    ai = [lim_ref[:, k * LANES:(k + 1) * LANES] for k in range(NLB // 2)]

    def step(t, carry):
        out = []
        for b in range(nb):
            xs = carry[b]
            new = [None] * NLB
            for k in range(NLB // 2):
                xr, xi = xs[k], xs[k + NLB // 2]
                rows = pl.ds(t, SSM_SLABS, stride=pitch)
                new[k] = ar[k] * xr - ai[k] * xi + bu_ref[b, k, rows, :]
                new[k + NLB // 2] = ar[k] * xi + ai[k] * xr + bu_ref[b, k + NLB // 2, rows, :]
                bu_ref[b, k, rows, :] = new[k]
                bu_ref[b, k + NLB // 2, rows, :] = new[k + NLB // 2]
            out.append(tuple(new))
        return tuple(out)

    init = tuple(tuple(st_ref[b, k] for k in range(NLB)) for b in range(nb))
    fin = lax.fori_loop(0, tp, step, init, unroll=8)
    for b in range(nb):
        for k in range(NLB):
            st_ref[b, k] = fin[b][k]

    for b in range(nb):
        for j in range(SSM_SLABS):
            xs = jnp.concatenate([bu_ref[b, k, j * pitch:j * pitch + tp, :] for k in range(NLB)], axis=1)
            r = _bdot(xs.astype(BF16), cm_ref[j])
            yo_ref[b * tp:(b + 1) * tp, j * LANES:(j + 1) * LANES] = r[:, :LANES]
            w_ref[b * tp:(b + 1) * tp, j * LANES:(j + 1) * LANES] = r[:, LANES:]

    def glu(y):
        ys = 0.5 * y * (1.0 + jnp.tanh(math.sqrt(2.0 / math.pi) * (y + 0.044715 * (y * y * y))))
        gate = _sigmoid(_wdot(ys.astype(BF16), wg_ref[...]) + bgl_ref[...])
        return ys * gate

    first_row = lax.broadcasted_iota(jnp.int32, (tp, D_MODEL), 0) == 0
    shifted = []
    for b in range(nb):
        w = w_ref[b * tp:(b + 1) * tp, :]
        shifted.append(jnp.where(first_row, wc_ref[b], pltpu.roll(w, 1, 0)))
        wc_ref[b] = w[tp - 1:tp, :]
    yo_ref[...] = glu(yo_ref[...] + d_ref[...] * uo_ref[...])
    w_ref[...] = glu(jnp.concatenate(shifted, axis=0) + k0_ref[...] + d_ref[...] * ue_ref[...])
    for b in range(nb):
        for j in range(SSM_SLABS):
            il_ref[j, pl.ds(0, tp, stride=2), :] = w_ref[b * tp:(b + 1) * tp, j * LANES:(j + 1) * LANES]
            il_ref[j, pl.ds(1, tp, stride=2), :] = yo_ref[b * tp:(b + 1) * tp, j * LANES:(j + 1) * LANES]
        for j in range(SSM_SLABS):
            o_ref[b, :, j * LANES:(j + 1) * LANES] = il_ref[j].astype(BF16)


def _s5_branch(z3, bmat, cmat, lam_re, lam_im, d_skip, w_glu, b_glu):
    nb, seqlen, _ = z3.shape
    tc = min(TILE_S5, seqlen)
    tp = tc // 2
    const2 = lambda c: (0, 0)
    const3 = lambda c: (0, 0, 0)
    single = pl.Buffered(1)
    return pl.pallas_call(
        _s5_kernel,
        grid=(seqlen // tc,),
        in_specs=[
            pl.BlockSpec((nb, tc, D_MODEL), lambda c: (0, c, 0)),
            pl.BlockSpec(bmat.shape, const3, pipeline_mode=single),
            pl.BlockSpec(cmat.shape, const3, pipeline_mode=single),
            pl.BlockSpec(lam_re.shape, const2),
            pl.BlockSpec(lam_im.shape, const2),
            pl.BlockSpec((1, D_MODEL), const2),
            pl.BlockSpec((D_MODEL, D_MODEL), const2, pipeline_mode=single),
            pl.BlockSpec((1, D_MODEL), const2),
        ],
        out_specs=pl.BlockSpec((nb, tc, D_MODEL), lambda c: (0, c, 0)),
        out_shape=jax.ShapeDtypeStruct((nb, seqlen, D_MODEL), BF16),
        scratch_shapes=[
            pltpu.VMEM((nb, NLB, SSM_SLABS * (tp + SLAB_ROW_PAD), LANES), F32),
            pltpu.VMEM((nb, NLB, SSM_SLABS, LANES), F32),
            pltpu.VMEM((nb, 1, D_MODEL), F32),
            pltpu.VMEM((SSM_SLABS, tc, LANES), F32),
        ] + [pltpu.VMEM((nb * tp, D_MODEL), F32)] * 5,
        compiler_params=_params("arbitrary"),
        name="s5_branch",
    )(z3, bmat, cmat, lam_re, lam_im, d_skip, w_glu, b_glu)


def _s5_matrices(lam_re, lam_im, log_dt, b_re, b_im, c_re, c_im):
    dt = jnp.exp(log_dt)[:, None]
    mag = jnp.exp(lam_re * dt)
    lb_re, lb_im = mag * jnp.cos(lam_im * dt), mag * jnp.sin(lam_im * dt)
    den = lam_re * lam_re + lam_im * lam_im
    cf_re = ((lb_re - 1.0) * lam_re + lb_im * lam_im) / den
    cf_im = (lb_im * lam_re - (lb_re - 1.0) * lam_im) / den
    eye = jnp.eye(SLAB_GROUPS, dtype=F32)[None, :, None, :, None]
    by_slab = (SSM_SLABS, SLAB_GROUPS)

    def col_factor(x):
        return x.reshape(by_slab + (SSM_STATE,))[:, None, None, :, :]

    def row_factor(x):
        return x.reshape(by_slab + (SSM_STATE,))[:, :, :, None, None]

    def in_place(x):
        return x.reshape(by_slab + (SSM_STATE, SSM_GROUP)).transpose(0, 1, 3, 2)[:, :, :, None, :] * eye

    def out_place(x):
        return x.reshape(by_slab + (SSM_GROUP, SSM_STATE)).transpose(0, 3, 1, 2)[:, None] * eye

    def in_flat(x):
        return x.reshape(SSM_SLABS, LANES, SLAB_STATES)

    def out_flat(x):
        return x.reshape(SSM_SLABS, SLAB_STATES, LANES)

    br, bi = in_place(b_re), in_place(b_im)
    bo_re = col_factor(cf_re) * br - col_factor(cf_im) * bi
    bo_im = col_factor(cf_re) * bi + col_factor(cf_im) * br
    be_re = col_factor(lb_re) * bo_re - col_factor(lb_im) * bo_im
    be_im = col_factor(lb_re) * bo_im + col_factor(lb_im) * bo_re
    cr, ci = out_place(c_re), out_place(c_im)
    cn_re = row_factor(lb_re) * cr - row_factor(lb_im) * ci
    cn_im = row_factor(lb_re) * ci + row_factor(lb_im) * cr
    b_odd = jnp.concatenate([in_flat(bo_re), in_flat(bo_im)], axis=2)
    cur = jnp.concatenate([out_flat(cr), out_flat(-ci)], axis=1)
    k0 = jnp.einsum('jrs,jsc->jrc', b_odd, cur, precision=lax.Precision.HIGHEST)
    top = jnp.concatenate([in_flat(be_re), in_flat(be_im), k0], axis=2)
    bot = jnp.concatenate([b_odd, jnp.zeros_like(k0)], axis=2)
    bmat = jnp.concatenate([top, bot], axis=1).astype(BF16)
    nxt = jnp.concatenate([out_flat(cn_re), out_flat(-cn_im)], axis=1)
    cmat = jnp.concatenate([cur, nxt], axis=2).astype(BF16)
    lre = (lb_re * lb_re - lb_im * lb_im).reshape(SSM_SLABS, SLAB_STATES)
    lim = (2.0 * lb_re * lb_im).reshape(SSM_SLABS, SLAB_STATES)
    return bmat, cmat, lre, lim


def _dattn_kernel(q_ref, qn_ref, k_ref, v_ref, lmb_ref, hn_ref, o_ref,
                  qm_ref, vx_ref, sx_ref, sy_ref, m_ref, acc_ref, *, lam_init):
    tq = q_ref.shape[0]
    tk = tq
    nstrips = tk // LANES
    nheads = q_ref.shape[1] // LANES
    nkb = k_ref.shape[0] // tk
    qi = pl.program_id(2)
    lane = lax.broadcasted_iota(jnp.int32, (tq, LANES), 1)
    nt = (((1,), (1,)), ((), ()))

    @pl.when(qi == 0)
    def _():
        for hh in range(nheads):
            vx_ref[hh, :, :DA_V_DIM] = v_ref[:, hh * LANES:(hh + 1) * LANES]
            vx_ref[hh, :, DA_V_DIM:] = jnp.ones((v_ref.shape[0], DA_V_DIM), BF16)

    def mask_maps(src_ref):
        for hh in range(nheads):
            q = src_ref[:, hh * LANES:(hh + 1) * LANES]
            zero = jnp.zeros_like(q)
            qm_ref[2 * hh] = jnp.where(lane < DA_HEAD_DIM, q, zero)
            qm_ref[2 * hh + 1] = jnp.where(lane >= DA_HEAD_DIM, q, zero)

    def scores(ci, ki, s_ref):
        start = pl.multiple_of(ki * tk, tk)
        hh = ci // 2
        kb = k_ref[pl.ds(start, tk), hh * LANES:(hh + 1) * LANES]
        s_ref[ci] = lax.dot_general(qm_ref[ci], kb, nt, preferred_element_type=F32)

    def update(ci, ki, s_ref, first=False):
        start = pl.multiple_of(ki * tk, tk)
        hh = ci // 2
        vb = vx_ref[hh, pl.ds(start, tk), :]
        s = s_ref[ci]
        m_next = jnp.max(s, axis=1, keepdims=True)
        if first:
            m_next = jnp.broadcast_to(m_next, (tq, LANES))
        else:
            m_prev = m_ref[ci]
            m_next = jnp.maximum(m_prev, m_next)
            alpha = jnp.exp2(m_prev - m_next)
        ps = [jnp.exp2(s[:, c * LANES:(c + 1) * LANES] - m_next) for c in range(nstrips)]
        pv = _bdot(jnp.concatenate(ps, axis=1).astype(BF16), vb)
        acc_ref[ci] = pv if first else jnp.concatenate([alpha, alpha], axis=1) * acc_ref[ci] + pv
        m_ref[ci] = m_next

    def update_diag(ci, ki, s_ref):
        half = tq // 2
        start = pl.multiple_of(ki * tk, tk)
        hh = ci // 2
        row = lax.broadcasted_iota(jnp.int32, (half, half), 0) // CHUNK
        col = lax.broadcasted_iota(jnp.int32, (half, half), 1) // CHUNK
        visible = col <= row
        for part in range(2):
            rows = slice(part * half, (part + 1) * half)
            ncols = half * (part + 1)
            strips = []
            for c in range(ncols // LANES):
                s = s_ref[ci, rows, c * LANES:(c + 1) * LANES]
                off = c * LANES - part * half
                if off >= 0:
                    s = jnp.where(visible[:, off:off + LANES], s, NEG_BIG)
                strips.append(s)
            m_prev = m_ref[ci, rows, :]
            m_cur = functools.reduce(jnp.maximum, strips)
            m_next = jnp.maximum(m_prev, jnp.max(m_cur, axis=1, keepdims=True))
            alpha = jnp.exp2(m_prev - m_next)
            p = jnp.concatenate([jnp.exp2(s - m_next) for s in strips], axis=1).astype(BF16)
            pv = _bdot(p, vx_ref[hh, pl.ds(start, ncols), :])
            acc_ref[ci, rows, :] = jnp.concatenate([alpha, alpha], axis=1) * acc_ref[ci, rows, :] + pv
            m_ref[ci, rows, :] = m_next

    def stage(k_next, s_next_ref, k_cur, s_cur_ref, diag=False):
        for ci in range(2 * nheads):
            if k_next is not None:
                scores(ci, k_next, s_next_ref)
            if diag:
                update_diag(ci, k_cur, s_cur_ref)
            else:
                update(ci, k_cur, s_cur_ref)

    @pl.when(qi == 0)
    def _():
        mask_maps(q_ref)
        m_ref[...] = jnp.full(m_ref.shape, NEG_BIG, F32)
        acc_ref[...] = jnp.zeros(acc_ref.shape, F32)
        for ci in range(2 * nheads):
            scores(ci, 0, sx_ref)

    def pair(p, carry):
        stage(2 * p + 2, sy_ref, 2 * p + 1, sx_ref)
        stage(2 * p + 3, sx_ref, 2 * p + 2, sy_ref)
        return carry

    lax.fori_loop(0, (qi - 1) // 2, pair, 0)
    diag_in_sx = (qi == 0) | (qi % 2 == 1)

    @pl.when(diag_in_sx)
    def _():
        stage(None, None, qi, sx_ref, diag=True)

    @pl.when(jnp.logical_not(diag_in_sx))
    def _():
        stage(qi, sy_ref, qi - 1, sx_ref)
        stage(None, None, qi, sy_ref, diag=True)

    lv = lmb_ref[...]
    lam = (jnp.exp(jnp.sum(lv[0:1] * lv[1:2], axis=1, keepdims=True))
           - jnp.exp(jnp.sum(lv[2:3] * lv[3:4], axis=1, keepdims=True)) + lam_init)
    for hh in range(nheads):
        o = (acc_ref[2 * hh, :, :DA_V_DIM] / acc_ref[2 * hh, :, DA_V_DIM:]
             - lam * (acc_ref[2 * hh + 1, :, :DA_V_DIM] / acc_ref[2 * hh + 1, :, DA_V_DIM:]))
        o = o * lax.rsqrt(jnp.mean(o * o, axis=-1, keepdims=True) + EPS) * hn_ref[...]
        o_ref[:, hh * LANES:(hh + 1) * LANES] = (o * (1.0 - lam_init)).astype(BF16)

    mask_maps(qn_ref)
    for ci in range(2 * nheads):
        scores(ci, 0, sy_ref)
    for ci in range(2 * nheads):
        scores(ci, min(1, nkb - 1), sx_ref)
        update(ci, 0, sy_ref, first=True)


def _diff_attention(z3, lmb, head_norm, lam_init):
    nb, seqlen, _ = z3.shape
    tq = min(TILE_ATTN, seqlen)
    hps = ATTN_HEADS_PER_STEP
    width = hps * LANES
    qcol, kcol, vcol = (c // width for c in (D_MODEL, 2 * D_MODEL, 3 * D_MODEL))
    return pl.pallas_call(
        functools.partial(_dattn_kernel, lam_init=lam_init),
        grid=(nb, DA_HEADS // hps, seqlen // tq),
        in_specs=[
            pl.BlockSpec((None, tq, width), lambda b, h, i: (b, i, qcol + h)),
            pl.BlockSpec((None, tq, width), lambda b, h, i: (b, jnp.minimum(i + 1, seqlen // tq - 1), qcol + h)),
            pl.BlockSpec((None, seqlen, width), lambda b, h, i: (b, 0, kcol + h)),
            pl.BlockSpec((None, seqlen, width), lambda b, h, i: (b, 0, vcol + h)),
            pl.BlockSpec((4, DA_HEAD_DIM), lambda b, h, i: (0, 0)),
            pl.BlockSpec((1, DA_V_DIM), lambda b, h, i: (0, 0)),
        ],
        out_specs=pl.BlockSpec((None, tq, width), lambda b, h, i: (b, i, h)),
        out_shape=jax.ShapeDtypeStruct((nb, seqlen, D_MODEL), BF16),
        scratch_shapes=[pltpu.VMEM((2 * hps, tq, LANES), BF16),
                        pltpu.VMEM((hps, seqlen, 2 * DA_V_DIM), BF16),
                        pltpu.VMEM((2 * hps, tq, tq), F32), pltpu.VMEM((2 * hps, tq, tq), F32),
                        pltpu.VMEM((2 * hps, tq, LANES), F32),
                        pltpu.VMEM((2 * hps, tq, 2 * DA_V_DIM), F32)],
        compiler_params=_params("parallel", "parallel", "arbitrary"),
        name="diff_attention",
    )(z3, z3, z3, z3, lmb, head_norm)


def _merge_kernel(ys_ref, ya_ref, g_ref, x_ref, wa_ref, wb_ref, wo_ref, n_ref, o_ref):
    a = _wdot(ys_ref[...], wa_ref[...])
    b = _wdot(ya_ref[...], wb_ref[...])
    ga = g_ref[:, :D_MODEL].astype(F32)
    gb = g_ref[:, D_MODEL:].astype(F32)
    merged = (ga * a + gb * b).astype(BF16)
    o_ref[...] = x_ref[...] + _rms(_wdot(merged, wo_ref[...]), n_ref[...])


def _merge(ys2, ya2, z2, x2, w_a, w_b, w_o, gain):
    t = x2.shape[0]
    tm = min(TILE_MERGE, t)
    row = lambda i: (i, 0)
    const = lambda i: (0, 0)
    wspec = pl.BlockSpec((D_MODEL, D_MODEL), const, pipeline_mode=pl.Buffered(1))
    return pl.pallas_call(
        _merge_kernel,
        grid=(t // tm,),
        in_specs=[
            pl.BlockSpec((tm, D_MODEL), row),
            pl.BlockSpec((tm, D_MODEL), row),
            pl.BlockSpec((tm, 2 * D_MODEL), lambda i: (i, 2)),
            pl.BlockSpec((tm, D_MODEL), row),
            wspec, wspec, wspec,
            pl.BlockSpec((1, D_MODEL), const),
        ],
        out_specs=pl.BlockSpec((tm, D_MODEL), row),
        out_shape=jax.ShapeDtypeStruct((t, D_MODEL), F32),
        compiler_params=_params("parallel"),
        name="merge",
    )(ys2, ya2, z2, x2, w_a, w_b, w_o, gain)


def _memkv_kernel(m_ref, g_ref, w_ref, o_ref):
    o_ref[...] = _wdot(_rms(m_ref[...], g_ref[...]).astype(BF16), w_ref[...]).astype(BF16)


def _memory_kv(mem2, gain, w_xkv):
    rows = mem2.shape[0]
    const = lambda i: (0, 0)
    return pl.pallas_call(
        _memkv_kernel,
        grid=(1,),
        in_specs=[
            pl.BlockSpec((rows, D_MODEL), const),
            pl.BlockSpec((1, D_MODEL), const),
            pl.BlockSpec((D_MODEL, 2 * D_MODEL), const),
        ],
        out_specs=pl.BlockSpec((rows, 2 * D_MODEL), const),
        out_shape=jax.ShapeDtypeStruct((rows, 2 * D_MODEL), BF16),
        compiler_params=_params("arbitrary"),
        name="memory_kv",
    )(mem2, gain, w_xkv)


def _xattn_kernel(x_ref, kv_ref, npre_ref, wq_ref, wo_ref, npost_ref, o_ref, oh_ref):
    x = x_ref[...]
    h = _rms(x, npre_ref[...]).astype(BF16)
    q = (_wdot(h, wq_ref[...]) * (XA_HEAD_DIM ** -0.5 * LOG2E)).astype(BF16)
    nt = (((1,), (1,)), ((), ()))
    for hd in range(XA_HEADS):
        lo = hd * XA_HEAD_DIM
        k = kv_ref[:, lo:lo + XA_HEAD_DIM]
        v = kv_ref[:, D_MODEL + lo:D_MODEL + lo + XA_HEAD_DIM]
        s = lax.dot_general(q[:, lo:lo + XA_HEAD_DIM], k, nt, preferred_element_type=F32)
        p = jnp.exp2(s - jnp.max(s, axis=1, keepdims=True))
        o = _bdot(p.astype(BF16), v) / jnp.sum(p, axis=1, keepdims=True)
        oh_ref[:, lo:lo + XA_HEAD_DIM] = o.astype(BF16)
    o_ref[...] = x + _rms(_wdot(oh_ref[...], wo_ref[...]), npost_ref[...])


def _cross_attention(x3, kv3, n_pre, w_xq, w_xo, n_post):
    nb, seqlen, _ = x3.shape
    mlen = kv3.shape[1]
    tm = min(TILE_XATTN, seqlen)
    const = lambda b, i: (0, 0)
    wspec = pl.BlockSpec((D_MODEL, D_MODEL), const, pipeline_mode=pl.Buffered(1))
    vspec = pl.BlockSpec((1, D_MODEL), const)
    return pl.pallas_call(
        _xattn_kernel,
        grid=(nb, seqlen // tm),
        in_specs=[
            pl.BlockSpec((None, tm, D_MODEL), lambda b, i: (b, i, 0)),
            pl.BlockSpec((None, mlen, 2 * D_MODEL), lambda b, i: (b, 0, 0)),
            vspec, wspec, wspec, vspec,
        ],
        out_specs=pl.BlockSpec((None, tm, D_MODEL), lambda b, i: (b, i, 0)),
        out_shape=jax.ShapeDtypeStruct((nb, seqlen, D_MODEL), F32),
        scratch_shapes=[pltpu.VMEM((tm, D_MODEL), BF16)],
        compiler_params=_params("parallel", "parallel"),
        name="cross_attention",
    )(x3, kv3, n_pre, w_xq, w_xo, n_post)


def _mlp_kernel(x_ref, npre_ref, w1_ref, w2_ref, npost_ref, o_ref):
    x = x_ref[...]
    h = _rms(x, npre_ref[...]).astype(BF16)
    f = jnp.zeros(x.shape, F32)
    for c in range(D_FF // D_MODEL):
        lo = c * D_MODEL
        a = jnp.maximum(_wdot(h, w1_ref[:, lo:lo + D_MODEL]), 0.0)
        f = f + _wdot((a * a).astype(BF16), w2_ref[lo:lo + D_MODEL, :])
    o_ref[...] = x + _rms(f, npost_ref[...])


def _mlp(x2, n_pre, w1, w2, n_post):
    t = x2.shape[0]
    tm = min(TILE_MLP, t)
    row = lambda i: (i, 0)
    const = lambda i: (0, 0)
    vspec = pl.BlockSpec((1, D_MODEL), const)
    return pl.pallas_call(
        _mlp_kernel,
        grid=(t // tm,),
        in_specs=[
            pl.BlockSpec((tm, D_MODEL), row),
            vspec,
            pl.BlockSpec((D_MODEL, D_FF), const, pipeline_mode=pl.Buffered(1)),
            pl.BlockSpec((D_FF, D_MODEL), const, pipeline_mode=pl.Buffered(1)),
            vspec,
        ],
        out_specs=pl.BlockSpec((tm, D_MODEL), row),
        out_shape=jax.ShapeDtypeStruct((t, D_MODEL), F32),
        compiler_params=_params("parallel"),
        name="mlp",
    )(x2, n_pre, w1, w2, n_post)


def _rope_tables(seqlen):
    inv = ROPE_THETA ** (-jnp.arange(0, DA_HEAD_DIM, 2, dtype=F32) / DA_HEAD_DIM)
    ang = jnp.arange(seqlen, dtype=F32)[:, None] * inv[None, :]
    cos = jnp.tile(jnp.cos(ang), (1, 4))
    sin = jnp.tile(jnp.concatenate([-jnp.sin(ang), jnp.sin(ang)], axis=1), (1, 2))
    return cos, sin


def kernel(x, mem, norm_mix_pre, w_in, b_gate, ssm_lambda_re, ssm_lambda_im, ssm_log_dt, ssm_b_re, ssm_b_im, ssm_c_re, ssm_c_im, ssm_d, w_glu, b_glu, w_ssm_proj, da_lambda_q1, da_lambda_k1, da_lambda_q2, da_lambda_k2, da_head_norm, w_da_proj, w_mix_out, norm_mix_post, norm_x_pre, norm_mem, w_xq, w_xkv, w_xo, norm_x_post, norm_ff_pre, w_ff1, w_ff2, norm_ff_post):
    nb, seqlen, _ = x.shape
    t = nb * seqlen
    depth = w_in.shape[0]
    tabs = _rope_tables(seqlen)
    x2 = x.reshape(t, D_MODEL)
    mem2 = mem.reshape(nb * mem.shape[1], D_MODEL)
    for l in range(depth):
        lam_init = 0.8 - 0.6 * math.exp(-0.3 * l)
        z2 = _in_projection(x2, norm_mix_pre[l][None], w_in[l], b_gate[l][None], tabs, seqlen)
        z3 = z2.reshape(nb, seqlen, IN_COLS)
        bmat, cmat, lre, lim = _s5_matrices(ssm_lambda_re[l], ssm_lambda_im[l], ssm_log_dt[l],
                                            ssm_b_re[l], ssm_b_im[l], ssm_c_re[l], ssm_c_im[l])
        ys = _s5_branch(z3, bmat, cmat, lre, lim, ssm_d[l][None], w_glu[l], b_glu[l][None])
        lmb = jnp.stack([da_lambda_q1[l], da_lambda_k1[l], da_lambda_q2[l], da_lambda_k2[l]])
        ya = _diff_attention(z3, lmb, da_head_norm[l][None], lam_init)
        x2 = _merge(ys.reshape(t, D_MODEL), ya.reshape(t, D_MODEL), z2, x2,
                    w_ssm_proj[l], w_da_proj[l], w_mix_out[l],
                    norm_mix_post[l][None])
        kv = _memory_kv(mem2, norm_mem[l][None], w_xkv[l])
        x3 = _cross_attention(x2.reshape(nb, seqlen, D_MODEL), kv.reshape(nb, -1, 2 * D_MODEL),
                              norm_x_pre[l][None], w_xq[l], w_xo[l],
                              norm_x_post[l][None])
        x2 = _mlp(x3.reshape(t, D_MODEL), norm_ff_pre[l][None], w_ff1[l],
                  w_ff2[l], norm_ff_post[l][None])
    return x2.reshape(nb, seqlen, D_MODEL)
```

```python
import functools
import math

import jax
import jax.numpy as jnp
from jax import lax
from jax.experimental import pallas as pl
from jax.experimental.pallas import tpu as pltpu

F32 = jnp.float32
BF16 = jnp.bfloat16

LANES = 128
TILE_IN_PROJ = 512
TILE_S5 = 512
TILE_ATTN = 512
TILE_MERGE = 1024
TILE_XATTN = 1024
TILE_MLP = 512

D_MODEL = 1024
EPS = 1e-6
CHUNK = 64
SSM_GROUP = 16
SSM_GROUPS = 64
SSM_STATE = 64
SSM_SLABS = 8
SLAB_GROUPS = SSM_GROUPS // SSM_SLABS
SLAB_STATES = SLAB_GROUPS * SSM_STATE
NLB = 2 * SLAB_STATES // LANES
SLAB_ROW_PAD = 8
DA_HEADS = 8
DA_HEAD_DIM = 64
DA_V_DIM = 128
ROPE_THETA = 10000.0
ATTN_HEADS_PER_STEP = 2
XA_HEADS = 4
XA_HEAD_DIM = 256
D_FF = 4096
IN_COLS = 6144
LOG2E = 1.4426950408889634
NEG_BIG = -1e30
VMEM_LIMIT_BYTES = 56 * 1024 * 1024


def _params(*semantics):
    return pltpu.CompilerParams(dimension_semantics=semantics, vmem_limit_bytes=VMEM_LIMIT_BYTES)


def _rms(x, g):
    return x * lax.rsqrt(jnp.mean(x * x, axis=-1, keepdims=True) + EPS) * g


def _bdot(a, b):
    return jnp.dot(a, b, preferred_element_type=F32)


def _wdot(a, w):
    return _bdot(a, w.astype(BF16))


def _sigmoid(x):
    return 0.5 * jnp.tanh(0.5 * x) + 0.5


def _inproj_kernel(x_ref, g_ref, w_ref, bg_ref, cos_ref, sin_ref, o_ref):
    tm = x_ref.shape[0]
    hn = _rms(x_ref[...], g_ref[...]).astype(BF16)
    lane = lax.broadcasted_iota(jnp.int32, (tm, LANES), 1)
    half = DA_HEAD_DIM // 2
    first_half = (lane % DA_HEAD_DIM) < half

    def rope(acc, col0, scale):
        c = cos_ref[...] * scale
        s = sin_ref[...] * scale
        for h in range(DA_HEADS):
            blk = acc[:, h * LANES:(h + 1) * LANES]
            partner = jnp.where(first_half, pltpu.roll(blk, LANES - half, 1), pltpu.roll(blk, half, 1))
            o_ref[:, col0 + h * LANES:col0 + (h + 1) * LANES] = (blk * c + partner * s).astype(BF16)

    for j in range(IN_COLS // D_MODEL):
        col0 = j * D_MODEL
        acc = _wdot(hn, w_ref[:, col0:col0 + D_MODEL])
        if j == 1:
            rope(acc, col0, DA_HEAD_DIM ** -0.5 * LOG2E)
        elif j == 2:
            rope(acc, col0, 1.0)
        elif j >= 4:
            gate = _sigmoid(acc + bg_ref[:, col0 - 4 * D_MODEL:col0 - 3 * D_MODEL])
            o_ref[:, col0:col0 + D_MODEL] = gate.astype(BF16)
        else:
            o_ref[:, col0:col0 + D_MODEL] = acc.astype(BF16)


def _in_projection(x2, gain, w_in, b_gate, tabs, seqlen):
    t = x2.shape[0]
    tm = min(TILE_IN_PROJ, seqlen)
    nl = seqlen // tm
    const = lambda i: (0, 0)
    tab_spec = pl.BlockSpec((tm, LANES), lambda i: (i % nl, 0))
    return pl.pallas_call(
        _inproj_kernel,
        grid=(t // tm,),
        in_specs=[
            pl.BlockSpec((tm, D_MODEL), lambda i: (i, 0)),
            pl.BlockSpec((1, D_MODEL), const),
            pl.BlockSpec((D_MODEL, IN_COLS), const, pipeline_mode=pl.Buffered(1)),
            pl.BlockSpec((1, 2 * D_MODEL), const),
            tab_spec, tab_spec,
        ],
        out_specs=pl.BlockSpec((tm, IN_COLS), lambda i: (i, 0)),
        out_shape=jax.ShapeDtypeStruct((t, IN_COLS), BF16),
        compiler_params=_params("parallel"),
        name="in_projection",
    )(x2, gain, w_in, b_gate, *tabs)


def _s5_kernel(u_ref, bm_ref, cm_ref, lre_ref, lim_ref, d_ref, wg_ref, bgl_ref, o_ref,
               bu_ref, st_ref, wc_ref, il_ref, ue_ref, uo_ref, yo_ref, w_ref, k0_ref):
    nb, tc, _ = u_ref.shape
    tp = tc // 2
    pitch = bu_ref.shape[2] // SSM_SLABS
    c = pl.program_id(0)

    @pl.when(c == 0)
    def _():
        st_ref[...] = jnp.zeros(st_ref.shape, F32)
        wc_ref[...] = jnp.zeros(wc_ref.shape, F32)

    for b in range(nb):
        for j in range(SSM_SLABS):
            il_ref[j] = u_ref[b, :, j * LANES:(j + 1) * LANES].astype(F32)
        for j in range(SSM_SLABS):
            u_e = il_ref[j, pl.ds(0, tp, stride=2), :]
            u_o = il_ref[j, pl.ds(1, tp, stride=2), :]
            ue_ref[b * tp:(b + 1) * tp, j * LANES:(j + 1) * LANES] = u_e
            uo_ref[b * tp:(b + 1) * tp, j * LANES:(j + 1) * LANES] = u_o
            lhs = jnp.concatenate([u_e, u_o], axis=1).astype(BF16)
            bu = _bdot(lhs, bm_ref[j])
            for k in range(NLB):
                bu_ref[b, k, j * pitch:j * pitch + tp, :] = bu[:, k * LANES:(k + 1) * LANES]
            k0_ref[b * tp:(b + 1) * tp, j * LANES:(j + 1) * LANES] = bu[:, NLB * LANES:]

    ar = [lre_ref[:, k * LANES:(k + 1) * LANES] for k in range(NLB // 2)]
    ai = [lim_ref[:, k * LANES:(k + 1) * LANES] for k in range(NLB // 2)]

    def step(t, carry):
        out = []
        for b in range(nb):
            xs = carry[b]
            new = [None] * NLB
            for k in range(NLB // 2):
                xr, xi = xs[k], xs[k + NLB // 2]
                rows = pl.ds(t, SSM_SLABS, stride=pitch)
                new[k] = ar[k] * xr - ai[k] * xi + bu_ref[b, k, rows, :]
                new[k + NLB // 2] = ar[k] * xi + ai[k] * xr + bu_ref[b, k + NLB // 2, rows, :]
                bu_ref[b, k, rows, :] = new[k]
                bu_ref[b, k + NLB // 2, rows, :] = new[k + NLB // 2]
            out.append(tuple(new))
        return tuple(out)

    init = tuple(tuple(st_ref[b, k] for k in range(NLB)) for b in range(nb))
    fin = lax.fori_loop(0, tp, step, init, unroll=8)
    for b in range(nb):
        for k in range(NLB):
            st_ref[b, k] = fin[b][k]

    for b in range(nb):
        for j in range(SSM_SLABS):
            xs = jnp.concatenate([bu_ref[b, k, j * pitch:j * pitch + tp, :] for k in range(NLB)], axis=1)
            r = _bdot(xs.astype(BF16), cm_ref[j])
            yo_ref[b * tp:(b + 1) * tp, j * LANES:(j + 1) * LANES] = r[:, :LANES]
            w_ref[b * tp:(b + 1) * tp, j * LANES:(j + 1) * LANES] = r[:, LANES:]

    def glu(y):
        ys = 0.5 * y * (1.0 + jnp.tanh(math.sqrt(2.0 / math.pi) * (y + 0.044715 * (y * y * y))))
        gate = _sigmoid(_wdot(ys.astype(BF16), wg_ref[...]) + bgl_ref[...])
        return ys * gate

    first_row = lax.broadcasted_iota(jnp.int32, (tp, D_MODEL), 0) == 0
    shifted = []
    for b in range(nb):
        w = w_ref[b * tp:(b + 1) * tp, :]
        shifted.append(jnp.where(first_row, wc_ref[b], pltpu.roll(w, 1, 0)))
        wc_ref[b] = w[tp - 1:tp, :]
    yo_ref[...] = glu(yo_ref[...] + d_ref[...] * uo_ref[...])
    w_ref[...] = glu(jnp.concatenate(shifted, axis=0) + k0_ref[...] + d_ref[...] * ue_ref[...])
    for b in range(nb):
        for j in range(SSM_SLABS):
            il_ref[j, pl.ds(0, tp, stride=2), :] = w_ref[b * tp:(b + 1) * tp, j * LANES:(j + 1) * LANES]
            il_ref[j, pl.ds(1, tp, stride=2), :] = yo_ref[b * tp:(b + 1) * tp, j * LANES:(j + 1) * LANES]
        for j in range(SSM_SLABS):
            o_ref[b, :, j * LANES:(j + 1) * LANES] = il_ref[j].astype(BF16)


def _s5_branch(z3, bmat, cmat, lam_re, lam_im, d_skip, w_glu, b_glu):
    nb, seqlen, _ = z3.shape
    tc = min(TILE_S5, seqlen)
    tp = tc // 2
    const2 = lambda c: (0, 0)
    const3 = lambda c: (0, 0, 0)
    single = pl.Buffered(1)
    return pl.pallas_call(
        _s5_kernel,
        grid=(seqlen // tc,),
        in_specs=[
            pl.BlockSpec((nb, tc, D_MODEL), lambda c: (0, c, 0)),
            pl.BlockSpec(bmat.shape, const3, pipeline_mode=single),
            pl.BlockSpec(cmat.shape, const3, pipeline_mode=single),
            pl.BlockSpec(lam_re.shape, const2),
            pl.BlockSpec(lam_im.shape, const2),
            pl.BlockSpec((1, D_MODEL), const2),
            pl.BlockSpec((D_MODEL, D_MODEL), const2, pipeline_mode=single),
            pl.BlockSpec((1, D_MODEL), const2),
        ],
        out_specs=pl.BlockSpec((nb, tc, D_MODEL), lambda c: (0, c, 0)),
        out_shape=jax.ShapeDtypeStruct((nb, seqlen, D_MODEL), BF16),
        scratch_shapes=[
            pltpu.VMEM((nb, NLB, SSM_SLABS * (tp + SLAB_ROW_PAD), LANES), F32),
            pltpu.VMEM((nb, NLB, SSM_SLABS, LANES), F32),
            pltpu.VMEM((nb, 1, D_MODEL), F32),
            pltpu.VMEM((SSM_SLABS, tc, LANES), F32),
        ] + [pltpu.VMEM((nb * tp, D_MODEL), F32)] * 5,
        compiler_params=_params("arbitrary"),
        name="s5_branch",
    )(z3, bmat, cmat, lam_re, lam_im, d_skip, w_glu, b_glu)


def _s5_matrices(lam_re, lam_im, log_dt, b_re, b_im, c_re, c_im):
    dt = jnp.exp(log_dt)[:, None]
    mag = jnp.exp(lam_re * dt)
    lb_re, lb_im = mag * jnp.cos(lam_im * dt), mag * jnp.sin(lam_im * dt)
    den = lam_re * lam_re + lam_im * lam_im
    cf_re = ((lb_re - 1.0) * lam_re + lb_im * lam_im) / den
    cf_im = (lb_im * lam_re - (lb_re - 1.0) * lam_im) / den
    eye = jnp.eye(SLAB_GROUPS, dtype=F32)[None, :, None, :, None]
    by_slab = (SSM_SLABS, SLAB_GROUPS)

    def col_factor(x):
        return x.reshape(by_slab + (SSM_STATE,))[:, None, None, :, :]

    def row_factor(x):
        return x.reshape(by_slab + (SSM_STATE,))[:, :, :, None, None]

    def in_place(x):
        return x.reshape(by_slab + (SSM_STATE, SSM_GROUP)).transpose(0, 1, 3, 2)[:, :, :, None, :] * eye

    def out_place(x):
        return x.reshape(by_slab + (SSM_GROUP, SSM_STATE)).transpose(0, 3, 1, 2)[:, None] * eye

    def in_flat(x):
        return x.reshape(SSM_SLABS, LANES, SLAB_STATES)

    def out_flat(x):
        return x.reshape(SSM_SLABS, SLAB_STATES, LANES)

    br, bi = in_place(b_re), in_place(b_im)
    bo_re = col_factor(cf_re) * br - col_factor(cf_im) * bi
    bo_im = col_factor(cf_re) * bi + col_factor(cf_im) * br
    be_re = col_factor(lb_re) * bo_re - col_factor(lb_im) * bo_im
    be_im = col_factor(lb_re) * bo_im + col_factor(lb_im) * bo_re
    cr, ci = out_place(c_re), out_place(c_im)
    cn_re = row_factor(lb_re) * cr - row_factor(lb_im) * ci
    cn_im = row_factor(lb_re) * ci + row_factor(lb_im) * cr
    b_odd = jnp.concatenate([in_flat(bo_re), in_flat(bo_im)], axis=2)
    cur = jnp.concatenate([out_flat(cr), out_flat(-ci)], axis=1)
    k0 = jnp.einsum('jrs,jsc->jrc', b_odd, cur, precision=lax.Precision.HIGHEST)
    top = jnp.concatenate([in_flat(be_re), in_flat(be_im), k0], axis=2)
    bot = jnp.concatenate([b_odd, jnp.zeros_like(k0)], axis=2)
    bmat = jnp.concatenate([top, bot], axis=1).astype(BF16)
    nxt = jnp.concatenate([out_flat(cn_re), out_flat(-cn_im)], axis=1)
    cmat = jnp.concatenate([cur, nxt], axis=2).astype(BF16)
    lre = (lb_re * lb_re - lb_im * lb_im).reshape(SSM_SLABS, SLAB_STATES)
    lim = (2.0 * lb_re * lb_im).reshape(SSM_SLABS, SLAB_STATES)
    return bmat, cmat, lre, lim


def _dattn_kernel(q_ref, qn_ref, k_ref, v_ref, lmb_ref, hn_ref, o_ref,
                  qm_ref, vx_ref, sx_ref, sy_ref, m_ref, acc_ref, *, lam_init):
    tq = q_ref.shape[0]
    tk = tq
    nstrips = tk // LANES
    nheads = q_ref.shape[1] // LANES
    nkb = k_ref.shape[0] // tk
    qi = pl.program_id(2)
    lane = lax.broadcasted_iota(jnp.int32, (tq, LANES), 1)
    nt = (((1,), (1,)), ((), ()))

    @pl.when(qi == 0)
    def _():
        for hh in range(nheads):
            vx_ref[hh, :, :DA_V_DIM] = v_ref[:, hh * LANES:(hh + 1) * LANES]
            vx_ref[hh, :, DA_V_DIM:] = jnp.ones((v_ref.shape[0], DA_V_DIM), BF16)

    def mask_maps(src_ref):
        for hh in range(nheads):
            q = src_ref[:, hh * LANES:(hh + 1) * LANES]
            zero = jnp.zeros_like(q)
            qm_ref[2 * hh] = jnp.where(lane < DA_HEAD_DIM, q, zero)
            qm_ref[2 * hh + 1] = jnp.where(lane >= DA_HEAD_DIM, q, zero)

    def scores(ci, ki, s_ref):
        start = pl.multiple_of(ki * tk, tk)
        hh = ci // 2
        kb = k_ref[pl.ds(start, tk), hh * LANES:(hh + 1) * LANES]
        s_ref[ci] = lax.dot_general(qm_ref[ci], kb, nt, preferred_element_type=F32)

    def update(ci, ki, s_ref, first=False):
        start = pl.multiple_of(ki * tk, tk)
        hh = ci // 2
        vb = vx_ref[hh, pl.ds(start, tk), :]
        s = s_ref[ci]
        m_next = jnp.max(s, axis=1, keepdims=True)
        if first:
            m_next = jnp.broadcast_to(m_next, (tq, LANES))
        else:
            m_prev = m_ref[ci]
            m_next = jnp.maximum(m_prev, m_next)
            alpha = jnp.exp2(m_prev - m_next)
        ps = [jnp.exp2(s[:, c * LANES:(c + 1) * LANES] - m_next) for c in range(nstrips)]
        pv = _bdot(jnp.concatenate(ps, axis=1).astype(BF16), vb)
        acc_ref[ci] = pv if first else jnp.concatenate([alpha, alpha], axis=1) * acc_ref[ci] + pv
        m_ref[ci] = m_next

    def update_diag(ci, ki, s_ref):
        half = tq // 2
        start = pl.multiple_of(ki * tk, tk)
        hh = ci // 2
        row = lax.broadcasted_iota(jnp.int32, (half, half), 0) // CHUNK
        col = lax.broadcasted_iota(jnp.int32, (half, half), 1) // CHUNK
        visible = col <= row
        for part in range(2):
            rows = slice(part * half, (part + 1) * half)
            ncols = half * (part + 1)
            strips = []
            for c in range(ncols // LANES):
                s = s_ref[ci, rows, c * LANES:(c + 1) * LANES]
                off = c * LANES - part * half
                if off >= 0:
                    s = jnp.where(visible[:, off:off + LANES], s, NEG_BIG)
                strips.append(s)
            m_prev = m_ref[ci, rows, :]
            m_cur = functools.reduce(jnp.maximum, strips)
            m_next = jnp.maximum(m_prev, jnp.max(m_cur, axis=1, keepdims=True))
            alpha = jnp.exp2(m_prev - m_next)
            p = jnp.concatenate([jnp.exp2(s - m_next) for s in strips], axis=1).astype(BF16)
            pv = _bdot(p, vx_ref[hh, pl.ds(start, ncols), :])
            acc_ref[ci, rows, :] = jnp.concatenate([alpha, alpha], axis=1) * acc_ref[ci, rows, :] + pv
            m_ref[ci, rows, :] = m_next

    def stage(k_next, s_next_ref, k_cur, s_cur_ref, diag=False):
        for ci in range(2 * nheads):
            if k_next is not None:
                scores(ci, k_next, s_next_ref)
            if diag:
                update_diag(ci, k_cur, s_cur_ref)
            else:
                update(ci, k_cur, s_cur_ref)

    @pl.when(qi == 0)
    def _():
        mask_maps(q_ref)
        m_ref[...] = jnp.full(m_ref.shape, NEG_BIG, F32)
        acc_ref[...] = jnp.zeros(acc_ref.shape, F32)
        for ci in range(2 * nheads):
            scores(ci, 0, sx_ref)

    def pair(p, carry):
        stage(2 * p + 2, sy_ref, 2 * p + 1, sx_ref)
        stage(2 * p + 3, sx_ref, 2 * p + 2, sy_ref)
        return carry

    lax.fori_loop(0, (qi - 1) // 2, pair, 0)
    diag_in_sx = (qi == 0) | (qi % 2 == 1)

    @pl.when(diag_in_sx)
    def _():
        stage(None, None, qi, sx_ref, diag=True)

    @pl.when(jnp.logical_not(diag_in_sx))
    def _():
        stage(qi, sy_ref, qi - 1, sx_ref)
        stage(None, None, qi, sy_ref, diag=True)

    lv = lmb_ref[...]
    lam = (jnp.exp(jnp.sum(lv[0:1] * lv[1:2], axis=1, keepdims=True))
           - jnp.exp(jnp.sum(lv[2:3] * lv[3:4], axis=1, keepdims=True)) + lam_init)
    for hh in range(nheads):
        o = (acc_ref[2 * hh, :, :DA_V_DIM] / acc_ref[2 * hh, :, DA_V_DIM:]
             - lam * (acc_ref[2 * hh + 1, :, :DA_V_DIM] / acc_ref[2 * hh + 1, :, DA_V_DIM:]))
        o = o * lax.rsqrt(jnp.mean(o * o, axis=-1, keepdims=True) + EPS) * hn_ref[...]
        o_ref[:, hh * LANES:(hh + 1) * LANES] = (o * (1.0 - lam_init)).astype(BF16)

    mask_maps(qn_ref)
    for ci in range(2 * nheads):
        scores(ci, 0, sy_ref)
    for ci in range(2 * nheads):
        scores(ci, min(1, nkb - 1), sx_ref)
        update(ci, 0, sy_ref, first=True)


def _diff_attention(z3, lmb, head_norm, lam_init):
    nb, seqlen, _ = z3.shape
    tq = min(TILE_ATTN, seqlen)
    hps = ATTN_HEADS_PER_STEP
    width = hps * LANES
    qcol, kcol, vcol = (c // width for c in (D_MODEL, 2 * D_MODEL, 3 * D_MODEL))
    return pl.pallas_call(
        functools.partial(_dattn_kernel, lam_init=lam_init),
        grid=(nb, DA_HEADS // hps, seqlen // tq),
        in_specs=[
            pl.BlockSpec((None, tq, width), lambda b, h, i: (b, i, qcol + h)),
            pl.BlockSpec((None, tq, width), lambda b, h, i: (b, jnp.minimum(i + 1, seqlen // tq - 1), qcol + h)),
            pl.BlockSpec((None, seqlen, width), lambda b, h, i: (b, 0, kcol + h)),
            pl.BlockSpec((None, seqlen, width), lambda b, h, i: (b, 0, vcol + h)),
            pl.BlockSpec((4, DA_HEAD_DIM), lambda b, h, i: (0, 0)),
            pl.BlockSpec((1, DA_V_DIM), lambda b, h, i: (0, 0)),
        ],
        out_specs=pl.BlockSpec((None, tq, width), lambda b, h, i: (b, i, h)),
        out_shape=jax.ShapeDtypeStruct((nb, seqlen, D_MODEL), BF16),
        scratch_shapes=[pltpu.VMEM((2 * hps, tq, LANES), BF16),
                        pltpu.VMEM((hps, seqlen, 2 * DA_V_DIM), BF16),
                        pltpu.VMEM((2 * hps, tq, tq), F32), pltpu.VMEM((2 * hps, tq, tq), F32),
                        pltpu.VMEM((2 * hps, tq, LANES), F32),
                        pltpu.VMEM((2 * hps, tq, 2 * DA_V_DIM), F32)],
        compiler_params=_params("parallel", "parallel", "arbitrary"),
        name="diff_attention",
    )(z3, z3, z3, z3, lmb, head_norm)


def _merge_kernel(ys_ref, ya_ref, g_ref, x_ref, wa_ref, wb_ref, wo_ref, n_ref, o_ref):
    a = _wdot(ys_ref[...], wa_ref[...])
    b = _wdot(ya_ref[...], wb_ref[...])
    ga = g_ref[:, :D_MODEL].astype(F32)
    gb = g_ref[:, D_MODEL:].astype(F32)
    merged = (ga * a + gb * b).astype(BF16)
    o_ref[...] = x_ref[...] + _rms(_wdot(merged, wo_ref[...]), n_ref[...])


def _merge(ys2, ya2, z2, x2, w_a, w_b, w_o, gain):
    t = x2.shape[0]
    tm = min(TILE_MERGE, t)
    row = lambda i: (i, 0)
    const = lambda i: (0, 0)
    wspec = pl.BlockSpec((D_MODEL, D_MODEL), const, pipeline_mode=pl.Buffered(1))
    return pl.pallas_call(
        _merge_kernel,
        grid=(t // tm,),
        in_specs=[
            pl.BlockSpec((tm, D_MODEL), row),
            pl.BlockSpec((tm, D_MODEL), row),
            pl.BlockSpec((tm, 2 * D_MODEL), lambda i: (i, 2)),
            pl.BlockSpec((tm, D_MODEL), row),
            wspec, wspec, wspec,
            pl.BlockSpec((1, D_MODEL), const),
        ],
        out_specs=pl.BlockSpec((tm, D_MODEL), row),
        out_shape=jax.ShapeDtypeStruct((t, D_MODEL), F32),
        compiler_params=_params("parallel"),
        name="merge",
    )(ys2, ya2, z2, x2, w_a, w_b, w_o, gain)


def _xattn_kernel(x_ref, mem_ref, nmem_ref, wkv_ref, npre_ref, wq_ref, wo_ref, npost_ref, o_ref,
                  kv_ref, oh_ref):
    @pl.when(pl.program_id(1) == 0)
    def _():
        kv_ref[...] = _wdot(_rms(mem_ref[...], nmem_ref[...]).astype(BF16), wkv_ref[...]).astype(BF16)

    x = x_ref[...]
    h = _rms(x, npre_ref[...]).astype(BF16)
    q = (_wdot(h, wq_ref[...]) * (XA_HEAD_DIM ** -0.5 * LOG2E)).astype(BF16)
    nt = (((1,), (1,)), ((), ()))
    for hd in range(XA_HEADS):
        lo = hd * XA_HEAD_DIM
        k = kv_ref[:, lo:lo + XA_HEAD_DIM]
        v = kv_ref[:, D_MODEL + lo:D_MODEL + lo + XA_HEAD_DIM]
        s = lax.dot_general(q[:, lo:lo + XA_HEAD_DIM], k, nt, preferred_element_type=F32)
        p = jnp.exp2(s - jnp.max(s, axis=1, keepdims=True))
        o = _bdot(p.astype(BF16), v) / jnp.sum(p, axis=1, keepdims=True)
        oh_ref[:, lo:lo + XA_HEAD_DIM] = o.astype(BF16)
    o_ref[...] = x + _rms(_wdot(oh_ref[...], wo_ref[...]), npost_ref[...])


def _cross_attention(x3, mem, n_mem, w_xkv, n_pre, w_xq, w_xo, n_post):
    nb, seqlen, _ = x3.shape
    mlen = mem.shape[1]
    tm = min(TILE_XATTN, seqlen)
    const = lambda b, i: (0, 0)
    single = pl.Buffered(1)
    wspec = pl.BlockSpec((D_MODEL, D_MODEL), const, pipeline_mode=single)
    vspec = pl.BlockSpec((1, D_MODEL), const)
    return pl.pallas_call(
        _xattn_kernel,
        grid=(nb, seqlen // tm),
        in_specs=[
            pl.BlockSpec((None, tm, D_MODEL), lambda b, i: (b, i, 0)),
            pl.BlockSpec((None, mlen, D_MODEL), lambda b, i: (b, 0, 0)),
            vspec,
            pl.BlockSpec((D_MODEL, 2 * D_MODEL), const, pipeline_mode=single),
            vspec, wspec, wspec, vspec,
        ],
        out_specs=pl.BlockSpec((None, tm, D_MODEL), lambda b, i: (b, i, 0)),
        out_shape=jax.ShapeDtypeStruct((nb, seqlen, D_MODEL), F32),
        scratch_shapes=[pltpu.VMEM((mlen, 2 * D_MODEL), BF16), pltpu.VMEM((tm, D_MODEL), BF16)],
        compiler_params=_params("parallel", "arbitrary"),
        name="cross_attention",
    )(x3, mem, n_mem, w_xkv, n_pre, w_xq, w_xo, n_post)


def _mlp_kernel(x_ref, npre_ref, w1_ref, w2_ref, npost_ref, o_ref):
    x = x_ref[...]
    h = _rms(x, npre_ref[...]).astype(BF16)
    f = jnp.zeros(x.shape, F32)
    for c in range(D_FF // D_MODEL):
        lo = c * D_MODEL
        a = jnp.maximum(_wdot(h, w1_ref[:, lo:lo + D_MODEL]), 0.0)
        f = f + _wdot((a * a).astype(BF16), w2_ref[lo:lo + D_MODEL, :])
    o_ref[...] = x + _rms(f, npost_ref[...])


def _mlp(x2, n_pre, w1, w2, n_post):
    t = x2.shape[0]
    tm = min(TILE_MLP, t)
    row = lambda i: (i, 0)
    const = lambda i: (0, 0)
    vspec = pl.BlockSpec((1, D_MODEL), const)
    return pl.pallas_call(
        _mlp_kernel,
        grid=(t // tm,),
        in_specs=[
            pl.BlockSpec((tm, D_MODEL), row),
            vspec,
            pl.BlockSpec((D_MODEL, D_FF), const, pipeline_mode=pl.Buffered(1)),
            pl.BlockSpec((D_FF, D_MODEL), const, pipeline_mode=pl.Buffered(1)),
            vspec,
        ],
        out_specs=pl.BlockSpec((tm, D_MODEL), row),
        out_shape=jax.ShapeDtypeStruct((t, D_MODEL), F32),
        compiler_params=_params("parallel"),
        name="mlp",
    )(x2, n_pre, w1, w2, n_post)


def _rope_tables(seqlen):
    inv = ROPE_THETA ** (-jnp.arange(0, DA_HEAD_DIM, 2, dtype=F32) / DA_HEAD_DIM)
    ang = jnp.arange(seqlen, dtype=F32)[:, None] * inv[None, :]
    cos = jnp.tile(jnp.cos(ang), (1, 4))
    sin = jnp.tile(jnp.concatenate([-jnp.sin(ang), jnp.sin(ang)], axis=1), (1, 2))
    return cos, sin


def kernel(x, mem, norm_mix_pre, w_in, b_gate, ssm_lambda_re, ssm_lambda_im, ssm_log_dt, ssm_b_re, ssm_b_im, ssm_c_re, ssm_c_im, ssm_d, w_glu, b_glu, w_ssm_proj, da_lambda_q1, da_lambda_k1, da_lambda_q2, da_lambda_k2, da_head_norm, w_da_proj, w_mix_out, norm_mix_post, norm_x_pre, norm_mem, w_xq, w_xkv, w_xo, norm_x_post, norm_ff_pre, w_ff1, w_ff2, norm_ff_post):
    nb, seqlen, _ = x.shape
    t = nb * seqlen
    depth = w_in.shape[0]
    tabs = _rope_tables(seqlen)
    x2 = x.reshape(t, D_MODEL)
    for l in range(depth):
        lam_init = 0.8 - 0.6 * math.exp(-0.3 * l)
        z2 = _in_projection(x2, norm_mix_pre[l][None], w_in[l], b_gate[l][None], tabs, seqlen)
        z3 = z2.reshape(nb, seqlen, IN_COLS)
        bmat, cmat, lre, lim = _s5_matrices(ssm_lambda_re[l], ssm_lambda_im[l], ssm_log_dt[l],
                                            ssm_b_re[l], ssm_b_im[l], ssm_c_re[l], ssm_c_im[l])
        ys = _s5_branch(z3, bmat, cmat, lre, lim, ssm_d[l][None], w_glu[l], b_glu[l][None])
        lmb = jnp.stack([da_lambda_q1[l], da_lambda_k1[l], da_lambda_q2[l], da_lambda_k2[l]])
        ya = _diff_attention(z3, lmb, da_head_norm[l][None], lam_init)
        x2 = _merge(ys.reshape(t, D_MODEL), ya.reshape(t, D_MODEL), z2, x2,
                    w_ssm_proj[l], w_da_proj[l], w_mix_out[l],
                    norm_mix_post[l][None])
        x3 = _cross_attention(x2.reshape(nb, seqlen, D_MODEL), mem, norm_mem[l][None], w_xkv[l],
                              norm_x_pre[l][None], w_xq[l], w_xo[l],
                              norm_x_post[l][None])
        x2 = _mlp(x3.reshape(t, D_MODEL), norm_ff_pre[l][None], w_ff1[l],
                  w_ff2[l], norm_ff_post[l][None])
    return x2.reshape(nb, seqlen, D_MODEL)
```

```python
import functools
import math

import jax
import jax.numpy as jnp
from jax import lax
from jax.experimental import pallas as pl
from jax.experimental.pallas import tpu as pltpu

F32 = jnp.float32
BF16 = jnp.bfloat16

LANES = 128
TILE_IN_PROJ = 512
TILE_S5 = 512
TILE_ATTN = 512
TILE_MERGE = 1024
TILE_XATTN = 1024
TILE_MLP = 512

D_MODEL = 1024
EPS = 1e-6
CHUNK = 64
SSM_GROUP = 16
SSM_GROUPS = 64
SSM_STATE = 64
SSM_SLABS = 8
SLAB_GROUPS = SSM_GROUPS // SSM_SLABS
SLAB_STATES = SLAB_GROUPS * SSM_STATE
NLB = 2 * SLAB_STATES // LANES
SLAB_ROW_PAD = 8
DA_HEADS = 8
DA_HEAD_DIM = 64
DA_V_DIM = 128
ROPE_THETA = 10000.0
ATTN_HEADS_PER_STEP = 2
ATTN_BLOCKS_PER_STEP = 2
XA_HEADS = 4
XA_HEAD_DIM = 256
D_FF = 4096
IN_COLS = 6144
LOG2E = 1.4426950408889634
NEG_BIG = -1e30
VMEM_LIMIT_BYTES = 56 * 1024 * 1024


def _params(*semantics):
    return pltpu.CompilerParams(dimension_semantics=semantics, vmem_limit_bytes=VMEM_LIMIT_BYTES)


def _rms(x, g):
    return x * lax.rsqrt(jnp.mean(x * x, axis=-1, keepdims=True) + EPS) * g


def _bdot(a, b):
    return jnp.dot(a, b, preferred_element_type=F32)


def _wdot(a, w):
    return _bdot(a, w.astype(BF16))


def _sigmoid(x):
    return 0.5 * jnp.tanh(0.5 * x) + 0.5


def _inproj_kernel(x_ref, g_ref, w_ref, bg_ref, cos_ref, sin_ref, o_ref):
    tm = x_ref.shape[0]
    hn = _rms(x_ref[...], g_ref[...]).astype(BF16)
    lane = lax.broadcasted_iota(jnp.int32, (tm, LANES), 1)
    half = DA_HEAD_DIM // 2
    first_half = (lane % DA_HEAD_DIM) < half

    def rope(acc, col0, scale):
        c = cos_ref[...] * scale
        s = sin_ref[...] * scale
        for h in range(DA_HEADS):
            blk = acc[:, h * LANES:(h + 1) * LANES]
            partner = jnp.where(first_half, pltpu.roll(blk, LANES - half, 1), pltpu.roll(blk, half, 1))
            o_ref[:, col0 + h * LANES:col0 + (h + 1) * LANES] = (blk * c + partner * s).astype(BF16)

    for j in range(IN_COLS // D_MODEL):
        col0 = j * D_MODEL
        acc = _wdot(hn, w_ref[:, col0:col0 + D_MODEL])
        if j == 1:
            rope(acc, col0, DA_HEAD_DIM ** -0.5 * LOG2E)
        elif j == 2:
            rope(acc, col0, 1.0)
        elif j >= 4:
            gate = _sigmoid(acc + bg_ref[:, col0 - 4 * D_MODEL:col0 - 3 * D_MODEL])
            o_ref[:, col0:col0 + D_MODEL] = gate.astype(BF16)
        else:
            o_ref[:, col0:col0 + D_MODEL] = acc.astype(BF16)


def _in_projection(x2, gain, w_in, b_gate, tabs, seqlen):
    t = x2.shape[0]
    tm = min(TILE_IN_PROJ, seqlen)
    nl = seqlen // tm
    const = lambda i: (0, 0)
    tab_spec = pl.BlockSpec((tm, LANES), lambda i: (i % nl, 0))
    return pl.pallas_call(
        _inproj_kernel,
        grid=(t // tm,),
        in_specs=[
            pl.BlockSpec((tm, D_MODEL), lambda i: (i, 0)),
            pl.BlockSpec((1, D_MODEL), const),
            pl.BlockSpec((D_MODEL, IN_COLS), const, pipeline_mode=pl.Buffered(1)),
            pl.BlockSpec((1, 2 * D_MODEL), const),
            tab_spec, tab_spec,
        ],
        out_specs=pl.BlockSpec((tm, IN_COLS), lambda i: (i, 0)),
        out_shape=jax.ShapeDtypeStruct((t, IN_COLS), BF16),
        compiler_params=_params("parallel"),
        name="in_projection",
    )(x2, gain, w_in, b_gate, *tabs)


def _s5_kernel(u_ref, bm_ref, cm_ref, lre_ref, lim_ref, d_ref, wg_ref, bgl_ref, o_ref,
               bu_ref, st_ref, wc_ref, il_ref, ue_ref, uo_ref, yo_ref, w_ref, k0_ref):
    nb, tc, _ = u_ref.shape
    tp = tc // 2
    pitch = bu_ref.shape[2] // SSM_SLABS
    c = pl.program_id(0)

    @pl.when(c == 0)
    def _():
        st_ref[...] = jnp.zeros(st_ref.shape, F32)
        wc_ref[...] = jnp.zeros(wc_ref.shape, F32)

    for b in range(nb):
        for j in range(SSM_SLABS):
            il_ref[j] = u_ref[b, :, j * LANES:(j + 1) * LANES].astype(F32)
        for j in range(SSM_SLABS):
            u_e = il_ref[j, pl.ds(0, tp, stride=2), :]
            u_o = il_ref[j, pl.ds(1, tp, stride=2), :]
            ue_ref[b * tp:(b + 1) * tp, j * LANES:(j + 1) * LANES] = u_e
            uo_ref[b * tp:(b + 1) * tp, j * LANES:(j + 1) * LANES] = u_o
            lhs = jnp.concatenate([u_e, u_o], axis=1).astype(BF16)
            bu = _bdot(lhs, bm_ref[j])
            for k in range(NLB):
                bu_ref[b, k, j * pitch:j * pitch + tp, :] = bu[:, k * LANES:(k + 1) * LANES]
            k0_ref[b * tp:(b + 1) * tp, j * LANES:(j + 1) * LANES] = bu[:, NLB * LANES:]

    ar = [lre_ref[:, k * LANES:(k + 1) * LANES] for k in range(NLB // 2)]
    ai = [lim_ref[:, k * LANES:(k + 1) * LANES] for k in range(NLB // 2)]

    def step(t, carry):
        out = []
        for b in range(nb):
            xs = carry[b]
            new = [None] * NLB
            for k in range(NLB // 2):
                xr, xi = xs[k], xs[k + NLB // 2]
                rows = pl.ds(t, SSM_SLABS, stride=pitch)
                new[k] = ar[k] * xr - ai[k] * xi + bu_ref[b, k, rows, :]
                new[k + NLB // 2] = ar[k] * xi + ai[k] * xr + bu_ref[b, k + NLB // 2, rows, :]
                bu_ref[b, k, rows, :] = new[k]
                bu_ref[b, k + NLB // 2, rows, :] = new[k + NLB // 2]
            out.append(tuple(new))
        return tuple(out)

    init = tuple(tuple(st_ref[b, k] for k in range(NLB)) for b in range(nb))
    fin = lax.fori_loop(0, tp, step, init, unroll=8)
    for b in range(nb):
        for k in range(NLB):
            st_ref[b, k] = fin[b][k]

    for b in range(nb):
        for j in range(SSM_SLABS):
            xs = jnp.concatenate([bu_ref[b, k, j * pitch:j * pitch + tp, :] for k in range(NLB)], axis=1)
            r = _bdot(xs.astype(BF16), cm_ref[j])
            yo_ref[b * tp:(b + 1) * tp, j * LANES:(j + 1) * LANES] = r[:, :LANES]
            w_ref[b * tp:(b + 1) * tp, j * LANES:(j + 1) * LANES] = r[:, LANES:]

    def glu(y):
        ys = 0.5 * y * (1.0 + jnp.tanh(math.sqrt(2.0 / math.pi) * (y + 0.044715 * (y * y * y))))
        gate = _sigmoid(_wdot(ys.astype(BF16), wg_ref[...]) + bgl_ref[...])
        return ys * gate

    first_row = lax.broadcasted_iota(jnp.int32, (tp, D_MODEL), 0) == 0
    shifted = []
    for b in range(nb):
        w = w_ref[b * tp:(b + 1) * tp, :]
        shifted.append(jnp.where(first_row, wc_ref[b], pltpu.roll(w, 1, 0)))
        wc_ref[b] = w[tp - 1:tp, :]
    yo_ref[...] = glu(yo_ref[...] + d_ref[...] * uo_ref[...])
    w_ref[...] = glu(jnp.concatenate(shifted, axis=0) + k0_ref[...] + d_ref[...] * ue_ref[...])
    for b in range(nb):
        for j in range(SSM_SLABS):
            il_ref[j, pl.ds(0, tp, stride=2), :] = w_ref[b * tp:(b + 1) * tp, j * LANES:(j + 1) * LANES]
            il_ref[j, pl.ds(1, tp, stride=2), :] = yo_ref[b * tp:(b + 1) * tp, j * LANES:(j + 1) * LANES]
        for j in range(SSM_SLABS):
            o_ref[b, :, j * LANES:(j + 1) * LANES] = il_ref[j].astype(BF16)


def _s5_branch(z3, bmat, cmat, lam_re, lam_im, d_skip, w_glu, b_glu):
    nb, seqlen, _ = z3.shape
    tc = min(TILE_S5, seqlen)
    tp = tc // 2
    const2 = lambda c: (0, 0)
    const3 = lambda c: (0, 0, 0)
    single = pl.Buffered(1)
    return pl.pallas_call(
        _s5_kernel,
        grid=(seqlen // tc,),
        in_specs=[
            pl.BlockSpec((nb, tc, D_MODEL), lambda c: (0, c, 0)),
            pl.BlockSpec(bmat.shape, const3, pipeline_mode=single),
            pl.BlockSpec(cmat.shape, const3, pipeline_mode=single),
            pl.BlockSpec(lam_re.shape, const2),
            pl.BlockSpec(lam_im.shape, const2),
            pl.BlockSpec((1, D_MODEL), const2),
            pl.BlockSpec((D_MODEL, D_MODEL), const2, pipeline_mode=single),
            pl.BlockSpec((1, D_MODEL), const2),
        ],
        out_specs=pl.BlockSpec((nb, tc, D_MODEL), lambda c: (0, c, 0)),
        out_shape=jax.ShapeDtypeStruct((nb, seqlen, D_MODEL), BF16),
        scratch_shapes=[
            pltpu.VMEM((nb, NLB, SSM_SLABS * (tp + SLAB_ROW_PAD), LANES), F32),
            pltpu.VMEM((nb, NLB, SSM_SLABS, LANES), F32),
            pltpu.VMEM((nb, 1, D_MODEL), F32),
            pltpu.VMEM((SSM_SLABS, tc, LANES), F32),
        ] + [pltpu.VMEM((nb * tp, D_MODEL), F32)] * 5,
        compiler_params=_params("arbitrary"),
        name="s5_branch",
    )(z3, bmat, cmat, lam_re, lam_im, d_skip, w_glu, b_glu)


def _s5_matrices(lam_re, lam_im, log_dt, b_re, b_im, c_re, c_im):
    dt = jnp.exp(log_dt)[:, None]
    mag = jnp.exp(lam_re * dt)
    lb_re, lb_im = mag * jnp.cos(lam_im * dt), mag * jnp.sin(lam_im * dt)
    den = lam_re * lam_re + lam_im * lam_im
    cf_re = ((lb_re - 1.0) * lam_re + lb_im * lam_im) / den
    cf_im = (lb_im * lam_re - (lb_re - 1.0) * lam_im) / den
    eye = jnp.eye(SLAB_GROUPS, dtype=F32)[None, :, None, :, None]
    by_slab = (SSM_SLABS, SLAB_GROUPS)

    def col_factor(x):
        return x.reshape(by_slab + (SSM_STATE,))[:, None, None, :, :]

    def row_factor(x):
        return x.reshape(by_slab + (SSM_STATE,))[:, :, :, None, None]

    def in_place(x):
        return x.reshape(by_slab + (SSM_STATE, SSM_GROUP)).transpose(0, 1, 3, 2)[:, :, :, None, :] * eye

    def out_place(x):
        return x.reshape(by_slab + (SSM_GROUP, SSM_STATE)).transpose(0, 3, 1, 2)[:, None] * eye

    def in_flat(x):
        return x.reshape(SSM_SLABS, LANES, SLAB_STATES)

    def out_flat(x):
        return x.reshape(SSM_SLABS, SLAB_STATES, LANES)

    br, bi = in_place(b_re), in_place(b_im)
    bo_re = col_factor(cf_re) * br - col_factor(cf_im) * bi
    bo_im = col_factor(cf_re) * bi + col_factor(cf_im) * br
    be_re = col_factor(lb_re) * bo_re - col_factor(lb_im) * bo_im
    be_im = col_factor(lb_re) * bo_im + col_factor(lb_im) * bo_re
    cr, ci = out_place(c_re), out_place(c_im)
    cn_re = row_factor(lb_re) * cr - row_factor(lb_im) * ci
    cn_im = row_factor(lb_re) * ci + row_factor(lb_im) * cr
    b_odd = jnp.concatenate([in_flat(bo_re), in_flat(bo_im)], axis=2)
    cur = jnp.concatenate([out_flat(cr), out_flat(-ci)], axis=1)
    k0 = jnp.einsum('jrs,jsc->jrc', b_odd, cur, precision=lax.Precision.HIGHEST)
    top = jnp.concatenate([in_flat(be_re), in_flat(be_im), k0], axis=2)
    bot = jnp.concatenate([b_odd, jnp.zeros_like(k0)], axis=2)
    bmat = jnp.concatenate([top, bot], axis=1).astype(BF16)
    nxt = jnp.concatenate([out_flat(cn_re), out_flat(-cn_im)], axis=1)
    cmat = jnp.concatenate([cur, nxt], axis=2).astype(BF16)
    lre = (lb_re * lb_re - lb_im * lb_im).reshape(SSM_SLABS, SLAB_STATES)
    lim = (2.0 * lb_re * lb_im).reshape(SSM_SLABS, SLAB_STATES)
    return bmat, cmat, lre, lim


def _dattn_kernel(q_ref, qn_ref, k_ref, v_ref, lmb_ref, hn_ref, o_ref,
                  qm_ref, vx_ref, sx_ref, sy_ref, m_ref, acc_ref, *, lam_init, tq):
    tk = tq
    nsub = q_ref.shape[0] // tq
    nstrips = tk // LANES
    nheads = q_ref.shape[1] // LANES
    nkb = k_ref.shape[0] // tk
    lane = lax.broadcasted_iota(jnp.int32, (tq, LANES), 1)
    nt = (((1,), (1,)), ((), ()))

    @pl.when(pl.program_id(2) == 0)
    def _():
        for hh in range(nheads):
            vx_ref[hh, :, :DA_V_DIM] = v_ref[:, hh * LANES:(hh + 1) * LANES]
            vx_ref[hh, :, DA_V_DIM:] = jnp.ones((v_ref.shape[0], DA_V_DIM), BF16)

    def mask_maps(src_ref, row0):
        for hh in range(nheads):
            q = src_ref[row0:row0 + tq, hh * LANES:(hh + 1) * LANES]
            zero = jnp.zeros_like(q)
            qm_ref[2 * hh] = jnp.where(lane < DA_HEAD_DIM, q, zero)
            qm_ref[2 * hh + 1] = jnp.where(lane >= DA_HEAD_DIM, q, zero)

    def scores(ci, ki, s_ref):
        start = pl.multiple_of(ki * tk, tk)
        hh = ci // 2
        kb = k_ref[pl.ds(start, tk), hh * LANES:(hh + 1) * LANES]
        s_ref[ci] = lax.dot_general(qm_ref[ci], kb, nt, preferred_element_type=F32)

    def update(ci, ki, s_ref, first=False):
        start = pl.multiple_of(ki * tk, tk)
        hh = ci // 2
        vb = vx_ref[hh, pl.ds(start, tk), :]
        s = s_ref[ci]
        m_next = jnp.max(s, axis=1, keepdims=True)
        if first:
            m_next = jnp.broadcast_to(m_next, (tq, LANES))
        else:
            m_prev = m_ref[ci]
            m_next = jnp.maximum(m_prev, m_next)
            alpha = jnp.exp2(m_prev - m_next)
        ps = [jnp.exp2(s[:, c * LANES:(c + 1) * LANES] - m_next) for c in range(nstrips)]
        pv = _bdot(jnp.concatenate(ps, axis=1).astype(BF16), vb)
        acc_ref[ci] = pv if first else jnp.concatenate([alpha, alpha], axis=1) * acc_ref[ci] + pv
        m_ref[ci] = m_next

    def update_diag(ci, ki, s_ref):
        half = tq // 2
        start = pl.multiple_of(ki * tk, tk)
        hh = ci // 2
        row = lax.broadcasted_iota(jnp.int32, (half, half), 0) // CHUNK
        col = lax.broadcasted_iota(jnp.int32, (half, half), 1) // CHUNK
        visible = col <= row
        for part in range(2):
            rows = slice(part * half, (part + 1) * half)
            ncols = half * (part + 1)
            strips = []
            for c in range(ncols // LANES):
                s = s_ref[ci, rows, c * LANES:(c + 1) * LANES]
                off = c * LANES - part * half
                if off >= 0:
                    s = jnp.where(visible[:, off:off + LANES], s, NEG_BIG)
                strips.append(s)
            m_prev = m_ref[ci, rows, :]
            m_cur = functools.reduce(jnp.maximum, strips)
            m_next = jnp.maximum(m_prev, jnp.max(m_cur, axis=1, keepdims=True))
            alpha = jnp.exp2(m_prev - m_next)
            p = jnp.concatenate([jnp.exp2(s - m_next) for s in strips], axis=1).astype(BF16)
            pv = _bdot(p, vx_ref[hh, pl.ds(start, ncols), :])
            acc_ref[ci, rows, :] = jnp.concatenate([alpha, alpha], axis=1) * acc_ref[ci, rows, :] + pv
            m_ref[ci, rows, :] = m_next

    def stage(k_next, s_next_ref, k_cur, s_cur_ref, diag=False):
        for ci in range(2 * nheads):
            if k_next is not None:
                scores(ci, k_next, s_next_ref)
            if diag:
                update_diag(ci, k_cur, s_cur_ref)
            else:
                update(ci, k_cur, s_cur_ref)

    def query_block(sub):
        qi = pl.program_id(2) * nsub + sub
        rows = slice(sub * tq, (sub + 1) * tq)

        @pl.when(qi == 0)
        def _():
            mask_maps(q_ref, 0)
            m_ref[...] = jnp.full(m_ref.shape, NEG_BIG, F32)
            acc_ref[...] = jnp.zeros(acc_ref.shape, F32)
            for ci in range(2 * nheads):
                scores(ci, 0, sx_ref)

        def pair(p, carry):
            stage(2 * p + 2, sy_ref, 2 * p + 1, sx_ref)
            stage(2 * p + 3, sx_ref, 2 * p + 2, sy_ref)
            return carry

        lax.fori_loop(0, (qi - 1) // 2, pair, 0)
        diag_in_sx = (qi == 0) | (qi % 2 == 1)

        @pl.when(diag_in_sx)
        def _():
            stage(None, None, qi, sx_ref, diag=True)

        @pl.when(jnp.logical_not(diag_in_sx))
        def _():
            stage(qi, sy_ref, qi - 1, sx_ref)
            stage(None, None, qi, sy_ref, diag=True)

        lv = lmb_ref[...]
        lam = (jnp.exp(jnp.sum(lv[0:1] * lv[1:2], axis=1, keepdims=True))
               - jnp.exp(jnp.sum(lv[2:3] * lv[3:4], axis=1, keepdims=True)) + lam_init)
        for hh in range(nheads):
            o = (acc_ref[2 * hh, :, :DA_V_DIM] / acc_ref[2 * hh, :, DA_V_DIM:]
                 - lam * (acc_ref[2 * hh + 1, :, :DA_V_DIM] / acc_ref[2 * hh + 1, :, DA_V_DIM:]))
            o = o * lax.rsqrt(jnp.mean(o * o, axis=-1, keepdims=True) + EPS) * hn_ref[...]
            o_ref[rows, hh * LANES:(hh + 1) * LANES] = (o * (1.0 - lam_init)).astype(BF16)

        if sub + 1 < nsub:
            mask_maps(q_ref, (sub + 1) * tq)
        else:
            mask_maps(qn_ref, 0)
        for ci in range(2 * nheads):
            scores(ci, 0, sy_ref)
        for ci in range(2 * nheads):
            scores(ci, min(1, nkb - 1), sx_ref)
            update(ci, 0, sy_ref, first=True)

    for sub in range(nsub):
        query_block(sub)


def _diff_attention(z3, lmb, head_norm, lam_init):
    nb, seqlen, _ = z3.shape
    tq = min(TILE_ATTN, seqlen)
    nq = seqlen // tq
    nsub = ATTN_BLOCKS_PER_STEP if nq % ATTN_BLOCKS_PER_STEP == 0 else 1
    hps = ATTN_HEADS_PER_STEP
    width = hps * LANES
    qcol, kcol, vcol = (c // width for c in (D_MODEL, 2 * D_MODEL, 3 * D_MODEL))
    return pl.pallas_call(
        functools.partial(_dattn_kernel, lam_init=lam_init, tq=tq),
        grid=(nb, DA_HEADS // hps, nq // nsub),
        in_specs=[
            pl.BlockSpec((None, nsub * tq, width), lambda b, h, i: (b, i, qcol + h)),
            pl.BlockSpec((None, tq, width), lambda b, h, i: (b, jnp.minimum((i + 1) * nsub, nq - 1), qcol + h)),
            pl.BlockSpec((None, seqlen, width), lambda b, h, i: (b, 0, kcol + h)),
            pl.BlockSpec((None, seqlen, width), lambda b, h, i: (b, 0, vcol + h)),
            pl.BlockSpec((4, DA_HEAD_DIM), lambda b, h, i: (0, 0)),
            pl.BlockSpec((1, DA_V_DIM), lambda b, h, i: (0, 0)),
        ],
        out_specs=pl.BlockSpec((None, nsub * tq, width), lambda b, h, i: (b, i, h)),
        out_shape=jax.ShapeDtypeStruct((nb, seqlen, D_MODEL), BF16),
        scratch_shapes=[pltpu.VMEM((2 * hps, tq, LANES), BF16),
                        pltpu.VMEM((hps, seqlen, 2 * DA_V_DIM), BF16),
                        pltpu.VMEM((2 * hps, tq, tq), F32), pltpu.VMEM((2 * hps, tq, tq), F32),
                        pltpu.VMEM((2 * hps, tq, LANES), F32),
                        pltpu.VMEM((2 * hps, tq, 2 * DA_V_DIM), F32)],
        compiler_params=_params("parallel", "parallel", "arbitrary"),
        name="diff_attention",
    )(z3, z3, z3, z3, lmb, head_norm)


def _merge_kernel(ys_ref, ya_ref, g_ref, x_ref, wa_ref, wb_ref, wo_ref, n_ref, o_ref):
    a = _wdot(ys_ref[...], wa_ref[...])
    b = _wdot(ya_ref[...], wb_ref[...])
    ga = g_ref[:, :D_MODEL].astype(F32)
    gb = g_ref[:, D_MODEL:].astype(F32)
    merged = (ga * a + gb * b).astype(BF16)
    o_ref[...] = x_ref[...] + _rms(_wdot(merged, wo_ref[...]), n_ref[...])


def _merge(ys2, ya2, z2, x2, w_a, w_b, w_o, gain):
    t = x2.shape[0]
    tm = min(TILE_MERGE, t)
    row = lambda i: (i, 0)
    const = lambda i: (0, 0)
    wspec = pl.BlockSpec((D_MODEL, D_MODEL), const, pipeline_mode=pl.Buffered(1))
    return pl.pallas_call(
        _merge_kernel,
        grid=(t // tm,),
        in_specs=[
            pl.BlockSpec((tm, D_MODEL), row),
            pl.BlockSpec((tm, D_MODEL), row),
            pl.BlockSpec((tm, 2 * D_MODEL), lambda i: (i, 2)),
            pl.BlockSpec((tm, D_MODEL), row),
            wspec, wspec, wspec,
            pl.BlockSpec((1, D_MODEL), const),
        ],
        out_specs=pl.BlockSpec((tm, D_MODEL), row),
        out_shape=jax.ShapeDtypeStruct((t, D_MODEL), F32),
        compiler_params=_params("parallel"),
        name="merge",
    )(ys2, ya2, z2, x2, w_a, w_b, w_o, gain)


def _xattn_kernel(x_ref, mem_ref, nmem_ref, wkv_ref, npre_ref, wq_ref, wo_ref, npost_ref, o_ref,
                  kv_ref, oh_ref):
    @pl.when(pl.program_id(1) == 0)
    def _():
        kv_ref[...] = _wdot(_rms(mem_ref[...], nmem_ref[...]).astype(BF16), wkv_ref[...]).astype(BF16)

    x = x_ref[...]
    h = _rms(x, npre_ref[...]).astype(BF16)
    q = (_wdot(h, wq_ref[...]) * (XA_HEAD_DIM ** -0.5 * LOG2E)).astype(BF16)
    nt = (((1,), (1,)), ((), ()))
    for hd in range(XA_HEADS):
        lo = hd * XA_HEAD_DIM
        k = kv_ref[:, lo:lo + XA_HEAD_DIM]
        v = kv_ref[:, D_MODEL + lo:D_MODEL + lo + XA_HEAD_DIM]
        s = lax.dot_general(q[:, lo:lo + XA_HEAD_DIM], k, nt, preferred_element_type=F32)
        p = jnp.exp2(s - jnp.max(s, axis=1, keepdims=True))
        o = _bdot(p.astype(BF16), v) / jnp.sum(p, axis=1, keepdims=True)
        oh_ref[:, lo:lo + XA_HEAD_DIM] = o.astype(BF16)
    o_ref[...] = x + _rms(_wdot(oh_ref[...], wo_ref[...]), npost_ref[...])


def _cross_attention(x3, mem, n_mem, w_xkv, n_pre, w_xq, w_xo, n_post):
    nb, seqlen, _ = x3.shape
    mlen = mem.shape[1]
    tm = min(TILE_XATTN, seqlen)
    const = lambda b, i: (0, 0)
    single = pl.Buffered(1)
    wspec = pl.BlockSpec((D_MODEL, D_MODEL), const, pipeline_mode=single)
    vspec = pl.BlockSpec((1, D_MODEL), const)
    return pl.pallas_call(
        _xattn_kernel,
        grid=(nb, seqlen // tm),
        in_specs=[
            pl.BlockSpec((None, tm, D_MODEL), lambda b, i: (b, i, 0)),
            pl.BlockSpec((None, mlen, D_MODEL), lambda b, i: (b, 0, 0)),
            vspec,
            pl.BlockSpec((D_MODEL, 2 * D_MODEL), const, pipeline_mode=single),
            vspec, wspec, wspec, vspec,
        ],
        out_specs=pl.BlockSpec((None, tm, D_MODEL), lambda b, i: (b, i, 0)),
        out_shape=jax.ShapeDtypeStruct((nb, seqlen, D_MODEL), F32),
        scratch_shapes=[pltpu.VMEM((mlen, 2 * D_MODEL), BF16), pltpu.VMEM((tm, D_MODEL), BF16)],
        compiler_params=_params("parallel", "arbitrary"),
        name="cross_attention",
    )(x3, mem, n_mem, w_xkv, n_pre, w_xq, w_xo, n_post)


def _mlp_kernel(x_ref, npre_ref, w1_ref, w2_ref, npost_ref, o_ref):
    x = x_ref[...]
    h = _rms(x, npre_ref[...]).astype(BF16)
    f = jnp.zeros(x.shape, F32)
    for c in range(D_FF // D_MODEL):
        lo = c * D_MODEL
        a = jnp.maximum(_wdot(h, w1_ref[:, lo:lo + D_MODEL]), 0.0)
        f = f + _wdot((a * a).astype(BF16), w2_ref[lo:lo + D_MODEL, :])
    o_ref[...] = x + _rms(f, npost_ref[...])


def _mlp(x2, n_pre, w1, w2, n_post):
    t = x2.shape[0]
    tm = min(TILE_MLP, t)
    row = lambda i: (i, 0)
    const = lambda i: (0, 0)
    vspec = pl.BlockSpec((1, D_MODEL), const)
    return pl.pallas_call(
        _mlp_kernel,
        grid=(t // tm,),
        in_specs=[
            pl.BlockSpec((tm, D_MODEL), row),
            vspec,
            pl.BlockSpec((D_MODEL, D_FF), const, pipeline_mode=pl.Buffered(1)),
            pl.BlockSpec((D_FF, D_MODEL), const, pipeline_mode=pl.Buffered(1)),
            vspec,
        ],
        out_specs=pl.BlockSpec((tm, D_MODEL), row),
        out_shape=jax.ShapeDtypeStruct((t, D_MODEL), F32),
        compiler_params=_params("parallel"),
        name="mlp",
    )(x2, n_pre, w1, w2, n_post)


def _rope_tables(seqlen):
    inv = ROPE_THETA ** (-jnp.arange(0, DA_HEAD_DIM, 2, dtype=F32) / DA_HEAD_DIM)
    ang = jnp.arange(seqlen, dtype=F32)[:, None] * inv[None, :]
    cos = jnp.tile(jnp.cos(ang), (1, 4))
    sin = jnp.tile(jnp.concatenate([-jnp.sin(ang), jnp.sin(ang)], axis=1), (1, 2))
    return cos, sin


def kernel(x, mem, norm_mix_pre, w_in, b_gate, ssm_lambda_re, ssm_lambda_im, ssm_log_dt, ssm_b_re, ssm_b_im, ssm_c_re, ssm_c_im, ssm_d, w_glu, b_glu, w_ssm_proj, da_lambda_q1, da_lambda_k1, da_lambda_q2, da_lambda_k2, da_head_norm, w_da_proj, w_mix_out, norm_mix_post, norm_x_pre, norm_mem, w_xq, w_xkv, w_xo, norm_x_post, norm_ff_pre, w_ff1, w_ff2, norm_ff_post):
    nb, seqlen, _ = x.shape
    t = nb * seqlen
    depth = w_in.shape[0]
    tabs = _rope_tables(seqlen)
    x2 = x.reshape(t, D_MODEL)
    for l in range(depth):
        lam_init = 0.8 - 0.6 * math.exp(-0.3 * l)
        z2 = _in_projection(x2, norm_mix_pre[l][None], w_in[l], b_gate[l][None], tabs, seqlen)
        z3 = z2.reshape(nb, seqlen, IN_COLS)
        bmat, cmat, lre, lim = _s5_matrices(ssm_lambda_re[l], ssm_lambda_im[l], ssm_log_dt[l],
                                            ssm_b_re[l], ssm_b_im[l], ssm_c_re[l], ssm_c_im[l])
        ys = _s5_branch(z3, bmat, cmat, lre, lim, ssm_d[l][None], w_glu[l], b_glu[l][None])
        lmb = jnp.stack([da_lambda_q1[l], da_lambda_k1[l], da_lambda_q2[l], da_lambda_k2[l]])
        ya = _diff_attention(z3, lmb, da_head_norm[l][None], lam_init)
        x2 = _merge(ys.reshape(t, D_MODEL), ya.reshape(t, D_MODEL), z2, x2,
                    w_ssm_proj[l], w_da_proj[l], w_mix_out[l],
                    norm_mix_post[l][None])
        x3 = _cross_attention(x2.reshape(nb, seqlen, D_MODEL), mem, norm_mem[l][None], w_xkv[l],
                              norm_x_pre[l][None], w_xq[l], w_xo[l],
                              norm_x_post[l][None])
        x2 = _mlp(x3.reshape(t, D_MODEL), norm_ff_pre[l][None], w_ff1[l],
                  w_ff2[l], norm_ff_post[l][None])
    return x2.reshape(nb, seqlen, D_MODEL)
```

```python
import functools
import math

import jax
import jax.numpy as jnp
from jax import lax
from jax.experimental import pallas as pl
from jax.experimental.pallas import tpu as pltpu

F32 = jnp.float32
BF16 = jnp.bfloat16

LANES = 128
TILE_IN_PROJ = 512
TILE_S5 = 512
TILE_ATTN = 512
TILE_MERGE = 1024
TILE_XATTN = 1024
TILE_MLP = 512

D_MODEL = 1024
EPS = 1e-6
CHUNK = 64
SSM_GROUP = 16
SSM_GROUPS = 64
SSM_STATE = 64
SSM_SLABS = 8
SLAB_GROUPS = SSM_GROUPS // SSM_SLABS
SLAB_STATES = SLAB_GROUPS * SSM_STATE
NLB = 2 * SLAB_STATES // LANES
SLAB_ROW_PAD = 8
DA_HEADS = 8
DA_HEAD_DIM = 64
DA_V_DIM = 128
ROPE_THETA = 10000.0
ATTN_HEADS_PER_STEP = 2
ATTN_BLOCKS_PER_STEP = 2
XA_HEADS = 4
XA_HEAD_DIM = 256
D_FF = 4096
IN_COLS = 6144
LOG2E = 1.4426950408889634
NEG_BIG = -1e30
VMEM_LIMIT_BYTES = 56 * 1024 * 1024


def _params(*semantics):
    return pltpu.CompilerParams(dimension_semantics=semantics, vmem_limit_bytes=VMEM_LIMIT_BYTES)


def _rms(x, g):
    return x * lax.rsqrt(jnp.mean(x * x, axis=-1, keepdims=True) + EPS) * g


def _bdot(a, b):
    return jnp.dot(a, b, preferred_element_type=F32)


def _wdot(a, w):
    return _bdot(a, w.astype(BF16))


def _sigmoid(x):
    return 0.5 * jnp.tanh(0.5 * x) + 0.5


def _inproj_kernel(x_ref, g_ref, w_ref, bg_ref, cos_ref, sin_ref, o_ref):
    tm = x_ref.shape[0]
    hn = _rms(x_ref[...], g_ref[...]).astype(BF16)
    lane = lax.broadcasted_iota(jnp.int32, (tm, LANES), 1)
    half = DA_HEAD_DIM // 2
    first_half = (lane % DA_HEAD_DIM) < half

    def rope(acc, col0, scale):
        c = cos_ref[...] * scale
        s = sin_ref[...] * scale
        for h in range(DA_HEADS):
            blk = acc[:, h * LANES:(h + 1) * LANES]
            partner = jnp.where(first_half, pltpu.roll(blk, LANES - half, 1), pltpu.roll(blk, half, 1))
            o_ref[:, col0 + h * LANES:col0 + (h + 1) * LANES] = (blk * c + partner * s).astype(BF16)

    for j in range(IN_COLS // D_MODEL):
        col0 = j * D_MODEL
        acc = _wdot(hn, w_ref[:, col0:col0 + D_MODEL])
        if j == 1:
            rope(acc, col0, DA_HEAD_DIM ** -0.5 * LOG2E)
        elif j == 2:
            rope(acc, col0, 1.0)
        elif j >= 4:
            gate = _sigmoid(acc + bg_ref[:, col0 - 4 * D_MODEL:col0 - 3 * D_MODEL])
            o_ref[:, col0:col0 + D_MODEL] = gate.astype(BF16)
        else:
            o_ref[:, col0:col0 + D_MODEL] = acc.astype(BF16)


def _in_projection(x2, gain, w_in, b_gate, tabs, seqlen):
    t = x2.shape[0]
    tm = min(TILE_IN_PROJ, seqlen)
    nl = seqlen // tm
    const = lambda i: (0, 0)
    tab_spec = pl.BlockSpec((tm, LANES), lambda i: (i % nl, 0))
    return pl.pallas_call(
        _inproj_kernel,
        grid=(t // tm,),
        in_specs=[
            pl.BlockSpec((tm, D_MODEL), lambda i: (i, 0)),
            pl.BlockSpec((1, D_MODEL), const),
            pl.BlockSpec((D_MODEL, IN_COLS), const, pipeline_mode=pl.Buffered(1)),
            pl.BlockSpec((1, 2 * D_MODEL), const),
            tab_spec, tab_spec,
        ],
        out_specs=pl.BlockSpec((tm, IN_COLS), lambda i: (i, 0)),
        out_shape=jax.ShapeDtypeStruct((t, IN_COLS), BF16),
        compiler_params=_params("parallel"),
        name="in_projection",
    )(x2, gain, w_in, b_gate, *tabs)


def _s5_kernel(u_ref, bm_ref, cm_ref, lre_ref, lim_ref, d_ref, wg_ref, bgl_ref, o_ref,
               bu_ref, st_ref, wc_ref, il_ref, ue_ref, uo_ref, yo_ref, w_ref, k0_ref):
    nb, tc, _ = u_ref.shape
    tp = tc // 2
    pitch = bu_ref.shape[2] // SSM_SLABS
    c = pl.program_id(0)

    @pl.when(c == 0)
    def _():
        st_ref[...] = jnp.zeros(st_ref.shape, F32)
        wc_ref[...] = jnp.zeros(wc_ref.shape, F32)

    for b in range(nb):
        for j in range(SSM_SLABS):
            il_ref[j] = u_ref[b, :, j * LANES:(j + 1) * LANES].astype(F32)
        for j in range(SSM_SLABS):
            u_e = il_ref[j, pl.ds(0, tp, stride=2), :]
            u_o = il_ref[j, pl.ds(1, tp, stride=2), :]
            ue_ref[b * tp:(b + 1) * tp, j * LANES:(j + 1) * LANES] = u_e
            uo_ref[b * tp:(b + 1) * tp, j * LANES:(j + 1) * LANES] = u_o
            lhs = jnp.concatenate([u_e, u_o], axis=1).astype(BF16)
            bu = _bdot(lhs, bm_ref[j])
            for k in range(NLB):
                bu_ref[b, k, j * pitch:j * pitch + tp, :] = bu[:, k * LANES:(k + 1) * LANES]
            k0_ref[b * tp:(b + 1) * tp, j * LANES:(j + 1) * LANES] = bu[:, NLB * LANES:]

    ar = [lre_ref[:, k * LANES:(k + 1) * LANES] for k in range(NLB // 2)]
    ai = [lim_ref[:, k * LANES:(k + 1) * LANES] for k in range(NLB // 2)]

    def step(t, carry):
        out = []
        for b in range(nb):
            xs = carry[b]
            new = [None] * NLB
            for k in range(NLB // 2):
                xr, xi = xs[k], xs[k + NLB // 2]
                rows = pl.ds(t, SSM_SLABS, stride=pitch)
                new[k] = ar[k] * xr - ai[k] * xi + bu_ref[b, k, rows, :]
                new[k + NLB // 2] = ar[k] * xi + ai[k] * xr + bu_ref[b, k + NLB // 2, rows, :]
                bu_ref[b, k, rows, :] = new[k]
                bu_ref[b, k + NLB // 2, rows, :] = new[k + NLB // 2]
            out.append(tuple(new))
        return tuple(out)

    init = tuple(tuple(st_ref[b, k] for k in range(NLB)) for b in range(nb))
    fin = lax.fori_loop(0, tp, step, init, unroll=8)
    for b in range(nb):
        for k in range(NLB):
            st_ref[b, k] = fin[b][k]

    for b in range(nb):
        for j in range(SSM_SLABS):
            xs = jnp.concatenate([bu_ref[b, k, j * pitch:j * pitch + tp, :] for k in range(NLB)], axis=1)
            r = _bdot(xs.astype(BF16), cm_ref[j])
            yo_ref[b * tp:(b + 1) * tp, j * LANES:(j + 1) * LANES] = r[:, :LANES]
            w_ref[b * tp:(b + 1) * tp, j * LANES:(j + 1) * LANES] = r[:, LANES:]

    def glu(y):
        ys = 0.5 * y * (1.0 + jnp.tanh(math.sqrt(2.0 / math.pi) * (y + 0.044715 * (y * y * y))))
        gate = _sigmoid(_wdot(ys.astype(BF16), wg_ref[...]) + bgl_ref[...])
        return ys * gate

    first_row = lax.broadcasted_iota(jnp.int32, (tp, D_MODEL), 0) == 0
    shifted = []
    for b in range(nb):
        w = w_ref[b * tp:(b + 1) * tp, :]
        shifted.append(jnp.where(first_row, wc_ref[b], pltpu.roll(w, 1, 0)))
        wc_ref[b] = w[tp - 1:tp, :]
    yo_ref[...] = glu(yo_ref[...] + d_ref[...] * uo_ref[...])
    w_ref[...] = glu(jnp.concatenate(shifted, axis=0) + k0_ref[...] + d_ref[...] * ue_ref[...])
    for b in range(nb):
        for j in range(SSM_SLABS):
            il_ref[j, pl.ds(0, tp, stride=2), :] = w_ref[b * tp:(b + 1) * tp, j * LANES:(j + 1) * LANES]
            il_ref[j, pl.ds(1, tp, stride=2), :] = yo_ref[b * tp:(b + 1) * tp, j * LANES:(j + 1) * LANES]
        for j in range(SSM_SLABS):
            o_ref[b, :, j * LANES:(j + 1) * LANES] = il_ref[j].astype(BF16)


def _s5_branch(z3, bmat, cmat, lam_re, lam_im, d_skip, w_glu, b_glu):
    nb, seqlen, _ = z3.shape
    tc = min(TILE_S5, seqlen)
    tp = tc // 2
    const2 = lambda c: (0, 0)
    const3 = lambda c: (0, 0, 0)
    single = pl.Buffered(1)
    return pl.pallas_call(
        _s5_kernel,
        grid=(seqlen // tc,),
        in_specs=[
            pl.BlockSpec((nb, tc, D_MODEL), lambda c: (0, c, 0)),
            pl.BlockSpec(bmat.shape, const3, pipeline_mode=single),
            pl.BlockSpec(cmat.shape, const3, pipeline_mode=single),
            pl.BlockSpec(lam_re.shape, const2),
            pl.BlockSpec(lam_im.shape, const2),
            pl.BlockSpec((1, D_MODEL), const2),
            pl.BlockSpec((D_MODEL, D_MODEL), const2, pipeline_mode=single),
            pl.BlockSpec((1, D_MODEL), const2),
        ],
        out_specs=pl.BlockSpec((nb, tc, D_MODEL), lambda c: (0, c, 0)),
        out_shape=jax.ShapeDtypeStruct((nb, seqlen, D_MODEL), BF16),
        scratch_shapes=[
            pltpu.VMEM((nb, NLB, SSM_SLABS * (tp + SLAB_ROW_PAD), LANES), F32),
            pltpu.VMEM((nb, NLB, SSM_SLABS, LANES), F32),
            pltpu.VMEM((nb, 1, D_MODEL), F32),
            pltpu.VMEM((SSM_SLABS, tc, LANES), F32),
        ] + [pltpu.VMEM((nb * tp, D_MODEL), F32)] * 5,
        compiler_params=_params("arbitrary"),
        name="s5_branch",
    )(z3, bmat, cmat, lam_re, lam_im, d_skip, w_glu, b_glu)


def _s5_matrices(lam_re, lam_im, log_dt, b_re, b_im, c_re, c_im):
    dt = jnp.exp(log_dt)[:, None]
    mag = jnp.exp(lam_re * dt)
    lb_re, lb_im = mag * jnp.cos(lam_im * dt), mag * jnp.sin(lam_im * dt)
    den = lam_re * lam_re + lam_im * lam_im
    cf_re = ((lb_re - 1.0) * lam_re + lb_im * lam_im) / den
    cf_im = (lb_im * lam_re - (lb_re - 1.0) * lam_im) / den
    eye = jnp.eye(SLAB_GROUPS, dtype=F32)[None, :, None, :, None]
    by_slab = (SSM_SLABS, SLAB_GROUPS)

    def col_factor(x):
        return x.reshape(by_slab + (SSM_STATE,))[:, None, None, :, :]

    def row_factor(x):
        return x.reshape(by_slab + (SSM_STATE,))[:, :, :, None, None]

    def in_place(x):
        return x.reshape(by_slab + (SSM_STATE, SSM_GROUP)).transpose(0, 1, 3, 2)[:, :, :, None, :] * eye

    def out_place(x):
        return x.reshape(by_slab + (SSM_GROUP, SSM_STATE)).transpose(0, 3, 1, 2)[:, None] * eye

    def in_flat(x):
        return x.reshape(SSM_SLABS, LANES, SLAB_STATES)

    def out_flat(x):
        return x.reshape(SSM_SLABS, SLAB_STATES, LANES)

    br, bi = in_place(b_re), in_place(b_im)
    bo_re = col_factor(cf_re) * br - col_factor(cf_im) * bi
    bo_im = col_factor(cf_re) * bi + col_factor(cf_im) * br
    be_re = col_factor(lb_re) * bo_re - col_factor(lb_im) * bo_im
    be_im = col_factor(lb_re) * bo_im + col_factor(lb_im) * bo_re
    cr, ci = out_place(c_re), out_place(c_im)
    cn_re = row_factor(lb_re) * cr - row_factor(lb_im) * ci
    cn_im = row_factor(lb_re) * ci + row_factor(lb_im) * cr
    b_odd = jnp.concatenate([in_flat(bo_re), in_flat(bo_im)], axis=2)
    cur = jnp.concatenate([out_flat(cr), out_flat(-ci)], axis=1)
    k0 = jnp.einsum('jrs,jsc->jrc', b_odd, cur, precision=lax.Precision.HIGHEST)
    top = jnp.concatenate([in_flat(be_re), in_flat(be_im), k0], axis=2)
    bot = jnp.concatenate([b_odd, jnp.zeros_like(k0)], axis=2)
    bmat = jnp.concatenate([top, bot], axis=1).astype(BF16)
    nxt = jnp.concatenate([out_flat(cn_re), out_flat(-cn_im)], axis=1)
    cmat = jnp.concatenate([cur, nxt], axis=2).astype(BF16)
    lre = (lb_re * lb_re - lb_im * lb_im).reshape(SSM_SLABS, SLAB_STATES)
    lim = (2.0 * lb_re * lb_im).reshape(SSM_SLABS, SLAB_STATES)
    return bmat, cmat, lre, lim


def _dattn_kernel(q_ref, qn_ref, k_ref, v_ref, lmb_ref, hn_ref, o_ref,
                  qm_ref, vx_ref, sx_ref, sy_ref, m_ref, acc_ref, *, lam_init, tq):
    tk = tq
    nsub = q_ref.shape[0] // tq
    nstrips = tk // LANES
    nheads = q_ref.shape[1] // LANES
    nkb = k_ref.shape[0] // tk
    lane = lax.broadcasted_iota(jnp.int32, (tq, LANES), 1)
    nt = (((1,), (1,)), ((), ()))

    @pl.when(pl.program_id(2) == 0)
    def _():
        for hh in range(nheads):
            vx_ref[hh, :, :DA_V_DIM] = v_ref[:, hh * LANES:(hh + 1) * LANES]
            vx_ref[hh, :, DA_V_DIM:] = jnp.ones((v_ref.shape[0], DA_V_DIM), BF16)

    def mask_maps(src_ref, row0):
        for hh in range(nheads):
            q = src_ref[row0:row0 + tq, hh * LANES:(hh + 1) * LANES]
            zero = jnp.zeros_like(q)
            qm_ref[2 * hh] = jnp.where(lane < DA_HEAD_DIM, q, zero)
            qm_ref[2 * hh + 1] = jnp.where(lane >= DA_HEAD_DIM, q, zero)

    def scores(ci, ki, s_ref):
        start = pl.multiple_of(ki * tk, tk)
        hh = ci // 2
        kb = k_ref[pl.ds(start, tk), hh * LANES:(hh + 1) * LANES]
        s = lax.dot_general(qm_ref[ci], kb, nt, preferred_element_type=F32)
        s_ref[ci, :, :tk] = s
        s_ref[ci, :, tk:] = functools.reduce(jnp.maximum, [s[:, c * LANES:(c + 1) * LANES] for c in range(nstrips)])

    def update(ci, ki, s_ref, first=False):
        start = pl.multiple_of(ki * tk, tk)
        hh = ci // 2
        vb = vx_ref[hh, pl.ds(start, tk), :]
        m_next = jnp.max(s_ref[ci, :, tk:], axis=1, keepdims=True)
        if first:
            m_next = jnp.broadcast_to(m_next, (tq, LANES))
        else:
            m_prev = m_ref[ci]
            m_next = jnp.maximum(m_prev, m_next)
            alpha = jnp.exp2(m_prev - m_next)
        ps = [jnp.exp2(s_ref[ci, :, c * LANES:(c + 1) * LANES] - m_next) for c in range(nstrips)]
        pv = _bdot(jnp.concatenate(ps, axis=1).astype(BF16), vb)
        acc_ref[ci] = pv if first else jnp.concatenate([alpha, alpha], axis=1) * acc_ref[ci] + pv
        m_ref[ci] = m_next

    def update_diag(ci, ki, s_ref):
        half = tq // 2
        start = pl.multiple_of(ki * tk, tk)
        hh = ci // 2
        row = lax.broadcasted_iota(jnp.int32, (half, half), 0) // CHUNK
        col = lax.broadcasted_iota(jnp.int32, (half, half), 1) // CHUNK
        visible = col <= row
        for part in range(2):
            rows = slice(part * half, (part + 1) * half)
            ncols = half * (part + 1)
            strips = []
            for c in range(ncols // LANES):
                s = s_ref[ci, rows, c * LANES:(c + 1) * LANES]
                off = c * LANES - part * half
                if off >= 0:
                    s = jnp.where(visible[:, off:off + LANES], s, NEG_BIG)
                strips.append(s)
            m_prev = m_ref[ci, rows, :]
            m_cur = functools.reduce(jnp.maximum, strips)
            m_next = jnp.maximum(m_prev, jnp.max(m_cur, axis=1, keepdims=True))
            alpha = jnp.exp2(m_prev - m_next)
            p = jnp.concatenate([jnp.exp2(s - m_next) for s in strips], axis=1).astype(BF16)
            pv = _bdot(p, vx_ref[hh, pl.ds(start, ncols), :])
            acc_ref[ci, rows, :] = jnp.concatenate([alpha, alpha], axis=1) * acc_ref[ci, rows, :] + pv
            m_ref[ci, rows, :] = m_next

    def stage(k_next, s_next_ref, k_cur, s_cur_ref, diag=False):
        for ci in range(2 * nheads):
            if k_next is not None:
                scores(ci, k_next, s_next_ref)
            if diag:
                update_diag(ci, k_cur, s_cur_ref)
            else:
                update(ci, k_cur, s_cur_ref)

    def query_block(sub):
        qi = pl.program_id(2) * nsub + sub
        rows = slice(sub * tq, (sub + 1) * tq)

        @pl.when(qi == 0)
        def _():
            mask_maps(q_ref, 0)
            m_ref[...] = jnp.full(m_ref.shape, NEG_BIG, F32)
            acc_ref[...] = jnp.zeros(acc_ref.shape, F32)
            for ci in range(2 * nheads):
                scores(ci, 0, sx_ref)

        def pair(p, carry):
            stage(2 * p + 2, sy_ref, 2 * p + 1, sx_ref)
            stage(2 * p + 3, sx_ref, 2 * p + 2, sy_ref)
            return carry

        lax.fori_loop(0, (qi - 1) // 2, pair, 0)
        diag_in_sx = (qi == 0) | (qi % 2 == 1)

        @pl.when(diag_in_sx)
        def _():
            stage(None, None, qi, sx_ref, diag=True)

        @pl.when(jnp.logical_not(diag_in_sx))
        def _():
            stage(qi, sy_ref, qi - 1, sx_ref)
            stage(None, None, qi, sy_ref, diag=True)

        lv = lmb_ref[...]
        lam = (jnp.exp(jnp.sum(lv[0:1] * lv[1:2], axis=1, keepdims=True))
               - jnp.exp(jnp.sum(lv[2:3] * lv[3:4], axis=1, keepdims=True)) + lam_init)
        for hh in range(nheads):
            o = (acc_ref[2 * hh, :, :DA_V_DIM] / acc_ref[2 * hh, :, DA_V_DIM:]
                 - lam * (acc_ref[2 * hh + 1, :, :DA_V_DIM] / acc_ref[2 * hh + 1, :, DA_V_DIM:]))
            o = o * lax.rsqrt(jnp.mean(o * o, axis=-1, keepdims=True) + EPS) * hn_ref[...]
            o_ref[rows, hh * LANES:(hh + 1) * LANES] = (o * (1.0 - lam_init)).astype(BF16)

        if sub + 1 < nsub:
            mask_maps(q_ref, (sub + 1) * tq)
        else:
            mask_maps(qn_ref, 0)
        for ci in range(2 * nheads):
            scores(ci, 0, sy_ref)
        for ci in range(2 * nheads):
            scores(ci, min(1, nkb - 1), sx_ref)
            update(ci, 0, sy_ref, first=True)

    for sub in range(nsub):
        query_block(sub)


def _diff_attention(z3, lmb, head_norm, lam_init):
    nb, seqlen, _ = z3.shape
    tq = min(TILE_ATTN, seqlen)
    nq = seqlen // tq
    nsub = ATTN_BLOCKS_PER_STEP if nq % ATTN_BLOCKS_PER_STEP == 0 else 1
    hps = ATTN_HEADS_PER_STEP
    width = hps * LANES
    qcol, kcol, vcol = (c // width for c in (D_MODEL, 2 * D_MODEL, 3 * D_MODEL))
    return pl.pallas_call(
        functools.partial(_dattn_kernel, lam_init=lam_init, tq=tq),
        grid=(nb, DA_HEADS // hps, nq // nsub),
        in_specs=[
            pl.BlockSpec((None, nsub * tq, width), lambda b, h, i: (b, i, qcol + h)),
            pl.BlockSpec((None, tq, width), lambda b, h, i: (b, jnp.minimum((i + 1) * nsub, nq - 1), qcol + h)),
            pl.BlockSpec((None, seqlen, width), lambda b, h, i: (b, 0, kcol + h)),
            pl.BlockSpec((None, seqlen, width), lambda b, h, i: (b, 0, vcol + h)),
            pl.BlockSpec((4, DA_HEAD_DIM), lambda b, h, i: (0, 0)),
            pl.BlockSpec((1, DA_V_DIM), lambda b, h, i: (0, 0)),
        ],
        out_specs=pl.BlockSpec((None, nsub * tq, width), lambda b, h, i: (b, i, h)),
        out_shape=jax.ShapeDtypeStruct((nb, seqlen, D_MODEL), BF16),
        scratch_shapes=[pltpu.VMEM((2 * hps, tq, LANES), BF16),
                        pltpu.VMEM((hps, seqlen, 2 * DA_V_DIM), BF16),
                        pltpu.VMEM((2 * hps, tq, tq + LANES), F32), pltpu.VMEM((2 * hps, tq, tq + LANES), F32),
                        pltpu.VMEM((2 * hps, tq, LANES), F32),
                        pltpu.VMEM((2 * hps, tq, 2 * DA_V_DIM), F32)],
        compiler_params=_params("parallel", "parallel", "arbitrary"),
        name="diff_attention",
    )(z3, z3, z3, z3, lmb, head_norm)


def _merge_kernel(ys_ref, ya_ref, g_ref, x_ref, wa_ref, wb_ref, wo_ref, n_ref, o_ref):
    a = _wdot(ys_ref[...], wa_ref[...])
    b = _wdot(ya_ref[...], wb_ref[...])
    ga = g_ref[:, :D_MODEL].astype(F32)
    gb = g_ref[:, D_MODEL:].astype(F32)
    merged = (ga * a + gb * b).astype(BF16)
    o_ref[...] = x_ref[...] + _rms(_wdot(merged, wo_ref[...]), n_ref[...])


def _merge(ys2, ya2, z2, x2, w_a, w_b, w_o, gain):
    t = x2.shape[0]
    tm = min(TILE_MERGE, t)
    row = lambda i: (i, 0)
    const = lambda i: (0, 0)
    wspec = pl.BlockSpec((D_MODEL, D_MODEL), const, pipeline_mode=pl.Buffered(1))
    return pl.pallas_call(
        _merge_kernel,
        grid=(t // tm,),
        in_specs=[
            pl.BlockSpec((tm, D_MODEL), row),
            pl.BlockSpec((tm, D_MODEL), row),
            pl.BlockSpec((tm, 2 * D_MODEL), lambda i: (i, 2)),
            pl.BlockSpec((tm, D_MODEL), row),
            wspec, wspec, wspec,
            pl.BlockSpec((1, D_MODEL), const),
        ],
        out_specs=pl.BlockSpec((tm, D_MODEL), row),
        out_shape=jax.ShapeDtypeStruct((t, D_MODEL), F32),
        compiler_params=_params("parallel"),
        name="merge",
    )(ys2, ya2, z2, x2, w_a, w_b, w_o, gain)


def _xattn_kernel(x_ref, mem_ref, nmem_ref, wkv_ref, npre_ref, wq_ref, wo_ref, npost_ref, o_ref,
                  kv_ref, oh_ref):
    @pl.when(pl.program_id(1) == 0)
    def _():
        kv_ref[...] = _wdot(_rms(mem_ref[...], nmem_ref[...]).astype(BF16), wkv_ref[...]).astype(BF16)

    x = x_ref[...]
    h = _rms(x, npre_ref[...]).astype(BF16)
    q = (_wdot(h, wq_ref[...]) * (XA_HEAD_DIM ** -0.5 * LOG2E)).astype(BF16)
    nt = (((1,), (1,)), ((), ()))
    for hd in range(XA_HEADS):
        lo = hd * XA_HEAD_DIM
        k = kv_ref[:, lo:lo + XA_HEAD_DIM]
        v = kv_ref[:, D_MODEL + lo:D_MODEL + lo + XA_HEAD_DIM]
        s = lax.dot_general(q[:, lo:lo + XA_HEAD_DIM], k, nt, preferred_element_type=F32)
        p = jnp.exp2(s - jnp.max(s, axis=1, keepdims=True))
        o = _bdot(p.astype(BF16), v) / jnp.sum(p, axis=1, keepdims=True)
        oh_ref[:, lo:lo + XA_HEAD_DIM] = o.astype(BF16)
    o_ref[...] = x + _rms(_wdot(oh_ref[...], wo_ref[...]), npost_ref[...])


def _cross_attention(x3, mem, n_mem, w_xkv, n_pre, w_xq, w_xo, n_post):
    nb, seqlen, _ = x3.shape
    mlen = mem.shape[1]
    tm = min(TILE_XATTN, seqlen)
    const = lambda b, i: (0, 0)
    single = pl.Buffered(1)
    wspec = pl.BlockSpec((D_MODEL, D_MODEL), const, pipeline_mode=single)
    vspec = pl.BlockSpec((1, D_MODEL), const)
    return pl.pallas_call(
        _xattn_kernel,
        grid=(nb, seqlen // tm),
        in_specs=[
            pl.BlockSpec((None, tm, D_MODEL), lambda b, i: (b, i, 0)),
            pl.BlockSpec((None, mlen, D_MODEL), lambda b, i: (b, 0, 0)),
            vspec,
            pl.BlockSpec((D_MODEL, 2 * D_MODEL), const, pipeline_mode=single),
            vspec, wspec, wspec, vspec,
        ],
        out_specs=pl.BlockSpec((None, tm, D_MODEL), lambda b, i: (b, i, 0)),
        out_shape=jax.ShapeDtypeStruct((nb, seqlen, D_MODEL), F32),
        scratch_shapes=[pltpu.VMEM((mlen, 2 * D_MODEL), BF16), pltpu.VMEM((tm, D_MODEL), BF16)],
        compiler_params=_params("parallel", "arbitrary"),
        name="cross_attention",
    )(x3, mem, n_mem, w_xkv, n_pre, w_xq, w_xo, n_post)


def _mlp_kernel(x_ref, npre_ref, w1_ref, w2_ref, npost_ref, o_ref):
    x = x_ref[...]
    h = _rms(x, npre_ref[...]).astype(BF16)
    f = jnp.zeros(x.shape, F32)
    for c in range(D_FF // D_MODEL):
        lo = c * D_MODEL
        a = jnp.maximum(_wdot(h, w1_ref[:, lo:lo + D_MODEL]), 0.0)
        f = f + _wdot((a * a).astype(BF16), w2_ref[lo:lo + D_MODEL, :])
    o_ref[...] = x + _rms(f, npost_ref[...])


def _mlp(x2, n_pre, w1, w2, n_post):
    t = x2.shape[0]
    tm = min(TILE_MLP, t)
    row = lambda i: (i, 0)
    const = lambda i: (0, 0)
    vspec = pl.BlockSpec((1, D_MODEL), const)
    return pl.pallas_call(
        _mlp_kernel,
        grid=(t // tm,),
        in_specs=[
            pl.BlockSpec((tm, D_MODEL), row),
            vspec,
            pl.BlockSpec((D_MODEL, D_FF), const, pipeline_mode=pl.Buffered(1)),
            pl.BlockSpec((D_FF, D_MODEL), const, pipeline_mode=pl.Buffered(1)),
            vspec,
        ],
        out_specs=pl.BlockSpec((tm, D_MODEL), row),
        out_shape=jax.ShapeDtypeStruct((t, D_MODEL), F32),
        compiler_params=_params("parallel"),
        name="mlp",
    )(x2, n_pre, w1, w2, n_post)


def _rope_tables(seqlen):
    inv = ROPE_THETA ** (-jnp.arange(0, DA_HEAD_DIM, 2, dtype=F32) / DA_HEAD_DIM)
    ang = jnp.arange(seqlen, dtype=F32)[:, None] * inv[None, :]
    cos = jnp.tile(jnp.cos(ang), (1, 4))
    sin = jnp.tile(jnp.concatenate([-jnp.sin(ang), jnp.sin(ang)], axis=1), (1, 2))
    return cos, sin


def kernel(x, mem, norm_mix_pre, w_in, b_gate, ssm_lambda_re, ssm_lambda_im, ssm_log_dt, ssm_b_re, ssm_b_im, ssm_c_re, ssm_c_im, ssm_d, w_glu, b_glu, w_ssm_proj, da_lambda_q1, da_lambda_k1, da_lambda_q2, da_lambda_k2, da_head_norm, w_da_proj, w_mix_out, norm_mix_post, norm_x_pre, norm_mem, w_xq, w_xkv, w_xo, norm_x_post, norm_ff_pre, w_ff1, w_ff2, norm_ff_post):
    nb, seqlen, _ = x.shape
    t = nb * seqlen
    depth = w_in.shape[0]
    tabs = _rope_tables(seqlen)
    x2 = x.reshape(t, D_MODEL)
    for l in range(depth):
        lam_init = 0.8 - 0.6 * math.exp(-0.3 * l)
        z2 = _in_projection(x2, norm_mix_pre[l][None], w_in[l], b_gate[l][None], tabs, seqlen)
        z3 = z2.reshape(nb, seqlen, IN_COLS)
        bmat, cmat, lre, lim = _s5_matrices(ssm_lambda_re[l], ssm_lambda_im[l], ssm_log_dt[l],
                                            ssm_b_re[l], ssm_b_im[l], ssm_c_re[l], ssm_c_im[l])
        ys = _s5_branch(z3, bmat, cmat, lre, lim, ssm_d[l][None], w_glu[l], b_glu[l][None])
        lmb = jnp.stack([da_lambda_q1[l], da_lambda_k1[l], da_lambda_q2[l], da_lambda_k2[l]])
        ya = _diff_attention(z3, lmb, da_head_norm[l][None], lam_init)
        x2 = _merge(ys.reshape(t, D_MODEL), ya.reshape(t, D_MODEL), z2, x2,
                    w_ssm_proj[l], w_da_proj[l], w_mix_out[l],
                    norm_mix_post[l][None])
        x3 = _cross_attention(x2.reshape(nb, seqlen, D_MODEL), mem, norm_mem[l][None], w_xkv[l],
                              norm_x_pre[l][None], w_xq[l], w_xo[l],
                              norm_x_post[l][None])
        x2 = _mlp(x3.reshape(t, D_MODEL), norm_ff_pre[l][None], w_ff1[l],
                  w_ff2[l], norm_ff_post[l][None])
    return x2.reshape(nb, seqlen, D_MODEL)
```

```python
import functools
import math

import jax
import jax.numpy as jnp
from jax import lax
from jax.experimental import pallas as pl
from jax.experimental.pallas import tpu as pltpu

F32 = jnp.float32
BF16 = jnp.bfloat16

LANES = 128
TILE_IN_PROJ = 512
TILE_S5 = 512
TILE_ATTN = 512
TILE_MERGE = 1024
TILE_XATTN = 1024
TILE_MLP = 512

D_MODEL = 1024
EPS = 1e-6
CHUNK = 64
SSM_GROUP = 16
SSM_GROUPS = 64
SSM_STATE = 64
SSM_SLABS = 8
SLAB_GROUPS = SSM_GROUPS // SSM_SLABS
SLAB_STATES = SLAB_GROUPS * SSM_STATE
NLB = 2 * SLAB_STATES // LANES
SLAB_ROW_PAD = 8
DA_HEADS = 8
DA_HEAD_DIM = 64
DA_V_DIM = 128
ROPE_THETA = 10000.0
ATTN_HEADS_PER_STEP = 2
ATTN_BLOCKS_PER_STEP = 2
XA_HEADS = 4
XA_HEAD_DIM = 256
D_FF = 4096
IN_COLS = 6144
LOG2E = 1.4426950408889634
NEG_BIG = -1e30
VMEM_LIMIT_BYTES = 56 * 1024 * 1024


def _params(*semantics):
    return pltpu.CompilerParams(dimension_semantics=semantics, vmem_limit_bytes=VMEM_LIMIT_BYTES)


def _rms(x, g):
    return x * lax.rsqrt(jnp.mean(x * x, axis=-1, keepdims=True) + EPS) * g


def _bdot(a, b):
    return jnp.dot(a, b, preferred_element_type=F32)


def _wdot(a, w):
    return _bdot(a, w.astype(BF16))


def _sigmoid(x):
    return 0.5 * jnp.tanh(0.5 * x) + 0.5


def _inproj_kernel(x_ref, g_ref, w_hbm, bg_ref, cos_ref, sin_ref, o_ref, w_ref, sem):
    tm = x_ref.shape[0]
    nblk = IN_COLS // D_MODEL

    def w_copy(j):
        cols = pl.ds(j * D_MODEL, D_MODEL)
        return pltpu.make_async_copy(w_hbm.at[:, cols], w_ref.at[:, cols], sem.at[j])

    @pl.when(pl.program_id(0) == 0)
    def _():
        for j in range(nblk):
            w_copy(j).start()

    hn = _rms(x_ref[...], g_ref[...]).astype(BF16)
    lane = lax.broadcasted_iota(jnp.int32, (tm, LANES), 1)
    half = DA_HEAD_DIM // 2
    first_half = (lane % DA_HEAD_DIM) < half

    def rope(acc, col0, scale):
        c = cos_ref[...] * scale
        s = sin_ref[...] * scale
        for h in range(DA_HEADS):
            blk = acc[:, h * LANES:(h + 1) * LANES]
            partner = jnp.where(first_half, pltpu.roll(blk, LANES - half, 1), pltpu.roll(blk, half, 1))
            o_ref[:, col0 + h * LANES:col0 + (h + 1) * LANES] = (blk * c + partner * s).astype(BF16)

    def project(wait_for_weights):
        for j in range(nblk):
            col0 = j * D_MODEL
            if wait_for_weights:
                w_copy(j).wait()
            acc = _wdot(hn, w_ref[:, col0:col0 + D_MODEL])
            if j == 1:
                rope(acc, col0, DA_HEAD_DIM ** -0.5 * LOG2E)
            elif j == 2:
                rope(acc, col0, 1.0)
            elif j >= 4:
                gate = _sigmoid(acc + bg_ref[:, col0 - 4 * D_MODEL:col0 - 3 * D_MODEL])
                o_ref[:, col0:col0 + D_MODEL] = gate.astype(BF16)
            else:
                o_ref[:, col0:col0 + D_MODEL] = acc.astype(BF16)

    @pl.when(pl.program_id(0) == 0)
    def _():
        project(True)

    @pl.when(pl.program_id(0) > 0)
    def _():
        project(False)


def _in_projection(x2, gain, w_in, b_gate, tabs, seqlen):
    t = x2.shape[0]
    tm = min(TILE_IN_PROJ, seqlen)
    nl = seqlen // tm
    const = lambda i: (0, 0)
    tab_spec = pl.BlockSpec((tm, LANES), lambda i: (i % nl, 0))
    return pl.pallas_call(
        _inproj_kernel,
        grid=(t // tm,),
        in_specs=[
            pl.BlockSpec((tm, D_MODEL), lambda i: (i, 0)),
            pl.BlockSpec((1, D_MODEL), const),
            pl.BlockSpec(memory_space=pl.ANY),
            pl.BlockSpec((1, 2 * D_MODEL), const),
            tab_spec, tab_spec,
        ],
        out_specs=pl.BlockSpec((tm, IN_COLS), lambda i: (i, 0)),
        out_shape=jax.ShapeDtypeStruct((t, IN_COLS), BF16),
        scratch_shapes=[pltpu.VMEM((D_MODEL, IN_COLS), F32),
                        pltpu.SemaphoreType.DMA((IN_COLS // D_MODEL,))],
        compiler_params=_params("arbitrary"),
        name="in_projection",
    )(x2, gain, w_in, b_gate, *tabs)


def _s5_kernel(u_ref, bm_ref, cm_ref, lre_ref, lim_ref, d_ref, wg_ref, bgl_ref, o_ref,
               bu_ref, st_ref, wc_ref, il_ref, ue_ref, uo_ref, yo_ref, w_ref, k0_ref):
    nb, tc, _ = u_ref.shape
    tp = tc // 2
    pitch = bu_ref.shape[2] // SSM_SLABS
    c = pl.program_id(0)

    @pl.when(c == 0)
    def _():
        st_ref[...] = jnp.zeros(st_ref.shape, F32)
        wc_ref[...] = jnp.zeros(wc_ref.shape, F32)

    for b in range(nb):
        for j in range(SSM_SLABS):
            il_ref[j] = u_ref[b, :, j * LANES:(j + 1) * LANES].astype(F32)
        for j in range(SSM_SLABS):
            u_e = il_ref[j, pl.ds(0, tp, stride=2), :]
            u_o = il_ref[j, pl.ds(1, tp, stride=2), :]
            ue_ref[b * tp:(b + 1) * tp, j * LANES:(j + 1) * LANES] = u_e
            uo_ref[b * tp:(b + 1) * tp, j * LANES:(j + 1) * LANES] = u_o
            lhs = jnp.concatenate([u_e, u_o], axis=1).astype(BF16)
            bu = _bdot(lhs, bm_ref[j])
            for k in range(NLB):
                bu_ref[b, k, j * pitch:j * pitch + tp, :] = bu[:, k * LANES:(k + 1) * LANES]
            k0_ref[b * tp:(b + 1) * tp, j * LANES:(j + 1) * LANES] = bu[:, NLB * LANES:]

    ar = [lre_ref[:, k * LANES:(k + 1) * LANES] for k in range(NLB // 2)]
    ai = [lim_ref[:, k * LANES:(k + 1) * LANES] for k in range(NLB // 2)]

    def step(t, carry):
        out = []
        for b in range(nb):
            xs = carry[b]
            new = [None] * NLB
            for k in range(NLB // 2):
                xr, xi = xs[k], xs[k + NLB // 2]
                rows = pl.ds(t, SSM_SLABS, stride=pitch)
                new[k] = ar[k] * xr - ai[k] * xi + bu_ref[b, k, rows, :]
                new[k + NLB // 2] = ar[k] * xi + ai[k] * xr + bu_ref[b, k + NLB // 2, rows, :]
                bu_ref[b, k, rows, :] = new[k]
                bu_ref[b, k + NLB // 2, rows, :] = new[k + NLB // 2]
            out.append(tuple(new))
        return tuple(out)

    init = tuple(tuple(st_ref[b, k] for k in range(NLB)) for b in range(nb))
    fin = lax.fori_loop(0, tp, step, init, unroll=8)
    for b in range(nb):
        for k in range(NLB):
            st_ref[b, k] = fin[b][k]

    for b in range(nb):
        for j in range(SSM_SLABS):
            xs = jnp.concatenate([bu_ref[b, k, j * pitch:j * pitch + tp, :] for k in range(NLB)], axis=1)
            r = _bdot(xs.astype(BF16), cm_ref[j])
            yo_ref[b * tp:(b + 1) * tp, j * LANES:(j + 1) * LANES] = r[:, :LANES]
            w_ref[b * tp:(b + 1) * tp, j * LANES:(j + 1) * LANES] = r[:, LANES:]

    def glu(y):
        ys = 0.5 * y * (1.0 + jnp.tanh(math.sqrt(2.0 / math.pi) * (y + 0.044715 * (y * y * y))))
        gate = _sigmoid(_wdot(ys.astype(BF16), wg_ref[...]) + bgl_ref[...])
        return ys * gate

    first_row = lax.broadcasted_iota(jnp.int32, (tp, D_MODEL), 0) == 0
    shifted = []
    for b in range(nb):
        w = w_ref[b * tp:(b + 1) * tp, :]
        shifted.append(jnp.where(first_row, wc_ref[b], pltpu.roll(w, 1, 0)))
        wc_ref[b] = w[tp - 1:tp, :]
    yo_ref[...] = glu(yo_ref[...] + d_ref[...] * uo_ref[...])
    w_ref[...] = glu(jnp.concatenate(shifted, axis=0) + k0_ref[...] + d_ref[...] * ue_ref[...])
    for b in range(nb):
        for j in range(SSM_SLABS):
            il_ref[j, pl.ds(0, tp, stride=2), :] = w_ref[b * tp:(b + 1) * tp, j * LANES:(j + 1) * LANES]
            il_ref[j, pl.ds(1, tp, stride=2), :] = yo_ref[b * tp:(b + 1) * tp, j * LANES:(j + 1) * LANES]
        for j in range(SSM_SLABS):
            o_ref[b, :, j * LANES:(j + 1) * LANES] = il_ref[j].astype(BF16)


def _s5_branch(z3, bmat, cmat, lam_re, lam_im, d_skip, w_glu, b_glu):
    nb, seqlen, _ = z3.shape
    tc = min(TILE_S5, seqlen)
    tp = tc // 2
    const2 = lambda c: (0, 0)
    const3 = lambda c: (0, 0, 0)
    single = pl.Buffered(1)
    return pl.pallas_call(
        _s5_kernel,
        grid=(seqlen // tc,),
        in_specs=[
            pl.BlockSpec((nb, tc, D_MODEL), lambda c: (0, c, 0)),
            pl.BlockSpec(bmat.shape, const3, pipeline_mode=single),
            pl.BlockSpec(cmat.shape, const3, pipeline_mode=single),
            pl.BlockSpec(lam_re.shape, const2),
            pl.BlockSpec(lam_im.shape, const2),
            pl.BlockSpec((1, D_MODEL), const2),
            pl.BlockSpec((D_MODEL, D_MODEL), const2, pipeline_mode=single),
            pl.BlockSpec((1, D_MODEL), const2),
        ],
        out_specs=pl.BlockSpec((nb, tc, D_MODEL), lambda c: (0, c, 0)),
        out_shape=jax.ShapeDtypeStruct((nb, seqlen, D_MODEL), BF16),
        scratch_shapes=[
            pltpu.VMEM((nb, NLB, SSM_SLABS * (tp + SLAB_ROW_PAD), LANES), F32),
            pltpu.VMEM((nb, NLB, SSM_SLABS, LANES), F32),
            pltpu.VMEM((nb, 1, D_MODEL), F32),
            pltpu.VMEM((SSM_SLABS, tc, LANES), F32),
        ] + [pltpu.VMEM((nb * tp, D_MODEL), F32)] * 5,
        compiler_params=_params("arbitrary"),
        name="s5_branch",
    )(z3, bmat, cmat, lam_re, lam_im, d_skip, w_glu, b_glu)


def _s5_matrices(lam_re, lam_im, log_dt, b_re, b_im, c_re, c_im):
    dt = jnp.exp(log_dt)[:, None]
    mag = jnp.exp(lam_re * dt)
    lb_re, lb_im = mag * jnp.cos(lam_im * dt), mag * jnp.sin(lam_im * dt)
    den = lam_re * lam_re + lam_im * lam_im
    cf_re = ((lb_re - 1.0) * lam_re + lb_im * lam_im) / den
    cf_im = (lb_im * lam_re - (lb_re - 1.0) * lam_im) / den
    eye = jnp.eye(SLAB_GROUPS, dtype=F32)[None, :, None, :, None]
    by_slab = (SSM_SLABS, SLAB_GROUPS)

    def col_factor(x):
        return x.reshape(by_slab + (SSM_STATE,))[:, None, None, :, :]

    def row_factor(x):
        return x.reshape(by_slab + (SSM_STATE,))[:, :, :, None, None]

    def in_place(x):
        return x.reshape(by_slab + (SSM_STATE, SSM_GROUP)).transpose(0, 1, 3, 2)[:, :, :, None, :] * eye

    def out_place(x):
        return x.reshape(by_slab + (SSM_GROUP, SSM_STATE)).transpose(0, 3, 1, 2)[:, None] * eye

    def in_flat(x):
        return x.reshape(SSM_SLABS, LANES, SLAB_STATES)

    def out_flat(x):
        return x.reshape(SSM_SLABS, SLAB_STATES, LANES)

    br, bi = in_place(b_re), in_place(b_im)
    bo_re = col_factor(cf_re) * br - col_factor(cf_im) * bi
    bo_im = col_factor(cf_re) * bi + col_factor(cf_im) * br
    be_re = col_factor(lb_re) * bo_re - col_factor(lb_im) * bo_im
    be_im = col_factor(lb_re) * bo_im + col_factor(lb_im) * bo_re
    cr, ci = out_place(c_re), out_place(c_im)
    cn_re = row_factor(lb_re) * cr - row_factor(lb_im) * ci
    cn_im = row_factor(lb_re) * ci + row_factor(lb_im) * cr
    b_odd = jnp.concatenate([in_flat(bo_re), in_flat(bo_im)], axis=2)
    cur = jnp.concatenate([out_flat(cr), out_flat(-ci)], axis=1)
    k0 = jnp.einsum('jrs,jsc->jrc', b_odd, cur, precision=lax.Precision.HIGHEST)
    top = jnp.concatenate([in_flat(be_re), in_flat(be_im), k0], axis=2)
    bot = jnp.concatenate([b_odd, jnp.zeros_like(k0)], axis=2)
    bmat = jnp.concatenate([top, bot], axis=1).astype(BF16)
    nxt = jnp.concatenate([out_flat(cn_re), out_flat(-cn_im)], axis=1)
    cmat = jnp.concatenate([cur, nxt], axis=2).astype(BF16)
    lre = (lb_re * lb_re - lb_im * lb_im).reshape(SSM_SLABS, SLAB_STATES)
    lim = (2.0 * lb_re * lb_im).reshape(SSM_SLABS, SLAB_STATES)
    return bmat, cmat, lre, lim


def _dattn_kernel(q_ref, qn_ref, k_ref, v_ref, lmb_ref, hn_ref, o_ref,
                  qm_ref, vx_ref, sx_ref, sy_ref, m_ref, acc_ref, *, lam_init, tq):
    tk = tq
    nsub = q_ref.shape[0] // tq
    nstrips = tk // LANES
    nheads = q_ref.shape[1] // LANES
    nkb = k_ref.shape[0] // tk
    lane = lax.broadcasted_iota(jnp.int32, (tq, LANES), 1)
    nt = (((1,), (1,)), ((), ()))

    @pl.when(pl.program_id(2) == 0)
    def _():
        for hh in range(nheads):
            vx_ref[hh, :, :DA_V_DIM] = v_ref[:, hh * LANES:(hh + 1) * LANES]
            vx_ref[hh, :, DA_V_DIM:] = jnp.ones((v_ref.shape[0], DA_V_DIM), BF16)

    def mask_maps(src_ref, row0):
        for hh in range(nheads):
            q = src_ref[row0:row0 + tq, hh * LANES:(hh + 1) * LANES]
            zero = jnp.zeros_like(q)
            qm_ref[2 * hh] = jnp.where(lane < DA_HEAD_DIM, q, zero)
            qm_ref[2 * hh + 1] = jnp.where(lane >= DA_HEAD_DIM, q, zero)

    def scores(ci, ki, s_ref):
        start = pl.multiple_of(ki * tk, tk)
        hh = ci // 2
        kb = k_ref[pl.ds(start, tk), hh * LANES:(hh + 1) * LANES]
        s = lax.dot_general(qm_ref[ci], kb, nt, preferred_element_type=F32)
        s_ref[ci, :, :tk] = s
        s_ref[ci, :, tk:] = functools.reduce(jnp.maximum, [s[:, c * LANES:(c + 1) * LANES] for c in range(nstrips)])

    def update(ci, ki, s_ref, first=False):
        start = pl.multiple_of(ki * tk, tk)
        hh = ci // 2
        vb = vx_ref[hh, pl.ds(start, tk), :]
        m_next = jnp.max(s_ref[ci, :, tk:], axis=1, keepdims=True)
        if first:
            m_next = jnp.broadcast_to(m_next, (tq, LANES))
        else:
            m_prev = m_ref[ci]
            m_next = jnp.maximum(m_prev, m_next)
            alpha = jnp.exp2(m_prev - m_next)
        ps = [jnp.exp2(s_ref[ci, :, c * LANES:(c + 1) * LANES] - m_next) for c in range(nstrips)]
        pv = _bdot(jnp.concatenate(ps, axis=1).astype(BF16), vb)
        acc_ref[ci] = pv if first else jnp.concatenate([alpha, alpha], axis=1) * acc_ref[ci] + pv
        m_ref[ci] = m_next

    def update_diag(ci, ki, s_ref):
        half = tq // 2
        start = pl.multiple_of(ki * tk, tk)
        hh = ci // 2
        row = lax.broadcasted_iota(jnp.int32, (half, half), 0) // CHUNK
        col = lax.broadcasted_iota(jnp.int32, (half, half), 1) // CHUNK
        visible = col <= row
        for part in range(2):
            rows = slice(part * half, (part + 1) * half)
            ncols = half * (part + 1)
            strips = []
            for c in range(ncols // LANES):
                s = s_ref[ci, rows, c * LANES:(c + 1) * LANES]
                off = c * LANES - part * half
                if off >= 0:
                    s = jnp.where(visible[:, off:off + LANES], s, NEG_BIG)
                strips.append(s)
            m_prev = m_ref[ci, rows, :]
            m_cur = functools.reduce(jnp.maximum, strips)
            m_next = jnp.maximum(m_prev, jnp.max(m_cur, axis=1, keepdims=True))
            alpha = jnp.exp2(m_prev - m_next)
            p = jnp.concatenate([jnp.exp2(s - m_next) for s in strips], axis=1).astype(BF16)
            pv = _bdot(p, vx_ref[hh, pl.ds(start, ncols), :])
            acc_ref[ci, rows, :] = jnp.concatenate([alpha, alpha], axis=1) * acc_ref[ci, rows, :] + pv
            m_ref[ci, rows, :] = m_next

    def stage(k_next, s_next_ref, k_cur, s_cur_ref, diag=False):
        for ci in range(2 * nheads):
            if k_next is not None:
                scores(ci, k_next, s_next_ref)
            if diag:
                update_diag(ci, k_cur, s_cur_ref)
            else:
                update(ci, k_cur, s_cur_ref)

    def query_block(sub):
        qi = pl.program_id(2) * nsub + sub
        rows = slice(sub * tq, (sub + 1) * tq)

        @pl.when(qi == 0)
        def _():
            mask_maps(q_ref, 0)
            m_ref[...] = jnp.full(m_ref.shape, NEG_BIG, F32)
            acc_ref[...] = jnp.zeros(acc_ref.shape, F32)
            for ci in range(2 * nheads):
                scores(ci, 0, sx_ref)

        def pair(p, carry):
            stage(2 * p + 2, sy_ref, 2 * p + 1, sx_ref)
            stage(2 * p + 3, sx_ref, 2 * p + 2, sy_ref)
            return carry

        lax.fori_loop(0, (qi - 1) // 2, pair, 0)
        diag_in_sx = (qi == 0) | (qi % 2 == 1)

        @pl.when(diag_in_sx)
        def _():
            stage(None, None, qi, sx_ref, diag=True)

        @pl.when(jnp.logical_not(diag_in_sx))
        def _():
            stage(qi, sy_ref, qi - 1, sx_ref)
            stage(None, None, qi, sy_ref, diag=True)

        lv = lmb_ref[...]
        lam = (jnp.exp(jnp.sum(lv[0:1] * lv[1:2], axis=1, keepdims=True))
               - jnp.exp(jnp.sum(lv[2:3] * lv[3:4], axis=1, keepdims=True)) + lam_init)
        for hh in range(nheads):
            o = (acc_ref[2 * hh, :, :DA_V_DIM] / acc_ref[2 * hh, :, DA_V_DIM:]
                 - lam * (acc_ref[2 * hh + 1, :, :DA_V_DIM] / acc_ref[2 * hh + 1, :, DA_V_DIM:]))
            o = o * lax.rsqrt(jnp.mean(o * o, axis=-1, keepdims=True) + EPS) * hn_ref[...]
            o_ref[rows, hh * LANES:(hh + 1) * LANES] = (o * (1.0 - lam_init)).astype(BF16)

        if sub + 1 < nsub:
            mask_maps(q_ref, (sub + 1) * tq)
        else:
            mask_maps(qn_ref, 0)
        for ci in range(2 * nheads):
            scores(ci, 0, sy_ref)
        for ci in range(2 * nheads):
            scores(ci, min(1, nkb - 1), sx_ref)
            update(ci, 0, sy_ref, first=True)

    for sub in range(nsub):
        query_block(sub)


def _diff_attention(z3, lmb, head_norm, lam_init):
    nb, seqlen, _ = z3.shape
    tq = min(TILE_ATTN, seqlen)
    nq = seqlen // tq
    nsub = ATTN_BLOCKS_PER_STEP if nq % ATTN_BLOCKS_PER_STEP == 0 else 1
    hps = ATTN_HEADS_PER_STEP
    width = hps * LANES
    qcol, kcol, vcol = (c // width for c in (D_MODEL, 2 * D_MODEL, 3 * D_MODEL))
    return pl.pallas_call(
        functools.partial(_dattn_kernel, lam_init=lam_init, tq=tq),
        grid=(nb, DA_HEADS // hps, nq // nsub),
        in_specs=[
            pl.BlockSpec((None, nsub * tq, width), lambda b, h, i: (b, i, qcol + h)),
            pl.BlockSpec((None, tq, width), lambda b, h, i: (b, jnp.minimum((i + 1) * nsub, nq - 1), qcol + h)),
            pl.BlockSpec((None, seqlen, width), lambda b, h, i: (b, 0, kcol + h)),
            pl.BlockSpec((None, seqlen, width), lambda b, h, i: (b, 0, vcol + h)),
            pl.BlockSpec((4, DA_HEAD_DIM), lambda b, h, i: (0, 0)),
            pl.BlockSpec((1, DA_V_DIM), lambda b, h, i: (0, 0)),
        ],
        out_specs=pl.BlockSpec((None, nsub * tq, width), lambda b, h, i: (b, i, h)),
        out_shape=jax.ShapeDtypeStruct((nb, seqlen, D_MODEL), BF16),
        scratch_shapes=[pltpu.VMEM((2 * hps, tq, LANES), BF16),
                        pltpu.VMEM((hps, seqlen, 2 * DA_V_DIM), BF16),
                        pltpu.VMEM((2 * hps, tq, tq + LANES), F32), pltpu.VMEM((2 * hps, tq, tq + LANES), F32),
                        pltpu.VMEM((2 * hps, tq, LANES), F32),
                        pltpu.VMEM((2 * hps, tq, 2 * DA_V_DIM), F32)],
        compiler_params=_params("parallel", "parallel", "arbitrary"),
        name="diff_attention",
    )(z3, z3, z3, z3, lmb, head_norm)


def _merge_kernel(ys_ref, ya_ref, g_ref, x_ref, wa_ref, wb_ref, wo_ref, n_ref, o_ref):
    a = _wdot(ys_ref[...], wa_ref[...])
    b = _wdot(ya_ref[...], wb_ref[...])
    ga = g_ref[:, :D_MODEL].astype(F32)
    gb = g_ref[:, D_MODEL:].astype(F32)
    merged = (ga * a + gb * b).astype(BF16)
    o_ref[...] = x_ref[...] + _rms(_wdot(merged, wo_ref[...]), n_ref[...])


def _merge(ys2, ya2, z2, x2, w_a, w_b, w_o, gain):
    t = x2.shape[0]
    tm = min(TILE_MERGE, t)
    row = lambda i: (i, 0)
    const = lambda i: (0, 0)
    wspec = pl.BlockSpec((D_MODEL, D_MODEL), const, pipeline_mode=pl.Buffered(1))
    return pl.pallas_call(
        _merge_kernel,
        grid=(t // tm,),
        in_specs=[
            pl.BlockSpec((tm, D_MODEL), row),
            pl.BlockSpec((tm, D_MODEL), row),
            pl.BlockSpec((tm, 2 * D_MODEL), lambda i: (i, 2)),
            pl.BlockSpec((tm, D_MODEL), row),
            wspec, wspec, wspec,
            pl.BlockSpec((1, D_MODEL), const),
        ],
        out_specs=pl.BlockSpec((tm, D_MODEL), row),
        out_shape=jax.ShapeDtypeStruct((t, D_MODEL), F32),
        compiler_params=_params("parallel"),
        name="merge",
    )(ys2, ya2, z2, x2, w_a, w_b, w_o, gain)


def _xattn_kernel(x_ref, mem_ref, nmem_ref, wkv_ref, npre_ref, wq_ref, wo_ref, npost_ref, o_ref,
                  kv_ref, oh_ref):
    @pl.when(pl.program_id(1) == 0)
    def _():
        kv_ref[...] = _wdot(_rms(mem_ref[...], nmem_ref[...]).astype(BF16), wkv_ref[...]).astype(BF16)

    x = x_ref[...]
    h = _rms(x, npre_ref[...]).astype(BF16)
    q = (_wdot(h, wq_ref[...]) * (XA_HEAD_DIM ** -0.5 * LOG2E)).astype(BF16)
    nt = (((1,), (1,)), ((), ()))
    for hd in range(XA_HEADS):
        lo = hd * XA_HEAD_DIM
        k = kv_ref[:, lo:lo + XA_HEAD_DIM]
        v = kv_ref[:, D_MODEL + lo:D_MODEL + lo + XA_HEAD_DIM]
        s = lax.dot_general(q[:, lo:lo + XA_HEAD_DIM], k, nt, preferred_element_type=F32)
        p = jnp.exp2(s - jnp.max(s, axis=1, keepdims=True))
        o = _bdot(p.astype(BF16), v) / jnp.sum(p, axis=1, keepdims=True)
        oh_ref[:, lo:lo + XA_HEAD_DIM] = o.astype(BF16)
    o_ref[...] = x + _rms(_wdot(oh_ref[...], wo_ref[...]), npost_ref[...])


def _cross_attention(x3, mem, n_mem, w_xkv, n_pre, w_xq, w_xo, n_post):
    nb, seqlen, _ = x3.shape
    mlen = mem.shape[1]
    tm = min(TILE_XATTN, seqlen)
    const = lambda b, i: (0, 0)
    single = pl.Buffered(1)
    wspec = pl.BlockSpec((D_MODEL, D_MODEL), const, pipeline_mode=single)
    vspec = pl.BlockSpec((1, D_MODEL), const)
    return pl.pallas_call(
        _xattn_kernel,
        grid=(nb, seqlen // tm),
        in_specs=[
            pl.BlockSpec((None, tm, D_MODEL), lambda b, i: (b, i, 0)),
            pl.BlockSpec((None, mlen, D_MODEL), lambda b, i: (b, 0, 0)),
            vspec,
            pl.BlockSpec((D_MODEL, 2 * D_MODEL), const, pipeline_mode=single),
            vspec, wspec, wspec, vspec,
        ],
        out_specs=pl.BlockSpec((None, tm, D_MODEL), lambda b, i: (b, i, 0)),
        out_shape=jax.ShapeDtypeStruct((nb, seqlen, D_MODEL), F32),
        scratch_shapes=[pltpu.VMEM((mlen, 2 * D_MODEL), BF16), pltpu.VMEM((tm, D_MODEL), BF16)],
        compiler_params=_params("parallel", "arbitrary"),
        name="cross_attention",
    )(x3, mem, n_mem, w_xkv, n_pre, w_xq, w_xo, n_post)


def _mlp_kernel(x_ref, npre_ref, w1_hbm, w2_hbm, npost_ref, o_ref, w1_ref, w2_ref, sem):
    nchunk = D_FF // D_MODEL

    def w1_copy(c):
        cols = pl.ds(c * D_MODEL, D_MODEL)
        return pltpu.make_async_copy(w1_hbm.at[:, cols], w1_ref.at[:, cols], sem.at[0, c])

    def w2_copy(c):
        rows = pl.ds(c * D_MODEL, D_MODEL)
        return pltpu.make_async_copy(w2_hbm.at[rows, :], w2_ref.at[rows, :], sem.at[1, c])

    @pl.when(pl.program_id(0) == 0)
    def _():
        for c in range(nchunk):
            w1_copy(c).start()
            w2_copy(c).start()

    def run(wait_for_weights):
        x = x_ref[...]
        h = _rms(x, npre_ref[...]).astype(BF16)
        f = jnp.zeros(x.shape, F32)
        for c in range(nchunk):
            lo = c * D_MODEL
            if wait_for_weights:
                w1_copy(c).wait()
            a = jnp.maximum(_wdot(h, w1_ref[:, lo:lo + D_MODEL]), 0.0)
            if wait_for_weights:
                w2_copy(c).wait()
            f = f + _wdot((a * a).astype(BF16), w2_ref[lo:lo + D_MODEL, :])
        o_ref[...] = x + _rms(f, npost_ref[...])

    @pl.when(pl.program_id(0) == 0)
    def _():
        run(True)

    @pl.when(pl.program_id(0) > 0)
    def _():
        run(False)


def _mlp(x2, n_pre, w1, w2, n_post):
    t = x2.shape[0]
    tm = min(TILE_MLP, t)
    row = lambda i: (i, 0)
    const = lambda i: (0, 0)
    vspec = pl.BlockSpec((1, D_MODEL), const)
    return pl.pallas_call(
        _mlp_kernel,
        grid=(t // tm,),
        in_specs=[
            pl.BlockSpec((tm, D_MODEL), row),
            vspec,
            pl.BlockSpec(memory_space=pl.ANY),
            pl.BlockSpec(memory_space=pl.ANY),
            vspec,
        ],
        out_specs=pl.BlockSpec((tm, D_MODEL), row),
        out_shape=jax.ShapeDtypeStruct((t, D_MODEL), F32),
        scratch_shapes=[pltpu.VMEM((D_MODEL, D_FF), F32), pltpu.VMEM((D_FF, D_MODEL), F32),
                        pltpu.SemaphoreType.DMA((2, D_FF // D_MODEL))],
        compiler_params=_params("arbitrary"),
        name="mlp",
    )(x2, n_pre, w1, w2, n_post)


def _rope_tables(seqlen):
    inv = ROPE_THETA ** (-jnp.arange(0, DA_HEAD_DIM, 2, dtype=F32) / DA_HEAD_DIM)
    ang = jnp.arange(seqlen, dtype=F32)[:, None] * inv[None, :]
    cos = jnp.tile(jnp.cos(ang), (1, 4))
    sin = jnp.tile(jnp.concatenate([-jnp.sin(ang), jnp.sin(ang)], axis=1), (1, 2))
    return cos, sin


def kernel(x, mem, norm_mix_pre, w_in, b_gate, ssm_lambda_re, ssm_lambda_im, ssm_log_dt, ssm_b_re, ssm_b_im, ssm_c_re, ssm_c_im, ssm_d, w_glu, b_glu, w_ssm_proj, da_lambda_q1, da_lambda_k1, da_lambda_q2, da_lambda_k2, da_head_norm, w_da_proj, w_mix_out, norm_mix_post, norm_x_pre, norm_mem, w_xq, w_xkv, w_xo, norm_x_post, norm_ff_pre, w_ff1, w_ff2, norm_ff_post):
    nb, seqlen, _ = x.shape
    t = nb * seqlen
    depth = w_in.shape[0]
    tabs = _rope_tables(seqlen)
    x2 = x.reshape(t, D_MODEL)
    for l in range(depth):
        lam_init = 0.8 - 0.6 * math.exp(-0.3 * l)
        z2 = _in_projection(x2, norm_mix_pre[l][None], w_in[l], b_gate[l][None], tabs, seqlen)
        z3 = z2.reshape(nb, seqlen, IN_COLS)
        bmat, cmat, lre, lim = _s5_matrices(ssm_lambda_re[l], ssm_lambda_im[l], ssm_log_dt[l],
                                            ssm_b_re[l], ssm_b_im[l], ssm_c_re[l], ssm_c_im[l])
        ys = _s5_branch(z3, bmat, cmat, lre, lim, ssm_d[l][None], w_glu[l], b_glu[l][None])
        lmb = jnp.stack([da_lambda_q1[l], da_lambda_k1[l], da_lambda_q2[l], da_lambda_k2[l]])
        ya = _diff_attention(z3, lmb, da_head_norm[l][None], lam_init)
        x2 = _merge(ys.reshape(t, D_MODEL), ya.reshape(t, D_MODEL), z2, x2,
                    w_ssm_proj[l], w_da_proj[l], w_mix_out[l],
                    norm_mix_post[l][None])
        x3 = _cross_attention(x2.reshape(nb, seqlen, D_MODEL), mem, norm_mem[l][None], w_xkv[l],
                              norm_x_pre[l][None], w_xq[l], w_xo[l],
                              norm_x_post[l][None])
        x2 = _mlp(x3.reshape(t, D_MODEL), norm_ff_pre[l][None], w_ff1[l],
                  w_ff2[l], norm_ff_post[l][None])
    return x2.reshape(nb, seqlen, D_MODEL)
```

```python
import functools
import math

import jax
import jax.numpy as jnp
from jax import lax
from jax.experimental import pallas as pl
from jax.experimental.pallas import tpu as pltpu

F32 = jnp.float32
BF16 = jnp.bfloat16

LANES = 128
TILE_IN_PROJ = 512
TILE_S5 = 512
TILE_ATTN = 512
TILE_MERGE = 1024
TILE_XATTN = 1024
TILE_MLP = 512

D_MODEL = 1024
EPS = 1e-6
CHUNK = 64
SSM_GROUP = 16
SSM_GROUPS = 64
SSM_STATE = 64
SSM_SLABS = 8
SLAB_GROUPS = SSM_GROUPS // SSM_SLABS
SLAB_STATES = SLAB_GROUPS * SSM_STATE
NLB = 2 * SLAB_STATES // LANES
SLAB_ROW_PAD = 8
DA_HEADS = 8
DA_HEAD_DIM = 64
DA_V_DIM = 128
ROPE_THETA = 10000.0
ATTN_HEADS_PER_STEP = 2
ATTN_BLOCKS_PER_STEP = 2
XA_HEADS = 4
XA_HEAD_DIM = 256
D_FF = 4096
IN_COLS = 6144
LOG2E = 1.4426950408889634
NEG_BIG = -1e30
VMEM_LIMIT_BYTES = 56 * 1024 * 1024


def _params(*semantics):
    return pltpu.CompilerParams(dimension_semantics=semantics, vmem_limit_bytes=VMEM_LIMIT_BYTES)


def _rms(x, g):
    return x * lax.rsqrt(jnp.mean(x * x, axis=-1, keepdims=True) + EPS) * g


def _bdot(a, b):
    return jnp.dot(a, b, preferred_element_type=F32)


def _wdot(a, w):
    return _bdot(a, w.astype(BF16))


def _sigmoid(x):
    return 0.5 * jnp.tanh(0.5 * x) + 0.5


def _inproj_kernel(x_ref, g_ref, w_ref, bg_ref, cos_ref, sin_ref, o_ref):
    tm = x_ref.shape[0]
    hn = _rms(x_ref[...], g_ref[...]).astype(BF16)
    lane = lax.broadcasted_iota(jnp.int32, (tm, LANES), 1)
    half = DA_HEAD_DIM // 2
    first_half = (lane % DA_HEAD_DIM) < half

    def rope(acc, col0, scale):
        c = cos_ref[...] * scale
        s = sin_ref[...] * scale
        for h in range(DA_HEADS):
            blk = acc[:, h * LANES:(h + 1) * LANES]
            partner = jnp.where(first_half, pltpu.roll(blk, LANES - half, 1), pltpu.roll(blk, half, 1))
            o_ref[:, col0 + h * LANES:col0 + (h + 1) * LANES] = (blk * c + partner * s).astype(BF16)

    for j in range(IN_COLS // D_MODEL):
        col0 = j * D_MODEL
        acc = _wdot(hn, w_ref[:, col0:col0 + D_MODEL])
        if j == 1:
            rope(acc, col0, DA_HEAD_DIM ** -0.5 * LOG2E)
        elif j == 2:
            rope(acc, col0, 1.0)
        elif j >= 4:
            gate = _sigmoid(acc + bg_ref[:, col0 - 4 * D_MODEL:col0 - 3 * D_MODEL])
            o_ref[:, col0:col0 + D_MODEL] = gate.astype(BF16)
        else:
            o_ref[:, col0:col0 + D_MODEL] = acc.astype(BF16)


def _in_projection(x2, gain, w_in, b_gate, tabs, seqlen):
    t = x2.shape[0]
    tm = min(TILE_IN_PROJ, seqlen)
    nl = seqlen // tm
    const = lambda i: (0, 0)
    tab_spec = pl.BlockSpec((tm, LANES), lambda i: (i % nl, 0))
    return pl.pallas_call(
        _inproj_kernel,
        grid=(t // tm,),
        in_specs=[
            pl.BlockSpec((tm, D_MODEL), lambda i: (i, 0)),
            pl.BlockSpec((1, D_MODEL), const),
            pl.BlockSpec((D_MODEL, IN_COLS), const, pipeline_mode=pl.Buffered(1)),
            pl.BlockSpec((1, 2 * D_MODEL), const),
            tab_spec, tab_spec,
        ],
        out_specs=pl.BlockSpec((tm, IN_COLS), lambda i: (i, 0)),
        out_shape=jax.ShapeDtypeStruct((t, IN_COLS), BF16),
        compiler_params=_params("parallel"),
        name="in_projection",
    )(x2, gain, w_in, b_gate, *tabs)


def _s5_kernel(u_ref, bm_ref, cm_ref, lre_ref, lim_ref, d_ref, wg_ref, bgl_ref, o_ref,
               bu_ref, st_ref, wc_ref, il_ref, ue_ref, uo_ref, yo_ref, w_ref, k0_ref):
    nb, tc, _ = u_ref.shape
    tp = tc // 2
    pitch = bu_ref.shape[2] // SSM_SLABS
    c = pl.program_id(0)

    @pl.when(c == 0)
    def _():
        st_ref[...] = jnp.zeros(st_ref.shape, F32)
        wc_ref[...] = jnp.zeros(wc_ref.shape, F32)

    for b in range(nb):
        for j in range(SSM_SLABS):
            il_ref[j] = u_ref[b, :, j * LANES:(j + 1) * LANES].astype(F32)
        for j in range(SSM_SLABS):
            u_e = il_ref[j, pl.ds(0, tp, stride=2), :]
            u_o = il_ref[j, pl.ds(1, tp, stride=2), :]
            ue_ref[b * tp:(b + 1) * tp, j * LANES:(j + 1) * LANES] = u_e
            uo_ref[b * tp:(b + 1) * tp, j * LANES:(j + 1) * LANES] = u_o
            lhs = jnp.concatenate([u_e, u_o], axis=1).astype(BF16)
            bu = _bdot(lhs, bm_ref[j])
            for k in range(NLB):
                bu_ref[b, k, j * pitch:j * pitch + tp, :] = bu[:, k * LANES:(k + 1) * LANES]
            k0_ref[b * tp:(b + 1) * tp, j * LANES:(j + 1) * LANES] = bu[:, NLB * LANES:]

    ar = [lre_ref[:, k * LANES:(k + 1) * LANES] for k in range(NLB // 2)]
    ai = [lim_ref[:, k * LANES:(k + 1) * LANES] for k in range(NLB // 2)]

    def step(t, carry):
        out = []
        for b in range(nb):
            xs = carry[b]
            new = [None] * NLB
            for k in range(NLB // 2):
                xr, xi = xs[k], xs[k + NLB // 2]
                rows = pl.ds(t, SSM_SLABS, stride=pitch)
                new[k] = ar[k] * xr - ai[k] * xi + bu_ref[b, k, rows, :]
                new[k + NLB // 2] = ar[k] * xi + ai[k] * xr + bu_ref[b, k + NLB // 2, rows, :]
                bu_ref[b, k, rows, :] = new[k]
                bu_ref[b, k + NLB // 2, rows, :] = new[k + NLB // 2]
            out.append(tuple(new))
        return tuple(out)

    init = tuple(tuple(st_ref[b, k] for k in range(NLB)) for b in range(nb))
    fin = lax.fori_loop(0, tp, step, init, unroll=8)
    for b in range(nb):
        for k in range(NLB):
            st_ref[b, k] = fin[b][k]

    for b in range(nb):
        for j in range(SSM_SLABS):
            xs = jnp.concatenate([bu_ref[b, k, j * pitch:j * pitch + tp, :] for k in range(NLB)], axis=1)
            r = _bdot(xs.astype(BF16), cm_ref[j])
            yo_ref[b * tp:(b + 1) * tp, j * LANES:(j + 1) * LANES] = r[:, :LANES]
            w_ref[b * tp:(b + 1) * tp, j * LANES:(j + 1) * LANES] = r[:, LANES:]

    def glu(y):
        ys = 0.5 * y * (1.0 + jnp.tanh(math.sqrt(2.0 / math.pi) * (y + 0.044715 * (y * y * y))))
        gate = _sigmoid(_wdot(ys.astype(BF16), wg_ref[...]) + bgl_ref[...])
        return ys * gate

    first_row = lax.broadcasted_iota(jnp.int32, (tp, D_MODEL), 0) == 0
    shifted = []
    for b in range(nb):
        w = w_ref[b * tp:(b + 1) * tp, :]
        shifted.append(jnp.where(first_row, wc_ref[b], pltpu.roll(w, 1, 0)))
        wc_ref[b] = w[tp - 1:tp, :]
    yo_ref[...] = glu(yo_ref[...] + d_ref[...] * uo_ref[...])
    w_ref[...] = glu(jnp.concatenate(shifted, axis=0) + k0_ref[...] + d_ref[...] * ue_ref[...])
    for b in range(nb):
        for j in range(SSM_SLABS):
            il_ref[j, pl.ds(0, tp, stride=2), :] = w_ref[b * tp:(b + 1) * tp, j * LANES:(j + 1) * LANES]
            il_ref[j, pl.ds(1, tp, stride=2), :] = yo_ref[b * tp:(b + 1) * tp, j * LANES:(j + 1) * LANES]
        for j in range(SSM_SLABS):
            o_ref[b, :, j * LANES:(j + 1) * LANES] = il_ref[j].astype(BF16)


def _s5_branch(z3, bmat, cmat, lam_re, lam_im, d_skip, w_glu, b_glu):
    nb, seqlen, _ = z3.shape
    tc = min(TILE_S5, seqlen)
    tp = tc // 2
    const2 = lambda c: (0, 0)
    const3 = lambda c: (0, 0, 0)
    single = pl.Buffered(1)
    return pl.pallas_call(
        _s5_kernel,
        grid=(seqlen // tc,),
        in_specs=[
            pl.BlockSpec((nb, tc, D_MODEL), lambda c: (0, c, 0)),
            pl.BlockSpec(bmat.shape, const3, pipeline_mode=single),
            pl.BlockSpec(cmat.shape, const3, pipeline_mode=single),
            pl.BlockSpec(lam_re.shape, const2),
            pl.BlockSpec(lam_im.shape, const2),
            pl.BlockSpec((1, D_MODEL), const2),
            pl.BlockSpec((D_MODEL, D_MODEL), const2, pipeline_mode=single),
            pl.BlockSpec((1, D_MODEL), const2),
        ],
        out_specs=pl.BlockSpec((nb, tc, D_MODEL), lambda c: (0, c, 0)),
        out_shape=jax.ShapeDtypeStruct((nb, seqlen, D_MODEL), BF16),
        scratch_shapes=[
            pltpu.VMEM((nb, NLB, SSM_SLABS * (tp + SLAB_ROW_PAD), LANES), F32),
            pltpu.VMEM((nb, NLB, SSM_SLABS, LANES), F32),
            pltpu.VMEM((nb, 1, D_MODEL), F32),
            pltpu.VMEM((SSM_SLABS, tc, LANES), F32),
        ] + [pltpu.VMEM((nb * tp, D_MODEL), F32)] * 5,
        compiler_params=_params("arbitrary"),
        name="s5_branch",
    )(z3, bmat, cmat, lam_re, lam_im, d_skip, w_glu, b_glu)


def _s5_matrices(lam_re, lam_im, log_dt, b_re, b_im, c_re, c_im):
    dt = jnp.exp(log_dt)[:, None]
    mag = jnp.exp(lam_re * dt)
    lb_re, lb_im = mag * jnp.cos(lam_im * dt), mag * jnp.sin(lam_im * dt)
    den = lam_re * lam_re + lam_im * lam_im
    cf_re = ((lb_re - 1.0) * lam_re + lb_im * lam_im) / den
    cf_im = (lb_im * lam_re - (lb_re - 1.0) * lam_im) / den
    eye = jnp.eye(SLAB_GROUPS, dtype=F32)[None, :, None, :, None]
    by_slab = (SSM_SLABS, SLAB_GROUPS)

    def col_factor(x):
        return x.reshape(by_slab + (SSM_STATE,))[:, None, None, :, :]

    def row_factor(x):
        return x.reshape(by_slab + (SSM_STATE,))[:, :, :, None, None]

    def in_place(x):
        return x.reshape(by_slab + (SSM_STATE, SSM_GROUP)).transpose(0, 1, 3, 2)[:, :, :, None, :] * eye

    def out_place(x):
        return x.reshape(by_slab + (SSM_GROUP, SSM_STATE)).transpose(0, 3, 1, 2)[:, None] * eye

    def in_flat(x):
        return x.reshape(SSM_SLABS, LANES, SLAB_STATES)

    def out_flat(x):
        return x.reshape(SSM_SLABS, SLAB_STATES, LANES)

    br, bi = in_place(b_re), in_place(b_im)
    bo_re = col_factor(cf_re) * br - col_factor(cf_im) * bi
    bo_im = col_factor(cf_re) * bi + col_factor(cf_im) * br
    be_re = col_factor(lb_re) * bo_re - col_factor(lb_im) * bo_im
    be_im = col_factor(lb_re) * bo_im + col_factor(lb_im) * bo_re
    cr, ci = out_place(c_re), out_place(c_im)
    cn_re = row_factor(lb_re) * cr - row_factor(lb_im) * ci
    cn_im = row_factor(lb_re) * ci + row_factor(lb_im) * cr
    b_odd = jnp.concatenate([in_flat(bo_re), in_flat(bo_im)], axis=2)
    cur = jnp.concatenate([out_flat(cr), out_flat(-ci)], axis=1)
    k0 = jnp.einsum('jrs,jsc->jrc', b_odd, cur, precision=lax.Precision.HIGHEST)
    top = jnp.concatenate([in_flat(be_re), in_flat(be_im), k0], axis=2)
    bot = jnp.concatenate([b_odd, jnp.zeros_like(k0)], axis=2)
    bmat = jnp.concatenate([top, bot], axis=1).astype(BF16)
    nxt = jnp.concatenate([out_flat(cn_re), out_flat(-cn_im)], axis=1)
    cmat = jnp.concatenate([cur, nxt], axis=2).astype(BF16)
    lre = (lb_re * lb_re - lb_im * lb_im).reshape(SSM_SLABS, SLAB_STATES)
    lim = (2.0 * lb_re * lb_im).reshape(SSM_SLABS, SLAB_STATES)
    return bmat, cmat, lre, lim


def _dattn_kernel(q_ref, qn_ref, k_ref, v_ref, lmb_ref, hn_ref, o_ref,
                  qm_ref, vx_ref, sx_ref, sy_ref, m_ref, acc_ref, *, lam_init, tq):
    tk = tq
    nsub = q_ref.shape[0] // tq
    nstrips = tk // LANES
    nheads = q_ref.shape[1] // LANES
    nkb = k_ref.shape[0] // tk
    lane = lax.broadcasted_iota(jnp.int32, (tq, LANES), 1)
    nt = (((1,), (1,)), ((), ()))

    @pl.when(pl.program_id(2) == 0)
    def _():
        for hh in range(nheads):
            vx_ref[hh, :, :DA_V_DIM] = v_ref[:, hh * LANES:(hh + 1) * LANES]
            vx_ref[hh, :, DA_V_DIM:] = jnp.ones((v_ref.shape[0], DA_V_DIM), BF16)

    def mask_maps(src_ref, row0):
        for hh in range(nheads):
            q = src_ref[row0:row0 + tq, hh * LANES:(hh + 1) * LANES]
            zero = jnp.zeros_like(q)
            qm_ref[2 * hh] = jnp.where(lane < DA_HEAD_DIM, q, zero)
            qm_ref[2 * hh + 1] = jnp.where(lane >= DA_HEAD_DIM, q, zero)

    def scores(ci, ki, s_ref):
        start = pl.multiple_of(ki * tk, tk)
        hh = ci // 2
        kb = k_ref[pl.ds(start, tk), hh * LANES:(hh + 1) * LANES]
        s = lax.dot_general(qm_ref[ci], kb, nt, preferred_element_type=F32)
        s_ref[ci, :, :tk] = s
        s_ref[ci, :, tk:] = functools.reduce(jnp.maximum, [s[:, c * LANES:(c + 1) * LANES] for c in range(nstrips)])

    def update(ci, ki, s_ref, first=False):
        start = pl.multiple_of(ki * tk, tk)
        hh = ci // 2
        vb = vx_ref[hh, pl.ds(start, tk), :]
        m_next = jnp.max(s_ref[ci, :, tk:], axis=1, keepdims=True)
        if first:
            m_next = jnp.broadcast_to(m_next, (tq, LANES))
        else:
            m_prev = m_ref[ci]
            m_next = jnp.maximum(m_prev, m_next)
            alpha = jnp.exp2(m_prev - m_next)
        ps = [jnp.exp2(s_ref[ci, :, c * LANES:(c + 1) * LANES] - m_next) for c in range(nstrips)]
        pv = _bdot(jnp.concatenate(ps, axis=1).astype(BF16), vb)
        acc_ref[ci] = pv if first else jnp.concatenate([alpha, alpha], axis=1) * acc_ref[ci] + pv
        m_ref[ci] = m_next

    def update_diag(ci, ki, s_ref):
        half = tq // 2
        start = pl.multiple_of(ki * tk, tk)
        hh = ci // 2
        row = lax.broadcasted_iota(jnp.int32, (half, half), 0) // CHUNK
        col = lax.broadcasted_iota(jnp.int32, (half, half), 1) // CHUNK
        visible = col <= row
        for part in range(2):
            rows = slice(part * half, (part + 1) * half)
            ncols = half * (part + 1)
            strips = []
            for c in range(ncols // LANES):
                s = s_ref[ci, rows, c * LANES:(c + 1) * LANES]
                off = c * LANES - part * half
                if off >= 0:
                    s = jnp.where(visible[:, off:off + LANES], s, NEG_BIG)
                strips.append(s)
            m_prev = m_ref[ci, rows, :]
            m_cur = functools.reduce(jnp.maximum, strips)
            m_next = jnp.maximum(m_prev, jnp.max(m_cur, axis=1, keepdims=True))
            alpha = jnp.exp2(m_prev - m_next)
            p = jnp.concatenate([jnp.exp2(s - m_next) for s in strips], axis=1).astype(BF16)
            pv = _bdot(p, vx_ref[hh, pl.ds(start, ncols), :])
            acc_ref[ci, rows, :] = jnp.concatenate([alpha, alpha], axis=1) * acc_ref[ci, rows, :] + pv
            m_ref[ci, rows, :] = m_next

    def stage(k_next, s_next_ref, k_cur, s_cur_ref, diag=False):
        for ci in range(2 * nheads):
            if k_next is not None:
                scores(ci, k_next, s_next_ref)
            if diag:
                update_diag(ci, k_cur, s_cur_ref)
            else:
                update(ci, k_cur, s_cur_ref)

    def query_block(sub):
        qi = pl.program_id(2) * nsub + sub
        rows = slice(sub * tq, (sub + 1) * tq)

        @pl.when(qi == 0)
        def _():
            mask_maps(q_ref, 0)
            m_ref[...] = jnp.full(m_ref.shape, NEG_BIG, F32)
            acc_ref[...] = jnp.zeros(acc_ref.shape, F32)
            for ci in range(2 * nheads):
                scores(ci, 0, sx_ref)

        def pair(p, carry):
            stage(2 * p + 2, sy_ref, 2 * p + 1, sx_ref)
            stage(2 * p + 3, sx_ref, 2 * p + 2, sy_ref)
            return carry

        lax.fori_loop(0, (qi - 1) // 2, pair, 0)
        diag_in_sx = (qi == 0) | (qi % 2 == 1)

        @pl.when(diag_in_sx)
        def _():
            stage(None, None, qi, sx_ref, diag=True)

        @pl.when(jnp.logical_not(diag_in_sx))
        def _():
            stage(qi, sy_ref, qi - 1, sx_ref)
            stage(None, None, qi, sy_ref, diag=True)

        lv = lmb_ref[...]
        lam = (jnp.exp(jnp.sum(lv[0:1] * lv[1:2], axis=1, keepdims=True))
               - jnp.exp(jnp.sum(lv[2:3] * lv[3:4], axis=1, keepdims=True)) + lam_init)
        for hh in range(nheads):
            o = (acc_ref[2 * hh, :, :DA_V_DIM] / acc_ref[2 * hh, :, DA_V_DIM:]
                 - lam * (acc_ref[2 * hh + 1, :, :DA_V_DIM] / acc_ref[2 * hh + 1, :, DA_V_DIM:]))
            o = o * lax.rsqrt(jnp.mean(o * o, axis=-1, keepdims=True) + EPS) * hn_ref[...]
            o_ref[rows, hh * LANES:(hh + 1) * LANES] = (o * (1.0 - lam_init)).astype(BF16)

        if sub + 1 < nsub:
            mask_maps(q_ref, (sub + 1) * tq)
        else:
            mask_maps(qn_ref, 0)
        for ci in range(2 * nheads):
            scores(ci, 0, sy_ref)
        for ci in range(2 * nheads):
            scores(ci, min(1, nkb - 1), sx_ref)
            update(ci, 0, sy_ref, first=True)

    for sub in range(nsub):
        query_block(sub)


def _diff_attention(z3, lmb, head_norm, lam_init):
    nb, seqlen, _ = z3.shape
    tq = min(TILE_ATTN, seqlen)
    nq = seqlen // tq
    nsub = ATTN_BLOCKS_PER_STEP if nq % ATTN_BLOCKS_PER_STEP == 0 else 1
    hps = ATTN_HEADS_PER_STEP
    width = hps * LANES
    qcol, kcol, vcol = (c // width for c in (D_MODEL, 2 * D_MODEL, 3 * D_MODEL))
    return pl.pallas_call(
        functools.partial(_dattn_kernel, lam_init=lam_init, tq=tq),
        grid=(nb, DA_HEADS // hps, nq // nsub),
        in_specs=[
            pl.BlockSpec((None, nsub * tq, width), lambda b, h, i: (b, i, qcol + h)),
            pl.BlockSpec((None, tq, width), lambda b, h, i: (b, jnp.minimum((i + 1) * nsub, nq - 1), qcol + h)),
            pl.BlockSpec((None, seqlen, width), lambda b, h, i: (b, 0, kcol + h)),
            pl.BlockSpec((None, seqlen, width), lambda b, h, i: (b, 0, vcol + h)),
            pl.BlockSpec((4, DA_HEAD_DIM), lambda b, h, i: (0, 0)),
            pl.BlockSpec((1, DA_V_DIM), lambda b, h, i: (0, 0)),
        ],
        out_specs=pl.BlockSpec((None, nsub * tq, width), lambda b, h, i: (b, i, h)),
        out_shape=jax.ShapeDtypeStruct((nb, seqlen, D_MODEL), BF16),
        scratch_shapes=[pltpu.VMEM((2 * hps, tq, LANES), BF16),
                        pltpu.VMEM((hps, seqlen, 2 * DA_V_DIM), BF16),
                        pltpu.VMEM((2 * hps, tq, tq + LANES), F32), pltpu.VMEM((2 * hps, tq, tq + LANES), F32),
                        pltpu.VMEM((2 * hps, tq, LANES), F32),
                        pltpu.VMEM((2 * hps, tq, 2 * DA_V_DIM), F32)],
        compiler_params=_params("parallel", "parallel", "arbitrary"),
        name="diff_attention",
    )(z3, z3, z3, z3, lmb, head_norm)


def _merge_kernel(ys_ref, ya_ref, g_ref, x_ref, wa_ref, wb_ref, wo_ref, n_ref, o_ref):
    a = _wdot(ys_ref[...], wa_ref[...])
    b = _wdot(ya_ref[...], wb_ref[...])
    ga = g_ref[:, :D_MODEL].astype(F32)
    gb = g_ref[:, D_MODEL:].astype(F32)
    merged = (ga * a + gb * b).astype(BF16)
    o_ref[...] = x_ref[...] + _rms(_wdot(merged, wo_ref[...]), n_ref[...])


def _merge(ys2, ya2, z2, x2, w_a, w_b, w_o, gain):
    t = x2.shape[0]
    tm = min(TILE_MERGE, t)
    row = lambda i: (i, 0)
    const = lambda i: (0, 0)
    wspec = pl.BlockSpec((D_MODEL, D_MODEL), const, pipeline_mode=pl.Buffered(1))
    return pl.pallas_call(
        _merge_kernel,
        grid=(t // tm,),
        in_specs=[
            pl.BlockSpec((tm, D_MODEL), row),
            pl.BlockSpec((tm, D_MODEL), row),
            pl.BlockSpec((tm, 2 * D_MODEL), lambda i: (i, 2)),
            pl.BlockSpec((tm, D_MODEL), row),
            wspec, wspec, wspec,
            pl.BlockSpec((1, D_MODEL), const),
        ],
        out_specs=pl.BlockSpec((tm, D_MODEL), row),
        out_shape=jax.ShapeDtypeStruct((t, D_MODEL), F32),
        compiler_params=_params("parallel"),
        name="merge",
    )(ys2, ya2, z2, x2, w_a, w_b, w_o, gain)


def _xattn_kernel(x_ref, mem_ref, nmem_ref, wkv_ref, npre_ref, wq_ref, wo_ref, npost_ref, o_ref,
                  kv_ref, oh_ref):
    @pl.when(pl.program_id(1) == 0)
    def _():
        kv_ref[...] = _wdot(_rms(mem_ref[...], nmem_ref[...]).astype(BF16), wkv_ref[...]).astype(BF16)

    x = x_ref[...]
    h = _rms(x, npre_ref[...]).astype(BF16)
    q = (_wdot(h, wq_ref[...]) * (XA_HEAD_DIM ** -0.5 * LOG2E)).astype(BF16)
    nt = (((1,), (1,)), ((), ()))
    for hd in range(XA_HEADS):
        lo = hd * XA_HEAD_DIM
        k = kv_ref[:, lo:lo + XA_HEAD_DIM]
        v = kv_ref[:, D_MODEL + lo:D_MODEL + lo + XA_HEAD_DIM]
        s = lax.dot_general(q[:, lo:lo + XA_HEAD_DIM], k, nt, preferred_element_type=F32)
        p = jnp.exp2(s - jnp.max(s, axis=1, keepdims=True))
        o = _bdot(p.astype(BF16), v) / jnp.sum(p, axis=1, keepdims=True)
        oh_ref[:, lo:lo + XA_HEAD_DIM] = o.astype(BF16)
    o_ref[...] = x + _rms(_wdot(oh_ref[...], wo_ref[...]), npost_ref[...])


def _cross_attention(x3, mem, n_mem, w_xkv, n_pre, w_xq, w_xo, n_post):
    nb, seqlen, _ = x3.shape
    mlen = mem.shape[1]
    tm = min(TILE_XATTN, seqlen)
    const = lambda b, i: (0, 0)
    single = pl.Buffered(1)
    wspec = pl.BlockSpec((D_MODEL, D_MODEL), const, pipeline_mode=single)
    vspec = pl.BlockSpec((1, D_MODEL), const)
    return pl.pallas_call(
        _xattn_kernel,
        grid=(nb, seqlen // tm),
        in_specs=[
            pl.BlockSpec((None, tm, D_MODEL), lambda b, i: (b, i, 0)),
            pl.BlockSpec((None, mlen, D_MODEL), lambda b, i: (b, 0, 0)),
            vspec,
            pl.BlockSpec((D_MODEL, 2 * D_MODEL), const, pipeline_mode=single),
            vspec, wspec, wspec, vspec,
        ],
        out_specs=pl.BlockSpec((None, tm, D_MODEL), lambda b, i: (b, i, 0)),
        out_shape=jax.ShapeDtypeStruct((nb, seqlen, D_MODEL), F32),
        scratch_shapes=[pltpu.VMEM((mlen, 2 * D_MODEL), BF16), pltpu.VMEM((tm, D_MODEL), BF16)],
        compiler_params=_params("parallel", "arbitrary"),
        name="cross_attention",
    )(x3, mem, n_mem, w_xkv, n_pre, w_xq, w_xo, n_post)


def _mlp_kernel(x_ref, npre_ref, w1_hbm, w2_hbm, npost_ref, o_ref, w1_ref, w2_ref, sem):
    nchunk = D_FF // D_MODEL

    def w1_copy(c):
        cols = pl.ds(c * D_MODEL, D_MODEL)
        return pltpu.make_async_copy(w1_hbm.at[:, cols], w1_ref.at[:, cols], sem.at[0, c])

    def w2_copy(c):
        rows = pl.ds(c * D_MODEL, D_MODEL)
        return pltpu.make_async_copy(w2_hbm.at[rows, :], w2_ref.at[rows, :], sem.at[1, c])

    @pl.when(pl.program_id(0) == 0)
    def _():
        for c in range(nchunk):
            w1_copy(c).start()
            w2_copy(c).start()

    def run(wait_for_weights):
        x = x_ref[...]
        h = _rms(x, npre_ref[...]).astype(BF16)
        f = jnp.zeros(x.shape, F32)
        for c in range(nchunk):
            lo = c * D_MODEL
            if wait_for_weights:
                w1_copy(c).wait()
            a = jnp.maximum(_wdot(h, w1_ref[:, lo:lo + D_MODEL]), 0.0)
            if wait_for_weights:
                w2_copy(c).wait()
            f = f + _wdot((a * a).astype(BF16), w2_ref[lo:lo + D_MODEL, :])
        o_ref[...] = x + _rms(f, npost_ref[...])

    @pl.when(pl.program_id(0) == 0)
    def _():
        run(True)

    @pl.when(pl.program_id(0) > 0)
    def _():
        run(False)


def _mlp(x2, n_pre, w1, w2, n_post):
    t = x2.shape[0]
    tm = min(TILE_MLP, t)
    row = lambda i: (i, 0)
    const = lambda i: (0, 0)
    vspec = pl.BlockSpec((1, D_MODEL), const)
    return pl.pallas_call(
        _mlp_kernel,
        grid=(t // tm,),
        in_specs=[
            pl.BlockSpec((tm, D_MODEL), row),
            vspec,
            pl.BlockSpec(memory_space=pl.ANY),
            pl.BlockSpec(memory_space=pl.ANY),
            vspec,
        ],
        out_specs=pl.BlockSpec((tm, D_MODEL), row),
        out_shape=jax.ShapeDtypeStruct((t, D_MODEL), F32),
        scratch_shapes=[pltpu.VMEM((D_MODEL, D_FF), F32), pltpu.VMEM((D_FF, D_MODEL), F32),
                        pltpu.SemaphoreType.DMA((2, D_FF // D_MODEL))],
        compiler_params=_params("arbitrary"),
        name="mlp",
    )(x2, n_pre, w1, w2, n_post)


def _rope_tables(seqlen):
    inv = ROPE_THETA ** (-jnp.arange(0, DA_HEAD_DIM, 2, dtype=F32) / DA_HEAD_DIM)
    ang = jnp.arange(seqlen, dtype=F32)[:, None] * inv[None, :]
    cos = jnp.tile(jnp.cos(ang), (1, 4))
    sin = jnp.tile(jnp.concatenate([-jnp.sin(ang), jnp.sin(ang)], axis=1), (1, 2))
    return cos, sin


def kernel(x, mem, norm_mix_pre, w_in, b_gate, ssm_lambda_re, ssm_lambda_im, ssm_log_dt, ssm_b_re, ssm_b_im, ssm_c_re, ssm_c_im, ssm_d, w_glu, b_glu, w_ssm_proj, da_lambda_q1, da_lambda_k1, da_lambda_q2, da_lambda_k2, da_head_norm, w_da_proj, w_mix_out, norm_mix_post, norm_x_pre, norm_mem, w_xq, w_xkv, w_xo, norm_x_post, norm_ff_pre, w_ff1, w_ff2, norm_ff_post):
    nb, seqlen, _ = x.shape
    t = nb * seqlen
    depth = w_in.shape[0]
    tabs = _rope_tables(seqlen)
    x2 = x.reshape(t, D_MODEL)
    for l in range(depth):
        lam_init = 0.8 - 0.6 * math.exp(-0.3 * l)
        z2 = _in_projection(x2, norm_mix_pre[l][None], w_in[l], b_gate[l][None], tabs, seqlen)
        z3 = z2.reshape(nb, seqlen, IN_COLS)
        bmat, cmat, lre, lim = _s5_matrices(ssm_lambda_re[l], ssm_lambda_im[l], ssm_log_dt[l],
                                            ssm_b_re[l], ssm_b_im[l], ssm_c_re[l], ssm_c_im[l])
        ys = _s5_branch(z3, bmat, cmat, lre, lim, ssm_d[l][None], w_glu[l], b_glu[l][None])
        lmb = jnp.stack([da_lambda_q1[l], da_lambda_k1[l], da_lambda_q2[l], da_lambda_k2[l]])
        ya = _diff_attention(z3, lmb, da_head_norm[l][None], lam_init)
        x2 = _merge(ys.reshape(t, D_MODEL), ya.reshape(t, D_MODEL), z2, x2,
                    w_ssm_proj[l], w_da_proj[l], w_mix_out[l],
                    norm_mix_post[l][None])
        x3 = _cross_attention(x2.reshape(nb, seqlen, D_MODEL), mem, norm_mem[l][None], w_xkv[l],
                              norm_x_pre[l][None], w_xq[l], w_xo[l],
                              norm_x_post[l][None])
        x2 = _mlp(x3.reshape(t, D_MODEL), norm_ff_pre[l][None], w_ff1[l],
                  w_ff2[l], norm_ff_post[l][None])
    return x2.reshape(nb, seqlen, D_MODEL)
```

```python
import functools
import math

import jax
import jax.numpy as jnp
from jax import lax
from jax.experimental import pallas as pl
from jax.experimental.pallas import tpu as pltpu

F32 = jnp.float32
BF16 = jnp.bfloat16

LANES = 128
TILE_IN_PROJ = 512
TILE_S5 = 512
TILE_ATTN = 512
TILE_MERGE_XATTN = 512
TILE_MLP = 512

D_MODEL = 1024
EPS = 1e-6
CHUNK = 64
SSM_GROUP = 16
SSM_GROUPS = 64
SSM_STATE = 64
SSM_SLABS = 8
SLAB_GROUPS = SSM_GROUPS // SSM_SLABS
SLAB_STATES = SLAB_GROUPS * SSM_STATE
NLB = 2 * SLAB_STATES // LANES
SLAB_ROW_PAD = 8
DA_HEADS = 8
DA_HEAD_DIM = 64
DA_V_DIM = 128
ROPE_THETA = 10000.0
ATTN_HEADS_PER_STEP = 2
ATTN_BLOCKS_PER_STEP = 2
XA_HEADS = 4
XA_HEAD_DIM = 256
D_FF = 4096
IN_COLS = 6144
LOG2E = 1.4426950408889634
NEG_BIG = -1e30
VMEM_LIMIT_BYTES = 56 * 1024 * 1024


def _params(*semantics):
    return pltpu.CompilerParams(dimension_semantics=semantics, vmem_limit_bytes=VMEM_LIMIT_BYTES)


def _rms(x, g):
    return x * lax.rsqrt(jnp.mean(x * x, axis=-1, keepdims=True) + EPS) * g


def _bdot(a, b):
    return jnp.dot(a, b, preferred_element_type=F32)


def _wdot(a, w):
    return _bdot(a, w.astype(BF16))


def _sigmoid(x):
    return 0.5 * jnp.tanh(0.5 * x) + 0.5


def _inproj_kernel(x_ref, g_ref, w_ref, bg_ref, cos_ref, sin_ref, o_ref):
    tm = x_ref.shape[0]
    hn = _rms(x_ref[...], g_ref[...]).astype(BF16)
    lane = lax.broadcasted_iota(jnp.int32, (tm, LANES), 1)
    half = DA_HEAD_DIM // 2
    first_half = (lane % DA_HEAD_DIM) < half

    def rope(acc, col0, scale):
        c = cos_ref[...] * scale
        s = sin_ref[...] * scale
        for h in range(DA_HEADS):
            blk = acc[:, h * LANES:(h + 1) * LANES]
            partner = jnp.where(first_half, pltpu.roll(blk, LANES - half, 1), pltpu.roll(blk, half, 1))
            o_ref[:, col0 + h * LANES:col0 + (h + 1) * LANES] = (blk * c + partner * s).astype(BF16)

    for j in range(IN_COLS // D_MODEL):
        col0 = j * D_MODEL
        acc = _wdot(hn, w_ref[:, col0:col0 + D_MODEL])
        if j == 1:
            rope(acc, col0, DA_HEAD_DIM ** -0.5 * LOG2E)
        elif j == 2:
            rope(acc, col0, 1.0)
        elif j >= 4:
            gate = _sigmoid(acc + bg_ref[:, col0 - 4 * D_MODEL:col0 - 3 * D_MODEL])
            o_ref[:, col0:col0 + D_MODEL] = gate.astype(BF16)
        else:
            o_ref[:, col0:col0 + D_MODEL] = acc.astype(BF16)


def _in_projection(x2, gain, w_in, b_gate, tabs, seqlen):
    t = x2.shape[0]
    tm = min(TILE_IN_PROJ, seqlen)
    nl = seqlen // tm
    const = lambda i: (0, 0)
    tab_spec = pl.BlockSpec((tm, LANES), lambda i: (i % nl, 0))
    return pl.pallas_call(
        _inproj_kernel,
        grid=(t // tm,),
        in_specs=[
            pl.BlockSpec((tm, D_MODEL), lambda i: (i, 0)),
            pl.BlockSpec((1, D_MODEL), const),
            pl.BlockSpec((D_MODEL, IN_COLS), const, pipeline_mode=pl.Buffered(1)),
            pl.BlockSpec((1, 2 * D_MODEL), const),
            tab_spec, tab_spec,
        ],
        out_specs=pl.BlockSpec((tm, IN_COLS), lambda i: (i, 0)),
        out_shape=jax.ShapeDtypeStruct((t, IN_COLS), BF16),
        compiler_params=_params("parallel"),
        name="in_projection",
    )(x2, gain, w_in, b_gate, *tabs)


def _s5_kernel(u_ref, bm_ref, cm_ref, lre_ref, lim_ref, d_ref, wg_ref, bgl_ref, o_ref,
               bu_ref, st_ref, wc_ref, il_ref, ue_ref, uo_ref, yo_ref, w_ref, k0_ref):
    nb, tc, _ = u_ref.shape
    tp = tc // 2
    pitch = bu_ref.shape[2] // SSM_SLABS
    c = pl.program_id(0)

    @pl.when(c == 0)
    def _():
        st_ref[...] = jnp.zeros(st_ref.shape, F32)
        wc_ref[...] = jnp.zeros(wc_ref.shape, F32)

    for b in range(nb):
        for j in range(SSM_SLABS):
            il_ref[j] = u_ref[b, :, j * LANES:(j + 1) * LANES].astype(F32)
        for j in range(SSM_SLABS):
            u_e = il_ref[j, pl.ds(0, tp, stride=2), :]
            u_o = il_ref[j, pl.ds(1, tp, stride=2), :]
            ue_ref[b * tp:(b + 1) * tp, j * LANES:(j + 1) * LANES] = u_e
            uo_ref[b * tp:(b + 1) * tp, j * LANES:(j + 1) * LANES] = u_o
            lhs = jnp.concatenate([u_e, u_o], axis=1).astype(BF16)
            bu = _bdot(lhs, bm_ref[j])
            for k in range(NLB):
                bu_ref[b, k, j * pitch:j * pitch + tp, :] = bu[:, k * LANES:(k + 1) * LANES]
            k0_ref[b * tp:(b + 1) * tp, j * LANES:(j + 1) * LANES] = bu[:, NLB * LANES:]

    ar = [lre_ref[:, k * LANES:(k + 1) * LANES] for k in range(NLB // 2)]
    ai = [lim_ref[:, k * LANES:(k + 1) * LANES] for k in range(NLB // 2)]

    def step(t, carry):
        out = []
        for b in range(nb):
            xs = carry[b]
            new = [None] * NLB
            for k in range(NLB // 2):
                xr, xi = xs[k], xs[k + NLB // 2]
                rows = pl.ds(t, SSM_SLABS, stride=pitch)
                new[k] = ar[k] * xr - ai[k] * xi + bu_ref[b, k, rows, :]
                new[k + NLB // 2] = ar[k] * xi + ai[k] * xr + bu_ref[b, k + NLB // 2, rows, :]
                bu_ref[b, k, rows, :] = new[k]
                bu_ref[b, k + NLB // 2, rows, :] = new[k + NLB // 2]
            out.append(tuple(new))
        return tuple(out)

    init = tuple(tuple(st_ref[b, k] for k in range(NLB)) for b in range(nb))
    fin = lax.fori_loop(0, tp, step, init, unroll=8)
    for b in range(nb):
        for k in range(NLB):
            st_ref[b, k] = fin[b][k]

    for b in range(nb):
        for j in range(SSM_SLABS):
            xs = jnp.concatenate([bu_ref[b, k, j * pitch:j * pitch + tp, :] for k in range(NLB)], axis=1)
            r = _bdot(xs.astype(BF16), cm_ref[j])
            yo_ref[b * tp:(b + 1) * tp, j * LANES:(j + 1) * LANES] = r[:, :LANES]
            w_ref[b * tp:(b + 1) * tp, j * LANES:(j + 1) * LANES] = r[:, LANES:]

    def glu(y):
        ys = 0.5 * y * (1.0 + jnp.tanh(math.sqrt(2.0 / math.pi) * (y + 0.044715 * (y * y * y))))
        gate = _sigmoid(_wdot(ys.astype(BF16), wg_ref[...]) + bgl_ref[...])
        return ys * gate

    first_row = lax.broadcasted_iota(jnp.int32, (tp, D_MODEL), 0) == 0
    shifted = []
    for b in range(nb):
        w = w_ref[b * tp:(b + 1) * tp, :]
        shifted.append(jnp.where(first_row, wc_ref[b], pltpu.roll(w, 1, 0)))
        wc_ref[b] = w[tp - 1:tp, :]
    yo_ref[...] = glu(yo_ref[...] + d_ref[...] * uo_ref[...])
    w_ref[...] = glu(jnp.concatenate(shifted, axis=0) + k0_ref[...] + d_ref[...] * ue_ref[...])
    for b in range(nb):
        for j in range(SSM_SLABS):
            il_ref[j, pl.ds(0, tp, stride=2), :] = w_ref[b * tp:(b + 1) * tp, j * LANES:(j + 1) * LANES]
            il_ref[j, pl.ds(1, tp, stride=2), :] = yo_ref[b * tp:(b + 1) * tp, j * LANES:(j + 1) * LANES]
        for j in range(SSM_SLABS):
            o_ref[b, :, j * LANES:(j + 1) * LANES] = il_ref[j].astype(BF16)


def _s5_branch(z3, bmat, cmat, lam_re, lam_im, d_skip, w_glu, b_glu):
    nb, seqlen, _ = z3.shape
    tc = min(TILE_S5, seqlen)
    tp = tc // 2
    const2 = lambda c: (0, 0)
    const3 = lambda c: (0, 0, 0)
    single = pl.Buffered(1)
    return pl.pallas_call(
        _s5_kernel,
        grid=(seqlen // tc,),
        in_specs=[
            pl.BlockSpec((nb, tc, D_MODEL), lambda c: (0, c, 0)),
            pl.BlockSpec(bmat.shape, const3, pipeline_mode=single),
            pl.BlockSpec(cmat.shape, const3, pipeline_mode=single),
            pl.BlockSpec(lam_re.shape, const2),
            pl.BlockSpec(lam_im.shape, const2),
            pl.BlockSpec((1, D_MODEL), const2),
            pl.BlockSpec((D_MODEL, D_MODEL), const2, pipeline_mode=single),
            pl.BlockSpec((1, D_MODEL), const2),
        ],
        out_specs=pl.BlockSpec((nb, tc, D_MODEL), lambda c: (0, c, 0)),
        out_shape=jax.ShapeDtypeStruct((nb, seqlen, D_MODEL), BF16),
        scratch_shapes=[
            pltpu.VMEM((nb, NLB, SSM_SLABS * (tp + SLAB_ROW_PAD), LANES), F32),
            pltpu.VMEM((nb, NLB, SSM_SLABS, LANES), F32),
            pltpu.VMEM((nb, 1, D_MODEL), F32),
            pltpu.VMEM((SSM_SLABS, tc, LANES), F32),
        ] + [pltpu.VMEM((nb * tp, D_MODEL), F32)] * 5,
        compiler_params=_params("arbitrary"),
        name="s5_branch",
    )(z3, bmat, cmat, lam_re, lam_im, d_skip, w_glu, b_glu)


def _s5_matrices(lam_re, lam_im, log_dt, b_re, b_im, c_re, c_im):
    dt = jnp.exp(log_dt)[:, None]
    mag = jnp.exp(lam_re * dt)
    lb_re, lb_im = mag * jnp.cos(lam_im * dt), mag * jnp.sin(lam_im * dt)
    den = lam_re * lam_re + lam_im * lam_im
    cf_re = ((lb_re - 1.0) * lam_re + lb_im * lam_im) / den
    cf_im = (lb_im * lam_re - (lb_re - 1.0) * lam_im) / den
    eye = jnp.eye(SLAB_GROUPS, dtype=F32)[None, :, None, :, None]
    by_slab = (SSM_SLABS, SLAB_GROUPS)

    def col_factor(x):
        return x.reshape(by_slab + (SSM_STATE,))[:, None, None, :, :]

    def row_factor(x):
        return x.reshape(by_slab + (SSM_STATE,))[:, :, :, None, None]

    def in_place(x):
        return x.reshape(by_slab + (SSM_STATE, SSM_GROUP)).transpose(0, 1, 3, 2)[:, :, :, None, :] * eye

    def out_place(x):
        return x.reshape(by_slab + (SSM_GROUP, SSM_STATE)).transpose(0, 3, 1, 2)[:, None] * eye

    def in_flat(x):
        return x.reshape(SSM_SLABS, LANES, SLAB_STATES)

    def out_flat(x):
        return x.reshape(SSM_SLABS, SLAB_STATES, LANES)

    br, bi = in_place(b_re), in_place(b_im)
    bo_re = col_factor(cf_re) * br - col_factor(cf_im) * bi
    bo_im = col_factor(cf_re) * bi + col_factor(cf_im) * br
    be_re = col_factor(lb_re) * bo_re - col_factor(lb_im) * bo_im
    be_im = col_factor(lb_re) * bo_im + col_factor(lb_im) * bo_re
    cr, ci = out_place(c_re), out_place(c_im)
    cn_re = row_factor(lb_re) * cr - row_factor(lb_im) * ci
    cn_im = row_factor(lb_re) * ci + row_factor(lb_im) * cr
    b_odd = jnp.concatenate([in_flat(bo_re), in_flat(bo_im)], axis=2)
    cur = jnp.concatenate([out_flat(cr), out_flat(-ci)], axis=1)
    k0 = jnp.einsum('jrs,jsc->jrc', b_odd, cur, precision=lax.Precision.HIGHEST)
    top = jnp.concatenate([in_flat(be_re), in_flat(be_im), k0], axis=2)
    bot = jnp.concatenate([b_odd, jnp.zeros_like(k0)], axis=2)
    bmat = jnp.concatenate([top, bot], axis=1).astype(BF16)
    nxt = jnp.concatenate([out_flat(cn_re), out_flat(-cn_im)], axis=1)
    cmat = jnp.concatenate([cur, nxt], axis=2).astype(BF16)
    lre = (lb_re * lb_re - lb_im * lb_im).reshape(SSM_SLABS, SLAB_STATES)
    lim = (2.0 * lb_re * lb_im).reshape(SSM_SLABS, SLAB_STATES)
    return bmat, cmat, lre, lim


def _dattn_kernel(q_ref, qn_ref, k_ref, v_ref, lmb_ref, hn_ref, o_ref,
                  qm_ref, vx_ref, sx_ref, sy_ref, m_ref, acc_ref, *, lam_init, tq):
    tk = tq
    nsub = q_ref.shape[0] // tq
    nstrips = tk // LANES
    nheads = q_ref.shape[1] // LANES
    nkb = k_ref.shape[0] // tk
    lane = lax.broadcasted_iota(jnp.int32, (tq, LANES), 1)
    nt = (((1,), (1,)), ((), ()))

    @pl.when(pl.program_id(2) == 0)
    def _():
        for hh in range(nheads):
            vx_ref[hh, :, :DA_V_DIM] = v_ref[:, hh * LANES:(hh + 1) * LANES]
            vx_ref[hh, :, DA_V_DIM:] = jnp.ones((v_ref.shape[0], DA_V_DIM), BF16)

    def mask_maps(src_ref, row0):
        for hh in range(nheads):
            q = src_ref[row0:row0 + tq, hh * LANES:(hh + 1) * LANES]
            zero = jnp.zeros_like(q)
            qm_ref[2 * hh] = jnp.where(lane < DA_HEAD_DIM, q, zero)
            qm_ref[2 * hh + 1] = jnp.where(lane >= DA_HEAD_DIM, q, zero)

    def scores(ci, ki, s_ref):
        start = pl.multiple_of(ki * tk, tk)
        hh = ci // 2
        kb = k_ref[pl.ds(start, tk), hh * LANES:(hh + 1) * LANES]
        s = lax.dot_general(qm_ref[ci], kb, nt, preferred_element_type=F32)
        s_ref[ci, :, :tk] = s
        s_ref[ci, :, tk:] = functools.reduce(jnp.maximum, [s[:, c * LANES:(c + 1) * LANES] for c in range(nstrips)])

    def update(ci, ki, s_ref, first=False):
        start = pl.multiple_of(ki * tk, tk)
        hh = ci // 2
        vb = vx_ref[hh, pl.ds(start, tk), :]
        m_next = jnp.max(s_ref[ci, :, tk:], axis=1, keepdims=True)
        if first:
            m_next = jnp.broadcast_to(m_next, (tq, LANES))
        else:
            m_prev = m_ref[ci]
            m_next = jnp.maximum(m_prev, m_next)
            alpha = jnp.exp2(m_prev - m_next)
        ps = [jnp.exp2(s_ref[ci, :, c * LANES:(c + 1) * LANES] - m_next) for c in range(nstrips)]
        pv = _bdot(jnp.concatenate(ps, axis=1).astype(BF16), vb)
        acc_ref[ci] = pv if first else jnp.concatenate([alpha, alpha], axis=1) * acc_ref[ci] + pv
        m_ref[ci] = m_next

    def update_diag(ci, ki, s_ref):
        half = tq // 2
        start = pl.multiple_of(ki * tk, tk)
        hh = ci // 2
        row = lax.broadcasted_iota(jnp.int32, (half, half), 0) // CHUNK
        col = lax.broadcasted_iota(jnp.int32, (half, half), 1) // CHUNK
        visible = col <= row
        for part in range(2):
            rows = slice(part * half, (part + 1) * half)
            ncols = half * (part + 1)
            strips = []
            for c in range(ncols // LANES):
                s = s_ref[ci, rows, c * LANES:(c + 1) * LANES]
                off = c * LANES - part * half
                if off >= 0:
                    s = jnp.where(visible[:, off:off + LANES], s, NEG_BIG)
                strips.append(s)
            m_prev = m_ref[ci, rows, :]
            m_cur = functools.reduce(jnp.maximum, strips)
            m_next = jnp.maximum(m_prev, jnp.max(m_cur, axis=1, keepdims=True))
            alpha = jnp.exp2(m_prev - m_next)
            p = jnp.concatenate([jnp.exp2(s - m_next) for s in strips], axis=1).astype(BF16)
            pv = _bdot(p, vx_ref[hh, pl.ds(start, ncols), :])
            acc_ref[ci, rows, :] = jnp.concatenate([alpha, alpha], axis=1) * acc_ref[ci, rows, :] + pv
            m_ref[ci, rows, :] = m_next

    def stage(k_next, s_next_ref, k_cur, s_cur_ref, diag=False):
        for ci in range(2 * nheads):
            if k_next is not None:
                scores(ci, k_next, s_next_ref)
            if diag:
                update_diag(ci, k_cur, s_cur_ref)
            else:
                update(ci, k_cur, s_cur_ref)

    def query_block(sub):
        qi = pl.program_id(2) * nsub + sub
        rows = slice(sub * tq, (sub + 1) * tq)

        @pl.when(qi == 0)
        def _():
            mask_maps(q_ref, 0)
            m_ref[...] = jnp.full(m_ref.shape, NEG_BIG, F32)
            acc_ref[...] = jnp.zeros(acc_ref.shape, F32)
            for ci in range(2 * nheads):
                scores(ci, 0, sx_ref)

        def pair(p, carry):
            stage(2 * p + 2, sy_ref, 2 * p + 1, sx_ref)
            stage(2 * p + 3, sx_ref, 2 * p + 2, sy_ref)
            return carry

        lax.fori_loop(0, (qi - 1) // 2, pair, 0)
        diag_in_sx = (qi == 0) | (qi % 2 == 1)

        @pl.when(diag_in_sx)
        def _():
            stage(None, None, qi, sx_ref, diag=True)

        @pl.when(jnp.logical_not(diag_in_sx))
        def _():
            stage(qi, sy_ref, qi - 1, sx_ref)
            stage(None, None, qi, sy_ref, diag=True)

        lv = lmb_ref[...]
        lam = (jnp.exp(jnp.sum(lv[0:1] * lv[1:2], axis=1, keepdims=True))
               - jnp.exp(jnp.sum(lv[2:3] * lv[3:4], axis=1, keepdims=True)) + lam_init)
        for hh in range(nheads):
            o = (acc_ref[2 * hh, :, :DA_V_DIM] / acc_ref[2 * hh, :, DA_V_DIM:]
                 - lam * (acc_ref[2 * hh + 1, :, :DA_V_DIM] / acc_ref[2 * hh + 1, :, DA_V_DIM:]))
            o = o * lax.rsqrt(jnp.mean(o * o, axis=-1, keepdims=True) + EPS) * hn_ref[...]
            o_ref[rows, hh * LANES:(hh + 1) * LANES] = (o * (1.0 - lam_init)).astype(BF16)

        if sub + 1 < nsub:
            mask_maps(q_ref, (sub + 1) * tq)
        else:
            mask_maps(qn_ref, 0)
        for ci in range(2 * nheads):
            scores(ci, 0, sy_ref)
        for ci in range(2 * nheads):
            scores(ci, min(1, nkb - 1), sx_ref)
            update(ci, 0, sy_ref, first=True)

    for sub in range(nsub):
        query_block(sub)


def _diff_attention(z3, lmb, head_norm, lam_init):
    nb, seqlen, _ = z3.shape
    tq = min(TILE_ATTN, seqlen)
    nq = seqlen // tq
    nsub = ATTN_BLOCKS_PER_STEP if nq % ATTN_BLOCKS_PER_STEP == 0 else 1
    hps = ATTN_HEADS_PER_STEP
    width = hps * LANES
    qcol, kcol, vcol = (c // width for c in (D_MODEL, 2 * D_MODEL, 3 * D_MODEL))
    return pl.pallas_call(
        functools.partial(_dattn_kernel, lam_init=lam_init, tq=tq),
        grid=(nb, DA_HEADS // hps, nq // nsub),
        in_specs=[
            pl.BlockSpec((None, nsub * tq, width), lambda b, h, i: (b, i, qcol + h)),
            pl.BlockSpec((None, tq, width), lambda b, h, i: (b, jnp.minimum((i + 1) * nsub, nq - 1), qcol + h)),
            pl.BlockSpec((None, seqlen, width), lambda b, h, i: (b, 0, kcol + h)),
            pl.BlockSpec((None, seqlen, width), lambda b, h, i: (b, 0, vcol + h)),
            pl.BlockSpec((4, DA_HEAD_DIM), lambda b, h, i: (0, 0)),
            pl.BlockSpec((1, DA_V_DIM), lambda b, h, i: (0, 0)),
        ],
        out_specs=pl.BlockSpec((None, nsub * tq, width), lambda b, h, i: (b, i, h)),
        out_shape=jax.ShapeDtypeStruct((nb, seqlen, D_MODEL), BF16),
        scratch_shapes=[pltpu.VMEM((2 * hps, tq, LANES), BF16),
                        pltpu.VMEM((hps, seqlen, 2 * DA_V_DIM), BF16),
                        pltpu.VMEM((2 * hps, tq, tq + LANES), F32), pltpu.VMEM((2 * hps, tq, tq + LANES), F32),
                        pltpu.VMEM((2 * hps, tq, LANES), F32),
                        pltpu.VMEM((2 * hps, tq, 2 * DA_V_DIM), F32)],
        compiler_params=_params("parallel", "parallel", "arbitrary"),
        name="diff_attention",
    )(z3, z3, z3, z3, lmb, head_norm)


def _xattn_kernel(ys_ref, ya_ref, g_ref, xin_ref, wa_ref, wb_ref, wmo_ref, nmix_ref,
                  mem_ref, nmem_ref, wkv_ref, npre_ref, wq_ref, wo_ref, npost_ref, o_ref,
                  kv_ref, oh_ref):
    @pl.when(pl.program_id(1) == 0)
    def _():
        kv_ref[...] = _wdot(_rms(mem_ref[...], nmem_ref[...]).astype(BF16), wkv_ref[...]).astype(BF16)

    a = _wdot(ys_ref[...], wa_ref[...])
    b = _wdot(ya_ref[...], wb_ref[...])
    ga = g_ref[:, :D_MODEL].astype(F32)
    gb = g_ref[:, D_MODEL:].astype(F32)
    merged = (ga * a + gb * b).astype(BF16)
    x = xin_ref[...] + _rms(_wdot(merged, wmo_ref[...]), nmix_ref[...])
    h = _rms(x, npre_ref[...]).astype(BF16)
    q = (_wdot(h, wq_ref[...]) * (XA_HEAD_DIM ** -0.5 * LOG2E)).astype(BF16)
    nt = (((1,), (1,)), ((), ()))
    for hd in range(XA_HEADS):
        lo = hd * XA_HEAD_DIM
        k = kv_ref[:, lo:lo + XA_HEAD_DIM]
        v = kv_ref[:, D_MODEL + lo:D_MODEL + lo + XA_HEAD_DIM]
        s = lax.dot_general(q[:, lo:lo + XA_HEAD_DIM], k, nt, preferred_element_type=F32)
        p = jnp.exp2(s - jnp.max(s, axis=1, keepdims=True))
        o = _bdot(p.astype(BF16), v) / jnp.sum(p, axis=1, keepdims=True)
        oh_ref[:, lo:lo + XA_HEAD_DIM] = o.astype(BF16)
    o_ref[...] = x + _rms(_wdot(oh_ref[...], wo_ref[...]), npost_ref[...])


def _merge_cross_attention(ys3, ya3, z3, x3, w_a, w_b, w_mo, n_mix, mem, n_mem, w_xkv, n_pre, w_xq, w_xo, n_post):
    nb, seqlen, _ = x3.shape
    mlen = mem.shape[1]
    tm = min(TILE_MERGE_XATTN, seqlen)
    tile = pl.BlockSpec((None, tm, D_MODEL), lambda b, i: (b, i, 0))
    const = lambda b, i: (0, 0)
    single = pl.Buffered(1)
    wspec = pl.BlockSpec((D_MODEL, D_MODEL), const, pipeline_mode=single)
    vspec = pl.BlockSpec((1, D_MODEL), const)
    return pl.pallas_call(
        _xattn_kernel,
        grid=(nb, seqlen // tm),
        in_specs=[
            tile, tile,
            pl.BlockSpec((None, tm, 2 * D_MODEL), lambda b, i: (b, i, 2)),
            tile,
            wspec, wspec, wspec, vspec,
            pl.BlockSpec((None, mlen, D_MODEL), lambda b, i: (b, 0, 0)),
            vspec,
            pl.BlockSpec((D_MODEL, 2 * D_MODEL), const, pipeline_mode=single),
            vspec, wspec, wspec, vspec,
        ],
        out_specs=pl.BlockSpec((None, tm, D_MODEL), lambda b, i: (b, i, 0)),
        out_shape=jax.ShapeDtypeStruct((nb, seqlen, D_MODEL), F32),
        scratch_shapes=[pltpu.VMEM((mlen, 2 * D_MODEL), BF16), pltpu.VMEM((tm, D_MODEL), BF16)],
        compiler_params=_params("parallel", "arbitrary"),
        name="merge_cross_attention",
    )(ys3, ya3, z3, x3, w_a, w_b, w_mo, n_mix, mem, n_mem, w_xkv, n_pre, w_xq, w_xo, n_post)


def _mlp_kernel(x_ref, npre_ref, w1_ref, w2_ref, npost_ref, o_ref):
    x = x_ref[...]
    h = _rms(x, npre_ref[...]).astype(BF16)
    f = jnp.zeros(x.shape, F32)
    for c in range(D_FF // D_MODEL):
        lo = c * D_MODEL
        a = jnp.maximum(_wdot(h, w1_ref[:, lo:lo + D_MODEL]), 0.0)
        f = f + _wdot((a * a).astype(BF16), w2_ref[lo:lo + D_MODEL, :])
    o_ref[...] = x + _rms(f, npost_ref[...])


def _mlp(x2, n_pre, w1, w2, n_post):
    t = x2.shape[0]
    tm = min(TILE_MLP, t)
    row = lambda i: (i, 0)
    const = lambda i: (0, 0)
    vspec = pl.BlockSpec((1, D_MODEL), const)
    return pl.pallas_call(
        _mlp_kernel,
        grid=(t // tm,),
        in_specs=[
            pl.BlockSpec((tm, D_MODEL), row),
            vspec,
            pl.BlockSpec((D_MODEL, D_FF), const, pipeline_mode=pl.Buffered(1)),
            pl.BlockSpec((D_FF, D_MODEL), const, pipeline_mode=pl.Buffered(1)),
            vspec,
        ],
        out_specs=pl.BlockSpec((tm, D_MODEL), row),
        out_shape=jax.ShapeDtypeStruct((t, D_MODEL), F32),
        compiler_params=_params("parallel"),
        name="mlp",
    )(x2, n_pre, w1, w2, n_post)


def _rope_tables(seqlen):
    inv = ROPE_THETA ** (-jnp.arange(0, DA_HEAD_DIM, 2, dtype=F32) / DA_HEAD_DIM)
    ang = jnp.arange(seqlen, dtype=F32)[:, None] * inv[None, :]
    cos = jnp.tile(jnp.cos(ang), (1, 4))
    sin = jnp.tile(jnp.concatenate([-jnp.sin(ang), jnp.sin(ang)], axis=1), (1, 2))
    return cos, sin


def kernel(x, mem, norm_mix_pre, w_in, b_gate, ssm_lambda_re, ssm_lambda_im, ssm_log_dt, ssm_b_re, ssm_b_im, ssm_c_re, ssm_c_im, ssm_d, w_glu, b_glu, w_ssm_proj, da_lambda_q1, da_lambda_k1, da_lambda_q2, da_lambda_k2, da_head_norm, w_da_proj, w_mix_out, norm_mix_post, norm_x_pre, norm_mem, w_xq, w_xkv, w_xo, norm_x_post, norm_ff_pre, w_ff1, w_ff2, norm_ff_post):
    nb, seqlen, _ = x.shape
    t = nb * seqlen
    depth = w_in.shape[0]
    tabs = _rope_tables(seqlen)
    x2 = x.reshape(t, D_MODEL)
    for l in range(depth):
        lam_init = 0.8 - 0.6 * math.exp(-0.3 * l)
        z2 = _in_projection(x2, norm_mix_pre[l][None], w_in[l], b_gate[l][None], tabs, seqlen)
        z3 = z2.reshape(nb, seqlen, IN_COLS)
        bmat, cmat, lre, lim = _s5_matrices(ssm_lambda_re[l], ssm_lambda_im[l], ssm_log_dt[l],
                                            ssm_b_re[l], ssm_b_im[l], ssm_c_re[l], ssm_c_im[l])
        ys = _s5_branch(z3, bmat, cmat, lre, lim, ssm_d[l][None], w_glu[l], b_glu[l][None])
        lmb = jnp.stack([da_lambda_q1[l], da_lambda_k1[l], da_lambda_q2[l], da_lambda_k2[l]])
        ya = _diff_attention(z3, lmb, da_head_norm[l][None], lam_init)
        x3 = _merge_cross_attention(ys, ya, z3, x2.reshape(nb, seqlen, D_MODEL),
                                    w_ssm_proj[l], w_da_proj[l], w_mix_out[l], norm_mix_post[l][None],
                                    mem, norm_mem[l][None], w_xkv[l],
                                    norm_x_pre[l][None], w_xq[l], w_xo[l], norm_x_post[l][None])
        x2 = _mlp(x3.reshape(t, D_MODEL), norm_ff_pre[l][None], w_ff1[l],
                  w_ff2[l], norm_ff_post[l][None])
    return x2.reshape(nb, seqlen, D_MODEL)
```
